```python
import jax, jax.numpy as jnp
from jax import lax
import numpy as np

D_MODEL = 1024
BATCH = 8
SEQ = 2048
DEPTH = 2
DEC_BATCH = 128
DEC_SEQ = 8
PAST_LEN = 16384
PAGE_SIZE = 128

D_MIX = D_MODEL
HEAD_DIM = 64
W_CONV = D_MIX // 4
W_GMLP = D_MIX // 2
W_POOL = D_MIX - W_CONV - W_GMLP
N_GMLP_HEADS = W_GMLP // HEAD_DIM
POOL_WINDOWS = (2, 4, 8, 16)
N_POOL_GROUPS = len(POOL_WINDOWS)
POOL_GROUP_DIM = W_POOL // N_POOL_GROUPS
CONV_WIDTH = 3
CONV_BUF = CONV_WIDTH - 1
POOL_BUF = max(POOL_WINDOWS) - 1
CHUNK = 128
D_FF = 4 * D_MODEL
D_IN_PROJ = 3 * W_CONV + 2 * W_GMLP + W_POOL
DEEPNORM_ALPHA = (2.0 * DEPTH) ** 0.25
DEEPNORM_BETA = (8.0 * DEPTH) ** -0.25
LN_EPS = 1e-5

kernel_name = "hymba_conv_gmlp_pool_deepnorm_adaln_step"


def layer_norm(x, g, b):
    xf = x.astype(jnp.float32)
    mu = jnp.mean(xf, axis=-1, keepdims=True)
    var = jnp.mean(jnp.square(xf - mu), axis=-1, keepdims=True)
    y = (xf - mu) * lax.rsqrt(var + LN_EPS)
    return (y * g.astype(jnp.float32) + b.astype(jnp.float32)).astype(x.dtype)


def short_conv(z, prefix, conv_w):
    t = z.shape[1]
    ext = jnp.concatenate([prefix, z], axis=1)
    y = ext[:, 0:t] * conv_w[0] + ext[:, 1:t + 1] * conv_w[1] + ext[:, 2:t + 2] * conv_w[2]
    return y, ext[:, -CONV_BUF:]


def chunk_sgu(u, v, sgu_g, sgu_b, w_s, b_s):
    bsz, t, _ = v.shape
    v_n = layer_norm(v, sgu_g, sgu_b)
    lc = min(t, CHUNK)
    nc = t // lc
    mask = jnp.tril(jnp.ones((lc, lc), dtype=bool))
    ws = jnp.where(mask[None], w_s[:, :lc, :lc], jnp.zeros((), w_s.dtype))
    vc = v_n.reshape(bsz, nc, lc, N_GMLP_HEADS, HEAD_DIM)
    s = jnp.einsum("hij,bnjhd->bnihd", ws, vc) + b_s[:, :lc].T[None, None, :, :, None]
    return u * s.reshape(bsz, t, W_GMLP), v_n


def multiscale_pool(z, prefix, n_valid, pool_w, pool_scale):
    bsz, t, _ = z.shape
    ext = jnp.concatenate([prefix, z], axis=1)
    cs = jnp.cumsum(ext.astype(jnp.float32), axis=1)
    cs = jnp.concatenate([jnp.zeros((bsz, 1, W_POOL), jnp.float32), cs], axis=1)
    p = POOL_BUF
    pos = jnp.arange(t)
    pooled = []
    for g, w in enumerate(POOL_WINDOWS):
        lo, hi = g * POOL_GROUP_DIM, (g + 1) * POOL_GROUP_DIM
        s = cs[:, p + 1:p + t + 1, lo:hi] - cs[:, p + 1 - w:p + t + 1 - w, lo:hi]
        count = jnp.minimum(w, pos + 1 + n_valid).astype(jnp.float32)
        pooled.append(s / count[None, :, None])
    pooled = jnp.concatenate(pooled, axis=-1)
    d = (pooled - z.astype(jnp.float32)).astype(z.dtype)
    y = jnp.einsum("btgc,gcd->btgd", d.reshape(bsz, t, N_POOL_GROUPS, POOL_GROUP_DIM), pool_w)
    return y.reshape(bsz, t, W_POOL) * pool_scale, ext[:, -POOL_BUF:]


def mixer_block(h, conv_prefix, pool_prefix, n_valid, w_in, conv_w, sgu_g, sgu_b, w_s, b_s,
                pool_w, pool_scale, w_out):
    proj = jnp.einsum("btd,de->bte", h, w_in)
    cuts = [W_CONV, 2 * W_CONV, 3 * W_CONV, 3 * W_CONV + W_GMLP, 3 * W_CONV + 2 * W_GMLP]
    a_b, a_c, a_x, g_u, g_v, p_x = jnp.split(proj, cuts, axis=-1)
    conv_out, new_conv = short_conv(a_c * a_x, conv_prefix, conv_w)
    y_a = a_b * conv_out
    y_b, v_rows = chunk_sgu(g_u, g_v, sgu_g, sgu_b, w_s, b_s)
    y_c, new_pool = multiscale_pool(p_x, pool_prefix, n_valid, pool_w, pool_scale)
    mix = jnp.concatenate([y_a, y_b, y_c], axis=-1)
    return jnp.einsum("bte,ed->btd", mix, w_out), new_conv, new_pool, v_rows


def decoder_layer(x, c, conv_prefix, pool_prefix, n_valid, w_ada, b_ada, w_in, conv_w, sgu_g,
                  sgu_b, w_s, b_s, pool_w, pool_scale, w_out, ln1_g, ln1_b, w_up, w_down,
                  ln2_g, ln2_b):
    mod = jnp.einsum("bd,de->be", jax.nn.silu(c), w_ada) + b_ada
    sh1, sc1, g1, sh2, sc2, g2 = jnp.split(mod[:, None, :], 6, axis=-1)
    h = x * (1.0 + sc1) + sh1
    mix, new_conv, new_pool, v_rows = mixer_block(h, conv_prefix, pool_prefix, n_valid, w_in,
                                                  conv_w, sgu_g, sgu_b, w_s, b_s, pool_w,
                                                  pool_scale, w_out)
    x = layer_norm(DEEPNORM_ALPHA * x + g1 * mix, ln1_g, ln1_b)
    h2 = x * (1.0 + sc2) + sh2
    f = jnp.einsum("btf,fd->btd", jnp.square(jax.nn.relu(jnp.einsum("btd,df->btf", h2, w_up))), w_down)
    x = layer_norm(DEEPNORM_ALPHA * x + g2 * f, ln2_g, ln2_b)
    return x, new_conv, new_pool, v_rows


def setup_inputs(seed: int = 0) -> dict:
    key = jax.random.key(seed)
    ks = jax.random.split(key, 24)

    def nrm(k, shape, scale=1.0):
        return jax.random.normal(k, shape, jnp.float32) * scale

    return {
        "x_prompt": nrm(ks[0], (BATCH, SEQ, D_MODEL)),
        "x_sample": nrm(ks[1], (DEC_BATCH, DEC_SEQ, D_MODEL)),
        "c_prompt": nrm(ks[2], (BATCH, D_MODEL)),
        "c_sample": nrm(ks[3], (DEC_BATCH, D_MODEL)),
        "state_conv": nrm(ks[4], (DEPTH, DEC_BATCH, CONV_BUF, W_CONV)),
        "state_pool": nrm(ks[5], (DEPTH, DEC_BATCH, POOL_BUF, W_POOL)),
        "w_ada": nrm(ks[6], (DEPTH, D_MODEL, 6 * D_MODEL), 0.5 * D_MODEL ** -0.5),
        "b_ada": nrm(ks[7], (DEPTH, 6 * D_MODEL), 0.01),
        "w_in": nrm(ks[8], (DEPTH, D_MODEL, D_IN_PROJ), D_MODEL ** -0.5),
        "conv_w": nrm(ks[9], (DEPTH, CONV_WIDTH, W_CONV), CONV_WIDTH ** -0.5),
        "sgu_g": 1.0 + nrm(ks[10], (DEPTH, W_GMLP), 0.1),
        "sgu_b": nrm(ks[11], (DEPTH, W_GMLP), 0.01),
        "w_s": nrm(ks[12], (DEPTH, N_GMLP_HEADS, CHUNK, CHUNK), CHUNK ** -0.5),
        "b_s": 1.0 + nrm(ks[13], (DEPTH, N_GMLP_HEADS, CHUNK), 0.1),
        "pool_w": nrm(ks[14], (DEPTH, N_POOL_GROUPS, POOL_GROUP_DIM, POOL_GROUP_DIM), POOL_GROUP_DIM ** -0.5),
        "pool_scale": 1.0 + nrm(ks[15], (DEPTH, W_POOL), 0.1),
        "w_out": nrm(ks[16], (DEPTH, D_MIX, D_MODEL), DEEPNORM_BETA * D_MIX ** -0.5),
        "ln1_g": 1.0 + nrm(ks[17], (DEPTH, D_MODEL), 0.1),
        "ln1_b": nrm(ks[18], (DEPTH, D_MODEL), 0.01),
        "w_up": nrm(ks[19], (DEPTH, D_MODEL, D_FF), D_MODEL ** -0.5),
        "w_down": nrm(ks[20], (DEPTH, D_FF, D_MODEL), DEEPNORM_BETA * D_FF ** -0.5),
        "ln2_g": 1.0 + nrm(ks[21], (DEPTH, D_MODEL), 0.1),
        "ln2_b": nrm(ks[22], (DEPTH, D_MODEL), 0.01),
    }


def reference(x_prompt, x_sample, c_prompt, c_sample, state_conv, state_pool, w_ada, b_ada, w_in,
              conv_w, sgu_g, sgu_b, w_s, b_s, pool_w, pool_scale, w_out, ln1_g, ln1_b, w_up,
              w_down, ln2_g, ln2_b):
    bp = x_prompt.shape[0]
    xp, xs = x_prompt, x_sample
    conv_p, pool_p, conv_s, pool_s, v_s = [], [], [], [], []
    for l in range(DEPTH):
        params = (w_ada[l], b_ada[l], w_in[l], conv_w[l], sgu_g[l], sgu_b[l], w_s[l], b_s[l],
                  pool_w[l], pool_scale[l], w_out[l], ln1_g[l], ln1_b[l], w_up[l], w_down[l],
                  ln2_g[l], ln2_b[l])
        zc = jnp.zeros((bp, CONV_BUF, W_CONV), xp.dtype)
        zp = jnp.zeros((bp, POOL_BUF, W_POOL), xp.dtype)
        xp, nc_p, np_p, _ = decoder_layer(xp, c_prompt, zc, zp, 0, *params)
        xs, nc_s, np_s, v_rows = decoder_layer(xs, c_sample, state_conv[l], state_pool[l], POOL_BUF, *params)
        conv_p.append(nc_p)
        pool_p.append(np_p)
        conv_s.append(nc_s)
        pool_s.append(np_s)
        v_s.append(v_rows)
    new_state_conv_prompt = jnp.stack(conv_p)
    new_state_pool_prompt = jnp.stack(pool_p)
    new_state_conv_sample = jnp.stack(conv_s)
    new_state_pool_sample = jnp.stack(pool_s)
    new_state_gmlp_v_sample = jnp.stack(v_s)
    return (xp, xs, new_state_conv_prompt, new_state_pool_prompt, new_state_conv_sample,
            new_state_pool_sample, new_state_gmlp_v_sample)
```

```python
import functools

import jax
import jax.numpy as jnp
from jax import lax
from jax.experimental import pallas as pl
from jax.experimental.pallas import tpu as pltpu

F32 = jnp.float32
BF16 = jnp.bfloat16

LN_EPS = 1e-5
HEAD_DIM = 64
CHUNK = 128
POOL_WINDOWS = (2, 4, 8, 16)
CONV_WIDTH = 3
LANES = 128
SUBLANES = 8
VMEM_LIMIT_BYTES = 56 * 1024 * 1024

PROMPT_ROWS = 512
FFN_CHUNK = 512


def _layer_norm(y, g, b):
    mu = jnp.mean(y, axis=-1, keepdims=True)
    d = y - mu
    var = jnp.mean(d * d, axis=-1, keepdims=True)
    return d * lax.rsqrt(var + LN_EPS) * g + b


def _dot(a, b):
    return jnp.dot(a, b, preferred_element_type=F32)


def _resident(shape):
    zeros = (0,) * len(shape)
    return pl.BlockSpec(shape, lambda *_: zeros, pipeline_mode=pl.Buffered(1))


def _pool_select(sums, counts, lane_group):
    out = sums[-1] / counts[-1]
    for g in range(len(sums) - 2, -1, -1):
        out = jnp.where(lane_group == g, sums[g] / counts[g], out)
    return out


def _ada_kernel(c_ref, w_ref, b_ref, o_ref):
    c = c_ref[...]
    s = jax.nn.silu(c).astype(BF16)
    o_ref[0] = _dot(s, w_ref[0].astype(BF16)) + b_ref[0]


def _ada_call(c_all, w_ada, b_ada, col_tile=1536):
    depth, d_model, n_mod = w_ada.shape
    rows = c_all.shape[0]
    return pl.pallas_call(
        _ada_kernel,
        grid=(depth, n_mod // col_tile),
        in_specs=[
            pl.BlockSpec((rows, d_model), lambda l, j: (0, 0)),
            pl.BlockSpec((1, d_model, col_tile), lambda l, j: (l, 0, j)),
            pl.BlockSpec((1, 1, col_tile), lambda l, j: (l, 0, j)),
        ],
        out_specs=pl.BlockSpec((1, rows, col_tile), lambda l, j: (l, 0, j)),
        out_shape=jax.ShapeDtypeStruct((depth, rows, n_mod), F32),
        compiler_params=pltpu.CompilerParams(
            dimension_semantics=("arbitrary", "arbitrary"),
            vmem_limit_bytes=VMEM_LIMIT_BYTES),
        name="ada_mod",
    )(c_all, w_ada, b_ada.reshape(depth, 1, n_mod))


def _ffn_kernel(x_ref, mod_ref, wup_ref, wdn_ref, g_ref, b_ref, o_ref, *, alpha):
    x = x_ref[...]
    g_dim, r_dim, c_dim = x.shape
    m = mod_ref[...]
    sh, sc, gate = m[:, :c_dim], m[:, c_dim:2 * c_dim], m[:, 2 * c_dim:]
    h = (x * (1.0 + sc) + sh).reshape(g_dim * r_dim, c_dim).astype(BF16)
    acc = jnp.zeros((g_dim * r_dim, c_dim), F32)
    for k in range(wup_ref.shape[0]):
        u = _dot(h, wup_ref[k])
        a = jnp.square(jnp.maximum(u, 0.0)).astype(BF16)
        acc = acc + _dot(a, wdn_ref[k])
    y = alpha * x + gate * acc.reshape(g_dim, r_dim, c_dim)
    o_ref[...] = _layer_norm(y, g_ref[...], b_ref[...])


def _ffn_call(x, mod, wup, wdn, ln_g, ln_b, *, x_block, x_index, mod_block, mod_index,
              grid, alpha, name):
    c_dim = x.shape[-1]
    return pl.pallas_call(
        functools.partial(_ffn_kernel, alpha=alpha),
        grid=grid,
        in_specs=[
            pl.BlockSpec(x_block, x_index),
            pl.BlockSpec(mod_block, mod_index),
            _resident(wup.shape),
            _resident(wdn.shape),
            _resident((1, c_dim)),
            _resident((1, c_dim)),
        ],
        out_specs=pl.BlockSpec(x_block, x_index),
        out_shape=jax.ShapeDtypeStruct(x.shape, F32),
        compiler_params=pltpu.CompilerParams(
            dimension_semantics=("arbitrary",) * len(grid),
            vmem_limit_bytes=VMEM_LIMIT_BYTES),
        name=name,
    )(x, mod, wup, wdn, ln_g, ln_b)


def _mix_prompt_kernel(x_ref, mod_ref, win_ref, cw_ref, sg_ref, sb_ref, ws_ref, bs_ref,
                       pw_ref, ps_ref, wout_ref, g_ref, b_ref,
                       o_ref, ncv_ref, npl_ref, zbuf, pbuf, *, alpha, w_conv, w_gmlp):
    t = pl.program_id(1)
    rows = x_ref.shape[1]
    c_dim = x_ref.shape[2]
    conv_hist = zbuf.shape[0] - rows
    pool_hist = pbuf.shape[0] - rows
    w_pool = pbuf.shape[1]

    @pl.when(t == 0)
    def _():
        zbuf[0:conv_hist, :] = jnp.zeros((conv_hist, w_conv), F32)
        pbuf[0:pool_hist, :] = jnp.zeros((pool_hist, w_pool), F32)

    x = x_ref[0]
    m = mod_ref[...]
    sh, sc, gate = m[:, :c_dim], m[:, c_dim:2 * c_dim], m[:, 2 * c_dim:]
    h = (x * (1.0 + sc) + sh).astype(BF16)
    proj = _dot(h, win_ref[...])
    o0 = 0
    a_b = proj[:, o0:o0 + w_conv]; o0 += w_conv
    a_c = proj[:, o0:o0 + w_conv]; o0 += w_conv
    a_x = proj[:, o0:o0 + w_conv]; o0 += w_conv
    g_u = proj[:, o0:o0 + w_gmlp]; o0 += w_gmlp
    g_v = proj[:, o0:o0 + w_gmlp]; o0 += w_gmlp
    p_x = proj[:, o0:o0 + w_pool]

    zbuf[conv_hist:conv_hist + rows, :] = a_c * a_x
    cw = cw_ref[...]
    conv = None
    for k in range(CONV_WIDTH):
        start = conv_hist - (CONV_WIDTH - 1) + k
        term = zbuf[start:start + rows, :] * cw[k:k + 1, :]
        conv = term if conv is None else conv + term
    y_a = a_b * conv
    ncv_ref[0] = zbuf[conv_hist + rows - (CONV_WIDTH - 1):conv_hist + rows, :]

    v_n = _layer_norm(g_v, sg_ref[...], sb_ref[...])
    n_pairs = w_gmlp // LANES
    row_i = lax.broadcasted_iota(jnp.int32, (CHUNK, 2 * CHUNK), 0)
    col_j = lax.broadcasted_iota(jnp.int32, (CHUNK, 2 * CHUNK), 1) % CHUNK
    causal = col_j <= row_i
    ws_m = [jnp.where(causal, ws_ref[p], jnp.zeros((), BF16)) for p in range(n_pairs)]
    low_half = lax.broadcasted_iota(jnp.int32, (CHUNK, LANES), 1) < HEAD_DIM
    s_rows = []
    for c in range(rows // CHUNK):
        s_cols = []
        for p in range(n_pairs):
            col = v_n[c * CHUNK:(c + 1) * CHUNK, p * LANES:(p + 1) * LANES]
            rhs = jnp.concatenate([jnp.where(low_half, col, 0.0),
                                   jnp.where(low_half, 0.0, col)], axis=0).astype(BF16)
            s_cols.append(_dot(ws_m[p], rhs))
        s_rows.append(jnp.concatenate(s_cols, axis=1) + bs_ref[...])
    y_b = g_u * jnp.concatenate(s_rows, axis=0)

    pbuf[pool_hist:pool_hist + rows, :] = p_x
    pos = t * rows + lax.broadcasted_iota(jnp.int32, (rows, w_pool), 0)
    lane_group = lax.broadcasted_iota(jnp.int32, (rows, w_pool), 1) // (w_pool // len(POOL_WINDOWS))
    sums, counts = [], []
    run = None
    k = 0
    for w in POOL_WINDOWS:
        while k < w:
            piece = pbuf[pool_hist - k:pool_hist - k + rows, :]
            run = piece if run is None else run + piece
            k += 1
        sums.append(run)
        counts.append(jnp.minimum(w, pos + 1).astype(F32))
    d = (_pool_select(sums, counts, lane_group) - p_x).astype(BF16)
    y_c = _dot(d, pw_ref[...]) * ps_ref[...]
    npl_ref[0] = pbuf[rows + 1:rows + pool_hist, :]

    mix = jnp.concatenate([y_a, y_b, y_c], axis=1).astype(BF16)
    y = alpha * x + gate * _dot(mix, wout_ref[...])
    o_ref[0] = _layer_norm(y, g_ref[...], b_ref[...])

    zbuf[0:conv_hist, :] = zbuf[rows:rows + conv_hist, :]
    pbuf[0:pool_hist, :] = pbuf[rows:rows + pool_hist, :]


def _mix_prompt_call(x, mod, w_in, conv_w, sgu_g, sgu_b, ws_cat, bs_full, pool_wbd, pool_scale,
                     w_out, ln_g, ln_b, *, alpha, rows):
    batch, seq, c_dim = x.shape
    w_conv = conv_w.shape[1]
    w_gmlp = sgu_g.shape[1]
    w_pool = pool_scale.shape[1]
    conv_buf = CONV_WIDTH - 1
    pool_buf = max(POOL_WINDOWS) - 1
    kern = functools.partial(_mix_prompt_kernel, alpha=alpha, w_conv=w_conv, w_gmlp=w_gmlp)
    return pl.pallas_call(
        kern,
        grid=(batch, seq // rows),
        in_specs=[
            pl.BlockSpec((1, rows, c_dim), lambda b, t: (b, t, 0)),
            pl.BlockSpec((None, 1, 3 * c_dim), lambda b, t: (b, 0, 0)),
            _resident(w_in.shape),
            _resident(conv_w.shape),
            _resident(sgu_g.shape),
            _resident(sgu_b.shape),
            _resident(ws_cat.shape),
            _resident(bs_full.shape),
            _resident(pool_wbd.shape),
            _resident(pool_scale.shape),
            _resident(w_out.shape),
            _resident(ln_g.shape),
            _resident(ln_b.shape),
        ],
        out_specs=[
            pl.BlockSpec((1, rows, c_dim), lambda b, t: (b, t, 0)),
            pl.BlockSpec((1, conv_buf, w_conv), lambda b, t: (b, 0, 0)),
            pl.BlockSpec((1, pool_buf, w_pool), lambda b, t: (b, 0, 0)),
        ],
        out_shape=[
            jax.ShapeDtypeStruct((batch, seq, c_dim), F32),
            jax.ShapeDtypeStruct((batch, conv_buf, w_conv), F32),
            jax.ShapeDtypeStruct((batch, pool_buf, w_pool), F32),
        ],
        scratch_shapes=[
            pltpu.VMEM((rows + SUBLANES, w_conv), F32),
            pltpu.VMEM((rows + 2 * SUBLANES, w_pool), F32),
        ],
        compiler_params=pltpu.CompilerParams(
            dimension_semantics=("arbitrary", "arbitrary"),
            vmem_limit_bytes=VMEM_LIMIT_BYTES),
        name="mix_prompt",
    )(x, mod, w_in, conv_w, sgu_g, sgu_b, ws_cat, bs_full, pool_wbd, pool_scale, w_out, ln_g, ln_b)


def _mix_sample_kernel(x_ref, mod_ref, cst_ref, pst_ref, win_ref, cw_ref, sg_ref, sb_ref,
                       wst_ref, bst_ref, pw_ref, ps_ref, wout_ref, g_ref, b_ref,
                       o_ref, ncv_ref, npl_ref, v_ref, *, alpha, w_conv, w_gmlp):
    x = x_ref[...]
    t_dim, b_dim, c_dim = x.shape
    w_pool = pst_ref.shape[2]
    n_valid = pst_ref.shape[0]
    m = mod_ref[...]
    sh, sc, gate = m[:, :c_dim], m[:, c_dim:2 * c_dim], m[:, 2 * c_dim:]
    h = (x * (1.0 + sc) + sh).reshape(t_dim * b_dim, c_dim).astype(BF16)
    proj = _dot(h, win_ref[...]).reshape(t_dim, b_dim, -1)
    o0 = 0
    a_b = proj[:, :, o0:o0 + w_conv]; o0 += w_conv
    a_c = proj[:, :, o0:o0 + w_conv]; o0 += w_conv
    a_x = proj[:, :, o0:o0 + w_conv]; o0 += w_conv
    g_u = proj[:, :, o0:o0 + w_gmlp]; o0 += w_gmlp
    g_v = proj[:, :, o0:o0 + w_gmlp]; o0 += w_gmlp
    p_x = proj[:, :, o0:o0 + w_pool]

    z = a_c * a_x
    ext = [cst_ref[i] for i in range(CONV_WIDTH - 1)] + [z[i] for i in range(t_dim)]
    cw = cw_ref[...]
    y_a = []
    for i in range(t_dim):
        conv = ext[i] * cw[0:1, :]
        for k in range(1, CONV_WIDTH):
            conv = conv + ext[i + k] * cw[k:k + 1, :]
        y_a.append(a_b[i] * conv)
    for i in range(CONV_WIDTH - 1):
        ncv_ref[i] = ext[len(ext) - (CONV_WIDTH - 1) + i]

    v_n = _layer_norm(g_v, sg_ref[...], sb_ref[...])
    v_ref[...] = v_n
    y_b = []
    for i in range(t_dim):
        s = bst_ref[i:i + 1, :] + wst_ref[i, 0:1, :] * v_n[0]
        for j in range(1, i + 1):
            s = s + wst_ref[i, j:j + 1, :] * v_n[j]
        y_b.append(g_u[i] * s)

    pext = [pst_ref[i] for i in range(n_valid)] + [p_x[i] for i in range(t_dim)]
    lane_group = lax.broadcasted_iota(jnp.int32, (b_dim, w_pool), 1) // (w_pool // len(POOL_WINDOWS))
    d = []
    for i in range(t_dim):
        last = n_valid + i
        sums, counts = [], []
        run = None
        k = 0
        for w in POOL_WINDOWS:
            while k < w:
                piece = pext[last - k]
                run = piece if run is None else run + piece
                k += 1
            sums.append(run)
            counts.append(float(min(w, i + 1 + n_valid)))
        d.append(_pool_select(sums, counts, lane_group) - p_x[i])
    d = jnp.stack(d).reshape(t_dim * b_dim, w_pool).astype(BF16)
    y_c = (_dot(d, pw_ref[...]) * ps_ref[...]).reshape(t_dim, b_dim, w_pool)
    for i in range(n_valid):
        npl_ref[i] = pext[len(pext) - n_valid + i]

    mix = jnp.concatenate([jnp.stack(y_a), jnp.stack(y_b), y_c], axis=2)
    mix = mix.reshape(t_dim * b_dim, c_dim).astype(BF16)
    y = alpha * x + gate * _dot(mix, wout_ref[...]).reshape(t_dim, b_dim, c_dim)
    o_ref[...] = _layer_norm(y, g_ref[...], b_ref[...])


def _mix_sample_call(x, mod, cst, pst, w_in, conv_w, sgu_g, sgu_b, wst, bst, pool_wbd, pool_scale,
                     w_out, ln_g, ln_b, *, alpha):
    t_dim, b_dim, c_dim = x.shape
    w_conv = conv_w.shape[1]
    w_gmlp = sgu_g.shape[1]
    w_pool = pool_scale.shape[1]
    kern = functools.partial(_mix_sample_kernel, alpha=alpha, w_conv=w_conv, w_gmlp=w_gmlp)
    return pl.pallas_call(
        kern,
        grid=(1,),
        in_specs=[
            pl.BlockSpec((t_dim, b_dim, c_dim), lambda i: (0, 0, 0)),
            pl.BlockSpec((b_dim, 3 * c_dim), lambda i: (0, 0)),
            _resident(cst.shape),
            _resident(pst.shape),
            _resident(w_in.shape),
            _resident(conv_w.shape),
            _resident(sgu_g.shape),
            _resident(sgu_b.shape),
            _resident(wst.shape),
            _resident(bst.shape),
            _resident(pool_wbd.shape),
            _resident(pool_scale.shape),
            _resident(w_out.shape),
            _resident(ln_g.shape),
            _resident(ln_b.shape),
        ],
        out_specs=[
            pl.BlockSpec((t_dim, b_dim, c_dim), lambda i: (0, 0, 0)),
            pl.BlockSpec(cst.shape, lambda i: (0, 0, 0)),
            pl.BlockSpec(pst.shape, lambda i: (0, 0, 0)),
            pl.BlockSpec((t_dim, b_dim, w_gmlp), lambda i: (0, 0, 0)),
        ],
        out_shape=[
            jax.ShapeDtypeStruct((t_dim, b_dim, c_dim), F32),
            jax.ShapeDtypeStruct(cst.shape, F32),
            jax.ShapeDtypeStruct(pst.shape, F32),
            jax.ShapeDtypeStruct((t_dim, b_dim, w_gmlp), F32),
        ],
        compiler_params=pltpu.CompilerParams(
            dimension_semantics=("arbitrary",),
            vmem_limit_bytes=VMEM_LIMIT_BYTES),
        name="mix_sample",
    )(x, mod, cst, pst, w_in, conv_w, sgu_g, sgu_b, wst, bst, pool_wbd, pool_scale, w_out, ln_g, ln_b)


def kernel(x_prompt, x_sample, c_prompt, c_sample, state_conv, state_pool, w_ada, b_ada, w_in,
           conv_w, sgu_g, sgu_b, w_s, b_s, pool_w, pool_scale, w_out, ln1_g, ln1_b, w_up,
           w_down, ln2_g, ln2_b):
    depth, d_model, _ = w_in.shape
    batch, seq, _ = x_prompt.shape
    dec_batch, dec_seq, _ = x_sample.shape
    d_ff = w_up.shape[2]
    n_heads = w_s.shape[1]
    n_groups, group_dim, _ = pool_w.shape[1:]
    w_pool = n_groups * group_dim
    alpha = (2.0 * depth) ** 0.25
    n_chunks = d_ff // FFN_CHUNK

    mod = _ada_call(jnp.concatenate([c_prompt, c_sample], axis=0), w_ada, b_ada)
    mod_p = mod[:, :batch].reshape(depth, batch, 1, -1)
    mod_s = mod[:, batch:]

    w_in_b = w_in.astype(BF16)
    w_out_b = w_out.astype(BF16)
    wup_b = w_up.astype(BF16).reshape(depth, d_model, n_chunks, FFN_CHUNK).transpose(0, 2, 1, 3)
    wdn_b = w_down.astype(BF16).reshape(depth, n_chunks, FFN_CHUNK, d_model)
    ws_cat = (w_s.astype(BF16).reshape(depth, n_heads // 2, 2, CHUNK, CHUNK)
              .transpose(0, 1, 3, 2, 4).reshape(depth, n_heads // 2, CHUNK, 2 * CHUNK))
    bs_full = jnp.repeat(jnp.swapaxes(b_s, 1, 2), HEAD_DIM, axis=2)
    ws_small = jnp.repeat(jnp.transpose(w_s[:, :, :dec_seq, :dec_seq], (0, 2, 3, 1)), HEAD_DIM, axis=3)
    bs_small = bs_full[:, :dec_seq]
    eye = jnp.eye(n_groups, dtype=F32)
    pool_wbd = (eye[None, :, None, :, None] * pool_w[:, :, :, None, :]).reshape(
        depth, w_pool, w_pool).astype(BF16)

    row2 = lambda a: a.reshape(depth, 1, -1)
    sgu_g2, sgu_b2, ps2 = row2(sgu_g), row2(sgu_b), row2(pool_scale)
    ln1_g2, ln1_b2, ln2_g2, ln2_b2 = row2(ln1_g), row2(ln1_b), row2(ln2_g), row2(ln2_b)

    xs = jnp.swapaxes(x_sample, 0, 1)
    cst = jnp.swapaxes(state_conv, 1, 2)
    pst = jnp.swapaxes(state_pool, 1, 2)

    xp = x_prompt
    conv_p, pool_p, conv_s, pool_s, v_s = [], [], [], [], []
    for l in range(depth):
        xp, ncv, npl = _mix_prompt_call(
            xp, mod_p[l], w_in_b[l], conv_w[l], sgu_g2[l], sgu_b2[l], ws_cat[l], bs_full[l],
            pool_wbd[l], ps2[l], w_out_b[l], ln1_g2[l], ln1_b2[l], alpha=alpha, rows=PROMPT_ROWS)
        conv_p.append(ncv)
        pool_p.append(npl)
        xp = _ffn_call(
            xp, mod_p[l], wup_b[l], wdn_b[l], ln2_g2[l], ln2_b2[l],
            x_block=(1, PROMPT_ROWS, d_model), x_index=lambda b, t: (b, t, 0),
            mod_block=(None, 1, 3 * d_model), mod_index=lambda b, t: (b, 0, 1),
            grid=(batch, seq // PROMPT_ROWS), alpha=alpha, name="ffn_prompt")

        xs, ncv, npl, v_rows = _mix_sample_call(
            xs, mod_s[l], cst[l], pst[l], w_in_b[l], conv_w[l], sgu_g2[l], sgu_b2[l],
            ws_small[l], bs_small[l], pool_wbd[l], ps2[l], w_out_b[l], ln1_g2[l], ln1_b2[l],
            alpha=alpha)
        conv_s.append(ncv)
        pool_s.append(npl)
        v_s.append(v_rows)
        xs = _ffn_call(
            xs, mod_s[l], wup_b[l], wdn_b[l], ln2_g2[l], ln2_b2[l],
            x_block=(dec_seq, dec_batch, d_model), x_index=lambda i: (0, 0, 0),
            mod_block=(dec_batch, 3 * d_model), mod_index=lambda i: (0, 1),
            grid=(1,), alpha=alpha, name="ffn_sample")

    to_batch_major = lambda a: jnp.swapaxes(jnp.stack(a), 1, 2)
    return (xp, jnp.swapaxes(xs, 0, 1), jnp.stack(conv_p), jnp.stack(pool_p),
            to_batch_major(conv_s), to_batch_major(pool_s), to_batch_major(v_s))
```

```python
import functools

import jax
import jax.numpy as jnp
from jax import lax
from jax.experimental import pallas as pl
from jax.experimental.pallas import tpu as pltpu

F32 = jnp.float32
BF16 = jnp.bfloat16

LN_EPS = 1e-5
HEAD_DIM = 64
CHUNK = 128
POOL_WINDOWS = (2, 4, 8, 16)
CONV_WIDTH = 3
LANES = 128
SUBLANES = 8
VMEM_LIMIT_BYTES = 56 * 1024 * 1024

PROMPT_ROWS = 1024
SUB_ROWS = 256
WEIGHT_COLS = 512
CONV_HIST = SUBLANES
POOL_HIST = 2 * SUBLANES


def _layer_norm(y, g, b):
    mu = jnp.mean(y, axis=-1, keepdims=True)
    d = y - mu
    var = jnp.mean(d * d, axis=-1, keepdims=True)
    return d * lax.rsqrt(var + LN_EPS) * g + b


def _dot(a, b):
    return jnp.dot(a, b, preferred_element_type=F32)


def _layer_block(shape, layer):
    zeros = (0,) * len(shape)
    return pl.BlockSpec((None,) + tuple(shape), lambda *_: (layer,) + zeros,
                        pipeline_mode=pl.Buffered(1))


def _whole(shape):
    zeros = (0,) * len(shape)
    return pl.BlockSpec(tuple(shape), lambda *_: zeros, pipeline_mode=pl.Buffered(1))


def _column_blocks(w, layer, width):
    rows, cols = w.shape[1:]
    return [pl.BlockSpec((None, rows, width), lambda *_, n=n: (layer, 0, n),
                         pipeline_mode=pl.Buffered(1)) for n in range(cols // width)]


def _lane_group(shape, n_groups):
    return lax.broadcasted_iota(jnp.int32, shape, len(shape) - 1) // (shape[-1] // n_groups)


def _select_by_group(values, lane_group):
    out = values[-1]
    for g in range(len(values) - 2, -1, -1):
        out = jnp.where(lane_group == g, values[g], out)
    return out


def _ada_kernel(cp_ref, cs_ref, w_ref, b_ref, op_ref, os_ref):
    w = w_ref[0].astype(BF16)
    bias = b_ref[pl.ds(pl.program_id(0), 1), :]
    op_ref[0] = _dot(jax.nn.silu(cp_ref[...]).astype(BF16), w) + bias
    os_ref[0] = _dot(jax.nn.silu(cs_ref[...]).astype(BF16), w) + bias


def _ada_call(c_prompt, c_sample, w_ada, b_ada, col_tile=1536):
    depth, d_model, n_mod = w_ada.shape
    bp, bs = c_prompt.shape[0], c_sample.shape[0]
    return pl.pallas_call(
        _ada_kernel,
        grid=(depth, n_mod // col_tile),
        in_specs=[
            pl.BlockSpec((bp, d_model), lambda l, j: (0, 0)),
            pl.BlockSpec((bs, d_model), lambda l, j: (0, 0)),
            pl.BlockSpec((1, d_model, col_tile), lambda l, j: (l, 0, j)),
            pl.BlockSpec((depth, col_tile), lambda l, j: (0, j)),
        ],
        out_specs=[
            pl.BlockSpec((1, bp, col_tile), lambda l, j: (l, 0, j)),
            pl.BlockSpec((1, bs, col_tile), lambda l, j: (l, 0, j)),
        ],
        out_shape=[
            jax.ShapeDtypeStruct((depth, bp, n_mod), F32),
            jax.ShapeDtypeStruct((depth, bs, n_mod), F32),
        ],
        compiler_params=pltpu.CompilerParams(
            dimension_semantics=("arbitrary", "arbitrary"),
            vmem_limit_bytes=VMEM_LIMIT_BYTES),
        name="ada_mod",
    )(c_prompt, c_sample, w_ada, b_ada)


def _ffn_kernel(x_ref, mod_ref, *refs, alpha, layer, row_axis, mod_row_axis, n_up, n_dn):
    wup_refs, wdn_refs = refs[:n_up], refs[n_up:n_up + n_dn]
    g_ref, b_ref, o_ref = refs[n_up + n_dn:]
    g_dim, r_dim, c_dim = x_ref.shape
    chunk = wup_refs[0].shape[1]
    if mod_row_axis is None:
        m = mod_ref[...]
    else:
        m = mod_ref[pl.ds(pl.program_id(mod_row_axis), 1), :]
    sh, sc1, gate = m[:, :c_dim], 1.0 + m[:, c_dim:2 * c_dim], m[:, 2 * c_dim:]
    ln_g, ln_b = g_ref[layer:layer + 1, :], b_ref[layer:layer + 1, :]
    step = SUB_ROWS if row_axis == 1 else SUB_ROWS // r_dim
    n_sub = (r_dim if row_axis == 1 else g_dim) // step
    for j in range(n_sub):
        idx = (slice(None), slice(j * step, (j + 1) * step)) if row_axis == 1 else (
            slice(j * step, (j + 1) * step),)
        x = x_ref[idx]
        rows = x.shape[0] * x.shape[1]
        h = (x * sc1 + sh).reshape(rows, c_dim).astype(BF16)
        acc = [None] * n_dn
        u_next = _dot(h, wup_refs[0][...])
        for k in range(n_up):
            u = u_next
            if k + 1 < n_up:
                u_next = _dot(h, wup_refs[k + 1][...])
            a = jnp.square(jnp.maximum(u, 0.0)).astype(BF16)
            for n in range(n_dn):
                part = _dot(a, wdn_refs[n][k * chunk:(k + 1) * chunk, :])
                acc[n] = part if acc[n] is None else acc[n] + part
        f = jnp.concatenate(acc, axis=1).reshape(x.shape)
        o_ref[idx] = _layer_norm(alpha * x + gate * f, ln_g, ln_b)


def _ffn_call(x, mod, wup, wdn, ln_g, ln_b, *, layer, x_block, x_index, mod_row_axis,
              row_axis, grid, alpha, name):
    n_mod = mod.shape[-1]
    up_specs = _column_blocks(wup, layer, WEIGHT_COLS)
    dn_specs = _column_blocks(wdn, layer, WEIGHT_COLS)
    kern = functools.partial(_ffn_kernel, alpha=alpha, layer=layer, row_axis=row_axis,
                             mod_row_axis=mod_row_axis, n_up=len(up_specs), n_dn=len(dn_specs))
    return pl.pallas_call(
        kern,
        grid=grid,
        in_specs=[
            pl.BlockSpec(x_block, x_index),
            pl.BlockSpec((None, mod.shape[1], n_mod // 2), lambda *_: (layer, 0, 1)),
            *up_specs, *dn_specs,
            _whole(ln_g.shape),
            _whole(ln_b.shape),
        ],
        out_specs=pl.BlockSpec(x_block, x_index),
        out_shape=jax.ShapeDtypeStruct(x.shape, F32),
        compiler_params=pltpu.CompilerParams(
            dimension_semantics=("arbitrary",) * len(grid),
            vmem_limit_bytes=VMEM_LIMIT_BYTES),
        name=name,
    )(x, mod, *([wup] * len(up_specs)), *([wdn] * len(dn_specs)), ln_g, ln_b)


def _mix_prompt_kernel(x_ref, mod_ref, *refs, alpha, layer, n_in, n_out):
    win_refs, refs = refs[:n_in], refs[n_in:]
    cw_ref, sg_ref, sb_ref, ws_ref, bs_ref, pw_ref, ps_ref = refs[:7]
    wout_refs, refs = refs[7:7 + n_out], refs[7 + n_out:]
    g_ref, b_ref, o_ref, ncv_ref, npl_ref, zc_ref, pc_ref = refs
    bi, t = pl.program_id(0), pl.program_id(1)
    rows, c_dim = x_ref.shape[1], x_ref.shape[2]
    w_conv, w_gmlp, w_pool = cw_ref.shape[2], sg_ref.shape[1], ps_ref.shape[1]
    n_heads = ws_ref.shape[0]

    @pl.when(t == 0)
    def _():
        zc_ref[...] = jnp.zeros(zc_ref.shape, F32)
        pc_ref[...] = jnp.zeros(pc_ref.shape, F32)

    m = mod_ref[pl.ds(bi, 1), :]
    sh, sc1, gate = m[:, :c_dim], 1.0 + m[:, c_dim:2 * c_dim], m[:, 2 * c_dim:]
    cw = cw_ref[layer]
    sgu_g, sgu_b = sg_ref[layer:layer + 1, :], sb_ref[layer:layer + 1, :]
    pool_scale = ps_ref[layer:layer + 1, :]
    ln_g, ln_b = g_ref[layer:layer + 1, :], b_ref[layer:layer + 1, :]

    row_i = lax.broadcasted_iota(jnp.int32, (CHUNK, 2 * CHUNK), 0)
    col_j = lax.broadcasted_iota(jnp.int32, (CHUNK, 2 * CHUNK), 1) % CHUNK
    causal = col_j <= row_i
    ws_m = []
    for p in range(n_heads // 2):
        pair = jnp.concatenate([ws_ref[2 * p], ws_ref[2 * p + 1]], axis=1)
        ws_m.append(jnp.where(causal, pair, 0.0).astype(BF16))
    low_half = lax.broadcasted_iota(jnp.int32, (CHUNK, LANES), 1) < HEAD_DIM
    lane_group = _lane_group((SUB_ROWS, w_pool), len(POOL_WINDOWS))
    window = _select_by_group([jnp.full((SUB_ROWS, w_pool), w, jnp.int32) for w in POOL_WINDOWS],
                              lane_group)
    row_id = lax.broadcasted_iota(jnp.int32, (SUB_ROWS, w_pool), 0)
    bias = bs_ref[...]

    z_hist = zc_ref[...]
    p_hist = pc_ref[...]

    def project(jj):
        h = (x_ref[0, jj * SUB_ROWS:(jj + 1) * SUB_ROWS, :] * sc1 + sh).astype(BF16)
        return jnp.concatenate([_dot(h, w[...]) for w in win_refs], axis=1)

    n_sub = rows // SUB_ROWS
    proj_next = project(0)
    for j in range(n_sub):
        r0 = j * SUB_ROWS
        x = x_ref[0, r0:r0 + SUB_ROWS, :]
        proj = proj_next
        if j + 1 < n_sub:
            proj_next = project(j + 1)
        o0 = 0
        a_b = proj[:, o0:o0 + w_conv]; o0 += w_conv
        a_c = proj[:, o0:o0 + w_conv]; o0 += w_conv
        a_x = proj[:, o0:o0 + w_conv]; o0 += w_conv
        g_u = proj[:, o0:o0 + w_gmlp]; o0 += w_gmlp
        g_v = proj[:, o0:o0 + w_gmlp]; o0 += w_gmlp
        p_x = proj[:, o0:o0 + w_pool]

        z = a_c * a_x
        z_ext = jnp.concatenate([z_hist, z], axis=0)
        conv = z_ext * cw[CONV_WIDTH - 1:CONV_WIDTH, :]
        for k in range(1, CONV_WIDTH):
            conv = conv + pltpu.roll(z_ext, k, 0) * cw[CONV_WIDTH - 1 - k:CONV_WIDTH - k, :]
        y_a = a_b * conv[CONV_HIST:, :]
        z_hist = z[SUB_ROWS - CONV_HIST:, :]

        v_n = _layer_norm(g_v, sgu_g, sgu_b)
        n_ch = SUB_ROWS // CHUNK
        s_cols = []
        for p in range(n_heads // 2):
            cols = [v_n[c * CHUNK:(c + 1) * CHUNK, p * LANES:(p + 1) * LANES] for c in range(n_ch)]
            rhs = jnp.concatenate(
                [jnp.concatenate([jnp.where(low_half, col, 0.0) for col in cols], axis=1),
                 jnp.concatenate([jnp.where(low_half, 0.0, col) for col in cols], axis=1)],
                axis=0).astype(BF16)
            s_cols.append(_dot(ws_m[p], rhs))
        s = jnp.concatenate(
            [jnp.concatenate([sc[:, c * LANES:(c + 1) * LANES] for sc in s_cols], axis=1) + bias
             for c in range(n_ch)], axis=0)
        y_b = g_u * s

        run = jnp.concatenate([p_hist, p_x], axis=0)
        sums = []
        for w in POOL_WINDOWS:
            run = run + pltpu.roll(run, w // 2, 0)
            sums.append(run[POOL_HIST:, :])
        count = jnp.minimum(window, t * rows + r0 + row_id + 1).astype(F32)
        d = (_select_by_group(sums, lane_group) / count - p_x).astype(BF16)
        y_c = _dot(d, pw_ref[...]) * pool_scale
        p_hist = p_x[SUB_ROWS - POOL_HIST:, :]

        mix = jnp.concatenate([y_a, y_b, y_c], axis=1).astype(BF16)
        out = jnp.concatenate([_dot(mix, w[...]) for w in wout_refs], axis=1)
        o_ref[0, r0:r0 + SUB_ROWS, :] = _layer_norm(alpha * x + gate * out, ln_g, ln_b)

    zc_ref[...] = z_hist
    pc_ref[...] = p_hist
    ncv_ref[0] = z_hist[CONV_HIST - (CONV_WIDTH - 1):, :]
    npl_ref[0] = p_hist[POOL_HIST - (max(POOL_WINDOWS) - 1):, :]


def _mix_prompt_call(x, mod, w_in, conv_w, sgu_g, sgu_b, w_s, bs_full, pool_wbd, pool_scale,
                     w_out, ln_g, ln_b, *, alpha, layer, rows):
    batch, seq, c_dim = x.shape
    w_conv, w_pool = conv_w.shape[2], pool_scale.shape[1]
    conv_buf = CONV_WIDTH - 1
    pool_buf = max(POOL_WINDOWS) - 1
    in_specs = _column_blocks(w_in, layer, WEIGHT_COLS)
    out_specs = _column_blocks(w_out, layer, WEIGHT_COLS)
    kern = functools.partial(_mix_prompt_kernel, alpha=alpha, layer=layer, n_in=len(in_specs),
                             n_out=len(out_specs))
    return pl.pallas_call(
        kern,
        grid=(batch, seq // rows),
        in_specs=[
            pl.BlockSpec((1, rows, c_dim), lambda b, t: (b, t, 0)),
            pl.BlockSpec((None, batch, mod.shape[2] // 2), lambda b, t: (layer, 0, 0)),
            *in_specs,
            _whole(conv_w.shape),
            _whole(sgu_g.shape),
            _whole(sgu_b.shape),
            _layer_block(w_s.shape[1:], layer),
            _layer_block(bs_full.shape[1:], layer),
            _layer_block(pool_wbd.shape[1:], layer),
            _whole(pool_scale.shape),
            *out_specs,
            _whole(ln_g.shape),
            _whole(ln_b.shape),
        ],
        out_specs=[
            pl.BlockSpec((1, rows, c_dim), lambda b, t: (b, t, 0)),
            pl.BlockSpec((1, conv_buf, w_conv), lambda b, t: (b, 0, 0)),
            pl.BlockSpec((1, pool_buf, w_pool), lambda b, t: (b, 0, 0)),
        ],
        out_shape=[
            jax.ShapeDtypeStruct((batch, seq, c_dim), F32),
            jax.ShapeDtypeStruct((batch, conv_buf, w_conv), F32),
            jax.ShapeDtypeStruct((batch, pool_buf, w_pool), F32),
        ],
        scratch_shapes=[
            pltpu.VMEM((CONV_HIST, w_conv), F32),
            pltpu.VMEM((POOL_HIST, w_pool), F32),
        ],
        compiler_params=pltpu.CompilerParams(
            dimension_semantics=("arbitrary", "arbitrary"),
            vmem_limit_bytes=VMEM_LIMIT_BYTES),
        name="mix_prompt",
    )(x, mod, *([w_in] * len(in_specs)), conv_w, sgu_g, sgu_b, w_s, bs_full, pool_wbd, pool_scale,
      *([w_out] * len(out_specs)), ln_g, ln_b)


def _mix_sample_kernel(x_ref, mod_ref, cst_ref, pst_ref, win_ref, cw_ref, sg_ref, sb_ref,
                       wst_ref, bst_ref, pw_ref, ps_ref, wout_ref, g_ref, b_ref,
                       o_ref, ncv_ref, npl_ref, v_ref, *, alpha, layer):
    x = x_ref[...]
    t_dim, b_dim, c_dim = x.shape
    w_conv, w_gmlp, w_pool = cw_ref.shape[2], sg_ref.shape[1], ps_ref.shape[1]
    n_valid = pst_ref.shape[0]
    m = mod_ref[...]
    sh, sc1, gate = m[:, :c_dim], 1.0 + m[:, c_dim:2 * c_dim], m[:, 2 * c_dim:]
    h = (x * sc1 + sh).reshape(t_dim * b_dim, c_dim).astype(BF16)
    proj = _dot(h, win_ref[...]).reshape(t_dim, b_dim, -1)
    o0 = 0
    a_b = proj[:, :, o0:o0 + w_conv]; o0 += w_conv
    a_c = proj[:, :, o0:o0 + w_conv]; o0 += w_conv
    a_x = proj[:, :, o0:o0 + w_conv]; o0 += w_conv
    g_u = proj[:, :, o0:o0 + w_gmlp]; o0 += w_gmlp
    g_v = proj[:, :, o0:o0 + w_gmlp]; o0 += w_gmlp
    p_x = proj[:, :, o0:o0 + w_pool]

    z = a_c * a_x
    ext = [cst_ref[i] for i in range(CONV_WIDTH - 1)] + [z[i] for i in range(t_dim)]
    cw = cw_ref[layer]
    y_a = []
    for i in range(t_dim):
        conv = ext[i] * cw[0:1, :]
        for k in range(1, CONV_WIDTH):
            conv = conv + ext[i + k] * cw[k:k + 1, :]
        y_a.append(a_b[i] * conv)
    for i in range(CONV_WIDTH - 1):
        ncv_ref[i] = ext[len(ext) - (CONV_WIDTH - 1) + i]

    v_n = _layer_norm(g_v, sg_ref[layer:layer + 1, :], sb_ref[layer:layer + 1, :])
    v_ref[...] = v_n
    y_b = []
    for i in range(t_dim):
        s = bst_ref[i:i + 1, :] + wst_ref[i, 0:1, :] * v_n[0]
        for j in range(1, i + 1):
            s = s + wst_ref[i, j:j + 1, :] * v_n[j]
        y_b.append(g_u[i] * s)

    pext = [pst_ref[i] for i in range(n_valid)] + [p_x[i] for i in range(t_dim)]
    lane_group = _lane_group((b_dim, w_pool), len(POOL_WINDOWS))
    d = []
    for i in range(t_dim):
        last = n_valid + i
        means = []
        run = None
        k = 0
        for w in POOL_WINDOWS:
            while k < w:
                run = pext[last - k] if run is None else run + pext[last - k]
                k += 1
            means.append(run / float(min(w, i + 1 + n_valid)))
        d.append(_select_by_group(means, lane_group) - p_x[i])
    d = jnp.stack(d).reshape(t_dim * b_dim, w_pool).astype(BF16)
    y_c = (_dot(d, pw_ref[...]) * ps_ref[layer:layer + 1, :]).reshape(t_dim, b_dim, w_pool)
    for i in range(n_valid):
        npl_ref[i] = pext[len(pext) - n_valid + i]

    mix = jnp.concatenate([jnp.stack(y_a), jnp.stack(y_b), y_c], axis=2)
    mix = mix.reshape(t_dim * b_dim, c_dim).astype(BF16)
    y = alpha * x + gate * _dot(mix, wout_ref[...]).reshape(t_dim, b_dim, c_dim)
    o_ref[...] = _layer_norm(y, g_ref[layer:layer + 1, :], b_ref[layer:layer + 1, :])


def _mix_sample_call(x, mod, cst, pst, w_in, conv_w, sgu_g, sgu_b, wst, bs_full, pool_wbd,
                     pool_scale, w_out, ln_g, ln_b, *, alpha, layer):
    t_dim, b_dim, c_dim = x.shape
    w_gmlp = sgu_g.shape[1]
    kern = functools.partial(_mix_sample_kernel, alpha=alpha, layer=layer)
    whole3 = lambda i: (0, 0, 0)
    return pl.pallas_call(
        kern,
        grid=(1,),
        in_specs=[
            pl.BlockSpec((t_dim, b_dim, c_dim), whole3),
            pl.BlockSpec((None, b_dim, mod.shape[2] // 2), lambda i: (layer, 0, 0)),
            _layer_block(cst.shape[1:], layer),
            _layer_block(pst.shape[1:], layer),
            _layer_block(w_in.shape[1:], layer),
            _whole(conv_w.shape),
            _whole(sgu_g.shape),
            _whole(sgu_b.shape),
            _layer_block(wst.shape[1:], layer),
            pl.BlockSpec((None, t_dim, w_gmlp), lambda i: (layer, 0, 0)),
            _layer_block(pool_wbd.shape[1:], layer),
            _whole(pool_scale.shape),
            _layer_block(w_out.shape[1:], layer),
            _whole(ln_g.shape),
            _whole(ln_b.shape),
        ],
        out_specs=[
            pl.BlockSpec((t_dim, b_dim, c_dim), whole3),
            pl.BlockSpec(cst.shape[1:], whole3),
            pl.BlockSpec(pst.shape[1:], whole3),
            pl.BlockSpec((t_dim, b_dim, w_gmlp), whole3),
        ],
        out_shape=[
            jax.ShapeDtypeStruct((t_dim, b_dim, c_dim), F32),
            jax.ShapeDtypeStruct(cst.shape[1:], F32),
            jax.ShapeDtypeStruct(pst.shape[1:], F32),
            jax.ShapeDtypeStruct((t_dim, b_dim, w_gmlp), F32),
        ],
        compiler_params=pltpu.CompilerParams(
            dimension_semantics=("arbitrary",),
            vmem_limit_bytes=VMEM_LIMIT_BYTES),
        name="mix_sample",
    )(x, mod, cst, pst, w_in, conv_w, sgu_g, sgu_b, wst, bs_full, pool_wbd, pool_scale, w_out,
      ln_g, ln_b)


def kernel(x_prompt, x_sample, c_prompt, c_sample, state_conv, state_pool, w_ada, b_ada, w_in,
           conv_w, sgu_g, sgu_b, w_s, b_s, pool_w, pool_scale, w_out, ln1_g, ln1_b, w_up,
           w_down, ln2_g, ln2_b):
    depth, d_model, _ = w_in.shape
    batch, seq, _ = x_prompt.shape
    dec_batch, dec_seq, _ = x_sample.shape
    n_groups, group_dim, _ = pool_w.shape[1:]
    w_pool = n_groups * group_dim
    alpha = (2.0 * depth) ** 0.25
    assert seq % PROMPT_ROWS == 0 and PROMPT_ROWS % SUB_ROWS == 0 and SUB_ROWS % CHUNK == 0
    assert (dec_seq * dec_batch) % SUB_ROWS == 0 and SUB_ROWS % dec_batch == 0

    mod_p, mod_s = _ada_call(c_prompt, c_sample, w_ada, b_ada)

    w_in_b, w_out_b = w_in.astype(BF16), w_out.astype(BF16)
    wup_b, wdn_b = w_up.astype(BF16), w_down.astype(BF16)
    bs_full = jnp.repeat(jnp.swapaxes(b_s, 1, 2), HEAD_DIM, axis=2)
    ws_small = jnp.repeat(jnp.transpose(w_s[:, :, :dec_seq, :dec_seq], (0, 2, 3, 1)), HEAD_DIM, axis=3)
    eye = jnp.eye(n_groups, dtype=F32)
    pool_wbd = (eye[None, :, None, :, None] * pool_w[:, :, :, None, :]).reshape(
        depth, w_pool, w_pool).astype(BF16)

    xs = jnp.swapaxes(x_sample, 0, 1)
    cst = jnp.swapaxes(state_conv, 1, 2)
    pst = jnp.swapaxes(state_pool, 1, 2)

    xp = x_prompt
    conv_p, pool_p, conv_s, pool_s, v_s = [], [], [], [], []
    for l in range(depth):
        xp, ncv, npl = _mix_prompt_call(
            xp, mod_p, w_in_b, conv_w, sgu_g, sgu_b, w_s, bs_full, pool_wbd, pool_scale,
            w_out_b, ln1_g, ln1_b, alpha=alpha, layer=l, rows=PROMPT_ROWS)
        conv_p.append(ncv)
        pool_p.append(npl)
        xp = _ffn_call(
            xp, mod_p, wup_b, wdn_b, ln2_g, ln2_b, layer=l,
            x_block=(1, PROMPT_ROWS, d_model), x_index=lambda b, t: (b, t, 0),
            mod_row_axis=0, row_axis=1, grid=(batch, seq // PROMPT_ROWS), alpha=alpha,
            name="ffn_prompt")

        xs, ncv, npl, v_rows = _mix_sample_call(
            xs, mod_s, cst, pst, w_in_b, conv_w, sgu_g, sgu_b, ws_small, bs_full, pool_wbd,
            pool_scale, w_out_b, ln1_g, ln1_b, alpha=alpha, layer=l)
        conv_s.append(ncv)
        pool_s.append(npl)
        v_s.append(v_rows)
        xs = _ffn_call(
            xs, mod_s, wup_b, wdn_b, ln2_g, ln2_b, layer=l,
            x_block=(dec_seq, dec_batch, d_model), x_index=lambda i: (0, 0, 0),
            mod_row_axis=None, row_axis=0, grid=(1,), alpha=alpha, name="ffn_sample")

    to_batch_major = lambda a: jnp.swapaxes(jnp.stack(a), 1, 2)
    return (xp, jnp.swapaxes(xs, 0, 1), jnp.stack(conv_p), jnp.stack(pool_p),
            to_batch_major(conv_s), to_batch_major(pool_s), to_batch_major(v_s))
```

```python
import functools

import jax
import jax.numpy as jnp
from jax import lax
from jax.experimental import pallas as pl
from jax.experimental.pallas import tpu as pltpu

F32 = jnp.float32
BF16 = jnp.bfloat16

LN_EPS = 1e-5
HEAD_DIM = 64
CHUNK = 128
POOL_WINDOWS = (2, 4, 8, 16)
CONV_WIDTH = 3
LANES = 128
SUBLANES = 8
VMEM_LIMIT_BYTES = 56 * 1024 * 1024

PROMPT_ROWS = 1024
SUB_ROWS = 256
WEIGHT_COLS = 512
CONV_HIST = SUBLANES
POOL_HIST = 2 * SUBLANES


def _layer_norm(y, g, b):
    mu = jnp.mean(y, axis=-1, keepdims=True)
    d = y - mu
    var = jnp.mean(d * d, axis=-1, keepdims=True)
    return d * lax.rsqrt(var + LN_EPS) * g + b


def _dot(a, b):
    return jnp.dot(a, b, preferred_element_type=F32)


def _layer_block(shape, layer):
    zeros = (0,) * len(shape)
    return pl.BlockSpec((None,) + tuple(shape), lambda *_: (layer,) + zeros,
                        pipeline_mode=pl.Buffered(1))


def _whole(shape):
    zeros = (0,) * len(shape)
    return pl.BlockSpec(tuple(shape), lambda *_: zeros, pipeline_mode=pl.Buffered(1))


def _column_blocks(w, width):
    rows, cols = w.shape
    return [pl.BlockSpec((rows, width), lambda *_, n=n: (0, n), pipeline_mode=pl.Buffered(1))
            for n in range(cols // width)]


def _lane_group(shape, n_groups):
    return lax.broadcasted_iota(jnp.int32, shape, len(shape) - 1) // (shape[-1] // n_groups)


def _select_by_group(values, lane_group):
    out = values[-1]
    for g in range(len(values) - 2, -1, -1):
        out = jnp.where(lane_group == g, values[g], out)
    return out


def _ada_kernel(cp_ref, cs_ref, w_ref, b_ref, op_ref, os_ref):
    w = w_ref[0].astype(BF16)
    bias = b_ref[pl.ds(pl.program_id(0), 1), :]
    op_ref[0] = _dot(jax.nn.silu(cp_ref[...]).astype(BF16), w) + bias
    os_ref[0] = _dot(jax.nn.silu(cs_ref[...]).astype(BF16), w) + bias


def _ada_call(c_prompt, c_sample, w_ada, b_ada, col_tile=1536):
    depth, d_model, n_mod = w_ada.shape
    bp, bs = c_prompt.shape[0], c_sample.shape[0]
    return pl.pallas_call(
        _ada_kernel,
        grid=(depth, n_mod // col_tile),
        in_specs=[
            pl.BlockSpec((bp, d_model), lambda l, j: (0, 0)),
            pl.BlockSpec((bs, d_model), lambda l, j: (0, 0)),
            pl.BlockSpec((1, d_model, col_tile), lambda l, j: (l, 0, j)),
            pl.BlockSpec((depth, col_tile), lambda l, j: (0, j)),
        ],
        out_specs=[
            pl.BlockSpec((1, bp, col_tile), lambda l, j: (l, 0, j)),
            pl.BlockSpec((1, bs, col_tile), lambda l, j: (l, 0, j)),
        ],
        out_shape=[
            jax.ShapeDtypeStruct((depth, bp, n_mod), F32),
            jax.ShapeDtypeStruct((depth, bs, n_mod), F32),
        ],
        compiler_params=pltpu.CompilerParams(
            dimension_semantics=("arbitrary", "arbitrary"),
            vmem_limit_bytes=VMEM_LIMIT_BYTES),
        name="ada_mod",
    )(c_prompt, c_sample, w_ada, b_ada)


def _ffn_rows(x, sh, sc1, gate, wup, wdn, n_up, n_dn, ln_g, ln_b, alpha):
    c_dim = x.shape[-1]
    h = (x * sc1 + sh).reshape(-1, c_dim).astype(BF16)
    acc = [None] * n_dn
    u_next = _dot(h, wup(0))
    for k in range(n_up):
        u = u_next
        if k + 1 < n_up:
            u_next = _dot(h, wup(k + 1))
        a = jnp.square(jnp.maximum(u, 0.0)).astype(BF16)
        for n in range(n_dn):
            part = _dot(a, wdn(n, k))
            acc[n] = part if acc[n] is None else acc[n] + part
    f = jnp.concatenate(acc, axis=1).reshape(x.shape)
    return _layer_norm(alpha * x + gate * f, ln_g, ln_b)


def _ffn_prompt_kernel(x_ref, mod_ref, wup_f_ref, wdn_f_ref, g_ref, b_ref,
                       o_ref, wup_o_ref, wdn_o_ref, wup_s, wdn_s, *, alpha, layer, n_t):
    i = pl.program_id(0)
    n_up, n_dn = wup_s.shape[0], wdn_s.shape[0]
    cols = wdn_s.shape[3]

    @pl.when(i < n_up)
    def _():
        wu = wup_f_ref[...].astype(BF16)
        wup_s[i] = wu
        wup_o_ref[...] = wu
        wd = wdn_f_ref[...].astype(BF16)
        wdn_o_ref[...] = wd
        for n in range(n_dn):
            wdn_s[n, i] = wd[:, n * cols:(n + 1) * cols]

    @pl.when(i >= n_up)
    def _():
        rows, c_dim = x_ref.shape[1], x_ref.shape[2]
        m = mod_ref[pl.ds((i - n_up) // n_t, 1), :]
        sh, sc1, gate = m[:, :c_dim], 1.0 + m[:, c_dim:2 * c_dim], m[:, 2 * c_dim:]
        ln_g, ln_b = g_ref[layer:layer + 1, :], b_ref[layer:layer + 1, :]
        for j in range(rows // SUB_ROWS):
            x = x_ref[0, j * SUB_ROWS:(j + 1) * SUB_ROWS, :]
            o_ref[0, j * SUB_ROWS:(j + 1) * SUB_ROWS, :] = _ffn_rows(
                x, sh, sc1, gate, lambda k: wup_s[k], lambda n, k: wdn_s[n, k], n_up, n_dn,
                ln_g, ln_b, alpha)


def _ffn_prompt_call(x, mod, w_up, w_down, ln_g, ln_b, *, layer, alpha, rows):
    batch, seq, c_dim = x.shape
    d_ff = w_up.shape[2]
    n_t = seq // rows
    n_up, n_dn = d_ff // WEIGHT_COLS, c_dim // WEIGHT_COLS
    last = n_up - 1

    def tile(i):
        tl = jnp.maximum(i - n_up, 0)
        return (tl // n_t, tl % n_t, 0)

    kern = functools.partial(_ffn_prompt_kernel, alpha=alpha, layer=layer, n_t=n_t)
    return pl.pallas_call(
        kern,
        grid=(n_up + batch * n_t,),
        in_specs=[
            pl.BlockSpec((1, rows, c_dim), tile),
            pl.BlockSpec((None, batch, mod.shape[2] // 2), lambda i: (layer, 0, 1)),
            pl.BlockSpec((None, c_dim, WEIGHT_COLS), lambda i: (layer, 0, jnp.minimum(i, last))),
            pl.BlockSpec((None, WEIGHT_COLS, c_dim), lambda i: (layer, jnp.minimum(i, last), 0)),
            _whole(ln_g.shape),
            _whole(ln_b.shape),
        ],
        out_specs=[
            pl.BlockSpec((1, rows, c_dim), tile),
            pl.BlockSpec((c_dim, WEIGHT_COLS), lambda i: (0, jnp.minimum(i, last))),
            pl.BlockSpec((WEIGHT_COLS, c_dim), lambda i: (jnp.minimum(i, last), 0)),
        ],
        out_shape=[
            jax.ShapeDtypeStruct(x.shape, F32),
            jax.ShapeDtypeStruct((c_dim, d_ff), BF16),
            jax.ShapeDtypeStruct((d_ff, c_dim), BF16),
        ],
        scratch_shapes=[
            pltpu.VMEM((n_up, c_dim, WEIGHT_COLS), BF16),
            pltpu.VMEM((n_dn, n_up, WEIGHT_COLS, WEIGHT_COLS), BF16),
        ],
        compiler_params=pltpu.CompilerParams(
            dimension_semantics=("arbitrary",),
            vmem_limit_bytes=VMEM_LIMIT_BYTES),
        name="ffn_prompt",
    )(x, mod, w_up, w_down, ln_g, ln_b)


def _ffn_sample_kernel(x_ref, mod_ref, *refs, alpha, layer, n_up, n_dn):
    wup_refs, wdn_refs = refs[:n_up], refs[n_up:n_up + n_dn]
    g_ref, b_ref, o_ref = refs[n_up + n_dn:]
    t_dim, b_dim, c_dim = x_ref.shape
    chunk = wup_refs[0].shape[1]
    m = mod_ref[...]
    sh, sc1, gate = m[:, :c_dim], 1.0 + m[:, c_dim:2 * c_dim], m[:, 2 * c_dim:]
    ln_g, ln_b = g_ref[layer:layer + 1, :], b_ref[layer:layer + 1, :]
    step = SUB_ROWS // b_dim
    for j in range(t_dim // step):
        x = x_ref[j * step:(j + 1) * step]
        o_ref[j * step:(j + 1) * step] = _ffn_rows(
            x, sh, sc1, gate, lambda k: wup_refs[k][...],
            lambda n, k: wdn_refs[n][k * chunk:(k + 1) * chunk, :], n_up, n_dn, ln_g, ln_b, alpha)


def _ffn_sample_call(x, mod, wup, wdn, ln_g, ln_b, *, layer, alpha):
    n_mod = mod.shape[-1]
    up_specs = _column_blocks(wup, WEIGHT_COLS)
    dn_specs = _column_blocks(wdn, WEIGHT_COLS)
    kern = functools.partial(_ffn_sample_kernel, alpha=alpha, layer=layer, n_up=len(up_specs),
                             n_dn=len(dn_specs))
    whole3 = lambda i: (0, 0, 0)
    return pl.pallas_call(
        kern,
        grid=(1,),
        in_specs=[
            pl.BlockSpec(x.shape, whole3),
            pl.BlockSpec((None, mod.shape[1], n_mod // 2), lambda i: (layer, 0, 1)),
            *up_specs, *dn_specs,
            _whole(ln_g.shape),
            _whole(ln_b.shape),
        ],
        out_specs=pl.BlockSpec(x.shape, whole3),
        out_shape=jax.ShapeDtypeStruct(x.shape, F32),
        compiler_params=pltpu.CompilerParams(
            dimension_semantics=("arbitrary",),
            vmem_limit_bytes=VMEM_LIMIT_BYTES),
        name="ffn_sample",
    )(x, mod, *([wup] * len(up_specs)), *([wdn] * len(dn_specs)), ln_g, ln_b)


def _mix_prompt_kernel(x_ref, mod_ref, win_f_ref, wout_f_ref, cw_ref, sg_ref, sb_ref, ws_ref,
                       bs_ref, pw_ref, ps_ref, g_ref, b_ref,
                       o_ref, ncv_ref, npl_ref, win_o_ref, wout_o_ref,
                       zc_ref, pc_ref, win_s, wout_s, *, alpha, layer, n_t):
    i = pl.program_id(0)
    n_in, n_out = win_s.shape[0], wout_s.shape[0]
    n_warm = n_in + n_out
    tile = i - n_warm
    bi, t = tile // n_t, tile % n_t

    @pl.when(i < n_in)
    def _():
        w = win_f_ref[...].astype(BF16)
        win_s[i] = w
        win_o_ref[...] = w

    @pl.when(jnp.logical_and(i >= n_in, i < n_warm))
    def _():
        w = wout_f_ref[...].astype(BF16)
        wout_s[i - n_in] = w
        wout_o_ref[...] = w

    @pl.when(jnp.logical_and(i >= n_warm, t == 0))
    def _():
        zc_ref[...] = jnp.zeros(zc_ref.shape, F32)
        pc_ref[...] = jnp.zeros(pc_ref.shape, F32)

    @pl.when(i >= n_warm)
    def _():
        _mix_prompt_tile(x_ref, mod_ref, cw_ref, sg_ref, sb_ref, ws_ref, bs_ref, pw_ref, ps_ref,
                         g_ref, b_ref, o_ref, ncv_ref, npl_ref, zc_ref, pc_ref, win_s, wout_s,
                         alpha=alpha, layer=layer, bi=bi, t=t)


def _mix_prompt_tile(x_ref, mod_ref, cw_ref, sg_ref, sb_ref, ws_ref, bs_ref, pw_ref, ps_ref,
                     g_ref, b_ref, o_ref, ncv_ref, npl_ref, zc_ref, pc_ref, win_s, wout_s, *,
                     alpha, layer, bi, t):
    rows, c_dim = x_ref.shape[1], x_ref.shape[2]
    w_conv, w_gmlp, w_pool = cw_ref.shape[2], sg_ref.shape[1], ps_ref.shape[1]
    n_heads = ws_ref.shape[0]
    n_in, n_out = win_s.shape[0], wout_s.shape[0]

    m = mod_ref[pl.ds(bi, 1), :]
    sh, sc1, gate = m[:, :c_dim], 1.0 + m[:, c_dim:2 * c_dim], m[:, 2 * c_dim:]
    cw = cw_ref[layer]
    sgu_g, sgu_b = sg_ref[layer:layer + 1, :], sb_ref[layer:layer + 1, :]
    pool_scale = ps_ref[layer:layer + 1, :]
    ln_g, ln_b = g_ref[layer:layer + 1, :], b_ref[layer:layer + 1, :]

    row_i = lax.broadcasted_iota(jnp.int32, (CHUNK, 2 * CHUNK), 0)
    col_j = lax.broadcasted_iota(jnp.int32, (CHUNK, 2 * CHUNK), 1) % CHUNK
    causal = col_j <= row_i
    ws_m = []
    for p in range(n_heads // 2):
        pair = jnp.concatenate([ws_ref[2 * p], ws_ref[2 * p + 1]], axis=1)
        ws_m.append(jnp.where(causal, pair, 0.0).astype(BF16))
    low_half = lax.broadcasted_iota(jnp.int32, (CHUNK, LANES), 1) < HEAD_DIM
    lane_group = _lane_group((SUB_ROWS, w_pool), len(POOL_WINDOWS))
    window = _select_by_group([jnp.full((SUB_ROWS, w_pool), w, jnp.int32) for w in POOL_WINDOWS],
                              lane_group)
    row_id = lax.broadcasted_iota(jnp.int32, (SUB_ROWS, w_pool), 0)
    bias = bs_ref[...]

    z_hist = zc_ref[...]
    p_hist = pc_ref[...]

    def project(jj):
        h = (x_ref[0, jj * SUB_ROWS:(jj + 1) * SUB_ROWS, :] * sc1 + sh).astype(BF16)
        return jnp.concatenate([_dot(h, win_s[n]) for n in range(n_in)], axis=1)

    n_sub = rows // SUB_ROWS
    proj_next = project(0)
    for j in range(n_sub):
        r0 = j * SUB_ROWS
        x = x_ref[0, r0:r0 + SUB_ROWS, :]
        proj = proj_next
        if j + 1 < n_sub:
            proj_next = project(j + 1)
        o0 = 0
        a_b = proj[:, o0:o0 + w_conv]; o0 += w_conv
        a_c = proj[:, o0:o0 + w_conv]; o0 += w_conv
        a_x = proj[:, o0:o0 + w_conv]; o0 += w_conv
        g_u = proj[:, o0:o0 + w_gmlp]; o0 += w_gmlp
        g_v = proj[:, o0:o0 + w_gmlp]; o0 += w_gmlp
        p_x = proj[:, o0:o0 + w_pool]

        z = a_c * a_x
        z_ext = jnp.concatenate([z_hist, z], axis=0)
        conv = z_ext * cw[CONV_WIDTH - 1:CONV_WIDTH, :]
        for k in range(1, CONV_WIDTH):
            conv = conv + pltpu.roll(z_ext, k, 0) * cw[CONV_WIDTH - 1 - k:CONV_WIDTH - k, :]
        y_a = a_b * conv[CONV_HIST:, :]
        z_hist = z[SUB_ROWS - CONV_HIST:, :]

        v_n = _layer_norm(g_v, sgu_g, sgu_b)
        n_ch = SUB_ROWS // CHUNK
        s_cols = []
        for p in range(n_heads // 2):
            cols = [v_n[c * CHUNK:(c + 1) * CHUNK, p * LANES:(p + 1) * LANES] for c in range(n_ch)]
            rhs = jnp.concatenate(
                [jnp.concatenate([jnp.where(low_half, col, 0.0) for col in cols], axis=1),
                 jnp.concatenate([jnp.where(low_half, 0.0, col) for col in cols], axis=1)],
                axis=0).astype(BF16)
            s_cols.append(_dot(ws_m[p], rhs))
        s = jnp.concatenate(
            [jnp.concatenate([sc[:, c * LANES:(c + 1) * LANES] for sc in s_cols], axis=1) + bias
             for c in range(n_ch)], axis=0)
        y_b = g_u * s

        run = jnp.concatenate([p_hist, p_x], axis=0)
        sums = []
        for w in POOL_WINDOWS:
            run = run + pltpu.roll(run, w // 2, 0)
            sums.append(run[POOL_HIST:, :])
        count = jnp.minimum(window, t * rows + r0 + row_id + 1).astype(F32)
        d = (_select_by_group(sums, lane_group) / count - p_x).astype(BF16)
        y_c = _dot(d, pw_ref[...]) * pool_scale
        p_hist = p_x[SUB_ROWS - POOL_HIST:, :]

        mix = jnp.concatenate([y_a, y_b, y_c], axis=1).astype(BF16)
        out = jnp.concatenate([_dot(mix, wout_s[n]) for n in range(n_out)], axis=1)
        o_ref[0, r0:r0 + SUB_ROWS, :] = _layer_norm(alpha * x + gate * out, ln_g, ln_b)

    zc_ref[...] = z_hist
    pc_ref[...] = p_hist
    ncv_ref[0] = z_hist[CONV_HIST - (CONV_WIDTH - 1):, :]
    npl_ref[0] = p_hist[POOL_HIST - (max(POOL_WINDOWS) - 1):, :]


def _mix_prompt_call(x, mod, w_in, conv_w, sgu_g, sgu_b, w_s, bs_full, pool_wbd, pool_scale,
                     w_out, ln_g, ln_b, *, alpha, layer, rows):
    batch, seq, c_dim = x.shape
    d_in = w_in.shape[2]
    w_conv, w_pool = conv_w.shape[2], pool_scale.shape[1]
    conv_buf = CONV_WIDTH - 1
    pool_buf = max(POOL_WINDOWS) - 1
    n_t = seq // rows
    n_in, n_out = d_in // WEIGHT_COLS, c_dim // WEIGHT_COLS
    n_warm = n_in + n_out

    def tile(i):
        tl = jnp.maximum(i - n_warm, 0)
        return (tl // n_t, tl % n_t, 0)

    def batch_row(i):
        return (jnp.maximum(i - n_warm, 0) // n_t, 0, 0)

    in_col = lambda i: jnp.minimum(i, n_in - 1)
    out_col = lambda i: jnp.clip(i - n_in, 0, n_out - 1)
    kern = functools.partial(_mix_prompt_kernel, alpha=alpha, layer=layer, n_t=n_t)
    return pl.pallas_call(
        kern,
        grid=(n_warm + batch * n_t,),
        in_specs=[
            pl.BlockSpec((1, rows, c_dim), tile),
            pl.BlockSpec((None, batch, mod.shape[2] // 2), lambda i: (layer, 0, 0)),
            pl.BlockSpec((None, c_dim, WEIGHT_COLS), lambda i: (layer, 0, in_col(i))),
            pl.BlockSpec((None, c_dim, WEIGHT_COLS), lambda i: (layer, 0, out_col(i))),
            _whole(conv_w.shape),
            _whole(sgu_g.shape),
            _whole(sgu_b.shape),
            _layer_block(w_s.shape[1:], layer),
            _layer_block(bs_full.shape[1:], layer),
            _layer_block(pool_wbd.shape[1:], layer),
            _whole(pool_scale.shape),
            _whole(ln_g.shape),
            _whole(ln_b.shape),
        ],
        out_specs=[
            pl.BlockSpec((1, rows, c_dim), tile),
            pl.BlockSpec((1, conv_buf, w_conv), batch_row),
            pl.BlockSpec((1, pool_buf, w_pool), batch_row),
            pl.BlockSpec((c_dim, WEIGHT_COLS), lambda i: (0, in_col(i))),
            pl.BlockSpec((c_dim, WEIGHT_COLS), lambda i: (0, out_col(i))),
        ],
        out_shape=[
            jax.ShapeDtypeStruct((batch, seq, c_dim), F32),
            jax.ShapeDtypeStruct((batch, conv_buf, w_conv), F32),
            jax.ShapeDtypeStruct((batch, pool_buf, w_pool), F32),
            jax.ShapeDtypeStruct((c_dim, d_in), BF16),
            jax.ShapeDtypeStruct((c_dim, c_dim), BF16),
        ],
        scratch_shapes=[
            pltpu.VMEM((CONV_HIST, w_conv), F32),
            pltpu.VMEM((POOL_HIST, w_pool), F32),
            pltpu.VMEM((n_in, c_dim, WEIGHT_COLS), BF16),
            pltpu.VMEM((n_out, c_dim, WEIGHT_COLS), BF16),
        ],
        compiler_params=pltpu.CompilerParams(
            dimension_semantics=("arbitrary",),
            vmem_limit_bytes=VMEM_LIMIT_BYTES),
        name="mix_prompt",
    )(x, mod, w_in, w_out, conv_w, sgu_g, sgu_b, w_s, bs_full, pool_wbd, pool_scale, ln_g, ln_b)


def _mix_sample_kernel(x_ref, mod_ref, cst_ref, pst_ref, *refs, alpha, layer, n_in, n_out):
    win_refs, refs = refs[:n_in], refs[n_in:]
    cw_ref, sg_ref, sb_ref, wst_ref, bst_ref, pw_ref, ps_ref = refs[:7]
    wout_refs, refs = refs[7:7 + n_out], refs[7 + n_out:]
    g_ref, b_ref, o_ref, ncv_ref, npl_ref, v_ref = refs
    x = x_ref[...]
    t_dim, b_dim, c_dim = x.shape
    w_conv, w_gmlp, w_pool = cw_ref.shape[2], sg_ref.shape[1], ps_ref.shape[1]
    n_valid = pst_ref.shape[0]
    m = mod_ref[...]
    sh, sc1, gate = m[:, :c_dim], 1.0 + m[:, c_dim:2 * c_dim], m[:, 2 * c_dim:]
    h = (x * sc1 + sh).reshape(t_dim * b_dim, c_dim).astype(BF16)
    proj = jnp.concatenate([_dot(h, w[...]) for w in win_refs], axis=1).reshape(t_dim, b_dim, -1)
    o0 = 0
    a_b = proj[:, :, o0:o0 + w_conv]; o0 += w_conv
    a_c = proj[:, :, o0:o0 + w_conv]; o0 += w_conv
    a_x = proj[:, :, o0:o0 + w_conv]; o0 += w_conv
    g_u = proj[:, :, o0:o0 + w_gmlp]; o0 += w_gmlp
    g_v = proj[:, :, o0:o0 + w_gmlp]; o0 += w_gmlp
    p_x = proj[:, :, o0:o0 + w_pool]

    z = a_c * a_x
    ext = [cst_ref[i] for i in range(CONV_WIDTH - 1)] + [z[i] for i in range(t_dim)]
    cw = cw_ref[layer]
    y_a = []
    for i in range(t_dim):
        conv = ext[i] * cw[0:1, :]
        for k in range(1, CONV_WIDTH):
            conv = conv + ext[i + k] * cw[k:k + 1, :]
        y_a.append(a_b[i] * conv)
    for i in range(CONV_WIDTH - 1):
        ncv_ref[i] = ext[len(ext) - (CONV_WIDTH - 1) + i]

    v_n = _layer_norm(g_v, sg_ref[layer:layer + 1, :], sb_ref[layer:layer + 1, :])
    v_ref[...] = v_n
    y_b = []
    for i in range(t_dim):
        s = bst_ref[i:i + 1, :] + wst_ref[i, 0:1, :] * v_n[0]
        for j in range(1, i + 1):
            s = s + wst_ref[i, j:j + 1, :] * v_n[j]
        y_b.append(g_u[i] * s)

    pext = [pst_ref[i] for i in range(n_valid)] + [p_x[i] for i in range(t_dim)]
    lane_group = _lane_group((b_dim, w_pool), len(POOL_WINDOWS))
    d = []
    for i in range(t_dim):
        last = n_valid + i
        means = []
        run = None
        k = 0
        for w in POOL_WINDOWS:
            while k < w:
                run = pext[last - k] if run is None else run + pext[last - k]
                k += 1
            means.append(run / float(min(w, i + 1 + n_valid)))
        d.append(_select_by_group(means, lane_group) - p_x[i])
    d = jnp.stack(d).reshape(t_dim * b_dim, w_pool).astype(BF16)
    y_c = (_dot(d, pw_ref[...]) * ps_ref[layer:layer + 1, :]).reshape(t_dim, b_dim, w_pool)
    for i in range(n_valid):
        npl_ref[i] = pext[len(pext) - n_valid + i]

    mix = jnp.concatenate([jnp.stack(y_a), jnp.stack(y_b), y_c], axis=2)
    mix = mix.reshape(t_dim * b_dim, c_dim).astype(BF16)
    out = jnp.concatenate([_dot(mix, w[...]) for w in wout_refs], axis=1)
    y = alpha * x + gate * out.reshape(t_dim, b_dim, c_dim)
    o_ref[...] = _layer_norm(y, g_ref[layer:layer + 1, :], b_ref[layer:layer + 1, :])


def _mix_sample_call(x, mod, cst, pst, w_in, conv_w, sgu_g, sgu_b, wst, bs_full, pool_wbd,
                     pool_scale, w_out, ln_g, ln_b, *, alpha, layer):
    t_dim, b_dim, c_dim = x.shape
    w_gmlp = sgu_g.shape[1]
    in_specs = _column_blocks(w_in, WEIGHT_COLS)
    out_specs = _column_blocks(w_out, WEIGHT_COLS)
    kern = functools.partial(_mix_sample_kernel, alpha=alpha, layer=layer, n_in=len(in_specs),
                             n_out=len(out_specs))
    whole3 = lambda i: (0, 0, 0)
    return pl.pallas_call(
        kern,
        grid=(1,),
        in_specs=[
            pl.BlockSpec((t_dim, b_dim, c_dim), whole3),
            pl.BlockSpec((None, b_dim, mod.shape[2] // 2), lambda i: (layer, 0, 0)),
            _layer_block(cst.shape[1:], layer),
            _layer_block(pst.shape[1:], layer),
            *in_specs,
            _whole(conv_w.shape),
            _whole(sgu_g.shape),
            _whole(sgu_b.shape),
            _layer_block(wst.shape[1:], layer),
            pl.BlockSpec((None, t_dim, w_gmlp), lambda i: (layer, 0, 0)),
            _layer_block(pool_wbd.shape[1:], layer),
            _whole(pool_scale.shape),
            *out_specs,
            _whole(ln_g.shape),
            _whole(ln_b.shape),
        ],
        out_specs=[
            pl.BlockSpec((t_dim, b_dim, c_dim), whole3),
            pl.BlockSpec(cst.shape[1:], whole3),
            pl.BlockSpec(pst.shape[1:], whole3),
            pl.BlockSpec((t_dim, b_dim, w_gmlp), whole3),
        ],
        out_shape=[
            jax.ShapeDtypeStruct((t_dim, b_dim, c_dim), F32),
            jax.ShapeDtypeStruct(cst.shape[1:], F32),
            jax.ShapeDtypeStruct(pst.shape[1:], F32),
            jax.ShapeDtypeStruct((t_dim, b_dim, w_gmlp), F32),
        ],
        compiler_params=pltpu.CompilerParams(
            dimension_semantics=("arbitrary",),
            vmem_limit_bytes=VMEM_LIMIT_BYTES),
        name="mix_sample",
    )(x, mod, cst, pst, *([w_in] * len(in_specs)), conv_w, sgu_g, sgu_b, wst, bs_full, pool_wbd,
      pool_scale, *([w_out] * len(out_specs)), ln_g, ln_b)


def kernel(x_prompt, x_sample, c_prompt, c_sample, state_conv, state_pool, w_ada, b_ada, w_in,
           conv_w, sgu_g, sgu_b, w_s, b_s, pool_w, pool_scale, w_out, ln1_g, ln1_b, w_up,
           w_down, ln2_g, ln2_b):
    depth = w_in.shape[0]
    seq = x_prompt.shape[1]
    dec_batch, dec_seq, _ = x_sample.shape
    n_groups, group_dim, _ = pool_w.shape[1:]
    w_pool = n_groups * group_dim
    alpha = (2.0 * depth) ** 0.25
    assert seq % PROMPT_ROWS == 0 and PROMPT_ROWS % SUB_ROWS == 0 and SUB_ROWS % CHUNK == 0
    assert (dec_seq * dec_batch) % SUB_ROWS == 0 and SUB_ROWS % dec_batch == 0

    mod_p, mod_s = _ada_call(c_prompt, c_sample, w_ada, b_ada)

    bs_full = jnp.repeat(jnp.swapaxes(b_s, 1, 2), HEAD_DIM, axis=2)
    ws_small = jnp.repeat(jnp.transpose(w_s[:, :, :dec_seq, :dec_seq], (0, 2, 3, 1)), HEAD_DIM, axis=3)
    eye = jnp.eye(n_groups, dtype=F32)
    pool_wbd = (eye[None, :, None, :, None] * pool_w[:, :, :, None, :]).reshape(
        depth, w_pool, w_pool).astype(BF16)

    xs = jnp.swapaxes(x_sample, 0, 1)
    cst = jnp.swapaxes(state_conv, 1, 2)
    pst = jnp.swapaxes(state_pool, 1, 2)

    xp = x_prompt
    conv_p, pool_p, conv_s, pool_s, v_s = [], [], [], [], []
    for l in range(depth):
        xp, ncv, npl, w_in_b, w_out_b = _mix_prompt_call(
            xp, mod_p, w_in, conv_w, sgu_g, sgu_b, w_s, bs_full, pool_wbd, pool_scale,
            w_out, ln1_g, ln1_b, alpha=alpha, layer=l, rows=PROMPT_ROWS)
        conv_p.append(ncv)
        pool_p.append(npl)
        xp, wup_b, wdn_b = _ffn_prompt_call(xp, mod_p, w_up, w_down, ln2_g, ln2_b, layer=l,
                                            alpha=alpha, rows=PROMPT_ROWS)

        xs, ncv, npl, v_rows = _mix_sample_call(
            xs, mod_s, cst, pst, w_in_b, conv_w, sgu_g, sgu_b, ws_small, bs_full, pool_wbd,
            pool_scale, w_out_b, ln1_g, ln1_b, alpha=alpha, layer=l)
        conv_s.append(ncv)
        pool_s.append(npl)
        v_s.append(v_rows)
        xs = _ffn_sample_call(xs, mod_s, wup_b, wdn_b, ln2_g, ln2_b, layer=l, alpha=alpha)

    to_batch_major = lambda a: jnp.swapaxes(jnp.stack(a), 1, 2)
    return (xp, jnp.swapaxes(xs, 0, 1), jnp.stack(conv_p), jnp.stack(pool_p),
            to_batch_major(conv_s), to_batch_major(pool_s), to_batch_major(v_s))
```

```python
import functools

import jax
import jax.numpy as jnp
from jax import lax
from jax.experimental import pallas as pl
from jax.experimental.pallas import tpu as pltpu

F32 = jnp.float32
BF16 = jnp.bfloat16

LN_EPS = 1e-5
HEAD_DIM = 64
CHUNK = 128
POOL_WINDOWS = (2, 4, 8, 16)
CONV_WIDTH = 3
LANES = 128
SUBLANES = 8
VMEM_LIMIT_BYTES = 56 * 1024 * 1024

PROMPT_ROWS = 1024
SUB_ROWS = 256
WEIGHT_COLS = 512
CONV_HIST = SUBLANES
POOL_HIST = 2 * SUBLANES


def _layer_norm(y, g, b):
    mu = jnp.mean(y, axis=-1, keepdims=True)
    d = y - mu
    var = jnp.mean(d * d, axis=-1, keepdims=True)
    return d * lax.rsqrt(var + LN_EPS) * g + b


def _dot(a, b):
    return jnp.dot(a, b, preferred_element_type=F32)


def _layer_block(shape, layer):
    zeros = (0,) * len(shape)
    return pl.BlockSpec((None,) + tuple(shape), lambda *_: (layer,) + zeros,
                        pipeline_mode=pl.Buffered(1))


def _whole(shape):
    zeros = (0,) * len(shape)
    return pl.BlockSpec(tuple(shape), lambda *_: zeros, pipeline_mode=pl.Buffered(1))


def _column_blocks(w, layer, width):
    rows, cols = w.shape[1:]
    return [pl.BlockSpec((None, rows, width), lambda *_, n=n: (layer, 0, n),
                         pipeline_mode=pl.Buffered(1)) for n in range(cols // width)]


def _lane_group(shape, n_groups):
    return lax.broadcasted_iota(jnp.int32, shape, len(shape) - 1) // (shape[-1] // n_groups)


def _select_by_group(values, lane_group):
    out = values[-1]
    for g in range(len(values) - 2, -1, -1):
        out = jnp.where(lane_group == g, values[g], out)
    return out


def _ada_kernel(cp_ref, cs_ref, w_ref, b_ref, op_ref, os_ref):
    l, k = pl.program_id(0), pl.program_id(1)
    w = w_ref[0].astype(BF16)
    part_p = _dot(jax.nn.silu(cp_ref[...]).astype(BF16), w)
    part_s = _dot(jax.nn.silu(cs_ref[...]).astype(BF16), w)

    @pl.when(k == 0)
    def _():
        bias = b_ref[pl.ds(l, 1), :]
        op_ref[0] = part_p + bias
        os_ref[0] = part_s + bias

    @pl.when(k > 0)
    def _():
        op_ref[0] += part_p
        os_ref[0] += part_s


def _ada_call(c_prompt, c_sample, w_ada, b_ada, row_tile=256):
    depth, d_model, n_mod = w_ada.shape
    bp, bs = c_prompt.shape[0], c_sample.shape[0]
    return pl.pallas_call(
        _ada_kernel,
        grid=(depth, d_model // row_tile),
        in_specs=[
            pl.BlockSpec((bp, row_tile), lambda l, k: (0, k)),
            pl.BlockSpec((bs, row_tile), lambda l, k: (0, k)),
            pl.BlockSpec((1, row_tile, n_mod), lambda l, k: (l, k, 0)),
            _whole(b_ada.shape),
        ],
        out_specs=[
            pl.BlockSpec((1, bp, n_mod), lambda l, k: (l, 0, 0)),
            pl.BlockSpec((1, bs, n_mod), lambda l, k: (l, 0, 0)),
        ],
        out_shape=[
            jax.ShapeDtypeStruct((depth, bp, n_mod), F32),
            jax.ShapeDtypeStruct((depth, bs, n_mod), F32),
        ],
        compiler_params=pltpu.CompilerParams(
            dimension_semantics=("arbitrary", "arbitrary"),
            vmem_limit_bytes=VMEM_LIMIT_BYTES),
        name="ada_mod",
    )(c_prompt, c_sample, w_ada, b_ada)


def _ffn_kernel(x_ref, mod_ref, *refs, alpha, layer, row_axis, mod_row_axis, n_up, n_dn,
                out_batch_major):
    wup_refs, wdn_refs = refs[:n_up], refs[n_up:n_up + n_dn]
    g_ref, b_ref, o_ref = refs[n_up + n_dn:]
    g_dim, r_dim, c_dim = x_ref.shape
    chunk = wup_refs[0].shape[1]
    if mod_row_axis is None:
        m = mod_ref[...]
    else:
        m = mod_ref[pl.ds(pl.program_id(mod_row_axis), 1), :]
    sh, sc1, gate = m[:, :c_dim], 1.0 + m[:, c_dim:2 * c_dim], m[:, 2 * c_dim:]
    ln_g, ln_b = g_ref[layer:layer + 1, :], b_ref[layer:layer + 1, :]
    step = SUB_ROWS if row_axis == 1 else SUB_ROWS // r_dim
    n_sub = (r_dim if row_axis == 1 else g_dim) // step
    outs = []
    for j in range(n_sub):
        idx = (slice(None), slice(j * step, (j + 1) * step)) if row_axis == 1 else (
            slice(j * step, (j + 1) * step),)
        x = x_ref[idx]
        rows = x.shape[0] * x.shape[1]
        h = (x * sc1 + sh).reshape(rows, c_dim).astype(BF16)
        acc = [None] * n_dn
        u_next = _dot(h, wup_refs[0][...])
        for k in range(n_up):
            u = u_next
            if k + 1 < n_up:
                u_next = _dot(h, wup_refs[k + 1][...])
            a = jnp.square(jnp.maximum(u, 0.0)).astype(BF16)
            for n in range(n_dn):
                part = _dot(a, wdn_refs[n][k * chunk:(k + 1) * chunk, :])
                acc[n] = part if acc[n] is None else acc[n] + part
        f = jnp.concatenate(acc, axis=1).reshape(x.shape)
        y = _layer_norm(alpha * x + gate * f, ln_g, ln_b)
        if out_batch_major:
            outs.append(y)
        else:
            o_ref[idx] = y
    if out_batch_major:
        o_ref[...] = jnp.swapaxes(jnp.concatenate(outs, axis=0), 0, 1)


def _ffn_call(x, mod, wup, wdn, ln_g, ln_b, *, layer, x_block, x_index, mod_row_axis,
              row_axis, grid, alpha, name, out_batch_major=False):
    n_mod = mod.shape[-1]
    up_specs = _column_blocks(wup, layer, WEIGHT_COLS)
    dn_specs = _column_blocks(wdn, layer, WEIGHT_COLS)
    kern = functools.partial(_ffn_kernel, alpha=alpha, layer=layer, row_axis=row_axis,
                             mod_row_axis=mod_row_axis, n_up=len(up_specs), n_dn=len(dn_specs),
                             out_batch_major=out_batch_major)
    out_shape = (x.shape[1], x.shape[0], x.shape[2]) if out_batch_major else x.shape
    out_block = out_shape if out_batch_major else x_block
    return pl.pallas_call(
        kern,
        grid=grid,
        in_specs=[
            pl.BlockSpec(x_block, x_index),
            pl.BlockSpec((None, mod.shape[1], n_mod // 2), lambda *_: (layer, 0, 1)),
            *up_specs, *dn_specs,
            _whole(ln_g.shape),
            _whole(ln_b.shape),
        ],
        out_specs=pl.BlockSpec(out_block, x_index),
        out_shape=jax.ShapeDtypeStruct(out_shape, F32),
        compiler_params=pltpu.CompilerParams(
            dimension_semantics=("arbitrary",) * len(grid),
            vmem_limit_bytes=VMEM_LIMIT_BYTES),
        name=name,
    )(x, mod, *([wup] * len(up_specs)), *([wdn] * len(dn_specs)), ln_g, ln_b)


def _mix_prompt_kernel(x_ref, mod_ref, *refs, alpha, layer, n_in, n_out):
    win_refs, refs = refs[:n_in], refs[n_in:]
    cw_ref, sg_ref, sb_ref, ws_ref, bs_ref, pw_ref, ps_ref = refs[:7]
    wout_refs, refs = refs[7:7 + n_out], refs[7 + n_out:]
    g_ref, b_ref, o_ref, ncv_ref, npl_ref, zc_ref, pc_ref = refs
    bi, t = pl.program_id(0), pl.program_id(1)
    rows, c_dim = x_ref.shape[1], x_ref.shape[2]
    w_conv, w_gmlp, w_pool = cw_ref.shape[2], sg_ref.shape[1], ps_ref.shape[1]
    n_heads = ws_ref.shape[0]

    @pl.when(t == 0)
    def _():
        zc_ref[...] = jnp.zeros(zc_ref.shape, F32)
        pc_ref[...] = jnp.zeros(pc_ref.shape, F32)

    m = mod_ref[pl.ds(bi, 1), :]
    sh, sc1, gate = m[:, :c_dim], 1.0 + m[:, c_dim:2 * c_dim], m[:, 2 * c_dim:]
    cw = cw_ref[layer]
    sgu_g, sgu_b = sg_ref[layer:layer + 1, :], sb_ref[layer:layer + 1, :]
    pool_scale = ps_ref[layer:layer + 1, :]
    ln_g, ln_b = g_ref[layer:layer + 1, :], b_ref[layer:layer + 1, :]

    row_i = lax.broadcasted_iota(jnp.int32, (CHUNK, 2 * CHUNK), 0)
    col_j = lax.broadcasted_iota(jnp.int32, (CHUNK, 2 * CHUNK), 1) % CHUNK
    causal = col_j <= row_i
    ws_m = []
    for p in range(n_heads // 2):
        pair = jnp.concatenate([ws_ref[2 * p], ws_ref[2 * p + 1]], axis=1)
        ws_m.append(jnp.where(causal, pair, 0.0).astype(BF16))
    low_half = lax.broadcasted_iota(jnp.int32, (CHUNK, LANES), 1) < HEAD_DIM
    lane_group = _lane_group((SUB_ROWS, w_pool), len(POOL_WINDOWS))
    window = _select_by_group([jnp.full((SUB_ROWS, w_pool), w, jnp.int32) for w in POOL_WINDOWS],
                              lane_group)
    row_id = lax.broadcasted_iota(jnp.int32, (SUB_ROWS, w_pool), 0)
    bias = bs_ref[...]

    z_hist = zc_ref[...]
    p_hist = pc_ref[...]

    def project(jj):
        h = (x_ref[0, jj * SUB_ROWS:(jj + 1) * SUB_ROWS, :] * sc1 + sh).astype(BF16)
        return jnp.concatenate([_dot(h, w[...]) for w in win_refs], axis=1)

    n_sub = rows // SUB_ROWS
    proj_next = project(0)
    for j in range(n_sub):
        r0 = j * SUB_ROWS
        x = x_ref[0, r0:r0 + SUB_ROWS, :]
        proj = proj_next
        if j + 1 < n_sub:
            proj_next = project(j + 1)
        o0 = 0
        a_b = proj[:, o0:o0 + w_conv]; o0 += w_conv
        a_c = proj[:, o0:o0 + w_conv]; o0 += w_conv
        a_x = proj[:, o0:o0 + w_conv]; o0 += w_conv
        g_u = proj[:, o0:o0 + w_gmlp]; o0 += w_gmlp
        g_v = proj[:, o0:o0 + w_gmlp]; o0 += w_gmlp
        p_x = proj[:, o0:o0 + w_pool]

        z = a_c * a_x
        z_ext = jnp.concatenate([z_hist, z], axis=0)
        conv = z_ext * cw[CONV_WIDTH - 1:CONV_WIDTH, :]
        for k in range(1, CONV_WIDTH):
            conv = conv + pltpu.roll(z_ext, k, 0) * cw[CONV_WIDTH - 1 - k:CONV_WIDTH - k, :]
        y_a = a_b * conv[CONV_HIST:, :]
        z_hist = z[SUB_ROWS - CONV_HIST:, :]

        v_n = _layer_norm(g_v, sgu_g, sgu_b)
        n_ch = SUB_ROWS // CHUNK
        s_cols = []
        for p in range(n_heads // 2):
            cols = [v_n[c * CHUNK:(c + 1) * CHUNK, p * LANES:(p + 1) * LANES] for c in range(n_ch)]
            rhs = jnp.concatenate(
                [jnp.concatenate([jnp.where(low_half, col, 0.0) for col in cols], axis=1),
                 jnp.concatenate([jnp.where(low_half, 0.0, col) for col in cols], axis=1)],
                axis=0).astype(BF16)
            s_cols.append(_dot(ws_m[p], rhs))
        s = jnp.concatenate(
            [jnp.concatenate([sc[:, c * LANES:(c + 1) * LANES] for sc in s_cols], axis=1) + bias
             for c in range(n_ch)], axis=0)
        y_b = g_u * s

        run = jnp.concatenate([p_hist, p_x], axis=0)
        sums = []
        for w in POOL_WINDOWS:
            run = run + pltpu.roll(run, w // 2, 0)
            sums.append(run[POOL_HIST:, :])
        count = jnp.minimum(window, t * rows + r0 + row_id + 1).astype(F32)
        d = (_select_by_group(sums, lane_group) / count - p_x).astype(BF16)
        y_c = _dot(d, pw_ref[...]) * pool_scale
        p_hist = p_x[SUB_ROWS - POOL_HIST:, :]

        mix = jnp.concatenate([y_a, y_b, y_c], axis=1).astype(BF16)
        out = jnp.concatenate([_dot(mix, w[...]) for w in wout_refs], axis=1)
        o_ref[0, r0:r0 + SUB_ROWS, :] = _layer_norm(alpha * x + gate * out, ln_g, ln_b)

    zc_ref[...] = z_hist
    pc_ref[...] = p_hist
    ncv_ref[0] = z_hist[CONV_HIST - (CONV_WIDTH - 1):, :]
    npl_ref[0] = p_hist[POOL_HIST - (max(POOL_WINDOWS) - 1):, :]


def _mix_prompt_call(x, mod, w_in, conv_w, sgu_g, sgu_b, w_s, bs_full, pool_wbd, pool_scale,
                     w_out, ln_g, ln_b, *, alpha, layer, rows):
    batch, seq, c_dim = x.shape
    w_conv, w_pool = conv_w.shape[2], pool_scale.shape[1]
    conv_buf = CONV_WIDTH - 1
    pool_buf = max(POOL_WINDOWS) - 1
    in_specs = _column_blocks(w_in, layer, WEIGHT_COLS)
    out_specs = _column_blocks(w_out, layer, WEIGHT_COLS)
    kern = functools.partial(_mix_prompt_kernel, alpha=alpha, layer=layer, n_in=len(in_specs),
                             n_out=len(out_specs))
    return pl.pallas_call(
        kern,
        grid=(batch, seq // rows),
        in_specs=[
            pl.BlockSpec((1, rows, c_dim), lambda b, t: (b, t, 0)),
            pl.BlockSpec((None, batch, mod.shape[2] // 2), lambda b, t: (layer, 0, 0)),
            *in_specs,
            _whole(conv_w.shape),
            _whole(sgu_g.shape),
            _whole(sgu_b.shape),
            _layer_block(w_s.shape[1:], layer),
            _layer_block(bs_full.shape[1:], layer),
            _layer_block(pool_wbd.shape[1:], layer),
            _whole(pool_scale.shape),
            *out_specs,
            _whole(ln_g.shape),
            _whole(ln_b.shape),
        ],
        out_specs=[
            pl.BlockSpec((1, rows, c_dim), lambda b, t: (b, t, 0)),
            pl.BlockSpec((1, conv_buf, w_conv), lambda b, t: (b, 0, 0)),
            pl.BlockSpec((1, pool_buf, w_pool), lambda b, t: (b, 0, 0)),
        ],
        out_shape=[
            jax.ShapeDtypeStruct((batch, seq, c_dim), F32),
            jax.ShapeDtypeStruct((batch, conv_buf, w_conv), F32),
            jax.ShapeDtypeStruct((batch, pool_buf, w_pool), F32),
        ],
        scratch_shapes=[
            pltpu.VMEM((CONV_HIST, w_conv), F32),
            pltpu.VMEM((POOL_HIST, w_pool), F32),
        ],
        compiler_params=pltpu.CompilerParams(
            dimension_semantics=("arbitrary", "arbitrary"),
            vmem_limit_bytes=VMEM_LIMIT_BYTES),
        name="mix_prompt",
    )(x, mod, *([w_in] * len(in_specs)), conv_w, sgu_g, sgu_b, w_s, bs_full, pool_wbd, pool_scale,
      *([w_out] * len(out_specs)), ln_g, ln_b)


def _mix_sample_kernel(x_ref, mod_ref, cst_ref, pst_ref, win_ref, cw_ref, sg_ref, sb_ref,
                       wst_ref, bst_ref, pw_ref, ps_ref, wout_ref, g_ref, b_ref,
                       o_ref, ncv_ref, npl_ref, v_ref, *, alpha, layer, x_batch_major):
    x = jnp.swapaxes(x_ref[...], 0, 1) if x_batch_major else x_ref[...]
    cst = jnp.swapaxes(cst_ref[...], 0, 1)
    pst = jnp.swapaxes(pst_ref[...], 0, 1)
    t_dim, b_dim, c_dim = x.shape
    w_conv, w_gmlp, w_pool = cw_ref.shape[2], sg_ref.shape[1], ps_ref.shape[1]
    n_valid = pst.shape[0]
    m = mod_ref[...]
    sh, sc1, gate = m[:, :c_dim], 1.0 + m[:, c_dim:2 * c_dim], m[:, 2 * c_dim:]
    h = (x * sc1 + sh).reshape(t_dim * b_dim, c_dim).astype(BF16)
    proj = _dot(h, win_ref[...]).reshape(t_dim, b_dim, -1)
    o0 = 0
    a_b = proj[:, :, o0:o0 + w_conv]; o0 += w_conv
    a_c = proj[:, :, o0:o0 + w_conv]; o0 += w_conv
    a_x = proj[:, :, o0:o0 + w_conv]; o0 += w_conv
    g_u = proj[:, :, o0:o0 + w_gmlp]; o0 += w_gmlp
    g_v = proj[:, :, o0:o0 + w_gmlp]; o0 += w_gmlp
    p_x = proj[:, :, o0:o0 + w_pool]

    z = a_c * a_x
    ext = [cst[i] for i in range(CONV_WIDTH - 1)] + [z[i] for i in range(t_dim)]
    cw = cw_ref[layer]
    y_a = []
    for i in range(t_dim):
        conv = ext[i] * cw[0:1, :]
        for k in range(1, CONV_WIDTH):
            conv = conv + ext[i + k] * cw[k:k + 1, :]
        y_a.append(a_b[i] * conv)
    ncv_ref[...] = jnp.swapaxes(jnp.stack(ext[len(ext) - (CONV_WIDTH - 1):]), 0, 1)

    v_n = _layer_norm(g_v, sg_ref[layer:layer + 1, :], sb_ref[layer:layer + 1, :])
    v_ref[...] = jnp.swapaxes(v_n, 0, 1)
    y_b = []
    for i in range(t_dim):
        s = bst_ref[i:i + 1, :] + wst_ref[i, 0:1, :] * v_n[0]
        for j in range(1, i + 1):
            s = s + wst_ref[i, j:j + 1, :] * v_n[j]
        y_b.append(g_u[i] * s)

    pext = [pst[i] for i in range(n_valid)] + [p_x[i] for i in range(t_dim)]
    lane_group = _lane_group((b_dim, w_pool), len(POOL_WINDOWS))
    d = []
    for i in range(t_dim):
        last = n_valid + i
        means = []
        run = None
        k = 0
        for w in POOL_WINDOWS:
            while k < w:
                run = pext[last - k] if run is None else run + pext[last - k]
                k += 1
            means.append(run / float(min(w, i + 1 + n_valid)))
        d.append(_select_by_group(means, lane_group) - p_x[i])
    d = jnp.stack(d).reshape(t_dim * b_dim, w_pool).astype(BF16)
    y_c = (_dot(d, pw_ref[...]) * ps_ref[layer:layer + 1, :]).reshape(t_dim, b_dim, w_pool)
    npl_ref[...] = jnp.swapaxes(jnp.stack(pext[len(pext) - n_valid:]), 0, 1)

    mix = jnp.concatenate([jnp.stack(y_a), jnp.stack(y_b), y_c], axis=2)
    mix = mix.reshape(t_dim * b_dim, c_dim).astype(BF16)
    y = alpha * x + gate * _dot(mix, wout_ref[...]).reshape(t_dim, b_dim, c_dim)
    o_ref[...] = _layer_norm(y, g_ref[layer:layer + 1, :], b_ref[layer:layer + 1, :])


def _mix_sample_call(x, mod, cst, pst, w_in, conv_w, sgu_g, sgu_b, wst, bs_full, pool_wbd,
                     pool_scale, w_out, ln_g, ln_b, *, alpha, layer, x_batch_major):
    if x_batch_major:
        b_dim, t_dim, c_dim = x.shape
    else:
        t_dim, b_dim, c_dim = x.shape
    w_gmlp = sgu_g.shape[1]
    kern = functools.partial(_mix_sample_kernel, alpha=alpha, layer=layer,
                             x_batch_major=x_batch_major)
    whole3 = lambda i: (0, 0, 0)
    return pl.pallas_call(
        kern,
        grid=(1,),
        in_specs=[
            pl.BlockSpec(x.shape, whole3),
            pl.BlockSpec((None, b_dim, mod.shape[2] // 2), lambda i: (layer, 0, 0)),
            _layer_block(cst.shape[1:], layer),
            _layer_block(pst.shape[1:], layer),
            _layer_block(w_in.shape[1:], layer),
            _whole(conv_w.shape),
            _whole(sgu_g.shape),
            _whole(sgu_b.shape),
            _layer_block(wst.shape[1:], layer),
            pl.BlockSpec((None, t_dim, w_gmlp), lambda i: (layer, 0, 0)),
            _layer_block(pool_wbd.shape[1:], layer),
            _whole(pool_scale.shape),
            _layer_block(w_out.shape[1:], layer),
            _whole(ln_g.shape),
            _whole(ln_b.shape),
        ],
        out_specs=[
            pl.BlockSpec((t_dim, b_dim, c_dim), whole3),
            pl.BlockSpec(cst.shape[1:], whole3),
            pl.BlockSpec(pst.shape[1:], whole3),
            pl.BlockSpec((b_dim, t_dim, w_gmlp), whole3),
        ],
        out_shape=[
            jax.ShapeDtypeStruct((t_dim, b_dim, c_dim), F32),
            jax.ShapeDtypeStruct(cst.shape[1:], F32),
            jax.ShapeDtypeStruct(pst.shape[1:], F32),
            jax.ShapeDtypeStruct((b_dim, t_dim, w_gmlp), F32),
        ],
        compiler_params=pltpu.CompilerParams(
            dimension_semantics=("arbitrary",),
            vmem_limit_bytes=VMEM_LIMIT_BYTES),
        name="mix_sample",
    )(x, mod, cst, pst, w_in, conv_w, sgu_g, sgu_b, wst, bs_full, pool_wbd, pool_scale, w_out,
      ln_g, ln_b)


def kernel(x_prompt, x_sample, c_prompt, c_sample, state_conv, state_pool, w_ada, b_ada, w_in,
           conv_w, sgu_g, sgu_b, w_s, b_s, pool_w, pool_scale, w_out, ln1_g, ln1_b, w_up,
           w_down, ln2_g, ln2_b):
    depth, d_model, _ = w_in.shape
    batch, seq, _ = x_prompt.shape
    dec_batch, dec_seq, _ = x_sample.shape
    n_groups, group_dim, _ = pool_w.shape[1:]
    w_pool = n_groups * group_dim
    alpha = (2.0 * depth) ** 0.25
    assert seq % PROMPT_ROWS == 0 and PROMPT_ROWS % SUB_ROWS == 0 and SUB_ROWS % CHUNK == 0
    assert (dec_seq * dec_batch) % SUB_ROWS == 0 and SUB_ROWS % dec_batch == 0

    mod_p, mod_s = _ada_call(c_prompt, c_sample, w_ada, b_ada)

    w_in_b, w_out_b = w_in.astype(BF16), w_out.astype(BF16)
    wup_b, wdn_b = w_up.astype(BF16), w_down.astype(BF16)
    bs_full = jnp.repeat(jnp.swapaxes(b_s, 1, 2), HEAD_DIM, axis=2)
    ws_small = jnp.repeat(jnp.transpose(w_s[:, :, :dec_seq, :dec_seq], (0, 2, 3, 1)), HEAD_DIM, axis=3)
    eye = jnp.eye(n_groups, dtype=F32)
    pool_wbd = (eye[None, :, None, :, None] * pool_w[:, :, :, None, :]).reshape(
        depth, w_pool, w_pool).astype(BF16)

    xp, xs = x_prompt, x_sample
    conv_p, pool_p, conv_s, pool_s, v_s = [], [], [], [], []
    for l in range(depth):
        xp, ncv, npl = _mix_prompt_call(
            xp, mod_p, w_in_b, conv_w, sgu_g, sgu_b, w_s, bs_full, pool_wbd, pool_scale,
            w_out_b, ln1_g, ln1_b, alpha=alpha, layer=l, rows=PROMPT_ROWS)
        conv_p.append(ncv)
        pool_p.append(npl)
        xp = _ffn_call(
            xp, mod_p, wup_b, wdn_b, ln2_g, ln2_b, layer=l,
            x_block=(1, PROMPT_ROWS, d_model), x_index=lambda b, t: (b, t, 0),
            mod_row_axis=0, row_axis=1, grid=(batch, seq // PROMPT_ROWS), alpha=alpha,
            name="ffn_prompt")

        xs, ncv, npl, v_rows = _mix_sample_call(
            xs, mod_s, state_conv, state_pool, w_in_b, conv_w, sgu_g, sgu_b, ws_small, bs_full,
            pool_wbd, pool_scale, w_out_b, ln1_g, ln1_b, alpha=alpha, layer=l,
            x_batch_major=(l == 0))
        conv_s.append(ncv)
        pool_s.append(npl)
        v_s.append(v_rows)
        xs = _ffn_call(
            xs, mod_s, wup_b, wdn_b, ln2_g, ln2_b, layer=l,
            x_block=(dec_seq, dec_batch, d_model), x_index=lambda i: (0, 0, 0),
            mod_row_axis=None, row_axis=0, grid=(1,), alpha=alpha, name="ffn_sample",
            out_batch_major=(l == depth - 1))

    return (xp, xs, jnp.stack(conv_p), jnp.stack(pool_p), jnp.stack(conv_s), jnp.stack(pool_s),
            jnp.stack(v_s))
```

```python
import functools

import jax
import jax.numpy as jnp
from jax import lax
from jax.experimental import pallas as pl
from jax.experimental.pallas import tpu as pltpu

F32 = jnp.float32
BF16 = jnp.bfloat16

LN_EPS = 1e-5
HEAD_DIM = 64
CHUNK = 128
POOL_WINDOWS = (2, 4, 8, 16)
CONV_WIDTH = 3
LANES = 128
SUBLANES = 8
VMEM_LIMIT_BYTES = 56 * 1024 * 1024

PROMPT_ROWS = 1024
SUB_ROWS = 256
UP_AHEAD = 2
PROJ_AHEAD = 2
WEIGHT_COLS = 512
CONV_HIST = SUBLANES
POOL_HIST = 2 * SUBLANES


def _layer_norm(y, g, b):
    mu = jnp.mean(y, axis=-1, keepdims=True)
    d = y - mu
    var = jnp.mean(d * d, axis=-1, keepdims=True)
    return d * lax.rsqrt(var + LN_EPS) * g + b


def _dot(a, b):
    return jnp.dot(a, b, preferred_element_type=F32)


def _layer_block(shape, layer):
    zeros = (0,) * len(shape)
    return pl.BlockSpec((None,) + tuple(shape), lambda *_: (layer,) + zeros,
                        pipeline_mode=pl.Buffered(1))


def _whole(shape):
    zeros = (0,) * len(shape)
    return pl.BlockSpec(tuple(shape), lambda *_: zeros, pipeline_mode=pl.Buffered(1))


def _column_blocks(w, layer, width):
    rows, cols = w.shape[1:]
    return [pl.BlockSpec((None, rows, width), lambda *_, n=n: (layer, 0, n),
                         pipeline_mode=pl.Buffered(1)) for n in range(cols // width)]


def _lane_group(shape, n_groups):
    return lax.broadcasted_iota(jnp.int32, shape, len(shape) - 1) // (shape[-1] // n_groups)


def _select_by_group(values, lane_group):
    out = values[-1]
    for g in range(len(values) - 2, -1, -1):
        out = jnp.where(lane_group == g, values[g], out)
    return out


def _ada_kernel(cp_ref, cs_ref, w_ref, b_ref, op_ref, os_ref):
    l, k = pl.program_id(0), pl.program_id(1)
    w = w_ref[0].astype(BF16)
    part_p = _dot(jax.nn.silu(cp_ref[...]).astype(BF16), w)
    part_s = _dot(jax.nn.silu(cs_ref[...]).astype(BF16), w)

    @pl.when(k == 0)
    def _():
        bias = b_ref[pl.ds(l, 1), :]
        op_ref[0] = part_p + bias
        os_ref[0] = part_s + bias

    @pl.when(k > 0)
    def _():
        op_ref[0] += part_p
        os_ref[0] += part_s


def _ada_call(c_prompt, c_sample, w_ada, b_ada, row_tile=256):
    depth, d_model, n_mod = w_ada.shape
    bp, bs = c_prompt.shape[0], c_sample.shape[0]
    return pl.pallas_call(
        _ada_kernel,
        grid=(depth, d_model // row_tile),
        in_specs=[
            pl.BlockSpec((bp, row_tile), lambda l, k: (0, k)),
            pl.BlockSpec((bs, row_tile), lambda l, k: (0, k)),
            pl.BlockSpec((1, row_tile, n_mod), lambda l, k: (l, k, 0)),
            _whole(b_ada.shape),
        ],
        out_specs=[
            pl.BlockSpec((1, bp, n_mod), lambda l, k: (l, 0, 0)),
            pl.BlockSpec((1, bs, n_mod), lambda l, k: (l, 0, 0)),
        ],
        out_shape=[
            jax.ShapeDtypeStruct((depth, bp, n_mod), F32),
            jax.ShapeDtypeStruct((depth, bs, n_mod), F32),
        ],
        compiler_params=pltpu.CompilerParams(
            dimension_semantics=("arbitrary", "arbitrary"),
            vmem_limit_bytes=VMEM_LIMIT_BYTES),
        name="ada_mod",
    )(c_prompt, c_sample, w_ada, b_ada)


def _ffn_kernel(x_ref, mod_ref, *refs, alpha, layer, row_axis, mod_row_axis, n_up, n_dn,
                out_batch_major):
    wup_refs, wdn_refs = refs[:n_up], refs[n_up:n_up + n_dn]
    g_ref, b_ref, o_ref = refs[n_up + n_dn:]
    g_dim, r_dim, c_dim = x_ref.shape
    chunk = wup_refs[0].shape[1]
    if mod_row_axis is None:
        m = mod_ref[...]
    else:
        m = mod_ref[pl.ds(pl.program_id(mod_row_axis), 1), :]
    sh, sc1, gate = m[:, :c_dim], 1.0 + m[:, c_dim:2 * c_dim], m[:, 2 * c_dim:]
    ln_g, ln_b = g_ref[layer:layer + 1, :], b_ref[layer:layer + 1, :]
    step = SUB_ROWS if row_axis == 1 else SUB_ROWS // r_dim
    n_sub = (r_dim if row_axis == 1 else g_dim) // step
    def rows_of(j):
        return (slice(None), slice(j * step, (j + 1) * step)) if row_axis == 1 else (
            slice(j * step, (j + 1) * step),)

    stages = [(j, k) for j in range(n_sub) for k in range(n_up)]
    xs, hs, ups = {}, {}, {}

    def issue_up(s):
        j, k = stages[s]
        if j not in hs:
            xs[j] = x_ref[rows_of(j)]
            hs[j] = (xs[j] * sc1 + sh).reshape(-1, c_dim).astype(BF16)
        ups[s] = _dot(hs[j], wup_refs[k][...])

    for s in range(min(UP_AHEAD, len(stages))):
        issue_up(s)
    outs = []
    acc = [None] * n_dn
    for s, (j, k) in enumerate(stages):
        if s + UP_AHEAD < len(stages):
            issue_up(s + UP_AHEAD)
        a = jnp.square(jnp.maximum(ups.pop(s), 0.0)).astype(BF16)
        for n in range(n_dn):
            part = _dot(a, wdn_refs[n][k * chunk:(k + 1) * chunk, :])
            acc[n] = part if acc[n] is None else acc[n] + part
        if k == n_up - 1:
            x = xs.pop(j)
            f = jnp.concatenate(acc, axis=1).reshape(x.shape)
            y = _layer_norm(alpha * x + gate * f, ln_g, ln_b)
            acc = [None] * n_dn
            if out_batch_major:
                outs.append(y)
            else:
                o_ref[rows_of(j)] = y
    if out_batch_major:
        o_ref[...] = jnp.swapaxes(jnp.concatenate(outs, axis=0), 0, 1)


def _ffn_call(x, mod, wup, wdn, ln_g, ln_b, *, layer, x_block, x_index, mod_row_axis,
              row_axis, grid, alpha, name, out_batch_major=False):
    n_mod = mod.shape[-1]
    up_specs = _column_blocks(wup, layer, WEIGHT_COLS)
    dn_specs = _column_blocks(wdn, layer, WEIGHT_COLS)
    kern = functools.partial(_ffn_kernel, alpha=alpha, layer=layer, row_axis=row_axis,
                             mod_row_axis=mod_row_axis, n_up=len(up_specs), n_dn=len(dn_specs),
                             out_batch_major=out_batch_major)
    out_shape = (x.shape[1], x.shape[0], x.shape[2]) if out_batch_major else x.shape
    out_block = out_shape if out_batch_major else x_block
    return pl.pallas_call(
        kern,
        grid=grid,
        in_specs=[
            pl.BlockSpec(x_block, x_index),
            pl.BlockSpec((None, mod.shape[1], n_mod // 2), lambda *_: (layer, 0, 1)),
            *up_specs, *dn_specs,
            _whole(ln_g.shape),
            _whole(ln_b.shape),
        ],
        out_specs=pl.BlockSpec(out_block, x_index),
        out_shape=jax.ShapeDtypeStruct(out_shape, F32),
        compiler_params=pltpu.CompilerParams(
            dimension_semantics=("arbitrary",) * len(grid),
            vmem_limit_bytes=VMEM_LIMIT_BYTES),
        name=name,
    )(x, mod, *([wup] * len(up_specs)), *([wdn] * len(dn_specs)), ln_g, ln_b)


def _mix_prompt_kernel(x_ref, mod_ref, *refs, alpha, layer, n_in, n_out):
    win_refs, refs = refs[:n_in], refs[n_in:]
    cw_ref, sg_ref, sb_ref, ws_ref, bs_ref, pw_ref, ps_ref = refs[:7]
    wout_refs, refs = refs[7:7 + n_out], refs[7 + n_out:]
    g_ref, b_ref, o_ref, ncv_ref, npl_ref, zc_ref, pc_ref = refs
    bi, t = pl.program_id(0), pl.program_id(1)
    rows, c_dim = x_ref.shape[1], x_ref.shape[2]
    w_conv, w_gmlp, w_pool = cw_ref.shape[2], sg_ref.shape[1], ps_ref.shape[1]
    n_heads = ws_ref.shape[0]

    @pl.when(t == 0)
    def _():
        zc_ref[...] = jnp.zeros(zc_ref.shape, F32)
        pc_ref[...] = jnp.zeros(pc_ref.shape, F32)

    m = mod_ref[pl.ds(bi, 1), :]
    sh, sc1, gate = m[:, :c_dim], 1.0 + m[:, c_dim:2 * c_dim], m[:, 2 * c_dim:]
    cw = cw_ref[layer]
    sgu_g, sgu_b = sg_ref[layer:layer + 1, :], sb_ref[layer:layer + 1, :]
    pool_scale = ps_ref[layer:layer + 1, :]
    ln_g, ln_b = g_ref[layer:layer + 1, :], b_ref[layer:layer + 1, :]

    row_i = lax.broadcasted_iota(jnp.int32, (CHUNK, 2 * CHUNK), 0)
    col_j = lax.broadcasted_iota(jnp.int32, (CHUNK, 2 * CHUNK), 1) % CHUNK
    causal = col_j <= row_i
    ws_m = []
    for p in range(n_heads // 2):
        pair = jnp.concatenate([ws_ref[2 * p], ws_ref[2 * p + 1]], axis=1)
        ws_m.append(jnp.where(causal, pair, 0.0).astype(BF16))
    low_half = lax.broadcasted_iota(jnp.int32, (CHUNK, LANES), 1) < HEAD_DIM
    lane_group = _lane_group((SUB_ROWS, w_pool), len(POOL_WINDOWS))
    window = _select_by_group([jnp.full((SUB_ROWS, w_pool), w, jnp.int32) for w in POOL_WINDOWS],
                              lane_group)
    row_id = lax.broadcasted_iota(jnp.int32, (SUB_ROWS, w_pool), 0)
    bias = bs_ref[...]

    z_hist = zc_ref[...]
    p_hist = pc_ref[...]

    def project(jj):
        h = (x_ref[0, jj * SUB_ROWS:(jj + 1) * SUB_ROWS, :] * sc1 + sh).astype(BF16)
        return jnp.concatenate([_dot(h, w[...]) for w in win_refs], axis=1)

    n_sub = rows // SUB_ROWS
    projs = [project(jj) for jj in range(min(PROJ_AHEAD, n_sub))]
    for j in range(n_sub):
        r0 = j * SUB_ROWS
        x = x_ref[0, r0:r0 + SUB_ROWS, :]
        proj = projs[j]
        o0 = 0
        a_b = proj[:, o0:o0 + w_conv]; o0 += w_conv
        a_c = proj[:, o0:o0 + w_conv]; o0 += w_conv
        a_x = proj[:, o0:o0 + w_conv]; o0 += w_conv
        g_u = proj[:, o0:o0 + w_gmlp]; o0 += w_gmlp
        g_v = proj[:, o0:o0 + w_gmlp]; o0 += w_gmlp
        p_x = proj[:, o0:o0 + w_pool]

        z = a_c * a_x
        z_ext = jnp.concatenate([z_hist, z], axis=0)
        conv = z_ext * cw[CONV_WIDTH - 1:CONV_WIDTH, :]
        for k in range(1, CONV_WIDTH):
            conv = conv + pltpu.roll(z_ext, k, 0) * cw[CONV_WIDTH - 1 - k:CONV_WIDTH - k, :]
        y_a = a_b * conv[CONV_HIST:, :]
        z_hist = z[SUB_ROWS - CONV_HIST:, :]

        v_n = _layer_norm(g_v, sgu_g, sgu_b)
        n_ch = SUB_ROWS // CHUNK
        s_cols = []
        for p in range(n_heads // 2):
            cols = [v_n[c * CHUNK:(c + 1) * CHUNK, p * LANES:(p + 1) * LANES] for c in range(n_ch)]
            rhs = jnp.concatenate(
                [jnp.concatenate([jnp.where(low_half, col, 0.0) for col in cols], axis=1),
                 jnp.concatenate([jnp.where(low_half, 0.0, col) for col in cols], axis=1)],
                axis=0).astype(BF16)
            s_cols.append(_dot(ws_m[p], rhs))
        s = jnp.concatenate(
            [jnp.concatenate([sc[:, c * LANES:(c + 1) * LANES] for sc in s_cols], axis=1) + bias
             for c in range(n_ch)], axis=0)
        run = jnp.concatenate([p_hist, p_x], axis=0)
        sums = []
        for w in POOL_WINDOWS:
            run = run + pltpu.roll(run, w // 2, 0)
            sums.append(run[POOL_HIST:, :])
        count = jnp.minimum(window, t * rows + r0 + row_id + 1).astype(F32)
        d = (_select_by_group(sums, lane_group) / count - p_x).astype(BF16)
        y_c_raw = _dot(d, pw_ref[...])
        p_hist = p_x[SUB_ROWS - POOL_HIST:, :]

        if j + PROJ_AHEAD < n_sub:
            projs.append(project(j + PROJ_AHEAD))

        y_b = g_u * s
        y_c = y_c_raw * pool_scale
        mix = jnp.concatenate([y_a, y_b, y_c], axis=1).astype(BF16)
        out = jnp.concatenate([_dot(mix, w[...]) for w in wout_refs], axis=1)
        o_ref[0, r0:r0 + SUB_ROWS, :] = _layer_norm(alpha * x + gate * out, ln_g, ln_b)

    zc_ref[...] = z_hist
    pc_ref[...] = p_hist
    ncv_ref[0] = z_hist[CONV_HIST - (CONV_WIDTH - 1):, :]
    npl_ref[0] = p_hist[POOL_HIST - (max(POOL_WINDOWS) - 1):, :]


def _mix_prompt_call(x, mod, w_in, conv_w, sgu_g, sgu_b, w_s, bs_full, pool_wbd, pool_scale,
                     w_out, ln_g, ln_b, *, alpha, layer, rows):
    batch, seq, c_dim = x.shape
    w_conv, w_pool = conv_w.shape[2], pool_scale.shape[1]
    conv_buf = CONV_WIDTH - 1
    pool_buf = max(POOL_WINDOWS) - 1
    in_specs = _column_blocks(w_in, layer, WEIGHT_COLS)
    out_specs = _column_blocks(w_out, layer, WEIGHT_COLS)
    kern = functools.partial(_mix_prompt_kernel, alpha=alpha, layer=layer, n_in=len(in_specs),
                             n_out=len(out_specs))
    return pl.pallas_call(
        kern,
        grid=(batch, seq // rows),
        in_specs=[
            pl.BlockSpec((1, rows, c_dim), lambda b, t: (b, t, 0)),
            pl.BlockSpec((None, batch, mod.shape[2] // 2), lambda b, t: (layer, 0, 0)),
            *in_specs,
            _whole(conv_w.shape),
            _whole(sgu_g.shape),
            _whole(sgu_b.shape),
            _layer_block(w_s.shape[1:], layer),
            _layer_block(bs_full.shape[1:], layer),
            _layer_block(pool_wbd.shape[1:], layer),
            _whole(pool_scale.shape),
            *out_specs,
            _whole(ln_g.shape),
            _whole(ln_b.shape),
        ],
        out_specs=[
            pl.BlockSpec((1, rows, c_dim), lambda b, t: (b, t, 0)),
            pl.BlockSpec((1, conv_buf, w_conv), lambda b, t: (b, 0, 0)),
            pl.BlockSpec((1, pool_buf, w_pool), lambda b, t: (b, 0, 0)),
        ],
        out_shape=[
            jax.ShapeDtypeStruct((batch, seq, c_dim), F32),
            jax.ShapeDtypeStruct((batch, conv_buf, w_conv), F32),
            jax.ShapeDtypeStruct((batch, pool_buf, w_pool), F32),
        ],
        scratch_shapes=[
            pltpu.VMEM((CONV_HIST, w_conv), F32),
            pltpu.VMEM((POOL_HIST, w_pool), F32),
        ],
        compiler_params=pltpu.CompilerParams(
            dimension_semantics=("arbitrary", "arbitrary"),
            vmem_limit_bytes=VMEM_LIMIT_BYTES),
        name="mix_prompt",
    )(x, mod, *([w_in] * len(in_specs)), conv_w, sgu_g, sgu_b, w_s, bs_full, pool_wbd, pool_scale,
      *([w_out] * len(out_specs)), ln_g, ln_b)


def _mix_sample_kernel(x_ref, mod_ref, cst_ref, pst_ref, win_ref, cw_ref, sg_ref, sb_ref,
                       wst_ref, bst_ref, pw_ref, ps_ref, wout_ref, g_ref, b_ref,
                       o_ref, ncv_ref, npl_ref, v_ref, *, alpha, layer, x_batch_major):
    x = jnp.swapaxes(x_ref[...], 0, 1) if x_batch_major else x_ref[...]
    cst = jnp.swapaxes(cst_ref[...], 0, 1)
    pst = jnp.swapaxes(pst_ref[...], 0, 1)
    t_dim, b_dim, c_dim = x.shape
    w_conv, w_gmlp, w_pool = cw_ref.shape[2], sg_ref.shape[1], ps_ref.shape[1]
    n_valid = pst.shape[0]
    m = mod_ref[...]
    sh, sc1, gate = m[:, :c_dim], 1.0 + m[:, c_dim:2 * c_dim], m[:, 2 * c_dim:]
    h = (x * sc1 + sh).reshape(t_dim * b_dim, c_dim).astype(BF16)
    proj = _dot(h, win_ref[...]).reshape(t_dim, b_dim, -1)
    o0 = 0
    a_b = proj[:, :, o0:o0 + w_conv]; o0 += w_conv
    a_c = proj[:, :, o0:o0 + w_conv]; o0 += w_conv
    a_x = proj[:, :, o0:o0 + w_conv]; o0 += w_conv
    g_u = proj[:, :, o0:o0 + w_gmlp]; o0 += w_gmlp
    g_v = proj[:, :, o0:o0 + w_gmlp]; o0 += w_gmlp
    p_x = proj[:, :, o0:o0 + w_pool]

    z = a_c * a_x
    ext = [cst[i] for i in range(CONV_WIDTH - 1)] + [z[i] for i in range(t_dim)]
    cw = cw_ref[layer]
    y_a = []
    for i in range(t_dim):
        conv = ext[i] * cw[0:1, :]
        for k in range(1, CONV_WIDTH):
            conv = conv + ext[i + k] * cw[k:k + 1, :]
        y_a.append(a_b[i] * conv)
    ncv_ref[...] = jnp.swapaxes(jnp.stack(ext[len(ext) - (CONV_WIDTH - 1):]), 0, 1)

    v_n = _layer_norm(g_v, sg_ref[layer:layer + 1, :], sb_ref[layer:layer + 1, :])
    v_ref[...] = jnp.swapaxes(v_n, 0, 1)
    y_b = []
    for i in range(t_dim):
        s = bst_ref[i:i + 1, :] + wst_ref[i, 0:1, :] * v_n[0]
        for j in range(1, i + 1):
            s = s + wst_ref[i, j:j + 1, :] * v_n[j]
        y_b.append(g_u[i] * s)

    pext = [pst[i] for i in range(n_valid)] + [p_x[i] for i in range(t_dim)]
    lane_group = _lane_group((b_dim, w_pool), len(POOL_WINDOWS))
    d = []
    for i in range(t_dim):
        last = n_valid + i
        means = []
        run = None
        k = 0
        for w in POOL_WINDOWS:
            while k < w:
                run = pext[last - k] if run is None else run + pext[last - k]
                k += 1
            means.append(run / float(min(w, i + 1 + n_valid)))
        d.append(_select_by_group(means, lane_group) - p_x[i])
    d = jnp.stack(d).reshape(t_dim * b_dim, w_pool).astype(BF16)
    y_c = (_dot(d, pw_ref[...]) * ps_ref[layer:layer + 1, :]).reshape(t_dim, b_dim, w_pool)
    npl_ref[...] = jnp.swapaxes(jnp.stack(pext[len(pext) - n_valid:]), 0, 1)

    mix = jnp.concatenate([jnp.stack(y_a), jnp.stack(y_b), y_c], axis=2)
    mix = mix.reshape(t_dim * b_dim, c_dim).astype(BF16)
    y = alpha * x + gate * _dot(mix, wout_ref[...]).reshape(t_dim, b_dim, c_dim)
    o_ref[...] = _layer_norm(y, g_ref[layer:layer + 1, :], b_ref[layer:layer + 1, :])


def _mix_sample_call(x, mod, cst, pst, w_in, conv_w, sgu_g, sgu_b, wst, bs_full, pool_wbd,
                     pool_scale, w_out, ln_g, ln_b, *, alpha, layer, x_batch_major):
    if x_batch_major:
        b_dim, t_dim, c_dim = x.shape
    else:
        t_dim, b_dim, c_dim = x.shape
    w_gmlp = sgu_g.shape[1]
    kern = functools.partial(_mix_sample_kernel, alpha=alpha, layer=layer,
                             x_batch_major=x_batch_major)
    whole3 = lambda i: (0, 0, 0)
    return pl.pallas_call(
        kern,
        grid=(1,),
        in_specs=[
            pl.BlockSpec(x.shape, whole3),
            pl.BlockSpec((None, b_dim, mod.shape[2] // 2), lambda i: (layer, 0, 0)),
            _layer_block(cst.shape[1:], layer),
            _layer_block(pst.shape[1:], layer),
            _layer_block(w_in.shape[1:], layer),
            _whole(conv_w.shape),
            _whole(sgu_g.shape),
            _whole(sgu_b.shape),
            _layer_block(wst.shape[1:], layer),
            pl.BlockSpec((None, t_dim, w_gmlp), lambda i: (layer, 0, 0)),
            _layer_block(pool_wbd.shape[1:], layer),
            _whole(pool_scale.shape),
            _layer_block(w_out.shape[1:], layer),
            _whole(ln_g.shape),
            _whole(ln_b.shape),
        ],
        out_specs=[
            pl.BlockSpec((t_dim, b_dim, c_dim), whole3),
            pl.BlockSpec(cst.shape[1:], whole3),
            pl.BlockSpec(pst.shape[1:], whole3),
            pl.BlockSpec((b_dim, t_dim, w_gmlp), whole3),
        ],
        out_shape=[
            jax.ShapeDtypeStruct((t_dim, b_dim, c_dim), F32),
            jax.ShapeDtypeStruct(cst.shape[1:], F32),
            jax.ShapeDtypeStruct(pst.shape[1:], F32),
            jax.ShapeDtypeStruct((b_dim, t_dim, w_gmlp), F32),
        ],
        compiler_params=pltpu.CompilerParams(
            dimension_semantics=("arbitrary",),
            vmem_limit_bytes=VMEM_LIMIT_BYTES),
        name="mix_sample",
    )(x, mod, cst, pst, w_in, conv_w, sgu_g, sgu_b, wst, bs_full, pool_wbd, pool_scale, w_out,
      ln_g, ln_b)


def kernel(x_prompt, x_sample, c_prompt, c_sample, state_conv, state_pool, w_ada, b_ada, w_in,
           conv_w, sgu_g, sgu_b, w_s, b_s, pool_w, pool_scale, w_out, ln1_g, ln1_b, w_up,
           w_down, ln2_g, ln2_b):
    depth, d_model, _ = w_in.shape
    batch, seq, _ = x_prompt.shape
    dec_batch, dec_seq, _ = x_sample.shape
    n_groups, group_dim, _ = pool_w.shape[1:]
    w_pool = n_groups * group_dim
    alpha = (2.0 * depth) ** 0.25
    assert seq % PROMPT_ROWS == 0 and PROMPT_ROWS % SUB_ROWS == 0 and SUB_ROWS % CHUNK == 0
    assert (dec_seq * dec_batch) % SUB_ROWS == 0 and SUB_ROWS % dec_batch == 0

    mod_p, mod_s = _ada_call(c_prompt, c_sample, w_ada, b_ada)

    w_in_b, w_out_b = w_in.astype(BF16), w_out.astype(BF16)
    wup_b, wdn_b = w_up.astype(BF16), w_down.astype(BF16)
    bs_full = jnp.repeat(jnp.swapaxes(b_s, 1, 2), HEAD_DIM, axis=2)
    ws_small = jnp.repeat(jnp.transpose(w_s[:, :, :dec_seq, :dec_seq], (0, 2, 3, 1)), HEAD_DIM, axis=3)
    eye = jnp.eye(n_groups, dtype=F32)
    pool_wbd = (eye[None, :, None, :, None] * pool_w[:, :, :, None, :]).reshape(
        depth, w_pool, w_pool).astype(BF16)

    xp, xs = x_prompt, x_sample
    conv_p, pool_p, conv_s, pool_s, v_s = [], [], [], [], []
    for l in range(depth):
        xp, ncv, npl = _mix_prompt_call(
            xp, mod_p, w_in_b, conv_w, sgu_g, sgu_b, w_s, bs_full, pool_wbd, pool_scale,
            w_out_b, ln1_g, ln1_b, alpha=alpha, layer=l, rows=PROMPT_ROWS)
        conv_p.append(ncv)
        pool_p.append(npl)
        xp = _ffn_call(
            xp, mod_p, wup_b, wdn_b, ln2_g, ln2_b, layer=l,
            x_block=(1, PROMPT_ROWS, d_model), x_index=lambda b, t: (b, t, 0),
            mod_row_axis=0, row_axis=1, grid=(batch, seq // PROMPT_ROWS), alpha=alpha,
            name="ffn_prompt")

        xs, ncv, npl, v_rows = _mix_sample_call(
            xs, mod_s, state_conv, state_pool, w_in_b, conv_w, sgu_g, sgu_b, ws_small, bs_full,
            pool_wbd, pool_scale, w_out_b, ln1_g, ln1_b, alpha=alpha, layer=l,
            x_batch_major=(l == 0))
        conv_s.append(ncv)
        pool_s.append(npl)
        v_s.append(v_rows)
        xs = _ffn_call(
            xs, mod_s, wup_b, wdn_b, ln2_g, ln2_b, layer=l,
            x_block=(dec_seq, dec_batch, d_model), x_index=lambda i: (0, 0, 0),
            mod_row_axis=None, row_axis=0, grid=(1,), alpha=alpha, name="ffn_sample",
            out_batch_major=(l == depth - 1))

    return (xp, xs, jnp.stack(conv_p), jnp.stack(pool_p), jnp.stack(conv_s), jnp.stack(pool_s),
            jnp.stack(v_s))
```

```python
import functools

import jax
import jax.numpy as jnp
from jax import lax
from jax.experimental import pallas as pl
from jax.experimental.pallas import tpu as pltpu

F32 = jnp.float32
BF16 = jnp.bfloat16

LN_EPS = 1e-5
HEAD_DIM = 64
CHUNK = 128
POOL_WINDOWS = (2, 4, 8, 16)
CONV_WIDTH = 3
LANES = 128
SUBLANES = 8
VMEM_LIMIT_BYTES = 56 * 1024 * 1024

PROMPT_ROWS = 1024
SUB_ROWS = 256
UP_AHEAD = 2
PROJ_AHEAD = 2
HEADS_AT = 2
MIX_OUT_AT = 5
WEIGHT_COLS = 512
CONV_HIST = SUBLANES
POOL_HIST = 2 * SUBLANES


def _layer_norm(y, g, b):
    mu = jnp.mean(y, axis=-1, keepdims=True)
    d = y - mu
    var = jnp.mean(d * d, axis=-1, keepdims=True)
    return d * lax.rsqrt(var + LN_EPS) * g + b


def _dot(a, b):
    return jnp.dot(a, b, preferred_element_type=F32)


def _layer_block(shape, layer):
    zeros = (0,) * len(shape)
    return pl.BlockSpec((None,) + tuple(shape), lambda *_: (layer,) + zeros,
                        pipeline_mode=pl.Buffered(1))


def _whole(shape):
    zeros = (0,) * len(shape)
    return pl.BlockSpec(tuple(shape), lambda *_: zeros, pipeline_mode=pl.Buffered(1))


def _column_blocks(w, layer, width):
    rows, cols = w.shape[1:]
    return [pl.BlockSpec((None, rows, width), lambda *_, n=n: (layer, 0, n),
                         pipeline_mode=pl.Buffered(1)) for n in range(cols // width)]


def _lane_group(shape, n_groups):
    return lax.broadcasted_iota(jnp.int32, shape, len(shape) - 1) // (shape[-1] // n_groups)


def _select_by_group(values, lane_group):
    out = values[-1]
    for g in range(len(values) - 2, -1, -1):
        out = jnp.where(lane_group == g, values[g], out)
    return out


def _ada_kernel(cp_ref, cs_ref, w_ref, b_ref, op_ref, os_ref):
    l, k = pl.program_id(0), pl.program_id(1)
    w = w_ref[0].astype(BF16)
    part_p = _dot(jax.nn.silu(cp_ref[...]).astype(BF16), w)
    part_s = _dot(jax.nn.silu(cs_ref[...]).astype(BF16), w)

    @pl.when(k == 0)
    def _():
        bias = b_ref[pl.ds(l, 1), :]
        op_ref[0] = part_p + bias
        os_ref[0] = part_s + bias

    @pl.when(k > 0)
    def _():
        op_ref[0] += part_p
        os_ref[0] += part_s


def _ada_call(c_prompt, c_sample, w_ada, b_ada, row_tile=256):
    depth, d_model, n_mod = w_ada.shape
    bp, bs = c_prompt.shape[0], c_sample.shape[0]
    return pl.pallas_call(
        _ada_kernel,
        grid=(depth, d_model // row_tile),
        in_specs=[
            pl.BlockSpec((bp, row_tile), lambda l, k: (0, k)),
            pl.BlockSpec((bs, row_tile), lambda l, k: (0, k)),
            pl.BlockSpec((1, row_tile, n_mod), lambda l, k: (l, k, 0)),
            _whole(b_ada.shape),
        ],
        out_specs=[
            pl.BlockSpec((1, bp, n_mod), lambda l, k: (l, 0, 0)),
            pl.BlockSpec((1, bs, n_mod), lambda l, k: (l, 0, 0)),
        ],
        out_shape=[
            jax.ShapeDtypeStruct((depth, bp, n_mod), F32),
            jax.ShapeDtypeStruct((depth, bs, n_mod), F32),
        ],
        compiler_params=pltpu.CompilerParams(
            dimension_semantics=("arbitrary", "arbitrary"),
            vmem_limit_bytes=VMEM_LIMIT_BYTES),
        name="ada_mod",
    )(c_prompt, c_sample, w_ada, b_ada)


def _ffn_sample_kernel(x_ref, mod_ref, *refs, alpha, layer, n_up, n_dn, out_batch_major):
    wup_refs, wdn_refs = refs[:n_up], refs[n_up:n_up + n_dn]
    g_ref, b_ref, o_ref = refs[n_up + n_dn:]
    t_dim, b_dim, c_dim = x_ref.shape
    chunk = wup_refs[0].shape[1]
    m = mod_ref[...]
    sh, sc1, gate = m[:, :c_dim], 1.0 + m[:, c_dim:2 * c_dim], m[:, 2 * c_dim:]
    ln_g, ln_b = g_ref[layer:layer + 1, :], b_ref[layer:layer + 1, :]
    step = SUB_ROWS // b_dim
    n_sub = t_dim // step

    stages = [(j, k) for j in range(n_sub) for k in range(n_up)]
    xs, hs, ups = {}, {}, {}

    def issue_up(s):
        j, k = stages[s]
        if j not in hs:
            xs[j] = x_ref[j * step:(j + 1) * step]
            hs[j] = (xs[j] * sc1 + sh).reshape(-1, c_dim).astype(BF16)
        ups[s] = _dot(hs[j], wup_refs[k][...])

    for s in range(min(UP_AHEAD, len(stages))):
        issue_up(s)
    outs = []
    acc = [None] * n_dn
    for s, (j, k) in enumerate(stages):
        if s + UP_AHEAD < len(stages):
            issue_up(s + UP_AHEAD)
        a = jnp.square(jnp.maximum(ups.pop(s), 0.0)).astype(BF16)
        for n in range(n_dn):
            part = _dot(a, wdn_refs[n][k * chunk:(k + 1) * chunk, :])
            acc[n] = part if acc[n] is None else acc[n] + part
        if k == n_up - 1:
            x = xs.pop(j)
            f = jnp.concatenate(acc, axis=1).reshape(x.shape)
            y = _layer_norm(alpha * x + gate * f, ln_g, ln_b)
            acc = [None] * n_dn
            if out_batch_major:
                outs.append(y)
            else:
                o_ref[j * step:(j + 1) * step] = y
    if out_batch_major:
        o_ref[...] = jnp.swapaxes(jnp.concatenate(outs, axis=0), 0, 1)


def _ffn_sample_call(x, mod, wup, wdn, ln_g, ln_b, *, layer, alpha, out_batch_major):
    n_mod = mod.shape[-1]
    up_specs = _column_blocks(wup, layer, WEIGHT_COLS)
    dn_specs = _column_blocks(wdn, layer, WEIGHT_COLS)
    kern = functools.partial(_ffn_sample_kernel, alpha=alpha, layer=layer, n_up=len(up_specs),
                             n_dn=len(dn_specs), out_batch_major=out_batch_major)
    out_shape = (x.shape[1], x.shape[0], x.shape[2]) if out_batch_major else x.shape
    whole3 = lambda i: (0, 0, 0)
    return pl.pallas_call(
        kern,
        grid=(1,),
        in_specs=[
            pl.BlockSpec(x.shape, whole3),
            pl.BlockSpec((None, mod.shape[1], n_mod // 2), lambda i: (layer, 0, 1)),
            *up_specs, *dn_specs,
            _whole(ln_g.shape),
            _whole(ln_b.shape),
        ],
        out_specs=pl.BlockSpec(out_shape, whole3),
        out_shape=jax.ShapeDtypeStruct(out_shape, F32),
        compiler_params=pltpu.CompilerParams(
            dimension_semantics=("arbitrary",),
            vmem_limit_bytes=VMEM_LIMIT_BYTES),
        name="ffn_sample",
    )(x, mod, *([wup] * len(up_specs)), *([wdn] * len(dn_specs)), ln_g, ln_b)


def _layer_prompt_kernel(x_ref, mod_ref, *refs, alpha, layer, n_in, n_out, n_up, n_dn):
    win_refs, refs = refs[:n_in], refs[n_in:]
    cw_ref, sg_ref, sb_ref, ws_ref, bs_ref, pw_ref, ps_ref = refs[:7]
    wout_refs, refs = refs[7:7 + n_out], refs[7 + n_out:]
    g_ref, b_ref = refs[:2]
    wup_refs, refs = refs[2:2 + n_up], refs[2 + n_up:]
    wdn_refs, refs = refs[:n_dn], refs[n_dn:]
    g2_ref, b2_ref, o_ref, ncv_ref, npl_ref, zc_ref, pc_ref = refs
    bi, t = pl.program_id(0), pl.program_id(1)
    rows, c_dim = x_ref.shape[1], x_ref.shape[2]
    w_conv, w_gmlp, w_pool = cw_ref.shape[2], sg_ref.shape[1], ps_ref.shape[1]
    n_heads = ws_ref.shape[0]
    chunk = wup_refs[0].shape[1]

    @pl.when(t == 0)
    def _():
        zc_ref[...] = jnp.zeros(zc_ref.shape, F32)
        pc_ref[...] = jnp.zeros(pc_ref.shape, F32)

    m = mod_ref[pl.ds(bi, 1), :]
    sh, sc1, gate = m[:, :c_dim], 1.0 + m[:, c_dim:2 * c_dim], m[:, 2 * c_dim:3 * c_dim]
    sh2, sc2, gate2 = (m[:, 3 * c_dim:4 * c_dim], 1.0 + m[:, 4 * c_dim:5 * c_dim],
                       m[:, 5 * c_dim:])
    cw = cw_ref[layer]
    sgu_g, sgu_b = sg_ref[layer:layer + 1, :], sb_ref[layer:layer + 1, :]
    pool_scale = ps_ref[layer:layer + 1, :]
    ln_g, ln_b = g_ref[layer:layer + 1, :], b_ref[layer:layer + 1, :]
    ln2_g, ln2_b = g2_ref[layer:layer + 1, :], b2_ref[layer:layer + 1, :]

    row_i = lax.broadcasted_iota(jnp.int32, (CHUNK, 2 * CHUNK), 0)
    col_j = lax.broadcasted_iota(jnp.int32, (CHUNK, 2 * CHUNK), 1) % CHUNK
    causal = col_j <= row_i
    ws_m = []
    for p in range(n_heads // 2):
        pair = jnp.concatenate([ws_ref[2 * p], ws_ref[2 * p + 1]], axis=1)
        ws_m.append(jnp.where(causal, pair, 0.0).astype(BF16))
    low_half = lax.broadcasted_iota(jnp.int32, (CHUNK, LANES), 1) < HEAD_DIM
    lane_group = _lane_group((SUB_ROWS, w_pool), len(POOL_WINDOWS))
    window = _select_by_group([jnp.full((SUB_ROWS, w_pool), w, jnp.int32) for w in POOL_WINDOWS],
                              lane_group)
    row_id = lax.broadcasted_iota(jnp.int32, (SUB_ROWS, w_pool), 0)
    bias = bs_ref[...]
    n_sub = rows // SUB_ROWS
    n_ch = SUB_ROWS // CHUNK
    hist = {"z": zc_ref[...], "p": pc_ref[...]}
    projs, mids, x_mid = {}, {}, {}

    def project(jj):
        h = (x_ref[0, jj * SUB_ROWS:(jj + 1) * SUB_ROWS, :] * sc1 + sh).astype(BF16)
        projs[jj] = jnp.concatenate([_dot(h, w[...]) for w in win_refs], axis=1)

    def heads(j):
        proj = projs.pop(j)
        o0 = 0
        a_b = proj[:, o0:o0 + w_conv]; o0 += w_conv
        a_c = proj[:, o0:o0 + w_conv]; o0 += w_conv
        a_x = proj[:, o0:o0 + w_conv]; o0 += w_conv
        g_u = proj[:, o0:o0 + w_gmlp]; o0 += w_gmlp
        g_v = proj[:, o0:o0 + w_gmlp]; o0 += w_gmlp
        p_x = proj[:, o0:o0 + w_pool]

        z = a_c * a_x
        z_ext = jnp.concatenate([hist["z"], z], axis=0)
        conv = z_ext * cw[CONV_WIDTH - 1:CONV_WIDTH, :]
        for k in range(1, CONV_WIDTH):
            conv = conv + pltpu.roll(z_ext, k, 0) * cw[CONV_WIDTH - 1 - k:CONV_WIDTH - k, :]
        y_a = a_b * conv[CONV_HIST:, :]
        hist["z"] = z[SUB_ROWS - CONV_HIST:, :]

        v_n = _layer_norm(g_v, sgu_g, sgu_b)
        s_cols = []
        for p in range(n_heads // 2):
            cols = [v_n[c * CHUNK:(c + 1) * CHUNK, p * LANES:(p + 1) * LANES] for c in range(n_ch)]
            rhs = jnp.concatenate(
                [jnp.concatenate([jnp.where(low_half, col, 0.0) for col in cols], axis=1),
                 jnp.concatenate([jnp.where(low_half, 0.0, col) for col in cols], axis=1)],
                axis=0).astype(BF16)
            s_cols.append(_dot(ws_m[p], rhs))

        run = jnp.concatenate([hist["p"], p_x], axis=0)
        sums = []
        for w in POOL_WINDOWS:
            run = run + pltpu.roll(run, w // 2, 0)
            sums.append(run[POOL_HIST:, :])
        count = jnp.minimum(window, t * rows + j * SUB_ROWS + row_id + 1).astype(F32)
        d = (_select_by_group(sums, lane_group) / count - p_x).astype(BF16)
        y_c_raw = _dot(d, pw_ref[...])
        hist["p"] = p_x[SUB_ROWS - POOL_HIST:, :]
        mids[j] = (y_a, g_u, s_cols, y_c_raw)

    def mix_out(j):
        y_a, g_u, s_cols, y_c_raw = mids.pop(j)
        s = jnp.concatenate(
            [jnp.concatenate([sc[:, c * LANES:(c + 1) * LANES] for sc in s_cols], axis=1) + bias
             for c in range(n_ch)], axis=0)
        mix = jnp.concatenate([y_a, g_u * s, y_c_raw * pool_scale], axis=1).astype(BF16)
        out = jnp.concatenate([_dot(mix, w[...]) for w in wout_refs], axis=1)
        x = x_ref[0, j * SUB_ROWS:(j + 1) * SUB_ROWS, :]
        x_mid[j] = _layer_norm(alpha * x + gate * out, ln_g, ln_b)

    for jj in range(min(PROJ_AHEAD, n_sub)):
        project(jj)
    heads(0)
    if PROJ_AHEAD < n_sub:
        project(PROJ_AHEAD)
    mix_out(0)

    stages = [(j, k) for j in range(n_sub) for k in range(n_up)]
    hs, ups = {}, {}

    def issue_up(s):
        j, k = stages[s]
        if j not in hs:
            hs[j] = (x_mid[j] * sc2 + sh2).astype(BF16)
        ups[s] = _dot(hs[j], wup_refs[k][...])

    for s in range(min(UP_AHEAD, len(stages))):
        issue_up(s)
    acc = [None] * n_dn
    for s, (j, k) in enumerate(stages):
        if k == HEADS_AT and j + 1 < n_sub:
            heads(j + 1)
            if j + 1 + PROJ_AHEAD < n_sub:
                project(j + 1 + PROJ_AHEAD)
        if k == MIX_OUT_AT and j + 1 < n_sub:
            mix_out(j + 1)
        if s + UP_AHEAD < len(stages):
            issue_up(s + UP_AHEAD)
        a = jnp.square(jnp.maximum(ups.pop(s), 0.0)).astype(BF16)
        for n in range(n_dn):
            part = _dot(a, wdn_refs[n][k * chunk:(k + 1) * chunk, :])
            acc[n] = part if acc[n] is None else acc[n] + part
        if k == n_up - 1:
            xm = x_mid.pop(j)
            f = jnp.concatenate(acc, axis=1)
            o_ref[0, j * SUB_ROWS:(j + 1) * SUB_ROWS, :] = _layer_norm(
                alpha * xm + gate2 * f, ln2_g, ln2_b)
            acc = [None] * n_dn

    zc_ref[...] = hist["z"]
    pc_ref[...] = hist["p"]
    ncv_ref[0] = hist["z"][CONV_HIST - (CONV_WIDTH - 1):, :]
    npl_ref[0] = hist["p"][POOL_HIST - (max(POOL_WINDOWS) - 1):, :]


def _layer_prompt_call(x, mod, w_in, conv_w, sgu_g, sgu_b, w_s, bs_full, pool_wbd, pool_scale,
                       w_out, ln_g, ln_b, wup, wdn, ln2_g, ln2_b, *, alpha, layer, rows):
    batch, seq, c_dim = x.shape
    w_conv, w_pool = conv_w.shape[2], pool_scale.shape[1]
    conv_buf = CONV_WIDTH - 1
    pool_buf = max(POOL_WINDOWS) - 1
    in_specs = _column_blocks(w_in, layer, WEIGHT_COLS)
    out_specs = _column_blocks(w_out, layer, WEIGHT_COLS)
    up_specs = _column_blocks(wup, layer, WEIGHT_COLS)
    dn_specs = _column_blocks(wdn, layer, WEIGHT_COLS)
    kern = functools.partial(_layer_prompt_kernel, alpha=alpha, layer=layer, n_in=len(in_specs),
                             n_out=len(out_specs), n_up=len(up_specs), n_dn=len(dn_specs))
    return pl.pallas_call(
        kern,
        grid=(batch, seq // rows),
        in_specs=[
            pl.BlockSpec((1, rows, c_dim), lambda b, t: (b, t, 0)),
            pl.BlockSpec((None, batch, mod.shape[2]), lambda b, t: (layer, 0, 0)),
            *in_specs,
            _whole(conv_w.shape),
            _whole(sgu_g.shape),
            _whole(sgu_b.shape),
            _layer_block(w_s.shape[1:], layer),
            _layer_block(bs_full.shape[1:], layer),
            _layer_block(pool_wbd.shape[1:], layer),
            _whole(pool_scale.shape),
            *out_specs,
            _whole(ln_g.shape),
            _whole(ln_b.shape),
            *up_specs, *dn_specs,
            _whole(ln2_g.shape),
            _whole(ln2_b.shape),
        ],
        out_specs=[
            pl.BlockSpec((1, rows, c_dim), lambda b, t: (b, t, 0)),
            pl.BlockSpec((1, conv_buf, w_conv), lambda b, t: (b, 0, 0)),
            pl.BlockSpec((1, pool_buf, w_pool), lambda b, t: (b, 0, 0)),
        ],
        out_shape=[
            jax.ShapeDtypeStruct((batch, seq, c_dim), F32),
            jax.ShapeDtypeStruct((batch, conv_buf, w_conv), F32),
            jax.ShapeDtypeStruct((batch, pool_buf, w_pool), F32),
        ],
        scratch_shapes=[
            pltpu.VMEM((CONV_HIST, w_conv), F32),
            pltpu.VMEM((POOL_HIST, w_pool), F32),
        ],
        compiler_params=pltpu.CompilerParams(
            dimension_semantics=("arbitrary", "arbitrary"),
            vmem_limit_bytes=VMEM_LIMIT_BYTES),
        name="layer_prompt",
    )(x, mod, *([w_in] * len(in_specs)), conv_w, sgu_g, sgu_b, w_s, bs_full, pool_wbd, pool_scale,
      *([w_out] * len(out_specs)), ln_g, ln_b, *([wup] * len(up_specs)),
      *([wdn] * len(dn_specs)), ln2_g, ln2_b)


def _mix_sample_kernel(x_ref, mod_ref, cst_ref, pst_ref, win_ref, cw_ref, sg_ref, sb_ref,
                       wst_ref, bst_ref, pw_ref, ps_ref, wout_ref, g_ref, b_ref,
                       o_ref, ncv_ref, npl_ref, v_ref, *, alpha, layer, x_batch_major):
    x = jnp.swapaxes(x_ref[...], 0, 1) if x_batch_major else x_ref[...]
    cst = jnp.swapaxes(cst_ref[...], 0, 1)
    pst = jnp.swapaxes(pst_ref[...], 0, 1)
    t_dim, b_dim, c_dim = x.shape
    w_conv, w_gmlp, w_pool = cw_ref.shape[2], sg_ref.shape[1], ps_ref.shape[1]
    n_valid = pst.shape[0]
    m = mod_ref[...]
    sh, sc1, gate = m[:, :c_dim], 1.0 + m[:, c_dim:2 * c_dim], m[:, 2 * c_dim:]
    h = (x * sc1 + sh).reshape(t_dim * b_dim, c_dim).astype(BF16)
    proj = _dot(h, win_ref[...]).reshape(t_dim, b_dim, -1)
    o0 = 0
    a_b = proj[:, :, o0:o0 + w_conv]; o0 += w_conv
    a_c = proj[:, :, o0:o0 + w_conv]; o0 += w_conv
    a_x = proj[:, :, o0:o0 + w_conv]; o0 += w_conv
    g_u = proj[:, :, o0:o0 + w_gmlp]; o0 += w_gmlp
    g_v = proj[:, :, o0:o0 + w_gmlp]; o0 += w_gmlp
    p_x = proj[:, :, o0:o0 + w_pool]

    z = a_c * a_x
    ext = [cst[i] for i in range(CONV_WIDTH - 1)] + [z[i] for i in range(t_dim)]
    cw = cw_ref[layer]
    y_a = []
    for i in range(t_dim):
        conv = ext[i] * cw[0:1, :]
        for k in range(1, CONV_WIDTH):
            conv = conv + ext[i + k] * cw[k:k + 1, :]
        y_a.append(a_b[i] * conv)
    ncv_ref[...] = jnp.swapaxes(jnp.stack(ext[len(ext) - (CONV_WIDTH - 1):]), 0, 1)

    v_n = _layer_norm(g_v, sg_ref[layer:layer + 1, :], sb_ref[layer:layer + 1, :])
    v_ref[...] = jnp.swapaxes(v_n, 0, 1)
    y_b = []
    for i in range(t_dim):
        s = bst_ref[i:i + 1, :] + wst_ref[i, 0:1, :] * v_n[0]
        for j in range(1, i + 1):
            s = s + wst_ref[i, j:j + 1, :] * v_n[j]
        y_b.append(g_u[i] * s)

    pext = [pst[i] for i in range(n_valid)] + [p_x[i] for i in range(t_dim)]
    lane_group = _lane_group((b_dim, w_pool), len(POOL_WINDOWS))
    d = []
    for i in range(t_dim):
        last = n_valid + i
        means = []
        run = None
        k = 0
        for w in POOL_WINDOWS:
            while k < w:
                run = pext[last - k] if run is None else run + pext[last - k]
                k += 1
            means.append(run / float(min(w, i + 1 + n_valid)))
        d.append(_select_by_group(means, lane_group) - p_x[i])
    d = jnp.stack(d).reshape(t_dim * b_dim, w_pool).astype(BF16)
    y_c = (_dot(d, pw_ref[...]) * ps_ref[layer:layer + 1, :]).reshape(t_dim, b_dim, w_pool)
    npl_ref[...] = jnp.swapaxes(jnp.stack(pext[len(pext) - n_valid:]), 0, 1)

    mix = jnp.concatenate([jnp.stack(y_a), jnp.stack(y_b), y_c], axis=2)
    mix = mix.reshape(t_dim * b_dim, c_dim).astype(BF16)
    y = alpha * x + gate * _dot(mix, wout_ref[...]).reshape(t_dim, b_dim, c_dim)
    o_ref[...] = _layer_norm(y, g_ref[layer:layer + 1, :], b_ref[layer:layer + 1, :])


def _mix_sample_call(x, mod, cst, pst, w_in, conv_w, sgu_g, sgu_b, wst, bs_full, pool_wbd,
                     pool_scale, w_out, ln_g, ln_b, *, alpha, layer, x_batch_major):
    if x_batch_major:
        b_dim, t_dim, c_dim = x.shape
    else:
        t_dim, b_dim, c_dim = x.shape
    w_gmlp = sgu_g.shape[1]
    kern = functools.partial(_mix_sample_kernel, alpha=alpha, layer=layer,
                             x_batch_major=x_batch_major)
    whole3 = lambda i: (0, 0, 0)
    return pl.pallas_call(
        kern,
        grid=(1,),
        in_specs=[
            pl.BlockSpec(x.shape, whole3),
            pl.BlockSpec((None, b_dim, mod.shape[2] // 2), lambda i: (layer, 0, 0)),
            _layer_block(cst.shape[1:], layer),
            _layer_block(pst.shape[1:], layer),
            _layer_block(w_in.shape[1:], layer),
            _whole(conv_w.shape),
            _whole(sgu_g.shape),
            _whole(sgu_b.shape),
            _layer_block(wst.shape[1:], layer),
            pl.BlockSpec((None, t_dim, w_gmlp), lambda i: (layer, 0, 0)),
            _layer_block(pool_wbd.shape[1:], layer),
            _whole(pool_scale.shape),
            _layer_block(w_out.shape[1:], layer),
            _whole(ln_g.shape),
            _whole(ln_b.shape),
        ],
        out_specs=[
            pl.BlockSpec((t_dim, b_dim, c_dim), whole3),
            pl.BlockSpec(cst.shape[1:], whole3),
            pl.BlockSpec(pst.shape[1:], whole3),
            pl.BlockSpec((b_dim, t_dim, w_gmlp), whole3),
        ],
        out_shape=[
            jax.ShapeDtypeStruct((t_dim, b_dim, c_dim), F32),
            jax.ShapeDtypeStruct(cst.shape[1:], F32),
            jax.ShapeDtypeStruct(pst.shape[1:], F32),
            jax.ShapeDtypeStruct((b_dim, t_dim, w_gmlp), F32),
        ],
        compiler_params=pltpu.CompilerParams(
            dimension_semantics=("arbitrary",),
            vmem_limit_bytes=VMEM_LIMIT_BYTES),
        name="mix_sample",
    )(x, mod, cst, pst, w_in, conv_w, sgu_g, sgu_b, wst, bs_full, pool_wbd, pool_scale, w_out,
      ln_g, ln_b)


def kernel(x_prompt, x_sample, c_prompt, c_sample, state_conv, state_pool, w_ada, b_ada, w_in,
           conv_w, sgu_g, sgu_b, w_s, b_s, pool_w, pool_scale, w_out, ln1_g, ln1_b, w_up,
           w_down, ln2_g, ln2_b):
    depth = w_in.shape[0]
    seq = x_prompt.shape[1]
    dec_batch, dec_seq, _ = x_sample.shape
    n_groups, group_dim, _ = pool_w.shape[1:]
    w_pool = n_groups * group_dim
    alpha = (2.0 * depth) ** 0.25
    assert seq % PROMPT_ROWS == 0 and PROMPT_ROWS % SUB_ROWS == 0 and SUB_ROWS % CHUNK == 0
    assert (dec_seq * dec_batch) % SUB_ROWS == 0 and SUB_ROWS % dec_batch == 0

    mod_p, mod_s = _ada_call(c_prompt, c_sample, w_ada, b_ada)

    w_in_b, w_out_b = w_in.astype(BF16), w_out.astype(BF16)
    wup_b, wdn_b = w_up.astype(BF16), w_down.astype(BF16)
    bs_full = jnp.repeat(jnp.swapaxes(b_s, 1, 2), HEAD_DIM, axis=2)
    ws_small = jnp.repeat(jnp.transpose(w_s[:, :, :dec_seq, :dec_seq], (0, 2, 3, 1)), HEAD_DIM, axis=3)
    eye = jnp.eye(n_groups, dtype=F32)
    pool_wbd = (eye[None, :, None, :, None] * pool_w[:, :, :, None, :]).reshape(
        depth, w_pool, w_pool).astype(BF16)

    xp, xs = x_prompt, x_sample
    conv_p, pool_p, conv_s, pool_s, v_s = [], [], [], [], []
    for l in range(depth):
        xp, ncv, npl = _layer_prompt_call(
            xp, mod_p, w_in_b, conv_w, sgu_g, sgu_b, w_s, bs_full, pool_wbd, pool_scale,
            w_out_b, ln1_g, ln1_b, wup_b, wdn_b, ln2_g, ln2_b, alpha=alpha, layer=l,
            rows=PROMPT_ROWS)
        conv_p.append(ncv)
        pool_p.append(npl)

        xs, ncv, npl, v_rows = _mix_sample_call(
            xs, mod_s, state_conv, state_pool, w_in_b, conv_w, sgu_g, sgu_b, ws_small, bs_full,
            pool_wbd, pool_scale, w_out_b, ln1_g, ln1_b, alpha=alpha, layer=l,
            x_batch_major=(l == 0))
        conv_s.append(ncv)
        pool_s.append(npl)
        v_s.append(v_rows)
        xs = _ffn_sample_call(xs, mod_s, wup_b, wdn_b, ln2_g, ln2_b, layer=l, alpha=alpha,
                              out_batch_major=(l == depth - 1))

    return (xp, xs, jnp.stack(conv_p), jnp.stack(pool_p), jnp.stack(conv_s), jnp.stack(pool_s),
            jnp.stack(v_s))
```

```python
import functools

import jax
import jax.numpy as jnp
from jax import lax
from jax.experimental import pallas as pl
from jax.experimental.pallas import tpu as pltpu

F32 = jnp.float32
BF16 = jnp.bfloat16

LN_EPS = 1e-5
HEAD_DIM = 64
CHUNK = 128
POOL_WINDOWS = (2, 4, 8, 16)
CONV_WIDTH = 3
LANES = 128
SUBLANES = 8
VMEM_LIMIT_BYTES = 56 * 1024 * 1024

PROMPT_ROWS = 1024
SUB_ROWS = 256
UP_AHEAD = 2
PROJ_AHEAD = 2
MIX_OUT_AT = 2
HEADS_AT = 5
WEIGHT_COLS = 512
CONV_HIST = SUBLANES
POOL_HIST = 2 * SUBLANES


def _layer_norm(y, g, b):
    mu = jnp.mean(y, axis=-1, keepdims=True)
    d = y - mu
    var = jnp.mean(d * d, axis=-1, keepdims=True)
    return d * lax.rsqrt(var + LN_EPS) * g + b


def _dot(a, b):
    return jnp.dot(a, b, preferred_element_type=F32)


def _layer_block(shape, layer):
    zeros = (0,) * len(shape)
    return pl.BlockSpec((None,) + tuple(shape), lambda *_: (layer,) + zeros,
                        pipeline_mode=pl.Buffered(1))


def _whole(shape):
    zeros = (0,) * len(shape)
    return pl.BlockSpec(tuple(shape), lambda *_: zeros, pipeline_mode=pl.Buffered(1))


def _column_blocks(w, layer, width):
    rows, cols = w.shape[1:]
    return [pl.BlockSpec((None, rows, width), lambda *_, n=n: (layer, 0, n),
                         pipeline_mode=pl.Buffered(1)) for n in range(cols // width)]


def _lane_group(shape, n_groups):
    return lax.broadcasted_iota(jnp.int32, shape, len(shape) - 1) // (shape[-1] // n_groups)


def _select_by_group(values, lane_group):
    out = values[-1]
    for g in range(len(values) - 2, -1, -1):
        out = jnp.where(lane_group == g, values[g], out)
    return out


def _ada_kernel(cp_ref, cs_ref, w_ref, b_ref, op_ref, os_ref):
    l, k = pl.program_id(0), pl.program_id(1)
    w = w_ref[0].astype(BF16)
    part_p = _dot(jax.nn.silu(cp_ref[...]).astype(BF16), w)
    part_s = _dot(jax.nn.silu(cs_ref[...]).astype(BF16), w)

    @pl.when(k == 0)
    def _():
        bias = b_ref[pl.ds(l, 1), :]
        op_ref[0] = part_p + bias
        os_ref[0] = part_s + bias

    @pl.when(k > 0)
    def _():
        op_ref[0] += part_p
        os_ref[0] += part_s


def _ada_call(c_prompt, c_sample, w_ada, b_ada, row_tile=256):
    depth, d_model, n_mod = w_ada.shape
    bp, bs = c_prompt.shape[0], c_sample.shape[0]
    return pl.pallas_call(
        _ada_kernel,
        grid=(depth, d_model // row_tile),
        in_specs=[
            pl.BlockSpec((bp, row_tile), lambda l, k: (0, k)),
            pl.BlockSpec((bs, row_tile), lambda l, k: (0, k)),
            pl.BlockSpec((1, row_tile, n_mod), lambda l, k: (l, k, 0)),
            _whole(b_ada.shape),
        ],
        out_specs=[
            pl.BlockSpec((1, bp, n_mod), lambda l, k: (l, 0, 0)),
            pl.BlockSpec((1, bs, n_mod), lambda l, k: (l, 0, 0)),
        ],
        out_shape=[
            jax.ShapeDtypeStruct((depth, bp, n_mod), F32),
            jax.ShapeDtypeStruct((depth, bs, n_mod), F32),
        ],
        compiler_params=pltpu.CompilerParams(
            dimension_semantics=("arbitrary", "arbitrary"),
            vmem_limit_bytes=VMEM_LIMIT_BYTES),
        name="ada_mod",
    )(c_prompt, c_sample, w_ada, b_ada)


def _ffn_sample_kernel(x_ref, mod_ref, *refs, alpha, layer, n_up, n_dn, out_batch_major):
    wup_refs, wdn_refs = refs[:n_up], refs[n_up:n_up + n_dn]
    g_ref, b_ref, o_ref = refs[n_up + n_dn:]
    t_dim, b_dim, c_dim = x_ref.shape
    chunk = wup_refs[0].shape[1]
    m = mod_ref[...]
    sh, sc1, gate = m[:, :c_dim], 1.0 + m[:, c_dim:2 * c_dim], m[:, 2 * c_dim:]
    ln_g, ln_b = g_ref[layer:layer + 1, :], b_ref[layer:layer + 1, :]
    step = SUB_ROWS // b_dim
    n_sub = t_dim // step

    stages = [(j, k) for j in range(n_sub) for k in range(n_up)]
    xs, hs, ups = {}, {}, {}

    def issue_up(s):
        j, k = stages[s]
        if j not in hs:
            xs[j] = x_ref[j * step:(j + 1) * step]
            hs[j] = (xs[j] * sc1 + sh).reshape(-1, c_dim).astype(BF16)
        ups[s] = _dot(hs[j], wup_refs[k][...])

    for s in range(min(UP_AHEAD, len(stages))):
        issue_up(s)
    outs = []
    acc = [None] * n_dn
    for s, (j, k) in enumerate(stages):
        if s + UP_AHEAD < len(stages):
            issue_up(s + UP_AHEAD)
        a = jnp.square(jnp.maximum(ups.pop(s), 0.0)).astype(BF16)
        for n in range(n_dn):
            part = _dot(a, wdn_refs[n][k * chunk:(k + 1) * chunk, :])
            acc[n] = part if acc[n] is None else acc[n] + part
        if k == n_up - 1:
            x = xs.pop(j)
            f = jnp.concatenate(acc, axis=1).reshape(x.shape)
            y = _layer_norm(alpha * x + gate * f, ln_g, ln_b)
            acc = [None] * n_dn
            if out_batch_major:
                outs.append(y)
            else:
                o_ref[j * step:(j + 1) * step] = y
    if out_batch_major:
        o_ref[...] = jnp.swapaxes(jnp.concatenate(outs, axis=0), 0, 1)


def _ffn_sample_call(x, mod, wup, wdn, ln_g, ln_b, *, layer, alpha, out_batch_major):
    n_mod = mod.shape[-1]
    up_specs = _column_blocks(wup, layer, WEIGHT_COLS)
    dn_specs = _column_blocks(wdn, layer, WEIGHT_COLS)
    kern = functools.partial(_ffn_sample_kernel, alpha=alpha, layer=layer, n_up=len(up_specs),
                             n_dn=len(dn_specs), out_batch_major=out_batch_major)
    out_shape = (x.shape[1], x.shape[0], x.shape[2]) if out_batch_major else x.shape
    whole3 = lambda i: (0, 0, 0)
    return pl.pallas_call(
        kern,
        grid=(1,),
        in_specs=[
            pl.BlockSpec(x.shape, whole3),
            pl.BlockSpec((None, mod.shape[1], n_mod // 2), lambda i: (layer, 0, 1)),
            *up_specs, *dn_specs,
            _whole(ln_g.shape),
            _whole(ln_b.shape),
        ],
        out_specs=pl.BlockSpec(out_shape, whole3),
        out_shape=jax.ShapeDtypeStruct(out_shape, F32),
        compiler_params=pltpu.CompilerParams(
            dimension_semantics=("arbitrary",),
            vmem_limit_bytes=VMEM_LIMIT_BYTES),
        name="ffn_sample",
    )(x, mod, *([wup] * len(up_specs)), *([wdn] * len(dn_specs)), ln_g, ln_b)


def _layer_prompt_kernel(x_ref, mod_ref, *refs, alpha, layer, n_in, n_out, n_up, n_dn):
    win_refs, refs = refs[:n_in], refs[n_in:]
    cw_ref, sg_ref, sb_ref, ws_ref, bs_ref, pw_ref, ps_ref = refs[:7]
    wout_refs, refs = refs[7:7 + n_out], refs[7 + n_out:]
    g_ref, b_ref = refs[:2]
    wup_refs, refs = refs[2:2 + n_up], refs[2 + n_up:]
    wdn_refs, refs = refs[:n_dn], refs[n_dn:]
    g2_ref, b2_ref, o_ref, ncv_ref, npl_ref, zc_ref, pc_ref = refs
    bi, t = pl.program_id(0), pl.program_id(1)
    rows, c_dim = x_ref.shape[1], x_ref.shape[2]
    w_conv, w_gmlp, w_pool = cw_ref.shape[2], sg_ref.shape[1], ps_ref.shape[1]
    n_heads = ws_ref.shape[0]
    chunk = wup_refs[0].shape[1]

    @pl.when(t == 0)
    def _():
        zc_ref[...] = jnp.zeros(zc_ref.shape, F32)
        pc_ref[...] = jnp.zeros(pc_ref.shape, F32)

    m = mod_ref[pl.ds(bi, 1), :]
    sh, sc1, gate = m[:, :c_dim], 1.0 + m[:, c_dim:2 * c_dim], m[:, 2 * c_dim:3 * c_dim]
    sh2, sc2, gate2 = (m[:, 3 * c_dim:4 * c_dim], 1.0 + m[:, 4 * c_dim:5 * c_dim],
                       m[:, 5 * c_dim:])
    cw = cw_ref[layer]
    sgu_g, sgu_b = sg_ref[layer:layer + 1, :], sb_ref[layer:layer + 1, :]
    pool_scale = ps_ref[layer:layer + 1, :]
    ln_g, ln_b = g_ref[layer:layer + 1, :], b_ref[layer:layer + 1, :]
    ln2_g, ln2_b = g2_ref[layer:layer + 1, :], b2_ref[layer:layer + 1, :]

    row_i = lax.broadcasted_iota(jnp.int32, (CHUNK, 2 * CHUNK), 0)
    col_j = lax.broadcasted_iota(jnp.int32, (CHUNK, 2 * CHUNK), 1) % CHUNK
    causal = col_j <= row_i
    ws_m = []
    for p in range(n_heads // 2):
        pair = jnp.concatenate([ws_ref[2 * p], ws_ref[2 * p + 1]], axis=1)
        ws_m.append(jnp.where(causal, pair, 0.0).astype(BF16))
    low_half = lax.broadcasted_iota(jnp.int32, (CHUNK, LANES), 1) < HEAD_DIM
    lane_group = _lane_group((SUB_ROWS, w_pool), len(POOL_WINDOWS))
    window = _select_by_group([jnp.full((SUB_ROWS, w_pool), w, jnp.int32) for w in POOL_WINDOWS],
                              lane_group)
    row_id = lax.broadcasted_iota(jnp.int32, (SUB_ROWS, w_pool), 0)
    bias = bs_ref[...]
    n_sub = rows // SUB_ROWS
    n_ch = SUB_ROWS // CHUNK
    hist = {"z": zc_ref[...], "p": pc_ref[...]}
    projs, mids, x_mid = {}, {}, {}

    def project(jj):
        h = (x_ref[0, jj * SUB_ROWS:(jj + 1) * SUB_ROWS, :] * sc1 + sh).astype(BF16)
        projs[jj] = jnp.concatenate([_dot(h, w[...]) for w in win_refs], axis=1)

    def heads(j):
        proj = projs.pop(j)
        o0 = 0
        a_b = proj[:, o0:o0 + w_conv]; o0 += w_conv
        a_c = proj[:, o0:o0 + w_conv]; o0 += w_conv
        a_x = proj[:, o0:o0 + w_conv]; o0 += w_conv
        g_u = proj[:, o0:o0 + w_gmlp]; o0 += w_gmlp
        g_v = proj[:, o0:o0 + w_gmlp]; o0 += w_gmlp
        p_x = proj[:, o0:o0 + w_pool]

        z = a_c * a_x
        z_ext = jnp.concatenate([hist["z"], z], axis=0)
        conv = z_ext * cw[CONV_WIDTH - 1:CONV_WIDTH, :]
        for k in range(1, CONV_WIDTH):
            conv = conv + pltpu.roll(z_ext, k, 0) * cw[CONV_WIDTH - 1 - k:CONV_WIDTH - k, :]
        y_a = a_b * conv[CONV_HIST:, :]
        hist["z"] = z[SUB_ROWS - CONV_HIST:, :]

        v_n = _layer_norm(g_v, sgu_g, sgu_b)
        s_cols = []
        for p in range(n_heads // 2):
            cols = [v_n[c * CHUNK:(c + 1) * CHUNK, p * LANES:(p + 1) * LANES] for c in range(n_ch)]
            rhs = jnp.concatenate(
                [jnp.concatenate([jnp.where(low_half, col, 0.0) for col in cols], axis=1),
                 jnp.concatenate([jnp.where(low_half, 0.0, col) for col in cols], axis=1)],
                axis=0).astype(BF16)
            s_cols.append(_dot(ws_m[p], rhs))

        run = jnp.concatenate([hist["p"], p_x], axis=0)
        sums = []
        for w in POOL_WINDOWS:
            run = run + pltpu.roll(run, w // 2, 0)
            sums.append(run[POOL_HIST:, :])
        count = jnp.minimum(window, t * rows + j * SUB_ROWS + row_id + 1).astype(F32)
        d = (_select_by_group(sums, lane_group) / count - p_x).astype(BF16)
        y_c_raw = _dot(d, pw_ref[...])
        hist["p"] = p_x[SUB_ROWS - POOL_HIST:, :]
        mids[j] = (y_a, g_u, s_cols, y_c_raw)

    def mix_out(j):
        y_a, g_u, s_cols, y_c_raw = mids.pop(j)
        s = jnp.concatenate(
            [jnp.concatenate([sc[:, c * LANES:(c + 1) * LANES] for sc in s_cols], axis=1) + bias
             for c in range(n_ch)], axis=0)
        mix = jnp.concatenate([y_a, g_u * s, y_c_raw * pool_scale], axis=1).astype(BF16)
        out = jnp.concatenate([_dot(mix, w[...]) for w in wout_refs], axis=1)
        x = x_ref[0, j * SUB_ROWS:(j + 1) * SUB_ROWS, :]
        x_mid[j] = _layer_norm(alpha * x + gate * out, ln_g, ln_b)

    def heads_and_project(j):
        heads(j)
        if j + PROJ_AHEAD < n_sub:
            project(j + PROJ_AHEAD)

    for jj in range(min(PROJ_AHEAD, n_sub)):
        project(jj)
    heads_and_project(0)
    mix_out(0)
    if n_sub > 1:
        heads_and_project(1)

    stages = [(j, k) for j in range(n_sub) for k in range(n_up)]
    hs, ups = {}, {}

    def issue_up(s):
        j, k = stages[s]
        if j not in hs:
            hs[j] = (x_mid[j] * sc2 + sh2).astype(BF16)
        ups[s] = _dot(hs[j], wup_refs[k][...])

    for s in range(min(UP_AHEAD, len(stages))):
        issue_up(s)
    acc = [None] * n_dn
    for s, (j, k) in enumerate(stages):
        if k == MIX_OUT_AT and j + 1 < n_sub:
            mix_out(j + 1)
        if k == HEADS_AT and j + 2 < n_sub:
            heads_and_project(j + 2)
        if s + UP_AHEAD < len(stages):
            issue_up(s + UP_AHEAD)
        a = jnp.square(jnp.maximum(ups.pop(s), 0.0)).astype(BF16)
        for n in range(n_dn):
            part = _dot(a, wdn_refs[n][k * chunk:(k + 1) * chunk, :])
            acc[n] = part if acc[n] is None else acc[n] + part
        if k == n_up - 1:
            xm = x_mid.pop(j)
            f = jnp.concatenate(acc, axis=1)
            o_ref[0, j * SUB_ROWS:(j + 1) * SUB_ROWS, :] = _layer_norm(
                alpha * xm + gate2 * f, ln2_g, ln2_b)
            acc = [None] * n_dn

    zc_ref[...] = hist["z"]
    pc_ref[...] = hist["p"]
    ncv_ref[0] = hist["z"][CONV_HIST - (CONV_WIDTH - 1):, :]
    npl_ref[0] = hist["p"][POOL_HIST - (max(POOL_WINDOWS) - 1):, :]


def _layer_prompt_call(x, mod, w_in, conv_w, sgu_g, sgu_b, w_s, bs_full, pool_wbd, pool_scale,
                       w_out, ln_g, ln_b, wup, wdn, ln2_g, ln2_b, *, alpha, layer, rows):
    batch, seq, c_dim = x.shape
    w_conv, w_pool = conv_w.shape[2], pool_scale.shape[1]
    conv_buf = CONV_WIDTH - 1
    pool_buf = max(POOL_WINDOWS) - 1
    in_specs = _column_blocks(w_in, layer, WEIGHT_COLS)
    out_specs = _column_blocks(w_out, layer, WEIGHT_COLS)
    up_specs = _column_blocks(wup, layer, WEIGHT_COLS)
    dn_specs = _column_blocks(wdn, layer, WEIGHT_COLS)
    kern = functools.partial(_layer_prompt_kernel, alpha=alpha, layer=layer, n_in=len(in_specs),
                             n_out=len(out_specs), n_up=len(up_specs), n_dn=len(dn_specs))
    return pl.pallas_call(
        kern,
        grid=(batch, seq // rows),
        in_specs=[
            pl.BlockSpec((1, rows, c_dim), lambda b, t: (b, t, 0)),
            pl.BlockSpec((None, batch, mod.shape[2]), lambda b, t: (layer, 0, 0)),
            *in_specs,
            _whole(conv_w.shape),
            _whole(sgu_g.shape),
            _whole(sgu_b.shape),
            _layer_block(w_s.shape[1:], layer),
            _layer_block(bs_full.shape[1:], layer),
            _layer_block(pool_wbd.shape[1:], layer),
            _whole(pool_scale.shape),
            *out_specs,
            _whole(ln_g.shape),
            _whole(ln_b.shape),
            *up_specs, *dn_specs,
            _whole(ln2_g.shape),
            _whole(ln2_b.shape),
        ],
        out_specs=[
            pl.BlockSpec((1, rows, c_dim), lambda b, t: (b, t, 0)),
            pl.BlockSpec((1, conv_buf, w_conv), lambda b, t: (b, 0, 0)),
            pl.BlockSpec((1, pool_buf, w_pool), lambda b, t: (b, 0, 0)),
        ],
        out_shape=[
            jax.ShapeDtypeStruct((batch, seq, c_dim), F32),
            jax.ShapeDtypeStruct((batch, conv_buf, w_conv), F32),
            jax.ShapeDtypeStruct((batch, pool_buf, w_pool), F32),
        ],
        scratch_shapes=[
            pltpu.VMEM((CONV_HIST, w_conv), F32),
            pltpu.VMEM((POOL_HIST, w_pool), F32),
        ],
        compiler_params=pltpu.CompilerParams(
            dimension_semantics=("arbitrary", "arbitrary"),
            vmem_limit_bytes=VMEM_LIMIT_BYTES),
        name="layer_prompt",
    )(x, mod, *([w_in] * len(in_specs)), conv_w, sgu_g, sgu_b, w_s, bs_full, pool_wbd, pool_scale,
      *([w_out] * len(out_specs)), ln_g, ln_b, *([wup] * len(up_specs)),
      *([wdn] * len(dn_specs)), ln2_g, ln2_b)


def _mix_sample_kernel(x_ref, mod_ref, cst_ref, pst_ref, win_ref, cw_ref, sg_ref, sb_ref,
                       wst_ref, bst_ref, pw_ref, ps_ref, wout_ref, g_ref, b_ref,
                       o_ref, ncv_ref, npl_ref, v_ref, *, alpha, layer, x_batch_major):
    x = jnp.swapaxes(x_ref[...], 0, 1) if x_batch_major else x_ref[...]
    cst = jnp.swapaxes(cst_ref[...], 0, 1)
    pst = jnp.swapaxes(pst_ref[...], 0, 1)
    t_dim, b_dim, c_dim = x.shape
    w_conv, w_gmlp, w_pool = cw_ref.shape[2], sg_ref.shape[1], ps_ref.shape[1]
    n_valid = pst.shape[0]
    m = mod_ref[...]
    sh, sc1, gate = m[:, :c_dim], 1.0 + m[:, c_dim:2 * c_dim], m[:, 2 * c_dim:]
    h = (x * sc1 + sh).reshape(t_dim * b_dim, c_dim).astype(BF16)
    proj = _dot(h, win_ref[...]).reshape(t_dim, b_dim, -1)
    o0 = 0
    a_b = proj[:, :, o0:o0 + w_conv]; o0 += w_conv
    a_c = proj[:, :, o0:o0 + w_conv]; o0 += w_conv
    a_x = proj[:, :, o0:o0 + w_conv]; o0 += w_conv
    g_u = proj[:, :, o0:o0 + w_gmlp]; o0 += w_gmlp
    g_v = proj[:, :, o0:o0 + w_gmlp]; o0 += w_gmlp
    p_x = proj[:, :, o0:o0 + w_pool]

    z = a_c * a_x
    ext = [cst[i] for i in range(CONV_WIDTH - 1)] + [z[i] for i in range(t_dim)]
    cw = cw_ref[layer]
    y_a = []
    for i in range(t_dim):
        conv = ext[i] * cw[0:1, :]
        for k in range(1, CONV_WIDTH):
            conv = conv + ext[i + k] * cw[k:k + 1, :]
        y_a.append(a_b[i] * conv)
    ncv_ref[...] = jnp.swapaxes(jnp.stack(ext[len(ext) - (CONV_WIDTH - 1):]), 0, 1)

    v_n = _layer_norm(g_v, sg_ref[layer:layer + 1, :], sb_ref[layer:layer + 1, :])
    v_ref[...] = jnp.swapaxes(v_n, 0, 1)
    y_b = []
    for i in range(t_dim):
        s = bst_ref[i:i + 1, :] + wst_ref[i, 0:1, :] * v_n[0]
        for j in range(1, i + 1):
            s = s + wst_ref[i, j:j + 1, :] * v_n[j]
        y_b.append(g_u[i] * s)

    pext = [pst[i] for i in range(n_valid)] + [p_x[i] for i in range(t_dim)]
    lane_group = _lane_group((b_dim, w_pool), len(POOL_WINDOWS))
    d = []
    for i in range(t_dim):
        last = n_valid + i
        means = []
        run = None
        k = 0
        for w in POOL_WINDOWS:
            while k < w:
                run = pext[last - k] if run is None else run + pext[last - k]
                k += 1
            means.append(run / float(min(w, i + 1 + n_valid)))
        d.append(_select_by_group(means, lane_group) - p_x[i])
    d = jnp.stack(d).reshape(t_dim * b_dim, w_pool).astype(BF16)
    y_c = (_dot(d, pw_ref[...]) * ps_ref[layer:layer + 1, :]).reshape(t_dim, b_dim, w_pool)
    npl_ref[...] = jnp.swapaxes(jnp.stack(pext[len(pext) - n_valid:]), 0, 1)

    mix = jnp.concatenate([jnp.stack(y_a), jnp.stack(y_b), y_c], axis=2)
    mix = mix.reshape(t_dim * b_dim, c_dim).astype(BF16)
    y = alpha * x + gate * _dot(mix, wout_ref[...]).reshape(t_dim, b_dim, c_dim)
    o_ref[...] = _layer_norm(y, g_ref[layer:layer + 1, :], b_ref[layer:layer + 1, :])


def _mix_sample_call(x, mod, cst, pst, w_in, conv_w, sgu_g, sgu_b, wst, bs_full, pool_wbd,
                     pool_scale, w_out, ln_g, ln_b, *, alpha, layer, x_batch_major):
    if x_batch_major:
        b_dim, t_dim, c_dim = x.shape
    else:
        t_dim, b_dim, c_dim = x.shape
    w_gmlp = sgu_g.shape[1]
    kern = functools.partial(_mix_sample_kernel, alpha=alpha, layer=layer,
                             x_batch_major=x_batch_major)
    whole3 = lambda i: (0, 0, 0)
    return pl.pallas_call(
        kern,
        grid=(1,),
        in_specs=[
            pl.BlockSpec(x.shape, whole3),
            pl.BlockSpec((None, b_dim, mod.shape[2] // 2), lambda i: (layer, 0, 0)),
            _layer_block(cst.shape[1:], layer),
            _layer_block(pst.shape[1:], layer),
            _layer_block(w_in.shape[1:], layer),
            _whole(conv_w.shape),
            _whole(sgu_g.shape),
            _whole(sgu_b.shape),
            _layer_block(wst.shape[1:], layer),
            pl.BlockSpec((None, t_dim, w_gmlp), lambda i: (layer, 0, 0)),
            _layer_block(pool_wbd.shape[1:], layer),
            _whole(pool_scale.shape),
            _layer_block(w_out.shape[1:], layer),
            _whole(ln_g.shape),
            _whole(ln_b.shape),
        ],
        out_specs=[
            pl.BlockSpec((t_dim, b_dim, c_dim), whole3),
            pl.BlockSpec(cst.shape[1:], whole3),
            pl.BlockSpec(pst.shape[1:], whole3),
            pl.BlockSpec((b_dim, t_dim, w_gmlp), whole3),
        ],
        out_shape=[
            jax.ShapeDtypeStruct((t_dim, b_dim, c_dim), F32),
            jax.ShapeDtypeStruct(cst.shape[1:], F32),
            jax.ShapeDtypeStruct(pst.shape[1:], F32),
            jax.ShapeDtypeStruct((b_dim, t_dim, w_gmlp), F32),
        ],
        compiler_params=pltpu.CompilerParams(
            dimension_semantics=("arbitrary",),
            vmem_limit_bytes=VMEM_LIMIT_BYTES),
        name="mix_sample",
    )(x, mod, cst, pst, w_in, conv_w, sgu_g, sgu_b, wst, bs_full, pool_wbd, pool_scale, w_out,
      ln_g, ln_b)


def kernel(x_prompt, x_sample, c_prompt, c_sample, state_conv, state_pool, w_ada, b_ada, w_in,
           conv_w, sgu_g, sgu_b, w_s, b_s, pool_w, pool_scale, w_out, ln1_g, ln1_b, w_up,
           w_down, ln2_g, ln2_b):
    depth = w_in.shape[0]
    seq = x_prompt.shape[1]
    dec_batch, dec_seq, _ = x_sample.shape
    n_groups, group_dim, _ = pool_w.shape[1:]
    w_pool = n_groups * group_dim
    alpha = (2.0 * depth) ** 0.25
    assert seq % PROMPT_ROWS == 0 and PROMPT_ROWS % SUB_ROWS == 0 and SUB_ROWS % CHUNK == 0
    assert (dec_seq * dec_batch) % SUB_ROWS == 0 and SUB_ROWS % dec_batch == 0

    mod_p, mod_s = _ada_call(c_prompt, c_sample, w_ada, b_ada)

    w_in_b, w_out_b = w_in.astype(BF16), w_out.astype(BF16)
    wup_b, wdn_b = w_up.astype(BF16), w_down.astype(BF16)
    bs_full = jnp.repeat(jnp.swapaxes(b_s, 1, 2), HEAD_DIM, axis=2)
    ws_small = jnp.repeat(jnp.transpose(w_s[:, :, :dec_seq, :dec_seq], (0, 2, 3, 1)), HEAD_DIM, axis=3)
    eye = jnp.eye(n_groups, dtype=F32)
    pool_wbd = (eye[None, :, None, :, None] * pool_w[:, :, :, None, :]).reshape(
        depth, w_pool, w_pool).astype(BF16)

    xp, xs = x_prompt, x_sample
    conv_p, pool_p, conv_s, pool_s, v_s = [], [], [], [], []
    for l in range(depth):
        xp, ncv, npl = _layer_prompt_call(
            xp, mod_p, w_in_b, conv_w, sgu_g, sgu_b, w_s, bs_full, pool_wbd, pool_scale,
            w_out_b, ln1_g, ln1_b, wup_b, wdn_b, ln2_g, ln2_b, alpha=alpha, layer=l,
            rows=PROMPT_ROWS)
        conv_p.append(ncv)
        pool_p.append(npl)

        xs, ncv, npl, v_rows = _mix_sample_call(
            xs, mod_s, state_conv, state_pool, w_in_b, conv_w, sgu_g, sgu_b, ws_small, bs_full,
            pool_wbd, pool_scale, w_out_b, ln1_g, ln1_b, alpha=alpha, layer=l,
            x_batch_major=(l == 0))
        conv_s.append(ncv)
        pool_s.append(npl)
        v_s.append(v_rows)
        xs = _ffn_sample_call(xs, mod_s, wup_b, wdn_b, ln2_g, ln2_b, layer=l, alpha=alpha,
                              out_batch_major=(l == depth - 1))

    return (xp, xs, jnp.stack(conv_p), jnp.stack(pool_p), jnp.stack(conv_s), jnp.stack(pool_s),
            jnp.stack(v_s))
```

```python
import functools

import jax
import jax.numpy as jnp
from jax import lax
from jax.experimental import pallas as pl
from jax.experimental.pallas import tpu as pltpu

F32 = jnp.float32
BF16 = jnp.bfloat16

LN_EPS = 1e-5
HEAD_DIM = 64
CHUNK = 128
POOL_WINDOWS = (2, 4, 8, 16)
CONV_WIDTH = 3
LANES = 128
SUBLANES = 8
VMEM_LIMIT_BYTES = 56 * 1024 * 1024

PROMPT_ROWS = 1024
SUB_ROWS = 256
UP_AHEAD = 2
PROJ_AHEAD = 2
MIX_OUT_AT = 2
HEADS_AT = 5
WEIGHT_COLS = 512
CONV_HIST = SUBLANES
POOL_HIST = 2 * SUBLANES


def _layer_norm(y, g, b):
    mu = jnp.mean(y, axis=-1, keepdims=True)
    d = y - mu
    var = jnp.mean(d * d, axis=-1, keepdims=True)
    return d * lax.rsqrt(var + LN_EPS) * g + b


def _dot(a, b):
    return jnp.dot(a, b, preferred_element_type=F32)


def _layer_block(shape, layer):
    zeros = (0,) * len(shape)
    return pl.BlockSpec((None,) + tuple(shape), lambda *_: (layer,) + zeros,
                        pipeline_mode=pl.Buffered(1))


def _whole(shape):
    zeros = (0,) * len(shape)
    return pl.BlockSpec(tuple(shape), lambda *_: zeros, pipeline_mode=pl.Buffered(1))


def _column_blocks(w, layer, width):
    rows, cols = w.shape[1:]
    return [pl.BlockSpec((None, rows, width), lambda *_, n=n: (layer, 0, n),
                         pipeline_mode=pl.Buffered(1)) for n in range(cols // width)]


def _lane_group(shape, n_groups):
    return lax.broadcasted_iota(jnp.int32, shape, len(shape) - 1) // (shape[-1] // n_groups)


def _select_by_group(values, lane_group):
    out = values[-1]
    for g in range(len(values) - 2, -1, -1):
        out = jnp.where(lane_group == g, values[g], out)
    return out


def _ada_kernel(cp_ref, cs_ref, w_ref, b_ref, op_ref, os_ref):
    l, k = pl.program_id(0), pl.program_id(1)
    w = w_ref[0].astype(BF16)
    part_p = _dot(jax.nn.silu(cp_ref[...]).astype(BF16), w)
    part_s = _dot(jax.nn.silu(cs_ref[...]).astype(BF16), w)

    @pl.when(k == 0)
    def _():
        bias = b_ref[pl.ds(l, 1), :]
        op_ref[0] = part_p + bias
        os_ref[0] = part_s + bias

    @pl.when(k > 0)
    def _():
        op_ref[0] += part_p
        os_ref[0] += part_s


def _ada_call(c_prompt, c_sample, w_ada, b_ada, row_tile=256):
    depth, d_model, n_mod = w_ada.shape
    bp, bs = c_prompt.shape[0], c_sample.shape[0]
    return pl.pallas_call(
        _ada_kernel,
        grid=(depth, d_model // row_tile),
        in_specs=[
            pl.BlockSpec((bp, row_tile), lambda l, k: (0, k)),
            pl.BlockSpec((bs, row_tile), lambda l, k: (0, k)),
            pl.BlockSpec((1, row_tile, n_mod), lambda l, k: (l, k, 0)),
            _whole(b_ada.shape),
        ],
        out_specs=[
            pl.BlockSpec((1, bp, n_mod), lambda l, k: (l, 0, 0)),
            pl.BlockSpec((1, bs, n_mod), lambda l, k: (l, 0, 0)),
        ],
        out_shape=[
            jax.ShapeDtypeStruct((depth, bp, n_mod), F32),
            jax.ShapeDtypeStruct((depth, bs, n_mod), F32),
        ],
        compiler_params=pltpu.CompilerParams(
            dimension_semantics=("arbitrary", "arbitrary"),
            vmem_limit_bytes=VMEM_LIMIT_BYTES),
        name="ada_mod",
    )(c_prompt, c_sample, w_ada, b_ada)


def _layer_pipeline(n_sub, project, heads, mix_out, x_mid, sh2, sc2, gate2, wup_refs, wdn_refs,
                    ln2_g, ln2_b, alpha, store):
    n_up, n_dn = len(wup_refs), len(wdn_refs)
    chunk = wup_refs[0].shape[1]

    def heads_and_project(j):
        heads(j)
        if j + PROJ_AHEAD < n_sub:
            project(j + PROJ_AHEAD)

    for jj in range(min(PROJ_AHEAD, n_sub)):
        project(jj)
    heads_and_project(0)
    mix_out(0)
    if n_sub > 1:
        heads_and_project(1)

    stages = [(j, k) for j in range(n_sub) for k in range(n_up)]
    hs, ups = {}, {}

    def issue_up(s):
        j, k = stages[s]
        if j not in hs:
            xm = x_mid[j]
            hs[j] = (xm * sc2 + sh2).reshape(-1, xm.shape[-1]).astype(BF16)
        ups[s] = _dot(hs[j], wup_refs[k][...])

    for s in range(min(UP_AHEAD, len(stages))):
        issue_up(s)
    acc = [None] * n_dn
    for s, (j, k) in enumerate(stages):
        if k == MIX_OUT_AT and j + 1 < n_sub:
            mix_out(j + 1)
        if k == HEADS_AT and j + 2 < n_sub:
            heads_and_project(j + 2)
        if s + UP_AHEAD < len(stages):
            issue_up(s + UP_AHEAD)
        a = jnp.square(jnp.maximum(ups.pop(s), 0.0)).astype(BF16)
        for n in range(n_dn):
            part = _dot(a, wdn_refs[n][k * chunk:(k + 1) * chunk, :])
            acc[n] = part if acc[n] is None else acc[n] + part
        if k == n_up - 1:
            xm = x_mid.pop(j)
            f = jnp.concatenate(acc, axis=1).reshape(xm.shape)
            store(j, _layer_norm(alpha * xm + gate2 * f, ln2_g, ln2_b))
            acc = [None] * n_dn


def _layer_prompt_kernel(x_ref, mod_ref, *refs, alpha, layer, n_in, n_out, n_up, n_dn):
    win_refs, refs = refs[:n_in], refs[n_in:]
    cw_ref, sg_ref, sb_ref, ws_ref, bs_ref, pw_ref, ps_ref = refs[:7]
    wout_refs, refs = refs[7:7 + n_out], refs[7 + n_out:]
    g_ref, b_ref = refs[:2]
    wup_refs, refs = refs[2:2 + n_up], refs[2 + n_up:]
    wdn_refs, refs = refs[:n_dn], refs[n_dn:]
    g2_ref, b2_ref, o_ref, ncv_ref, npl_ref, zc_ref, pc_ref = refs
    bi, t = pl.program_id(0), pl.program_id(1)
    rows, c_dim = x_ref.shape[1], x_ref.shape[2]
    w_conv, w_gmlp, w_pool = cw_ref.shape[2], sg_ref.shape[1], ps_ref.shape[1]
    n_heads = ws_ref.shape[0]

    @pl.when(t == 0)
    def _():
        zc_ref[...] = jnp.zeros(zc_ref.shape, F32)
        pc_ref[...] = jnp.zeros(pc_ref.shape, F32)

    m = mod_ref[pl.ds(bi, 1), :]
    sh, sc1, gate = m[:, :c_dim], 1.0 + m[:, c_dim:2 * c_dim], m[:, 2 * c_dim:3 * c_dim]
    sh2, sc2, gate2 = (m[:, 3 * c_dim:4 * c_dim], 1.0 + m[:, 4 * c_dim:5 * c_dim],
                       m[:, 5 * c_dim:])
    cw = cw_ref[layer]
    sgu_g, sgu_b = sg_ref[layer:layer + 1, :], sb_ref[layer:layer + 1, :]
    pool_scale = ps_ref[layer:layer + 1, :]
    ln_g, ln_b = g_ref[layer:layer + 1, :], b_ref[layer:layer + 1, :]
    ln2_g, ln2_b = g2_ref[layer:layer + 1, :], b2_ref[layer:layer + 1, :]

    row_i = lax.broadcasted_iota(jnp.int32, (CHUNK, 2 * CHUNK), 0)
    col_j = lax.broadcasted_iota(jnp.int32, (CHUNK, 2 * CHUNK), 1) % CHUNK
    causal = col_j <= row_i
    ws_m = []
    for p in range(n_heads // 2):
        pair = jnp.concatenate([ws_ref[2 * p], ws_ref[2 * p + 1]], axis=1)
        ws_m.append(jnp.where(causal, pair, 0.0).astype(BF16))
    low_half = lax.broadcasted_iota(jnp.int32, (CHUNK, LANES), 1) < HEAD_DIM
    lane_group = _lane_group((SUB_ROWS, w_pool), len(POOL_WINDOWS))
    window = _select_by_group([jnp.full((SUB_ROWS, w_pool), w, jnp.int32) for w in POOL_WINDOWS],
                              lane_group)
    row_id = lax.broadcasted_iota(jnp.int32, (SUB_ROWS, w_pool), 0)
    bias = bs_ref[...]
    n_sub = rows // SUB_ROWS
    n_ch = SUB_ROWS // CHUNK
    hist = {"z": zc_ref[...], "p": pc_ref[...]}
    projs, mids, x_mid = {}, {}, {}

    def project(jj):
        h = (x_ref[0, jj * SUB_ROWS:(jj + 1) * SUB_ROWS, :] * sc1 + sh).astype(BF16)
        projs[jj] = jnp.concatenate([_dot(h, w[...]) for w in win_refs], axis=1)

    def heads(j):
        proj = projs.pop(j)
        o0 = 0
        a_b = proj[:, o0:o0 + w_conv]; o0 += w_conv
        a_c = proj[:, o0:o0 + w_conv]; o0 += w_conv
        a_x = proj[:, o0:o0 + w_conv]; o0 += w_conv
        g_u = proj[:, o0:o0 + w_gmlp]; o0 += w_gmlp
        g_v = proj[:, o0:o0 + w_gmlp]; o0 += w_gmlp
        p_x = proj[:, o0:o0 + w_pool]

        z = a_c * a_x
        z_ext = jnp.concatenate([hist["z"], z], axis=0)
        conv = z_ext * cw[CONV_WIDTH - 1:CONV_WIDTH, :]
        for k in range(1, CONV_WIDTH):
            conv = conv + pltpu.roll(z_ext, k, 0) * cw[CONV_WIDTH - 1 - k:CONV_WIDTH - k, :]
        y_a = a_b * conv[CONV_HIST:, :]
        hist["z"] = z[SUB_ROWS - CONV_HIST:, :]

        v_n = _layer_norm(g_v, sgu_g, sgu_b)
        s_cols = []
        for p in range(n_heads // 2):
            cols = [v_n[c * CHUNK:(c + 1) * CHUNK, p * LANES:(p + 1) * LANES] for c in range(n_ch)]
            rhs = jnp.concatenate(
                [jnp.concatenate([jnp.where(low_half, col, 0.0) for col in cols], axis=1),
                 jnp.concatenate([jnp.where(low_half, 0.0, col) for col in cols], axis=1)],
                axis=0).astype(BF16)
            s_cols.append(_dot(ws_m[p], rhs))

        run = jnp.concatenate([hist["p"], p_x], axis=0)
        sums = []
        for w in POOL_WINDOWS:
            run = run + pltpu.roll(run, w // 2, 0)
            sums.append(run[POOL_HIST:, :])
        count = jnp.minimum(window, t * rows + j * SUB_ROWS + row_id + 1).astype(F32)
        d = (_select_by_group(sums, lane_group) / count - p_x).astype(BF16)
        y_c_raw = _dot(d, pw_ref[...])
        hist["p"] = p_x[SUB_ROWS - POOL_HIST:, :]
        mids[j] = (y_a, g_u, s_cols, y_c_raw)

    def mix_out(j):
        y_a, g_u, s_cols, y_c_raw = mids.pop(j)
        s = jnp.concatenate(
            [jnp.concatenate([sc[:, c * LANES:(c + 1) * LANES] for sc in s_cols], axis=1) + bias
             for c in range(n_ch)], axis=0)
        mix = jnp.concatenate([y_a, g_u * s, y_c_raw * pool_scale], axis=1).astype(BF16)
        out = jnp.concatenate([_dot(mix, w[...]) for w in wout_refs], axis=1)
        x = x_ref[0, j * SUB_ROWS:(j + 1) * SUB_ROWS, :]
        x_mid[j] = _layer_norm(alpha * x + gate * out, ln_g, ln_b)

    def store(j, y):
        o_ref[0, j * SUB_ROWS:(j + 1) * SUB_ROWS, :] = y

    _layer_pipeline(n_sub, project, heads, mix_out, x_mid, sh2, sc2, gate2, wup_refs, wdn_refs,
                    ln2_g, ln2_b, alpha, store)

    zc_ref[...] = hist["z"]
    pc_ref[...] = hist["p"]
    ncv_ref[0] = hist["z"][CONV_HIST - (CONV_WIDTH - 1):, :]
    npl_ref[0] = hist["p"][POOL_HIST - (max(POOL_WINDOWS) - 1):, :]


def _layer_prompt_call(x, mod, w_in, conv_w, sgu_g, sgu_b, w_s, bs_full, pool_wbd, pool_scale,
                       w_out, ln_g, ln_b, wup, wdn, ln2_g, ln2_b, *, alpha, layer, rows):
    batch, seq, c_dim = x.shape
    w_conv, w_pool = conv_w.shape[2], pool_scale.shape[1]
    conv_buf = CONV_WIDTH - 1
    pool_buf = max(POOL_WINDOWS) - 1
    in_specs = _column_blocks(w_in, layer, WEIGHT_COLS)
    out_specs = _column_blocks(w_out, layer, WEIGHT_COLS)
    up_specs = _column_blocks(wup, layer, WEIGHT_COLS)
    dn_specs = _column_blocks(wdn, layer, WEIGHT_COLS)
    kern = functools.partial(_layer_prompt_kernel, alpha=alpha, layer=layer, n_in=len(in_specs),
                             n_out=len(out_specs), n_up=len(up_specs), n_dn=len(dn_specs))
    return pl.pallas_call(
        kern,
        grid=(batch, seq // rows),
        in_specs=[
            pl.BlockSpec((1, rows, c_dim), lambda b, t: (b, t, 0)),
            pl.BlockSpec((None, batch, mod.shape[2]), lambda b, t: (layer, 0, 0)),
            *in_specs,
            _whole(conv_w.shape),
            _whole(sgu_g.shape),
            _whole(sgu_b.shape),
            _layer_block(w_s.shape[1:], layer),
            _layer_block(bs_full.shape[1:], layer),
            _layer_block(pool_wbd.shape[1:], layer),
            _whole(pool_scale.shape),
            *out_specs,
            _whole(ln_g.shape),
            _whole(ln_b.shape),
            *up_specs, *dn_specs,
            _whole(ln2_g.shape),
            _whole(ln2_b.shape),
        ],
        out_specs=[
            pl.BlockSpec((1, rows, c_dim), lambda b, t: (b, t, 0)),
            pl.BlockSpec((1, conv_buf, w_conv), lambda b, t: (b, 0, 0)),
            pl.BlockSpec((1, pool_buf, w_pool), lambda b, t: (b, 0, 0)),
        ],
        out_shape=[
            jax.ShapeDtypeStruct((batch, seq, c_dim), F32),
            jax.ShapeDtypeStruct((batch, conv_buf, w_conv), F32),
            jax.ShapeDtypeStruct((batch, pool_buf, w_pool), F32),
        ],
        scratch_shapes=[
            pltpu.VMEM((CONV_HIST, w_conv), F32),
            pltpu.VMEM((POOL_HIST, w_pool), F32),
        ],
        compiler_params=pltpu.CompilerParams(
            dimension_semantics=("arbitrary", "arbitrary"),
            vmem_limit_bytes=VMEM_LIMIT_BYTES),
        name="layer_prompt",
    )(x, mod, *([w_in] * len(in_specs)), conv_w, sgu_g, sgu_b, w_s, bs_full, pool_wbd, pool_scale,
      *([w_out] * len(out_specs)), ln_g, ln_b, *([wup] * len(up_specs)),
      *([wdn] * len(dn_specs)), ln2_g, ln2_b)


def _layer_sample_kernel(x_ref, mod_ref, cst_ref, pst_ref, *refs, alpha, layer, n_in, n_out,
                         n_up, n_dn, x_batch_major, out_batch_major):
    win_refs, refs = refs[:n_in], refs[n_in:]
    cw_ref, sg_ref, sb_ref, wst_ref, bst_ref, pw_ref, ps_ref = refs[:7]
    wout_refs, refs = refs[7:7 + n_out], refs[7 + n_out:]
    g_ref, b_ref = refs[:2]
    wup_refs, refs = refs[2:2 + n_up], refs[2 + n_up:]
    wdn_refs, refs = refs[:n_dn], refs[n_dn:]
    g2_ref, b2_ref, o_ref, ncv_ref, npl_ref, v_ref = refs
    xt = jnp.swapaxes(x_ref[...], 0, 1) if x_batch_major else None
    cst = jnp.swapaxes(cst_ref[...], 0, 1)
    pst = jnp.swapaxes(pst_ref[...], 0, 1)
    t_dim, b_dim = v_ref.shape[1], v_ref.shape[0]
    c_dim = x_ref.shape[2]
    w_conv, w_gmlp, w_pool = cw_ref.shape[2], sg_ref.shape[1], ps_ref.shape[1]
    n_valid = pst.shape[0]
    step = SUB_ROWS // b_dim
    n_sub = t_dim // step

    m = mod_ref[...]
    sh, sc1, gate = m[:, :c_dim], 1.0 + m[:, c_dim:2 * c_dim], m[:, 2 * c_dim:3 * c_dim]
    sh2, sc2, gate2 = (m[:, 3 * c_dim:4 * c_dim], 1.0 + m[:, 4 * c_dim:5 * c_dim],
                       m[:, 5 * c_dim:])
    cw = cw_ref[layer]
    sgu_g, sgu_b = sg_ref[layer:layer + 1, :], sb_ref[layer:layer + 1, :]
    pool_scale = ps_ref[layer:layer + 1, :]
    ln_g, ln_b = g_ref[layer:layer + 1, :], b_ref[layer:layer + 1, :]
    ln2_g, ln2_b = g2_ref[layer:layer + 1, :], b2_ref[layer:layer + 1, :]
    lane_group = _lane_group((b_dim, w_pool), len(POOL_WINDOWS))

    z_ext = [cst[i] for i in range(CONV_WIDTH - 1)]
    p_ext = [pst[i] for i in range(n_valid)]
    v_all = []
    projs, mids, x_mid, outs = {}, {}, {}, {}

    def x_rows(j):
        if x_batch_major:
            return xt[j * step:(j + 1) * step]
        return x_ref[j * step:(j + 1) * step]

    def project(j):
        h = (x_rows(j) * sc1 + sh).reshape(step * b_dim, c_dim).astype(BF16)
        projs[j] = jnp.concatenate([_dot(h, w[...]) for w in win_refs], axis=1).reshape(
            step, b_dim, -1)

    def heads(j):
        proj = projs.pop(j)
        o0 = 0
        a_b = proj[:, :, o0:o0 + w_conv]; o0 += w_conv
        a_c = proj[:, :, o0:o0 + w_conv]; o0 += w_conv
        a_x = proj[:, :, o0:o0 + w_conv]; o0 += w_conv
        g_u = proj[:, :, o0:o0 + w_gmlp]; o0 += w_gmlp
        g_v = proj[:, :, o0:o0 + w_gmlp]; o0 += w_gmlp
        p_x = proj[:, :, o0:o0 + w_pool]
        z = a_c * a_x
        v_n = _layer_norm(g_v, sgu_g, sgu_b)
        y_a, y_b, d = [], [], []
        for i in range(step):
            pos = j * step + i
            z_ext.append(z[i])
            conv = z_ext[pos] * cw[0:1, :]
            for k in range(1, CONV_WIDTH):
                conv = conv + z_ext[pos + k] * cw[k:k + 1, :]
            y_a.append(a_b[i] * conv)
            v_all.append(v_n[i])
            s = bst_ref[pos:pos + 1, :] + wst_ref[pos, 0:1, :] * v_all[0]
            for q in range(1, pos + 1):
                s = s + wst_ref[pos, q:q + 1, :] * v_all[q]
            y_b.append(g_u[i] * s)
            p_ext.append(p_x[i])
            last = n_valid + pos
            means, run, k = [], None, 0
            for w in POOL_WINDOWS:
                while k < w:
                    run = p_ext[last - k] if run is None else run + p_ext[last - k]
                    k += 1
                means.append(run / float(min(w, pos + 1 + n_valid)))
            d.append(_select_by_group(means, lane_group) - p_x[i])
        d = jnp.stack(d).reshape(step * b_dim, w_pool).astype(BF16)
        mids[j] = (jnp.stack(y_a), jnp.stack(y_b), _dot(d, pw_ref[...]))

    def mix_out(j):
        y_a, y_b, y_c_raw = mids.pop(j)
        y_c = (y_c_raw * pool_scale).reshape(step, b_dim, w_pool)
        mix = jnp.concatenate([y_a, y_b, y_c], axis=2).reshape(step * b_dim, c_dim).astype(BF16)
        out = jnp.concatenate([_dot(mix, w[...]) for w in wout_refs], axis=1)
        x_mid[j] = _layer_norm(alpha * x_rows(j) + gate * out.reshape(step, b_dim, c_dim),
                               ln_g, ln_b)

    def store(j, y):
        if out_batch_major:
            outs[j] = y
        else:
            o_ref[j * step:(j + 1) * step] = y

    _layer_pipeline(n_sub, project, heads, mix_out, x_mid, sh2, sc2, gate2, wup_refs, wdn_refs,
                    ln2_g, ln2_b, alpha, store)

    if out_batch_major:
        o_ref[...] = jnp.swapaxes(jnp.concatenate([outs[j] for j in range(n_sub)], axis=0), 0, 1)
    ncv_ref[...] = jnp.swapaxes(jnp.stack(z_ext[len(z_ext) - (CONV_WIDTH - 1):]), 0, 1)
    npl_ref[...] = jnp.swapaxes(jnp.stack(p_ext[len(p_ext) - n_valid:]), 0, 1)
    v_ref[...] = jnp.swapaxes(jnp.stack(v_all), 0, 1)


def _layer_sample_call(x, mod, cst, pst, w_in, conv_w, sgu_g, sgu_b, wst, bs_full, pool_wbd,
                       pool_scale, w_out, ln_g, ln_b, wup, wdn, ln2_g, ln2_b, *, alpha, layer,
                       x_batch_major, out_batch_major):
    if x_batch_major:
        b_dim, t_dim, c_dim = x.shape
    else:
        t_dim, b_dim, c_dim = x.shape
    w_gmlp = sgu_g.shape[1]
    in_specs = _column_blocks(w_in, layer, WEIGHT_COLS)
    out_specs = _column_blocks(w_out, layer, WEIGHT_COLS)
    up_specs = _column_blocks(wup, layer, WEIGHT_COLS)
    dn_specs = _column_blocks(wdn, layer, WEIGHT_COLS)
    kern = functools.partial(_layer_sample_kernel, alpha=alpha, layer=layer, n_in=len(in_specs),
                             n_out=len(out_specs), n_up=len(up_specs), n_dn=len(dn_specs),
                             x_batch_major=x_batch_major, out_batch_major=out_batch_major)
    out_shape = (b_dim, t_dim, c_dim) if out_batch_major else (t_dim, b_dim, c_dim)
    whole3 = lambda i: (0, 0, 0)
    return pl.pallas_call(
        kern,
        grid=(1,),
        in_specs=[
            pl.BlockSpec(x.shape, whole3),
            pl.BlockSpec((None, b_dim, mod.shape[2]), lambda i: (layer, 0, 0)),
            _layer_block(cst.shape[1:], layer),
            _layer_block(pst.shape[1:], layer),
            *in_specs,
            _whole(conv_w.shape),
            _whole(sgu_g.shape),
            _whole(sgu_b.shape),
            _layer_block(wst.shape[1:], layer),
            pl.BlockSpec((None, t_dim, w_gmlp), lambda i: (layer, 0, 0)),
            _layer_block(pool_wbd.shape[1:], layer),
            _whole(pool_scale.shape),
            *out_specs,
            _whole(ln_g.shape),
            _whole(ln_b.shape),
            *up_specs, *dn_specs,
            _whole(ln2_g.shape),
            _whole(ln2_b.shape),
        ],
        out_specs=[
            pl.BlockSpec(out_shape, whole3),
            pl.BlockSpec(cst.shape[1:], whole3),
            pl.BlockSpec(pst.shape[1:], whole3),
            pl.BlockSpec((b_dim, t_dim, w_gmlp), whole3),
        ],
        out_shape=[
            jax.ShapeDtypeStruct(out_shape, F32),
            jax.ShapeDtypeStruct(cst.shape[1:], F32),
            jax.ShapeDtypeStruct(pst.shape[1:], F32),
            jax.ShapeDtypeStruct((b_dim, t_dim, w_gmlp), F32),
        ],
        compiler_params=pltpu.CompilerParams(
            dimension_semantics=("arbitrary",),
            vmem_limit_bytes=VMEM_LIMIT_BYTES),
        name="layer_sample",
    )(x, mod, cst, pst, *([w_in] * len(in_specs)), conv_w, sgu_g, sgu_b, wst, bs_full, pool_wbd,
      pool_scale, *([w_out] * len(out_specs)), ln_g, ln_b, *([wup] * len(up_specs)),
      *([wdn] * len(dn_specs)), ln2_g, ln2_b)


def kernel(x_prompt, x_sample, c_prompt, c_sample, state_conv, state_pool, w_ada, b_ada, w_in,
           conv_w, sgu_g, sgu_b, w_s, b_s, pool_w, pool_scale, w_out, ln1_g, ln1_b, w_up,
           w_down, ln2_g, ln2_b):
    depth = w_in.shape[0]
    seq = x_prompt.shape[1]
    dec_batch, dec_seq, _ = x_sample.shape
    n_groups, group_dim, _ = pool_w.shape[1:]
    w_pool = n_groups * group_dim
    alpha = (2.0 * depth) ** 0.25
    assert seq % PROMPT_ROWS == 0 and PROMPT_ROWS % SUB_ROWS == 0 and SUB_ROWS % CHUNK == 0
    assert (dec_seq * dec_batch) % SUB_ROWS == 0 and SUB_ROWS % dec_batch == 0

    mod_p, mod_s = _ada_call(c_prompt, c_sample, w_ada, b_ada)

    w_in_b, w_out_b = w_in.astype(BF16), w_out.astype(BF16)
    wup_b, wdn_b = w_up.astype(BF16), w_down.astype(BF16)
    bs_full = jnp.repeat(jnp.swapaxes(b_s, 1, 2), HEAD_DIM, axis=2)
    ws_small = jnp.repeat(jnp.transpose(w_s[:, :, :dec_seq, :dec_seq], (0, 2, 3, 1)), HEAD_DIM, axis=3)
    eye = jnp.eye(n_groups, dtype=F32)
    pool_wbd = (eye[None, :, None, :, None] * pool_w[:, :, :, None, :]).reshape(
        depth, w_pool, w_pool).astype(BF16)

    xp, xs = x_prompt, x_sample
    conv_p, pool_p, conv_s, pool_s, v_s = [], [], [], [], []
    for l in range(depth):
        xp, ncv, npl = _layer_prompt_call(
            xp, mod_p, w_in_b, conv_w, sgu_g, sgu_b, w_s, bs_full, pool_wbd, pool_scale,
            w_out_b, ln1_g, ln1_b, wup_b, wdn_b, ln2_g, ln2_b, alpha=alpha, layer=l,
            rows=PROMPT_ROWS)
        conv_p.append(ncv)
        pool_p.append(npl)

        xs, ncv, npl, v_rows = _layer_sample_call(
            xs, mod_s, state_conv, state_pool, w_in_b, conv_w, sgu_g, sgu_b, ws_small, bs_full,
            pool_wbd, pool_scale, w_out_b, ln1_g, ln1_b, wup_b, wdn_b, ln2_g, ln2_b, alpha=alpha,
            layer=l, x_batch_major=(l == 0), out_batch_major=(l == depth - 1))
        conv_s.append(ncv)
        pool_s.append(npl)
        v_s.append(v_rows)

    return (xp, xs, jnp.stack(conv_p), jnp.stack(pool_p), jnp.stack(conv_s), jnp.stack(pool_s),
            jnp.stack(v_s))
```

```python
import functools

import jax
import jax.numpy as jnp
from jax import lax
from jax.experimental import pallas as pl
from jax.experimental.pallas import tpu as pltpu

F32 = jnp.float32
BF16 = jnp.bfloat16

LN_EPS = 1e-5
HEAD_DIM = 64
CHUNK = 128
POOL_WINDOWS = (2, 4, 8, 16)
CONV_WIDTH = 3
LANES = 128
SUBLANES = 8
VMEM_LIMIT_BYTES = 56 * 1024 * 1024

PROMPT_ROWS = 1024
SUB_ROWS = 256
UP_AHEAD = 2
PROJ_AHEAD = 2
MIX_OUT_AT = 2
HEADS_AT = 5
WEIGHT_COLS = 512
CONV_HIST = SUBLANES
POOL_HIST = 2 * SUBLANES


def _layer_norm(y, g, b):
    mu = jnp.mean(y, axis=-1, keepdims=True)
    d = y - mu
    var = jnp.mean(d * d, axis=-1, keepdims=True)
    return d * lax.rsqrt(var + LN_EPS) * g + b


def _dot(a, b):
    return jnp.dot(a, b, preferred_element_type=F32)


def _layer_block(shape, layer):
    zeros = (0,) * len(shape)
    return pl.BlockSpec((None,) + tuple(shape), lambda *_: (layer,) + zeros,
                        pipeline_mode=pl.Buffered(1))


def _whole(shape):
    zeros = (0,) * len(shape)
    return pl.BlockSpec(tuple(shape), lambda *_: zeros, pipeline_mode=pl.Buffered(1))


def _column_blocks(w, layer, width):
    rows, cols = w.shape[1:]
    return [pl.BlockSpec((None, rows, width), lambda *_, n=n: (layer, 0, n),
                         pipeline_mode=pl.Buffered(1)) for n in range(cols // width)]


def _lane_group(shape, n_groups):
    return lax.broadcasted_iota(jnp.int32, shape, len(shape) - 1) // (shape[-1] // n_groups)


def _select_by_group(values, lane_group):
    out = values[-1]
    for g in range(len(values) - 2, -1, -1):
        out = jnp.where(lane_group == g, values[g], out)
    return out


def _ada_kernel(cp_ref, cs_ref, w_ref, b_ref, op_ref, os_ref):
    l, k = pl.program_id(0), pl.program_id(1)
    bp = cp_ref.shape[0]
    c = jnp.concatenate([cp_ref[...], cs_ref[...]], axis=0)
    part = _dot(jax.nn.silu(c).astype(BF16), w_ref[0].astype(BF16))
    part_p, part_s = part[:bp], part[bp:]

    @pl.when(k == 0)
    def _():
        bias = b_ref[pl.ds(l, 1), :]
        op_ref[0] = part_p + bias
        os_ref[0] = part_s + bias

    @pl.when(k > 0)
    def _():
        op_ref[0] += part_p
        os_ref[0] += part_s


def _ada_call(c_prompt, c_sample, w_ada, b_ada, row_tile=256):
    depth, d_model, n_mod = w_ada.shape
    bp, bs = c_prompt.shape[0], c_sample.shape[0]
    return pl.pallas_call(
        _ada_kernel,
        grid=(depth, d_model // row_tile),
        in_specs=[
            pl.BlockSpec((bp, row_tile), lambda l, k: (0, k)),
            pl.BlockSpec((bs, row_tile), lambda l, k: (0, k)),
            pl.BlockSpec((1, row_tile, n_mod), lambda l, k: (l, k, 0)),
            _whole(b_ada.shape),
        ],
        out_specs=[
            pl.BlockSpec((1, bp, n_mod), lambda l, k: (l, 0, 0)),
            pl.BlockSpec((1, bs, n_mod), lambda l, k: (l, 0, 0)),
        ],
        out_shape=[
            jax.ShapeDtypeStruct((depth, bp, n_mod), F32),
            jax.ShapeDtypeStruct((depth, bs, n_mod), F32),
        ],
        compiler_params=pltpu.CompilerParams(
            dimension_semantics=("arbitrary", "arbitrary"),
            vmem_limit_bytes=VMEM_LIMIT_BYTES),
        name="ada_mod",
    )(c_prompt, c_sample, w_ada, b_ada)


def _layer_pipeline(n_sub, project, heads, mix_out, x_mid, sh2, sc2, gate2, wup_refs, wdn_refs,
                    ln2_g, ln2_b, alpha, store):
    n_up, n_dn = len(wup_refs), len(wdn_refs)
    chunk = wup_refs[0].shape[1]

    def heads_and_project(j):
        heads(j)
        if j + PROJ_AHEAD < n_sub:
            project(j + PROJ_AHEAD)

    for jj in range(min(PROJ_AHEAD, n_sub)):
        project(jj)
    heads_and_project(0)
    mix_out(0)
    if n_sub > 1:
        heads_and_project(1)

    stages = [(j, k) for j in range(n_sub) for k in range(n_up)]
    hs, ups = {}, {}

    def issue_up(s):
        j, k = stages[s]
        if j not in hs:
            xm = x_mid[j]
            hs[j] = (xm * sc2 + sh2).reshape(-1, xm.shape[-1]).astype(BF16)
        ups[s] = _dot(hs[j], wup_refs[k][...])

    for s in range(min(UP_AHEAD, len(stages))):
        issue_up(s)
    acc = [None] * n_dn
    for s, (j, k) in enumerate(stages):
        if k == MIX_OUT_AT and j + 1 < n_sub:
            mix_out(j + 1)
        if k == HEADS_AT and j + 2 < n_sub:
            heads_and_project(j + 2)
        if s + UP_AHEAD < len(stages):
            issue_up(s + UP_AHEAD)
        a = jnp.square(jnp.maximum(ups.pop(s), 0.0)).astype(BF16)
        for n in range(n_dn):
            part = _dot(a, wdn_refs[n][k * chunk:(k + 1) * chunk, :])
            acc[n] = part if acc[n] is None else acc[n] + part
        if k == n_up - 1:
            xm = x_mid.pop(j)
            f = jnp.concatenate(acc, axis=1).reshape(xm.shape)
            store(j, _layer_norm(alpha * xm + gate2 * f, ln2_g, ln2_b))
            acc = [None] * n_dn


def _layer_prompt_kernel(x_ref, mod_ref, *refs, alpha, layer, n_in, n_out, n_up, n_dn):
    win_refs, refs = refs[:n_in], refs[n_in:]
    cw_ref, sg_ref, sb_ref, ws_ref, bs_ref, pw_ref, ps_ref = refs[:7]
    wout_refs, refs = refs[7:7 + n_out], refs[7 + n_out:]
    g_ref, b_ref = refs[:2]
    wup_refs, refs = refs[2:2 + n_up], refs[2 + n_up:]
    wdn_refs, refs = refs[:n_dn], refs[n_dn:]
    g2_ref, b2_ref, o_ref, ncv_ref, npl_ref, zc_ref, pc_ref = refs
    bi, t = pl.program_id(0), pl.program_id(1)
    rows, c_dim = x_ref.shape[1], x_ref.shape[2]
    w_conv, w_gmlp, w_pool = cw_ref.shape[2], sg_ref.shape[1], ps_ref.shape[1]
    n_heads = ws_ref.shape[0]

    @pl.when(t == 0)
    def _():
        zc_ref[...] = jnp.zeros(zc_ref.shape, F32)
        pc_ref[...] = jnp.zeros(pc_ref.shape, F32)

    m = mod_ref[pl.ds(bi, 1), :]
    sh, sc1, gate = m[:, :c_dim], 1.0 + m[:, c_dim:2 * c_dim], m[:, 2 * c_dim:3 * c_dim]
    sh2, sc2, gate2 = (m[:, 3 * c_dim:4 * c_dim], 1.0 + m[:, 4 * c_dim:5 * c_dim],
                       m[:, 5 * c_dim:])
    cw = cw_ref[layer]
    sgu_g, sgu_b = sg_ref[layer:layer + 1, :], sb_ref[layer:layer + 1, :]
    pool_scale = ps_ref[layer:layer + 1, :]
    ln_g, ln_b = g_ref[layer:layer + 1, :], b_ref[layer:layer + 1, :]
    ln2_g, ln2_b = g2_ref[layer:layer + 1, :], b2_ref[layer:layer + 1, :]

    row_i = lax.broadcasted_iota(jnp.int32, (CHUNK, 2 * CHUNK), 0)
    col_j = lax.broadcasted_iota(jnp.int32, (CHUNK, 2 * CHUNK), 1) % CHUNK
    causal = col_j <= row_i
    ws_m = []
    for p in range(n_heads // 2):
        pair = jnp.concatenate([ws_ref[2 * p], ws_ref[2 * p + 1]], axis=1)
        ws_m.append(jnp.where(causal, pair, 0.0).astype(BF16))
    low_half = lax.broadcasted_iota(jnp.int32, (CHUNK, LANES), 1) < HEAD_DIM
    lane_group = _lane_group((SUB_ROWS, w_pool), len(POOL_WINDOWS))
    window = _select_by_group([jnp.full((SUB_ROWS, w_pool), w, jnp.int32) for w in POOL_WINDOWS],
                              lane_group)
    row_id = lax.broadcasted_iota(jnp.int32, (SUB_ROWS, w_pool), 0)
    bias = bs_ref[...]
    n_sub = rows // SUB_ROWS
    n_ch = SUB_ROWS // CHUNK
    hist = {"z": zc_ref[...], "p": pc_ref[...]}
    projs, mids, x_mid = {}, {}, {}

    def project(jj):
        h = (x_ref[0, jj * SUB_ROWS:(jj + 1) * SUB_ROWS, :] * sc1 + sh).astype(BF16)
        projs[jj] = jnp.concatenate([_dot(h, w[...]) for w in win_refs], axis=1)

    def heads(j):
        proj = projs.pop(j)
        o0 = 0
        a_b = proj[:, o0:o0 + w_conv]; o0 += w_conv
        a_c = proj[:, o0:o0 + w_conv]; o0 += w_conv
        a_x = proj[:, o0:o0 + w_conv]; o0 += w_conv
        g_u = proj[:, o0:o0 + w_gmlp]; o0 += w_gmlp
        g_v = proj[:, o0:o0 + w_gmlp]; o0 += w_gmlp
        p_x = proj[:, o0:o0 + w_pool]

        z = a_c * a_x
        z_ext = jnp.concatenate([hist["z"], z], axis=0)
        conv = z_ext * cw[CONV_WIDTH - 1:CONV_WIDTH, :]
        for k in range(1, CONV_WIDTH):
            conv = conv + pltpu.roll(z_ext, k, 0) * cw[CONV_WIDTH - 1 - k:CONV_WIDTH - k, :]
        y_a = a_b * conv[CONV_HIST:, :]
        hist["z"] = z[SUB_ROWS - CONV_HIST:, :]

        v_n = _layer_norm(g_v, sgu_g, sgu_b)
        s_cols = []
        for p in range(n_heads // 2):
            cols = [v_n[c * CHUNK:(c + 1) * CHUNK, p * LANES:(p + 1) * LANES] for c in range(n_ch)]
            rhs = jnp.concatenate(
                [jnp.concatenate([jnp.where(low_half, col, 0.0) for col in cols], axis=1),
                 jnp.concatenate([jnp.where(low_half, 0.0, col) for col in cols], axis=1)],
                axis=0).astype(BF16)
            s_cols.append(_dot(ws_m[p], rhs))

        run = jnp.concatenate([hist["p"], p_x], axis=0)
        sums = []
        for w in POOL_WINDOWS:
            run = run + pltpu.roll(run, w // 2, 0)
            sums.append(run[POOL_HIST:, :])
        count = jnp.minimum(window, t * rows + j * SUB_ROWS + row_id + 1).astype(F32)
        d = (_select_by_group(sums, lane_group) / count - p_x).astype(BF16)
        y_c_raw = _dot(d, pw_ref[...])
        hist["p"] = p_x[SUB_ROWS - POOL_HIST:, :]
        mids[j] = (y_a, g_u, s_cols, y_c_raw)

    def mix_out(j):
        y_a, g_u, s_cols, y_c_raw = mids.pop(j)
        s = jnp.concatenate(
            [jnp.concatenate([sc[:, c * LANES:(c + 1) * LANES] for sc in s_cols], axis=1) + bias
             for c in range(n_ch)], axis=0)
        mix = jnp.concatenate([y_a, g_u * s, y_c_raw * pool_scale], axis=1).astype(BF16)
        out = jnp.concatenate([_dot(mix, w[...]) for w in wout_refs], axis=1)
        x = x_ref[0, j * SUB_ROWS:(j + 1) * SUB_ROWS, :]
        x_mid[j] = _layer_norm(alpha * x + gate * out, ln_g, ln_b)

    def store(j, y):
        o_ref[0, j * SUB_ROWS:(j + 1) * SUB_ROWS, :] = y

    _layer_pipeline(n_sub, project, heads, mix_out, x_mid, sh2, sc2, gate2, wup_refs, wdn_refs,
                    ln2_g, ln2_b, alpha, store)

    zc_ref[...] = hist["z"]
    pc_ref[...] = hist["p"]
    ncv_ref[0] = hist["z"][CONV_HIST - (CONV_WIDTH - 1):, :]
    npl_ref[0] = hist["p"][POOL_HIST - (max(POOL_WINDOWS) - 1):, :]


def _layer_prompt_call(x, mod, w_in, conv_w, sgu_g, sgu_b, w_s, bs_full, pool_wbd, pool_scale,
                       w_out, ln_g, ln_b, wup, wdn, ln2_g, ln2_b, *, alpha, layer, rows):
    batch, seq, c_dim = x.shape
    w_conv, w_pool = conv_w.shape[2], pool_scale.shape[1]
    conv_buf = CONV_WIDTH - 1
    pool_buf = max(POOL_WINDOWS) - 1
    in_specs = _column_blocks(w_in, layer, WEIGHT_COLS)
    out_specs = _column_blocks(w_out, layer, WEIGHT_COLS)
    up_specs = _column_blocks(wup, layer, WEIGHT_COLS)
    dn_specs = _column_blocks(wdn, layer, WEIGHT_COLS)
    kern = functools.partial(_layer_prompt_kernel, alpha=alpha, layer=layer, n_in=len(in_specs),
                             n_out=len(out_specs), n_up=len(up_specs), n_dn=len(dn_specs))
    return pl.pallas_call(
        kern,
        grid=(batch, seq // rows),
        in_specs=[
            pl.BlockSpec((1, rows, c_dim), lambda b, t: (b, t, 0)),
            pl.BlockSpec((None, batch, mod.shape[2]), lambda b, t: (layer, 0, 0)),
            *in_specs,
            _whole(conv_w.shape),
            _whole(sgu_g.shape),
            _whole(sgu_b.shape),
            _layer_block(w_s.shape[1:], layer),
            _layer_block(bs_full.shape[1:], layer),
            _layer_block(pool_wbd.shape[1:], layer),
            _whole(pool_scale.shape),
            *out_specs,
            _whole(ln_g.shape),
            _whole(ln_b.shape),
            *up_specs, *dn_specs,
            _whole(ln2_g.shape),
            _whole(ln2_b.shape),
        ],
        out_specs=[
            pl.BlockSpec((1, rows, c_dim), lambda b, t: (b, t, 0)),
            pl.BlockSpec((1, conv_buf, w_conv), lambda b, t: (b, 0, 0)),
            pl.BlockSpec((1, pool_buf, w_pool), lambda b, t: (b, 0, 0)),
        ],
        out_shape=[
            jax.ShapeDtypeStruct((batch, seq, c_dim), F32),
            jax.ShapeDtypeStruct((batch, conv_buf, w_conv), F32),
            jax.ShapeDtypeStruct((batch, pool_buf, w_pool), F32),
        ],
        scratch_shapes=[
            pltpu.VMEM((CONV_HIST, w_conv), F32),
            pltpu.VMEM((POOL_HIST, w_pool), F32),
        ],
        compiler_params=pltpu.CompilerParams(
            dimension_semantics=("arbitrary", "arbitrary"),
            vmem_limit_bytes=VMEM_LIMIT_BYTES),
        name="layer_prompt",
    )(x, mod, *([w_in] * len(in_specs)), conv_w, sgu_g, sgu_b, w_s, bs_full, pool_wbd, pool_scale,
      *([w_out] * len(out_specs)), ln_g, ln_b, *([wup] * len(up_specs)),
      *([wdn] * len(dn_specs)), ln2_g, ln2_b)


def _layer_sample_kernel(x_ref, mod_ref, cst_ref, pst_ref, *refs, alpha, layer, n_in, n_out,
                         n_up, n_dn, x_batch_major, out_batch_major):
    win_refs, refs = refs[:n_in], refs[n_in:]
    cw_ref, sg_ref, sb_ref, wst_ref, bst_ref, pw_ref, ps_ref = refs[:7]
    wout_refs, refs = refs[7:7 + n_out], refs[7 + n_out:]
    g_ref, b_ref = refs[:2]
    wup_refs, refs = refs[2:2 + n_up], refs[2 + n_up:]
    wdn_refs, refs = refs[:n_dn], refs[n_dn:]
    g2_ref, b2_ref, o_ref, ncv_ref, npl_ref, v_ref = refs
    xt = jnp.swapaxes(x_ref[...], 0, 1) if x_batch_major else None
    cst = jnp.swapaxes(cst_ref[...], 0, 1)
    pst = jnp.swapaxes(pst_ref[...], 0, 1)
    t_dim, b_dim = v_ref.shape[1], v_ref.shape[0]
    c_dim = x_ref.shape[2]
    w_conv, w_gmlp, w_pool = cw_ref.shape[2], sg_ref.shape[1], ps_ref.shape[1]
    n_valid = pst.shape[0]
    step = SUB_ROWS // b_dim
    n_sub = t_dim // step

    m = mod_ref[...]
    sh, sc1, gate = m[:, :c_dim], 1.0 + m[:, c_dim:2 * c_dim], m[:, 2 * c_dim:3 * c_dim]
    sh2, sc2, gate2 = (m[:, 3 * c_dim:4 * c_dim], 1.0 + m[:, 4 * c_dim:5 * c_dim],
                       m[:, 5 * c_dim:])
    cw = cw_ref[layer]
    sgu_g, sgu_b = sg_ref[layer:layer + 1, :], sb_ref[layer:layer + 1, :]
    pool_scale = ps_ref[layer:layer + 1, :]
    ln_g, ln_b = g_ref[layer:layer + 1, :], b_ref[layer:layer + 1, :]
    ln2_g, ln2_b = g2_ref[layer:layer + 1, :], b2_ref[layer:layer + 1, :]
    lane_group = _lane_group((b_dim, w_pool), len(POOL_WINDOWS))

    z_ext = [cst[i] for i in range(CONV_WIDTH - 1)]
    p_ext = [pst[i] for i in range(n_valid)]
    v_all = []
    projs, mids, x_mid, outs = {}, {}, {}, {}

    def x_rows(j):
        if x_batch_major:
            return xt[j * step:(j + 1) * step]
        return x_ref[j * step:(j + 1) * step]

    def project(j):
        h = (x_rows(j) * sc1 + sh).reshape(step * b_dim, c_dim).astype(BF16)
        projs[j] = jnp.concatenate([_dot(h, w[...]) for w in win_refs], axis=1).reshape(
            step, b_dim, -1)

    def heads(j):
        proj = projs.pop(j)
        o0 = 0
        a_b = proj[:, :, o0:o0 + w_conv]; o0 += w_conv
        a_c = proj[:, :, o0:o0 + w_conv]; o0 += w_conv
        a_x = proj[:, :, o0:o0 + w_conv]; o0 += w_conv
        g_u = proj[:, :, o0:o0 + w_gmlp]; o0 += w_gmlp
        g_v = proj[:, :, o0:o0 + w_gmlp]; o0 += w_gmlp
        p_x = proj[:, :, o0:o0 + w_pool]
        z = a_c * a_x
        v_n = _layer_norm(g_v, sgu_g, sgu_b)
        y_a, y_b, d = [], [], []
        for i in range(step):
            pos = j * step + i
            z_ext.append(z[i])
            conv = z_ext[pos] * cw[0:1, :]
            for k in range(1, CONV_WIDTH):
                conv = conv + z_ext[pos + k] * cw[k:k + 1, :]
            y_a.append(a_b[i] * conv)
            v_all.append(v_n[i])
            s = bst_ref[pos:pos + 1, :] + wst_ref[pos, 0:1, :] * v_all[0]
            for q in range(1, pos + 1):
                s = s + wst_ref[pos, q:q + 1, :] * v_all[q]
            y_b.append(g_u[i] * s)
            p_ext.append(p_x[i])
            last = n_valid + pos
            means, run, k = [], None, 0
            for w in POOL_WINDOWS:
                while k < w:
                    run = p_ext[last - k] if run is None else run + p_ext[last - k]
                    k += 1
                means.append(run / float(min(w, pos + 1 + n_valid)))
            d.append(_select_by_group(means, lane_group) - p_x[i])
        d = jnp.stack(d).reshape(step * b_dim, w_pool).astype(BF16)
        mids[j] = (jnp.stack(y_a), jnp.stack(y_b), _dot(d, pw_ref[...]))

    def mix_out(j):
        y_a, y_b, y_c_raw = mids.pop(j)
        y_c = (y_c_raw * pool_scale).reshape(step, b_dim, w_pool)
        mix = jnp.concatenate([y_a, y_b, y_c], axis=2).reshape(step * b_dim, c_dim).astype(BF16)
        out = jnp.concatenate([_dot(mix, w[...]) for w in wout_refs], axis=1)
        x_mid[j] = _layer_norm(alpha * x_rows(j) + gate * out.reshape(step, b_dim, c_dim),
                               ln_g, ln_b)

    def store(j, y):
        if out_batch_major:
            outs[j] = y
        else:
            o_ref[j * step:(j + 1) * step] = y

    _layer_pipeline(n_sub, project, heads, mix_out, x_mid, sh2, sc2, gate2, wup_refs, wdn_refs,
                    ln2_g, ln2_b, alpha, store)

    if out_batch_major:
        o_ref[...] = jnp.swapaxes(jnp.concatenate([outs[j] for j in range(n_sub)], axis=0), 0, 1)
    ncv_ref[...] = jnp.swapaxes(jnp.stack(z_ext[len(z_ext) - (CONV_WIDTH - 1):]), 0, 1)
    npl_ref[...] = jnp.swapaxes(jnp.stack(p_ext[len(p_ext) - n_valid:]), 0, 1)
    v_ref[...] = jnp.swapaxes(jnp.stack(v_all), 0, 1)


def _layer_sample_call(x, mod, cst, pst, w_in, conv_w, sgu_g, sgu_b, wst, bs_full, pool_wbd,
                       pool_scale, w_out, ln_g, ln_b, wup, wdn, ln2_g, ln2_b, *, alpha, layer,
                       x_batch_major, out_batch_major):
    if x_batch_major:
        b_dim, t_dim, c_dim = x.shape
    else:
        t_dim, b_dim, c_dim = x.shape
    w_gmlp = sgu_g.shape[1]
    in_specs = _column_blocks(w_in, layer, WEIGHT_COLS)
    out_specs = _column_blocks(w_out, layer, WEIGHT_COLS)
    up_specs = _column_blocks(wup, layer, WEIGHT_COLS)
    dn_specs = _column_blocks(wdn, layer, WEIGHT_COLS)
    kern = functools.partial(_layer_sample_kernel, alpha=alpha, layer=layer, n_in=len(in_specs),
                             n_out=len(out_specs), n_up=len(up_specs), n_dn=len(dn_specs),
                             x_batch_major=x_batch_major, out_batch_major=out_batch_major)
    out_shape = (b_dim, t_dim, c_dim) if out_batch_major else (t_dim, b_dim, c_dim)
    whole3 = lambda i: (0, 0, 0)
    return pl.pallas_call(
        kern,
        grid=(1,),
        in_specs=[
            pl.BlockSpec(x.shape, whole3),
            pl.BlockSpec((None, b_dim, mod.shape[2]), lambda i: (layer, 0, 0)),
            _layer_block(cst.shape[1:], layer),
            _layer_block(pst.shape[1:], layer),
            *in_specs,
            _whole(conv_w.shape),
            _whole(sgu_g.shape),
            _whole(sgu_b.shape),
            _layer_block(wst.shape[1:], layer),
            pl.BlockSpec((None, t_dim, w_gmlp), lambda i: (layer, 0, 0)),
            _layer_block(pool_wbd.shape[1:], layer),
            _whole(pool_scale.shape),
            *out_specs,
            _whole(ln_g.shape),
            _whole(ln_b.shape),
            *up_specs, *dn_specs,
            _whole(ln2_g.shape),
            _whole(ln2_b.shape),
        ],
        out_specs=[
            pl.BlockSpec(out_shape, whole3),
            pl.BlockSpec(cst.shape[1:], whole3),
            pl.BlockSpec(pst.shape[1:], whole3),
            pl.BlockSpec((b_dim, t_dim, w_gmlp), whole3),
        ],
        out_shape=[
            jax.ShapeDtypeStruct(out_shape, F32),
            jax.ShapeDtypeStruct(cst.shape[1:], F32),
            jax.ShapeDtypeStruct(pst.shape[1:], F32),
            jax.ShapeDtypeStruct((b_dim, t_dim, w_gmlp), F32),
        ],
        compiler_params=pltpu.CompilerParams(
            dimension_semantics=("arbitrary",),
            vmem_limit_bytes=VMEM_LIMIT_BYTES),
        name="layer_sample",
    )(x, mod, cst, pst, *([w_in] * len(in_specs)), conv_w, sgu_g, sgu_b, wst, bs_full, pool_wbd,
      pool_scale, *([w_out] * len(out_specs)), ln_g, ln_b, *([wup] * len(up_specs)),
      *([wdn] * len(dn_specs)), ln2_g, ln2_b)


def kernel(x_prompt, x_sample, c_prompt, c_sample, state_conv, state_pool, w_ada, b_ada, w_in,
           conv_w, sgu_g, sgu_b, w_s, b_s, pool_w, pool_scale, w_out, ln1_g, ln1_b, w_up,
           w_down, ln2_g, ln2_b):
    depth = w_in.shape[0]
    seq = x_prompt.shape[1]
    dec_batch, dec_seq, _ = x_sample.shape
    n_groups, group_dim, _ = pool_w.shape[1:]
    w_pool = n_groups * group_dim
    alpha = (2.0 * depth) ** 0.25
    assert seq % PROMPT_ROWS == 0 and PROMPT_ROWS % SUB_ROWS == 0 and SUB_ROWS % CHUNK == 0
    assert (dec_seq * dec_batch) % SUB_ROWS == 0 and SUB_ROWS % dec_batch == 0

    mod_p, mod_s = _ada_call(c_prompt, c_sample, w_ada, b_ada)

    w_in_b, w_out_b = w_in.astype(BF16), w_out.astype(BF16)
    wup_b, wdn_b = w_up.astype(BF16), w_down.astype(BF16)
    bs_full = jnp.repeat(jnp.swapaxes(b_s, 1, 2), HEAD_DIM, axis=2)
    ws_small = jnp.repeat(jnp.transpose(w_s[:, :, :dec_seq, :dec_seq], (0, 2, 3, 1)), HEAD_DIM, axis=3)
    eye = jnp.eye(n_groups, dtype=F32)
    pool_wbd = (eye[None, :, None, :, None] * pool_w[:, :, :, None, :]).reshape(
        depth, w_pool, w_pool).astype(BF16)

    xp, xs = x_prompt, x_sample
    conv_p, pool_p, conv_s, pool_s, v_s = [], [], [], [], []
    for l in range(depth):
        xp, ncv, npl = _layer_prompt_call(
            xp, mod_p, w_in_b, conv_w, sgu_g, sgu_b, w_s, bs_full, pool_wbd, pool_scale,
            w_out_b, ln1_g, ln1_b, wup_b, wdn_b, ln2_g, ln2_b, alpha=alpha, layer=l,
            rows=PROMPT_ROWS)
        conv_p.append(ncv)
        pool_p.append(npl)

        xs, ncv, npl, v_rows = _layer_sample_call(
            xs, mod_s, state_conv, state_pool, w_in_b, conv_w, sgu_g, sgu_b, ws_small, bs_full,
            pool_wbd, pool_scale, w_out_b, ln1_g, ln1_b, wup_b, wdn_b, ln2_g, ln2_b, alpha=alpha,
            layer=l, x_batch_major=(l == 0), out_batch_major=(l == depth - 1))
        conv_s.append(ncv)
        pool_s.append(npl)
        v_s.append(v_rows)

    return (xp, xs, jnp.stack(conv_p), jnp.stack(pool_p), jnp.stack(conv_s), jnp.stack(pool_s),
            jnp.stack(v_s))
```

```python
import functools

import jax
import jax.numpy as jnp
from jax import lax
from jax.experimental import pallas as pl
from jax.experimental.pallas import tpu as pltpu

F32 = jnp.float32
BF16 = jnp.bfloat16

LN_EPS = 1e-5
HEAD_DIM = 64
CHUNK = 128
POOL_WINDOWS = (2, 4, 8, 16)
CONV_WIDTH = 3
LANES = 128
SUBLANES = 8
VMEM_LIMIT_BYTES = 56 * 1024 * 1024

PROMPT_ROWS = 1024
SUB_ROWS = 256
UP_AHEAD = 2
PROJ_AHEAD = 2
MIX_OUT_AT = 2
HEADS_AT = 5
WEIGHT_COLS = 512
CONV_HIST = SUBLANES
POOL_HIST = 2 * SUBLANES


def _layer_norm(y, g, b):
    mu = jnp.mean(y, axis=-1, keepdims=True)
    d = y - mu
    var = jnp.mean(d * d, axis=-1, keepdims=True)
    return d * lax.rsqrt(var + LN_EPS) * g + b


def _dot(a, b):
    return jnp.dot(a, b, preferred_element_type=F32)


def _layer_block(shape, layer):
    zeros = (0,) * len(shape)
    return pl.BlockSpec((None,) + tuple(shape), lambda *_: (layer,) + zeros,
                        pipeline_mode=pl.Buffered(1))


def _whole(shape):
    zeros = (0,) * len(shape)
    return pl.BlockSpec(tuple(shape), lambda *_: zeros, pipeline_mode=pl.Buffered(1))


def _column_blocks(w, width):
    rows, cols = w.shape
    return [pl.BlockSpec((rows, width), lambda *_, n=n: (0, n), pipeline_mode=pl.Buffered(1))
            for n in range(cols // width)]


def _lane_group(shape, n_groups):
    return lax.broadcasted_iota(jnp.int32, shape, len(shape) - 1) // (shape[-1] // n_groups)


def _select_by_group(values, lane_group):
    out = values[-1]
    for g in range(len(values) - 2, -1, -1):
        out = jnp.where(lane_group == g, values[g], out)
    return out


def _ada_kernel(cp_ref, cs_ref, w_ref, b_ref, op_ref, os_ref):
    l, k = pl.program_id(0), pl.program_id(1)
    bp = cp_ref.shape[0]
    c = jnp.concatenate([cp_ref[...], cs_ref[...]], axis=0)
    part = _dot(jax.nn.silu(c).astype(BF16), w_ref[0].astype(BF16))
    part_p, part_s = part[:bp], part[bp:]

    @pl.when(k == 0)
    def _():
        bias = b_ref[pl.ds(l, 1), :]
        op_ref[0] = part_p + bias
        os_ref[0] = part_s + bias

    @pl.when(k > 0)
    def _():
        op_ref[0] += part_p
        os_ref[0] += part_s


def _ada_call(c_prompt, c_sample, w_ada, b_ada, row_tile=256):
    depth, d_model, n_mod = w_ada.shape
    bp, bs = c_prompt.shape[0], c_sample.shape[0]
    return pl.pallas_call(
        _ada_kernel,
        grid=(depth, d_model // row_tile),
        in_specs=[
            pl.BlockSpec((bp, row_tile), lambda l, k: (0, k)),
            pl.BlockSpec((bs, row_tile), lambda l, k: (0, k)),
            pl.BlockSpec((1, row_tile, n_mod), lambda l, k: (l, k, 0)),
            _whole(b_ada.shape),
        ],
        out_specs=[
            pl.BlockSpec((1, bp, n_mod), lambda l, k: (l, 0, 0)),
            pl.BlockSpec((1, bs, n_mod), lambda l, k: (l, 0, 0)),
        ],
        out_shape=[
            jax.ShapeDtypeStruct((depth, bp, n_mod), F32),
            jax.ShapeDtypeStruct((depth, bs, n_mod), F32),
        ],
        compiler_params=pltpu.CompilerParams(
            dimension_semantics=("arbitrary", "arbitrary"),
            vmem_limit_bytes=VMEM_LIMIT_BYTES),
        name="ada_mod",
    )(c_prompt, c_sample, w_ada, b_ada)


def _layer_pipeline(n_sub, project, heads, mix_out, x_mid, sh2, sc2, gate2, wup_refs, wdn_refs,
                    ln2_g, ln2_b, alpha, store):
    n_up, n_dn = len(wup_refs), len(wdn_refs)
    chunk = wup_refs[0].shape[1]

    def heads_and_project(j):
        heads(j)
        if j + PROJ_AHEAD < n_sub:
            project(j + PROJ_AHEAD)

    for jj in range(min(PROJ_AHEAD, n_sub)):
        project(jj)
    heads_and_project(0)
    mix_out(0)
    if n_sub > 1:
        heads_and_project(1)

    stages = [(j, k) for j in range(n_sub) for k in range(n_up)]
    hs, ups = {}, {}

    def issue_up(s):
        j, k = stages[s]
        if j not in hs:
            xm = x_mid[j]
            hs[j] = (xm * sc2 + sh2).reshape(-1, xm.shape[-1]).astype(BF16)
        ups[s] = _dot(hs[j], wup_refs[k][...])

    for s in range(min(UP_AHEAD, len(stages))):
        issue_up(s)
    acc = [None] * n_dn
    for s, (j, k) in enumerate(stages):
        if k == MIX_OUT_AT and j + 1 < n_sub:
            mix_out(j + 1)
        if k == HEADS_AT and j + 2 < n_sub:
            heads_and_project(j + 2)
        if s + UP_AHEAD < len(stages):
            issue_up(s + UP_AHEAD)
        a = jnp.square(jnp.maximum(ups.pop(s), 0.0)).astype(BF16)
        for n in range(n_dn):
            part = _dot(a, wdn_refs[n][k * chunk:(k + 1) * chunk, :])
            acc[n] = part if acc[n] is None else acc[n] + part
        if k == n_up - 1:
            xm = x_mid.pop(j)
            f = jnp.concatenate(acc, axis=1).reshape(xm.shape)
            store(j, _layer_norm(alpha * xm + gate2 * f, ln2_g, ln2_b))
            acc = [None] * n_dn


def _layer_prompt_kernel(x_ref, mod_ref, *refs, alpha, layer, n_in, n_out, n_up, n_dn):
    win_refs, refs = refs[:n_in], refs[n_in:]
    cw_ref, sg_ref, sb_ref, ws_ref, bs_ref, pw_ref, ps_ref = refs[:7]
    wout_refs, refs = refs[7:7 + n_out], refs[7 + n_out:]
    g_ref, b_ref = refs[:2]
    wup_refs, refs = refs[2:2 + n_up], refs[2 + n_up:]
    wdn_refs, refs = refs[:n_dn], refs[n_dn:]
    g2_ref, b2_ref, o_ref, ncv_ref, npl_ref, zc_ref, pc_ref = refs
    bi, t = pl.program_id(0), pl.program_id(1)
    rows, c_dim = x_ref.shape[1], x_ref.shape[2]
    w_conv, w_gmlp, w_pool = cw_ref.shape[2], sg_ref.shape[1], ps_ref.shape[1]
    n_heads = ws_ref.shape[0]

    @pl.when(t == 0)
    def _():
        zc_ref[...] = jnp.zeros(zc_ref.shape, F32)
        pc_ref[...] = jnp.zeros(pc_ref.shape, F32)

    m = mod_ref[pl.ds(bi, 1), :]
    sh, sc1, gate = m[:, :c_dim], 1.0 + m[:, c_dim:2 * c_dim], m[:, 2 * c_dim:3 * c_dim]
    sh2, sc2, gate2 = (m[:, 3 * c_dim:4 * c_dim], 1.0 + m[:, 4 * c_dim:5 * c_dim],
                       m[:, 5 * c_dim:])
    cw = cw_ref[layer]
    sgu_g, sgu_b = sg_ref[layer:layer + 1, :], sb_ref[layer:layer + 1, :]
    pool_scale = ps_ref[layer:layer + 1, :]
    ln_g, ln_b = g_ref[layer:layer + 1, :], b_ref[layer:layer + 1, :]
    ln2_g, ln2_b = g2_ref[layer:layer + 1, :], b2_ref[layer:layer + 1, :]

    row_i = lax.broadcasted_iota(jnp.int32, (CHUNK, 2 * CHUNK), 0)
    col_j = lax.broadcasted_iota(jnp.int32, (CHUNK, 2 * CHUNK), 1) % CHUNK
    causal = col_j <= row_i
    ws_m = []
    for p in range(n_heads // 2):
        pair = jnp.concatenate([ws_ref[2 * p], ws_ref[2 * p + 1]], axis=1)
        ws_m.append(jnp.where(causal, pair, 0.0).astype(BF16))
    low_half = lax.broadcasted_iota(jnp.int32, (CHUNK, LANES), 1) < HEAD_DIM
    lane_group = _lane_group((SUB_ROWS, w_pool), len(POOL_WINDOWS))
    window = _select_by_group([jnp.full((SUB_ROWS, w_pool), w, jnp.int32) for w in POOL_WINDOWS],
                              lane_group)
    row_id = lax.broadcasted_iota(jnp.int32, (SUB_ROWS, w_pool), 0)
    bias = bs_ref[...]
    n_sub = rows // SUB_ROWS
    n_ch = SUB_ROWS // CHUNK
    hist = {"z": zc_ref[...], "p": pc_ref[...]}
    projs, mids, x_mid = {}, {}, {}

    def project(jj):
        h = (x_ref[0, jj * SUB_ROWS:(jj + 1) * SUB_ROWS, :] * sc1 + sh).astype(BF16)
        projs[jj] = jnp.concatenate([_dot(h, w[...]) for w in win_refs], axis=1)

    def heads(j):
        proj = projs.pop(j)
        o0 = 0
        a_b = proj[:, o0:o0 + w_conv]; o0 += w_conv
        a_c = proj[:, o0:o0 + w_conv]; o0 += w_conv
        a_x = proj[:, o0:o0 + w_conv]; o0 += w_conv
        g_u = proj[:, o0:o0 + w_gmlp]; o0 += w_gmlp
        g_v = proj[:, o0:o0 + w_gmlp]; o0 += w_gmlp
        p_x = proj[:, o0:o0 + w_pool]

        z = a_c * a_x
        z_ext = jnp.concatenate([hist["z"], z], axis=0)
        conv = z_ext * cw[CONV_WIDTH - 1:CONV_WIDTH, :]
        for k in range(1, CONV_WIDTH):
            conv = conv + pltpu.roll(z_ext, k, 0) * cw[CONV_WIDTH - 1 - k:CONV_WIDTH - k, :]
        y_a = a_b * conv[CONV_HIST:, :]
        hist["z"] = z[SUB_ROWS - CONV_HIST:, :]

        v_n = _layer_norm(g_v, sgu_g, sgu_b)
        s_cols = []
        for p in range(n_heads // 2):
            cols = [v_n[c * CHUNK:(c + 1) * CHUNK, p * LANES:(p + 1) * LANES] for c in range(n_ch)]
            rhs = jnp.concatenate(
                [jnp.concatenate([jnp.where(low_half, col, 0.0) for col in cols], axis=1),
                 jnp.concatenate([jnp.where(low_half, 0.0, col) for col in cols], axis=1)],
                axis=0).astype(BF16)
            s_cols.append(_dot(ws_m[p], rhs))

        run = jnp.concatenate([hist["p"], p_x], axis=0)
        sums = []
        for w in POOL_WINDOWS:
            run = run + pltpu.roll(run, w // 2, 0)
            sums.append(run[POOL_HIST:, :])
        count = jnp.minimum(window, t * rows + j * SUB_ROWS + row_id + 1).astype(F32)
        d = (_select_by_group(sums, lane_group) / count - p_x).astype(BF16)
        y_c_raw = _dot(d, pw_ref[...])
        hist["p"] = p_x[SUB_ROWS - POOL_HIST:, :]
        mids[j] = (y_a, g_u, s_cols, y_c_raw)

    def mix_out(j):
        y_a, g_u, s_cols, y_c_raw = mids.pop(j)
        s = jnp.concatenate(
            [jnp.concatenate([sc[:, c * LANES:(c + 1) * LANES] for sc in s_cols], axis=1) + bias
             for c in range(n_ch)], axis=0)
        mix = jnp.concatenate([y_a, g_u * s, y_c_raw * pool_scale], axis=1).astype(BF16)
        out = jnp.concatenate([_dot(mix, w[...]) for w in wout_refs], axis=1)
        x = x_ref[0, j * SUB_ROWS:(j + 1) * SUB_ROWS, :]
        x_mid[j] = _layer_norm(alpha * x + gate * out, ln_g, ln_b)

    def store(j, y):
        o_ref[0, j * SUB_ROWS:(j + 1) * SUB_ROWS, :] = y

    _layer_pipeline(n_sub, project, heads, mix_out, x_mid, sh2, sc2, gate2, wup_refs, wdn_refs,
                    ln2_g, ln2_b, alpha, store)

    zc_ref[...] = hist["z"]
    pc_ref[...] = hist["p"]
    ncv_ref[0] = hist["z"][CONV_HIST - (CONV_WIDTH - 1):, :]
    npl_ref[0] = hist["p"][POOL_HIST - (max(POOL_WINDOWS) - 1):, :]


def _layer_prompt_call(x, mod, w_in, conv_w, sgu_g, sgu_b, w_s, bs_full, pool_wbd, pool_scale,
                       w_out, ln_g, ln_b, wup, wdn, ln2_g, ln2_b, *, alpha, layer, rows):
    batch, seq, c_dim = x.shape
    w_conv, w_pool = conv_w.shape[2], pool_scale.shape[1]
    conv_buf = CONV_WIDTH - 1
    pool_buf = max(POOL_WINDOWS) - 1
    in_specs = _column_blocks(w_in, WEIGHT_COLS)
    out_specs = _column_blocks(w_out, WEIGHT_COLS)
    up_specs = _column_blocks(wup, WEIGHT_COLS)
    dn_specs = _column_blocks(wdn, WEIGHT_COLS)
    kern = functools.partial(_layer_prompt_kernel, alpha=alpha, layer=layer, n_in=len(in_specs),
                             n_out=len(out_specs), n_up=len(up_specs), n_dn=len(dn_specs))
    return pl.pallas_call(
        kern,
        grid=(batch, seq // rows),
        in_specs=[
            pl.BlockSpec((1, rows, c_dim), lambda b, t: (b, t, 0)),
            pl.BlockSpec((None, batch, mod.shape[2]), lambda b, t: (layer, 0, 0)),
            *in_specs,
            _whole(conv_w.shape),
            _whole(sgu_g.shape),
            _whole(sgu_b.shape),
            _layer_block(w_s.shape[1:], layer),
            _layer_block(bs_full.shape[1:], layer),
            _layer_block(pool_wbd.shape[1:], layer),
            _whole(pool_scale.shape),
            *out_specs,
            _whole(ln_g.shape),
            _whole(ln_b.shape),
            *up_specs, *dn_specs,
            _whole(ln2_g.shape),
            _whole(ln2_b.shape),
        ],
        out_specs=[
            pl.BlockSpec((1, rows, c_dim), lambda b, t: (b, t, 0)),
            pl.BlockSpec((1, conv_buf, w_conv), lambda b, t: (b, 0, 0)),
            pl.BlockSpec((1, pool_buf, w_pool), lambda b, t: (b, 0, 0)),
        ],
        out_shape=[
            jax.ShapeDtypeStruct((batch, seq, c_dim), F32),
            jax.ShapeDtypeStruct((batch, conv_buf, w_conv), F32),
            jax.ShapeDtypeStruct((batch, pool_buf, w_pool), F32),
        ],
        scratch_shapes=[
            pltpu.VMEM((CONV_HIST, w_conv), F32),
            pltpu.VMEM((POOL_HIST, w_pool), F32),
        ],
        compiler_params=pltpu.CompilerParams(
            dimension_semantics=("arbitrary", "arbitrary"),
            vmem_limit_bytes=VMEM_LIMIT_BYTES),
        name="layer_prompt",
    )(x, mod, *([w_in] * len(in_specs)), conv_w, sgu_g, sgu_b, w_s, bs_full, pool_wbd, pool_scale,
      *([w_out] * len(out_specs)), ln_g, ln_b, *([wup] * len(up_specs)),
      *([wdn] * len(dn_specs)), ln2_g, ln2_b)


def _layer_sample_kernel(x_ref, mod_ref, cst_ref, pst_ref, *refs, alpha, layer, n_in, n_out,
                         n_up, n_dn, x_batch_major, out_batch_major):
    win_refs, refs = refs[:n_in], refs[n_in:]
    cw_ref, sg_ref, sb_ref, wst_ref, bst_ref, pw_ref, ps_ref = refs[:7]
    wout_refs, refs = refs[7:7 + n_out], refs[7 + n_out:]
    g_ref, b_ref = refs[:2]
    wup_refs, refs = refs[2:2 + n_up], refs[2 + n_up:]
    wdn_refs, refs = refs[:n_dn], refs[n_dn:]
    g2_ref, b2_ref, o_ref, ncv_ref, npl_ref, v_ref = refs
    xt = jnp.swapaxes(x_ref[...], 0, 1) if x_batch_major else None
    cst = jnp.swapaxes(cst_ref[...], 0, 1)
    pst = jnp.swapaxes(pst_ref[...], 0, 1)
    t_dim, b_dim = v_ref.shape[1], v_ref.shape[0]
    c_dim = x_ref.shape[2]
    w_conv, w_gmlp, w_pool = cw_ref.shape[2], sg_ref.shape[1], ps_ref.shape[1]
    n_valid = pst.shape[0]
    step = SUB_ROWS // b_dim
    n_sub = t_dim // step

    m = mod_ref[...]
    sh, sc1, gate = m[:, :c_dim], 1.0 + m[:, c_dim:2 * c_dim], m[:, 2 * c_dim:3 * c_dim]
    sh2, sc2, gate2 = (m[:, 3 * c_dim:4 * c_dim], 1.0 + m[:, 4 * c_dim:5 * c_dim],
                       m[:, 5 * c_dim:])
    cw = cw_ref[layer]
    sgu_g, sgu_b = sg_ref[layer:layer + 1, :], sb_ref[layer:layer + 1, :]
    pool_scale = ps_ref[layer:layer + 1, :]
    ln_g, ln_b = g_ref[layer:layer + 1, :], b_ref[layer:layer + 1, :]
    ln2_g, ln2_b = g2_ref[layer:layer + 1, :], b2_ref[layer:layer + 1, :]
    lane_group = _lane_group((b_dim, w_pool), len(POOL_WINDOWS))

    z_ext = [cst[i] for i in range(CONV_WIDTH - 1)]
    p_ext = [pst[i] for i in range(n_valid)]
    v_all = []
    projs, mids, x_mid, outs = {}, {}, {}, {}

    def x_rows(j):
        if x_batch_major:
            return xt[j * step:(j + 1) * step]
        return x_ref[j * step:(j + 1) * step]

    def project(j):
        h = (x_rows(j) * sc1 + sh).reshape(step * b_dim, c_dim).astype(BF16)
        projs[j] = jnp.concatenate([_dot(h, w[...]) for w in win_refs], axis=1).reshape(
            step, b_dim, -1)

    def heads(j):
        proj = projs.pop(j)
        o0 = 0
        a_b = proj[:, :, o0:o0 + w_conv]; o0 += w_conv
        a_c = proj[:, :, o0:o0 + w_conv]; o0 += w_conv
        a_x = proj[:, :, o0:o0 + w_conv]; o0 += w_conv
        g_u = proj[:, :, o0:o0 + w_gmlp]; o0 += w_gmlp
        g_v = proj[:, :, o0:o0 + w_gmlp]; o0 += w_gmlp
        p_x = proj[:, :, o0:o0 + w_pool]
        z = a_c * a_x
        v_n = _layer_norm(g_v, sgu_g, sgu_b)
        y_a, y_b, d = [], [], []
        for i in range(step):
            pos = j * step + i
            z_ext.append(z[i])
            conv = z_ext[pos] * cw[0:1, :]
            for k in range(1, CONV_WIDTH):
                conv = conv + z_ext[pos + k] * cw[k:k + 1, :]
            y_a.append(a_b[i] * conv)
            v_all.append(v_n[i])
            s = bst_ref[pos:pos + 1, :] + wst_ref[pos, 0:1, :] * v_all[0]
            for q in range(1, pos + 1):
                s = s + wst_ref[pos, q:q + 1, :] * v_all[q]
            y_b.append(g_u[i] * s)
            p_ext.append(p_x[i])
            last = n_valid + pos
            means, run, k = [], None, 0
            for w in POOL_WINDOWS:
                while k < w:
                    run = p_ext[last - k] if run is None else run + p_ext[last - k]
                    k += 1
                means.append(run / float(min(w, pos + 1 + n_valid)))
            d.append(_select_by_group(means, lane_group) - p_x[i])
        d = jnp.stack(d).reshape(step * b_dim, w_pool).astype(BF16)
        mids[j] = (jnp.stack(y_a), jnp.stack(y_b), _dot(d, pw_ref[...]))

    def mix_out(j):
        y_a, y_b, y_c_raw = mids.pop(j)
        y_c = (y_c_raw * pool_scale).reshape(step, b_dim, w_pool)
        mix = jnp.concatenate([y_a, y_b, y_c], axis=2).reshape(step * b_dim, c_dim).astype(BF16)
        out = jnp.concatenate([_dot(mix, w[...]) for w in wout_refs], axis=1)
        x_mid[j] = _layer_norm(alpha * x_rows(j) + gate * out.reshape(step, b_dim, c_dim),
                               ln_g, ln_b)

    def store(j, y):
        if out_batch_major:
            outs[j] = y
        else:
            o_ref[j * step:(j + 1) * step] = y

    _layer_pipeline(n_sub, project, heads, mix_out, x_mid, sh2, sc2, gate2, wup_refs, wdn_refs,
                    ln2_g, ln2_b, alpha, store)

    if out_batch_major:
        o_ref[...] = jnp.swapaxes(jnp.concatenate([outs[j] for j in range(n_sub)], axis=0), 0, 1)
    ncv_ref[...] = jnp.swapaxes(jnp.stack(z_ext[len(z_ext) - (CONV_WIDTH - 1):]), 0, 1)
    npl_ref[...] = jnp.swapaxes(jnp.stack(p_ext[len(p_ext) - n_valid:]), 0, 1)
    v_ref[...] = jnp.swapaxes(jnp.stack(v_all), 0, 1)


def _layer_sample_call(x, mod, cst, pst, w_in, conv_w, sgu_g, sgu_b, wst, bs_full, pool_wbd,
                       pool_scale, w_out, ln_g, ln_b, wup, wdn, ln2_g, ln2_b, *, alpha, layer,
                       x_batch_major, out_batch_major):
    if x_batch_major:
        b_dim, t_dim, c_dim = x.shape
    else:
        t_dim, b_dim, c_dim = x.shape
    w_gmlp = sgu_g.shape[1]
    in_specs = _column_blocks(w_in, WEIGHT_COLS)
    out_specs = _column_blocks(w_out, WEIGHT_COLS)
    up_specs = _column_blocks(wup, WEIGHT_COLS)
    dn_specs = _column_blocks(wdn, WEIGHT_COLS)
    kern = functools.partial(_layer_sample_kernel, alpha=alpha, layer=layer, n_in=len(in_specs),
                             n_out=len(out_specs), n_up=len(up_specs), n_dn=len(dn_specs),
                             x_batch_major=x_batch_major, out_batch_major=out_batch_major)
    out_shape = (b_dim, t_dim, c_dim) if out_batch_major else (t_dim, b_dim, c_dim)
    whole3 = lambda i: (0, 0, 0)
    return pl.pallas_call(
        kern,
        grid=(1,),
        in_specs=[
            pl.BlockSpec(x.shape, whole3),
            pl.BlockSpec((None, b_dim, mod.shape[2]), lambda i: (layer, 0, 0)),
            _layer_block(cst.shape[1:], layer),
            _layer_block(pst.shape[1:], layer),
            *in_specs,
            _whole(conv_w.shape),
            _whole(sgu_g.shape),
            _whole(sgu_b.shape),
            _layer_block(wst.shape[1:], layer),
            pl.BlockSpec((None, t_dim, w_gmlp), lambda i: (layer, 0, 0)),
            _layer_block(pool_wbd.shape[1:], layer),
            _whole(pool_scale.shape),
            *out_specs,
            _whole(ln_g.shape),
            _whole(ln_b.shape),
            *up_specs, *dn_specs,
            _whole(ln2_g.shape),
            _whole(ln2_b.shape),
        ],
        out_specs=[
            pl.BlockSpec(out_shape, whole3),
            pl.BlockSpec(cst.shape[1:], whole3),
            pl.BlockSpec(pst.shape[1:], whole3),
            pl.BlockSpec((b_dim, t_dim, w_gmlp), whole3),
        ],
        out_shape=[
            jax.ShapeDtypeStruct(out_shape, F32),
            jax.ShapeDtypeStruct(cst.shape[1:], F32),
            jax.ShapeDtypeStruct(pst.shape[1:], F32),
            jax.ShapeDtypeStruct((b_dim, t_dim, w_gmlp), F32),
        ],
        compiler_params=pltpu.CompilerParams(
            dimension_semantics=("arbitrary",),
            vmem_limit_bytes=VMEM_LIMIT_BYTES),
        name="layer_sample",
    )(x, mod, cst, pst, *([w_in] * len(in_specs)), conv_w, sgu_g, sgu_b, wst, bs_full, pool_wbd,
      pool_scale, *([w_out] * len(out_specs)), ln_g, ln_b, *([wup] * len(up_specs)),
      *([wdn] * len(dn_specs)), ln2_g, ln2_b)


def kernel(x_prompt, x_sample, c_prompt, c_sample, state_conv, state_pool, w_ada, b_ada, w_in,
           conv_w, sgu_g, sgu_b, w_s, b_s, pool_w, pool_scale, w_out, ln1_g, ln1_b, w_up,
           w_down, ln2_g, ln2_b):
    depth = w_in.shape[0]
    seq = x_prompt.shape[1]
    dec_batch, dec_seq, _ = x_sample.shape
    n_groups, group_dim, _ = pool_w.shape[1:]
    w_pool = n_groups * group_dim
    alpha = (2.0 * depth) ** 0.25
    assert seq % PROMPT_ROWS == 0 and PROMPT_ROWS % SUB_ROWS == 0 and SUB_ROWS % CHUNK == 0
    assert (dec_seq * dec_batch) % SUB_ROWS == 0 and SUB_ROWS % dec_batch == 0

    mod_p, mod_s = _ada_call(c_prompt, c_sample, w_ada, b_ada)

    bs_full = jnp.repeat(jnp.swapaxes(b_s, 1, 2), HEAD_DIM, axis=2)
    ws_small = jnp.repeat(jnp.transpose(w_s[:, :, :dec_seq, :dec_seq], (0, 2, 3, 1)), HEAD_DIM, axis=3)
    eye = jnp.eye(n_groups, dtype=F32)
    pool_wbd = (eye[None, :, None, :, None] * pool_w[:, :, :, None, :]).reshape(
        depth, w_pool, w_pool).astype(BF16)

    xp, xs = x_prompt, x_sample
    conv_p, pool_p, conv_s, pool_s, v_s = [], [], [], [], []
    for l in range(depth):
        w_in_b, w_out_b = w_in[l].astype(BF16), w_out[l].astype(BF16)
        wup_b, wdn_b = w_up[l].astype(BF16), w_down[l].astype(BF16)
        xp, ncv, npl = _layer_prompt_call(
            xp, mod_p, w_in_b, conv_w, sgu_g, sgu_b, w_s, bs_full, pool_wbd, pool_scale,
            w_out_b, ln1_g, ln1_b, wup_b, wdn_b, ln2_g, ln2_b, alpha=alpha, layer=l,
            rows=PROMPT_ROWS)
        conv_p.append(ncv)
        pool_p.append(npl)

        xs, ncv, npl, v_rows = _layer_sample_call(
            xs, mod_s, state_conv, state_pool, w_in_b, conv_w, sgu_g, sgu_b, ws_small, bs_full,
            pool_wbd, pool_scale, w_out_b, ln1_g, ln1_b, wup_b, wdn_b, ln2_g, ln2_b, alpha=alpha,
            layer=l, x_batch_major=(l == 0), out_batch_major=(l == depth - 1))
        conv_s.append(ncv)
        pool_s.append(npl)
        v_s.append(v_rows)

    return (xp, xs, jnp.stack(conv_p), jnp.stack(pool_p), jnp.stack(conv_s), jnp.stack(pool_s),
            jnp.stack(v_s))
```

```python
import functools

import jax
import jax.numpy as jnp
from jax import lax
from jax.experimental import pallas as pl
from jax.experimental.pallas import tpu as pltpu

F32 = jnp.float32
BF16 = jnp.bfloat16

LN_EPS = 1e-5
HEAD_DIM = 64
CHUNK = 128
POOL_WINDOWS = (2, 4, 8, 16)
CONV_WIDTH = 3
LANES = 128
SUBLANES = 8
VMEM_LIMIT_BYTES = 56 * 1024 * 1024

PROMPT_ROWS = 1024
SUB_ROWS = 256
UP_AHEAD = 2
PROJ_AHEAD = 2
MIX_OUT_AT = 2
HEADS_AT = 5
WEIGHT_COLS = 512
CONV_HIST = SUBLANES
POOL_HIST = 2 * SUBLANES


def _layer_norm(y, g, b):
    mu = jnp.mean(y, axis=-1, keepdims=True)
    d = y - mu
    var = jnp.mean(d * d, axis=-1, keepdims=True)
    return d * lax.rsqrt(var + LN_EPS) * g + b


def _dot(a, b):
    return jnp.dot(a, b, preferred_element_type=F32)


def _layer_block(shape, layer):
    zeros = (0,) * len(shape)
    return pl.BlockSpec((None,) + tuple(shape), lambda *_: (layer,) + zeros,
                        pipeline_mode=pl.Buffered(1))


def _whole(shape):
    zeros = (0,) * len(shape)
    return pl.BlockSpec(tuple(shape), lambda *_: zeros, pipeline_mode=pl.Buffered(1))


def _column_blocks(w, layer, width):
    rows, cols = w.shape[1:]
    return [pl.BlockSpec((None, rows, width), lambda *_, n=n: (layer, 0, n),
                         pipeline_mode=pl.Buffered(1)) for n in range(cols // width)]


def _lane_group(shape, n_groups):
    return lax.broadcasted_iota(jnp.int32, shape, len(shape) - 1) // (shape[-1] // n_groups)


def _select_by_group(values, lane_group):
    out = values[-1]
    for g in range(len(values) - 2, -1, -1):
        out = jnp.where(lane_group == g, values[g], out)
    return out


def _ada_kernel(cp_ref, cs_ref, w_ref, b_ref, op_ref, os_ref):
    l, k = pl.program_id(0), pl.program_id(1)
    bp = cp_ref.shape[0]
    c = jnp.concatenate([cp_ref[...], cs_ref[...]], axis=0)
    part = _dot(jax.nn.silu(c).astype(BF16), w_ref[0].astype(BF16))
    part_p, part_s = part[:bp], part[bp:]

    @pl.when(k == 0)
    def _():
        bias = b_ref[pl.ds(l, 1), :]
        op_ref[0] = part_p + bias
        os_ref[0] = part_s + bias

    @pl.when(k > 0)
    def _():
        op_ref[0] += part_p
        os_ref[0] += part_s


def _ada_call(c_prompt, c_sample, w_ada, b_ada, row_tile=256):
    depth, d_model, n_mod = w_ada.shape
    bp, bs = c_prompt.shape[0], c_sample.shape[0]
    return pl.pallas_call(
        _ada_kernel,
        grid=(depth, d_model // row_tile),
        in_specs=[
            pl.BlockSpec((bp, row_tile), lambda l, k: (0, k)),
            pl.BlockSpec((bs, row_tile), lambda l, k: (0, k)),
            pl.BlockSpec((1, row_tile, n_mod), lambda l, k: (l, k, 0)),
            _whole(b_ada.shape),
        ],
        out_specs=[
            pl.BlockSpec((1, bp, n_mod), lambda l, k: (l, 0, 0)),
            pl.BlockSpec((1, bs, n_mod), lambda l, k: (l, 0, 0)),
        ],
        out_shape=[
            jax.ShapeDtypeStruct((depth, bp, n_mod), F32),
            jax.ShapeDtypeStruct((depth, bs, n_mod), F32),
        ],
        compiler_params=pltpu.CompilerParams(
            dimension_semantics=("arbitrary", "arbitrary"),
            vmem_limit_bytes=VMEM_LIMIT_BYTES),
        name="ada_mod",
    )(c_prompt, c_sample, w_ada, b_ada)


def _layer_pipeline(n_sub, project, heads, mix_out, x_mid, sh2, sc2, gate2, wup_refs, wdn_refs,
                    ln2_g, ln2_b, alpha, store):
    n_up, n_dn = len(wup_refs), len(wdn_refs)
    chunk = wup_refs[0].shape[1]

    def heads_and_project(j):
        heads(j)
        if j + PROJ_AHEAD < n_sub:
            project(j + PROJ_AHEAD)

    for jj in range(min(PROJ_AHEAD, n_sub)):
        project(jj)
    heads_and_project(0)
    mix_out(0)
    if n_sub > 1:
        heads_and_project(1)

    stages = [(j, k) for j in range(n_sub) for k in range(n_up)]
    hs, ups = {}, {}

    def issue_up(s):
        j, k = stages[s]
        if j not in hs:
            xm = x_mid[j]
            hs[j] = (xm * sc2 + sh2).reshape(-1, xm.shape[-1]).astype(BF16)
        ups[s] = _dot(hs[j], wup_refs[k][...])

    for s in range(min(UP_AHEAD, len(stages))):
        issue_up(s)
    acc = [None] * n_dn
    for s, (j, k) in enumerate(stages):
        if k == MIX_OUT_AT and j + 1 < n_sub:
            mix_out(j + 1)
        if k == HEADS_AT and j + 2 < n_sub:
            heads_and_project(j + 2)
        if s + UP_AHEAD < len(stages):
            issue_up(s + UP_AHEAD)
        a = jnp.square(jnp.maximum(ups.pop(s), 0.0)).astype(BF16)
        for n in range(n_dn):
            part = _dot(a, wdn_refs[n][k * chunk:(k + 1) * chunk, :])
            acc[n] = part if acc[n] is None else acc[n] + part
        if k == n_up - 1:
            xm = x_mid.pop(j)
            f = jnp.concatenate(acc, axis=1).reshape(xm.shape)
            store(j, _layer_norm(alpha * xm + gate2 * f, ln2_g, ln2_b))
            acc = [None] * n_dn


def _layer_prompt_kernel(x_ref, mod_ref, *refs, alpha, layer, n_in, n_out, n_up, n_dn):
    win_refs, refs = refs[:n_in], refs[n_in:]
    cw_ref, sg_ref, sb_ref, ws_ref, bs_ref, pw_ref, ps_ref = refs[:7]
    wout_refs, refs = refs[7:7 + n_out], refs[7 + n_out:]
    g_ref, b_ref = refs[:2]
    wup_refs, refs = refs[2:2 + n_up], refs[2 + n_up:]
    wdn_refs, refs = refs[:n_dn], refs[n_dn:]
    g2_ref, b2_ref, o_ref, ncv_ref, npl_ref, zc_ref, pc_ref = refs
    bi, t = pl.program_id(0), pl.program_id(1)
    rows, c_dim = x_ref.shape[1], x_ref.shape[2]
    w_conv, w_gmlp, w_pool = cw_ref.shape[2], sg_ref.shape[1], ps_ref.shape[1]
    n_heads = ws_ref.shape[0]

    @pl.when(t == 0)
    def _():
        zc_ref[...] = jnp.zeros(zc_ref.shape, F32)
        pc_ref[...] = jnp.zeros(pc_ref.shape, F32)

    m = mod_ref[pl.ds(bi, 1), :]
    sh, sc1, gate = m[:, :c_dim], 1.0 + m[:, c_dim:2 * c_dim], m[:, 2 * c_dim:3 * c_dim]
    sh2, sc2, gate2 = (m[:, 3 * c_dim:4 * c_dim], 1.0 + m[:, 4 * c_dim:5 * c_dim],
                       m[:, 5 * c_dim:])
    cw = cw_ref[layer]
    sgu_g, sgu_b = sg_ref[layer:layer + 1, :], sb_ref[layer:layer + 1, :]
    pool_scale = ps_ref[layer:layer + 1, :]
    ln_g, ln_b = g_ref[layer:layer + 1, :], b_ref[layer:layer + 1, :]
    ln2_g, ln2_b = g2_ref[layer:layer + 1, :], b2_ref[layer:layer + 1, :]

    row_i = lax.broadcasted_iota(jnp.int32, (CHUNK, 2 * CHUNK), 0)
    col_j = lax.broadcasted_iota(jnp.int32, (CHUNK, 2 * CHUNK), 1) % CHUNK
    causal = col_j <= row_i
    ws_m = []
    for p in range(n_heads // 2):
        pair = jnp.concatenate([ws_ref[2 * p], ws_ref[2 * p + 1]], axis=1)
        ws_m.append(jnp.where(causal, pair, 0.0).astype(BF16))
    low_half = lax.broadcasted_iota(jnp.int32, (CHUNK, LANES), 1) < HEAD_DIM
    lane_group = _lane_group((SUB_ROWS, w_pool), len(POOL_WINDOWS))
    window = _select_by_group([jnp.full((SUB_ROWS, w_pool), w, jnp.int32) for w in POOL_WINDOWS],
                              lane_group)
    row_id = lax.broadcasted_iota(jnp.int32, (SUB_ROWS, w_pool), 0)
    bias = bs_ref[...]
    n_sub = rows // SUB_ROWS
    n_ch = SUB_ROWS // CHUNK
    hist = {"z": zc_ref[...], "p": pc_ref[...]}
    projs, mids, x_mid = {}, {}, {}

    def project(jj):
        h = (x_ref[0, jj * SUB_ROWS:(jj + 1) * SUB_ROWS, :] * sc1 + sh).astype(BF16)
        projs[jj] = jnp.concatenate([_dot(h, w[...]) for w in win_refs], axis=1)

    def heads(j):
        proj = projs.pop(j)
        o0 = 0
        a_b = proj[:, o0:o0 + w_conv]; o0 += w_conv
        a_c = proj[:, o0:o0 + w_conv]; o0 += w_conv
        a_x = proj[:, o0:o0 + w_conv]; o0 += w_conv
        g_u = proj[:, o0:o0 + w_gmlp]; o0 += w_gmlp
        g_v = proj[:, o0:o0 + w_gmlp]; o0 += w_gmlp
        p_x = proj[:, o0:o0 + w_pool]

        z = a_c * a_x
        z_ext = jnp.concatenate([hist["z"], z], axis=0)
        conv = z_ext * cw[CONV_WIDTH - 1:CONV_WIDTH, :]
        for k in range(1, CONV_WIDTH):
            conv = conv + pltpu.roll(z_ext, k, 0) * cw[CONV_WIDTH - 1 - k:CONV_WIDTH - k, :]
        y_a = a_b * conv[CONV_HIST:, :]
        hist["z"] = z[SUB_ROWS - CONV_HIST:, :]

        v_n = _layer_norm(g_v, sgu_g, sgu_b)
        s_cols = []
        for p in range(n_heads // 2):
            cols = [v_n[c * CHUNK:(c + 1) * CHUNK, p * LANES:(p + 1) * LANES] for c in range(n_ch)]
            rhs = jnp.concatenate(
                [jnp.concatenate([jnp.where(low_half, col, 0.0) for col in cols], axis=1),
                 jnp.concatenate([jnp.where(low_half, 0.0, col) for col in cols], axis=1)],
                axis=0).astype(BF16)
            s_cols.append(_dot(ws_m[p], rhs))

        run = jnp.concatenate([hist["p"], p_x], axis=0)
        sums = []
        for w in POOL_WINDOWS:
            run = run + pltpu.roll(run, w // 2, 0)
            sums.append(run[POOL_HIST:, :])
        count = jnp.minimum(window, t * rows + j * SUB_ROWS + row_id + 1).astype(F32)
        d = (_select_by_group(sums, lane_group) / count - p_x).astype(BF16)
        y_c_raw = _dot(d, pw_ref[...])
        hist["p"] = p_x[SUB_ROWS - POOL_HIST:, :]
        mids[j] = (y_a, g_u, s_cols, y_c_raw)

    def mix_out(j):
        y_a, g_u, s_cols, y_c_raw = mids.pop(j)
        s = jnp.concatenate(
            [jnp.concatenate([sc[:, c * LANES:(c + 1) * LANES] for sc in s_cols], axis=1) + bias
             for c in range(n_ch)], axis=0)
        mix = jnp.concatenate([y_a, g_u * s, y_c_raw * pool_scale], axis=1).astype(BF16)
        out = jnp.concatenate([_dot(mix, w[...]) for w in wout_refs], axis=1)
        x = x_ref[0, j * SUB_ROWS:(j + 1) * SUB_ROWS, :]
        x_mid[j] = _layer_norm(alpha * x + gate * out, ln_g, ln_b)

    def store(j, y):
        o_ref[0, j * SUB_ROWS:(j + 1) * SUB_ROWS, :] = y

    _layer_pipeline(n_sub, project, heads, mix_out, x_mid, sh2, sc2, gate2, wup_refs, wdn_refs,
                    ln2_g, ln2_b, alpha, store)

    zc_ref[...] = hist["z"]
    pc_ref[...] = hist["p"]
    ncv_ref[0] = hist["z"][CONV_HIST - (CONV_WIDTH - 1):, :]
    npl_ref[0] = hist["p"][POOL_HIST - (max(POOL_WINDOWS) - 1):, :]


def _layer_prompt_call(x, mod, w_in, conv_w, sgu_g, sgu_b, w_s, bs_full, pool_wbd, pool_scale,
                       w_out, ln_g, ln_b, wup, wdn, ln2_g, ln2_b, *, alpha, layer, rows):
    batch, seq, c_dim = x.shape
    w_conv, w_pool = conv_w.shape[2], pool_scale.shape[1]
    conv_buf = CONV_WIDTH - 1
    pool_buf = max(POOL_WINDOWS) - 1
    in_specs = _column_blocks(w_in, layer, WEIGHT_COLS)
    out_specs = _column_blocks(w_out, layer, WEIGHT_COLS)
    up_specs = _column_blocks(wup, layer, WEIGHT_COLS)
    dn_specs = _column_blocks(wdn, layer, WEIGHT_COLS)
    kern = functools.partial(_layer_prompt_kernel, alpha=alpha, layer=layer, n_in=len(in_specs),
                             n_out=len(out_specs), n_up=len(up_specs), n_dn=len(dn_specs))
    return pl.pallas_call(
        kern,
        grid=(batch, seq // rows),
        in_specs=[
            pl.BlockSpec((1, rows, c_dim), lambda b, t: (b, t, 0)),
            pl.BlockSpec((None, batch, mod.shape[2]), lambda b, t: (layer, 0, 0)),
            *in_specs,
            _whole(conv_w.shape),
            _whole(sgu_g.shape),
            _whole(sgu_b.shape),
            _layer_block(w_s.shape[1:], layer),
            _layer_block(bs_full.shape[1:], layer),
            _layer_block(pool_wbd.shape[1:], layer),
            _whole(pool_scale.shape),
            *out_specs,
            _whole(ln_g.shape),
            _whole(ln_b.shape),
            *up_specs, *dn_specs,
            _whole(ln2_g.shape),
            _whole(ln2_b.shape),
        ],
        out_specs=[
            pl.BlockSpec((1, rows, c_dim), lambda b, t: (b, t, 0)),
            pl.BlockSpec((1, conv_buf, w_conv), lambda b, t: (b, 0, 0)),
            pl.BlockSpec((1, pool_buf, w_pool), lambda b, t: (b, 0, 0)),
        ],
        out_shape=[
            jax.ShapeDtypeStruct((batch, seq, c_dim), F32),
            jax.ShapeDtypeStruct((batch, conv_buf, w_conv), F32),
            jax.ShapeDtypeStruct((batch, pool_buf, w_pool), F32),
        ],
        scratch_shapes=[
            pltpu.VMEM((CONV_HIST, w_conv), F32),
            pltpu.VMEM((POOL_HIST, w_pool), F32),
        ],
        compiler_params=pltpu.CompilerParams(
            dimension_semantics=("arbitrary", "arbitrary"),
            vmem_limit_bytes=VMEM_LIMIT_BYTES),
        name="layer_prompt",
    )(x, mod, *([w_in] * len(in_specs)), conv_w, sgu_g, sgu_b, w_s, bs_full, pool_wbd, pool_scale,
      *([w_out] * len(out_specs)), ln_g, ln_b, *([wup] * len(up_specs)),
      *([wdn] * len(dn_specs)), ln2_g, ln2_b)


def _layer_sample_kernel(x_ref, mod_ref, cst_ref, pst_ref, *refs, alpha, layer, n_in, n_out,
                         n_up, n_dn, x_batch_major, out_batch_major):
    win_refs, refs = refs[:n_in], refs[n_in:]
    cw_ref, sg_ref, sb_ref, wst_ref, bst_ref, pw_ref, ps_ref = refs[:7]
    wout_refs, refs = refs[7:7 + n_out], refs[7 + n_out:]
    g_ref, b_ref = refs[:2]
    wup_refs, refs = refs[2:2 + n_up], refs[2 + n_up:]
    wdn_refs, refs = refs[:n_dn], refs[n_dn:]
    g2_ref, b2_ref, _, _, _, o_ref, ncv_ref, npl_ref, v_ref = refs
    xt = jnp.swapaxes(x_ref[...], 0, 1) if x_batch_major else None
    cst = jnp.swapaxes(cst_ref[...], 0, 1)
    pst = jnp.swapaxes(pst_ref[...], 0, 1)
    t_dim, b_dim = v_ref.shape[1], v_ref.shape[0]
    c_dim = x_ref.shape[2]
    w_conv, w_gmlp, w_pool = cw_ref.shape[2], sg_ref.shape[1], ps_ref.shape[1]
    n_valid = pst.shape[0]
    step = SUB_ROWS // b_dim
    n_sub = t_dim // step

    m = mod_ref[...]
    sh, sc1, gate = m[:, :c_dim], 1.0 + m[:, c_dim:2 * c_dim], m[:, 2 * c_dim:3 * c_dim]
    sh2, sc2, gate2 = (m[:, 3 * c_dim:4 * c_dim], 1.0 + m[:, 4 * c_dim:5 * c_dim],
                       m[:, 5 * c_dim:])
    cw = cw_ref[layer]
    sgu_g, sgu_b = sg_ref[layer:layer + 1, :], sb_ref[layer:layer + 1, :]
    pool_scale = ps_ref[layer:layer + 1, :]
    ln_g, ln_b = g_ref[layer:layer + 1, :], b_ref[layer:layer + 1, :]
    ln2_g, ln2_b = g2_ref[layer:layer + 1, :], b2_ref[layer:layer + 1, :]
    lane_group = _lane_group((b_dim, w_pool), len(POOL_WINDOWS))

    z_ext = [cst[i] for i in range(CONV_WIDTH - 1)]
    p_ext = [pst[i] for i in range(n_valid)]
    v_all = []
    projs, mids, x_mid, outs = {}, {}, {}, {}

    def x_rows(j):
        if x_batch_major:
            return xt[j * step:(j + 1) * step]
        return x_ref[j * step:(j + 1) * step]

    def project(j):
        h = (x_rows(j) * sc1 + sh).reshape(step * b_dim, c_dim).astype(BF16)
        projs[j] = jnp.concatenate([_dot(h, w[...]) for w in win_refs], axis=1).reshape(
            step, b_dim, -1)

    def heads(j):
        proj = projs.pop(j)
        o0 = 0
        a_b = proj[:, :, o0:o0 + w_conv]; o0 += w_conv
        a_c = proj[:, :, o0:o0 + w_conv]; o0 += w_conv
        a_x = proj[:, :, o0:o0 + w_conv]; o0 += w_conv
        g_u = proj[:, :, o0:o0 + w_gmlp]; o0 += w_gmlp
        g_v = proj[:, :, o0:o0 + w_gmlp]; o0 += w_gmlp
        p_x = proj[:, :, o0:o0 + w_pool]
        z = a_c * a_x
        v_n = _layer_norm(g_v, sgu_g, sgu_b)
        y_a, y_b, d = [], [], []
        for i in range(step):
            pos = j * step + i
            z_ext.append(z[i])
            conv = z_ext[pos] * cw[0:1, :]
            for k in range(1, CONV_WIDTH):
                conv = conv + z_ext[pos + k] * cw[k:k + 1, :]
            y_a.append(a_b[i] * conv)
            v_all.append(v_n[i])
            s = bst_ref[pos:pos + 1, :] + wst_ref[pos, 0:1, :] * v_all[0]
            for q in range(1, pos + 1):
                s = s + wst_ref[pos, q:q + 1, :] * v_all[q]
            y_b.append(g_u[i] * s)
            p_ext.append(p_x[i])
            last = n_valid + pos
            means, run, k = [], None, 0
            for w in POOL_WINDOWS:
                while k < w:
                    run = p_ext[last - k] if run is None else run + p_ext[last - k]
                    k += 1
                means.append(run / float(min(w, pos + 1 + n_valid)))
            d.append(_select_by_group(means, lane_group) - p_x[i])
        d = jnp.stack(d).reshape(step * b_dim, w_pool).astype(BF16)
        mids[j] = (jnp.stack(y_a), jnp.stack(y_b), _dot(d, pw_ref[...]))

    def mix_out(j):
        y_a, y_b, y_c_raw = mids.pop(j)
        y_c = (y_c_raw * pool_scale).reshape(step, b_dim, w_pool)
        mix = jnp.concatenate([y_a, y_b, y_c], axis=2).reshape(step * b_dim, c_dim).astype(BF16)
        out = jnp.concatenate([_dot(mix, w[...]) for w in wout_refs], axis=1)
        x_mid[j] = _layer_norm(alpha * x_rows(j) + gate * out.reshape(step, b_dim, c_dim),
                               ln_g, ln_b)

    def store(j, y):
        if out_batch_major:
            outs[j] = y
        else:
            o_ref[j * step:(j + 1) * step] = y

    _layer_pipeline(n_sub, project, heads, mix_out, x_mid, sh2, sc2, gate2, wup_refs, wdn_refs,
                    ln2_g, ln2_b, alpha, store)

    if out_batch_major:
        o_ref[...] = jnp.swapaxes(jnp.concatenate([outs[j] for j in range(n_sub)], axis=0), 0, 1)
    ncv_ref[...] = jnp.swapaxes(jnp.stack(z_ext[len(z_ext) - (CONV_WIDTH - 1):]), 0, 1)
    npl_ref[...] = jnp.swapaxes(jnp.stack(p_ext[len(p_ext) - n_valid:]), 0, 1)
    v_ref[...] = jnp.swapaxes(jnp.stack(v_all), 0, 1)


def _layer_sample_call(x, mod, cst, pst, w_in, conv_w, sgu_g, sgu_b, wst, bs_full, pool_wbd,
                       pool_scale, w_out, ln_g, ln_b, wup, wdn, ln2_g, ln2_b, states, *, alpha,
                       layer, x_batch_major, out_batch_major):
    if x_batch_major:
        b_dim, t_dim, c_dim = x.shape
    else:
        t_dim, b_dim, c_dim = x.shape
    w_gmlp = sgu_g.shape[1]
    in_specs = _column_blocks(w_in, layer, WEIGHT_COLS)
    out_specs = _column_blocks(w_out, layer, WEIGHT_COLS)
    up_specs = _column_blocks(wup, layer, WEIGHT_COLS)
    dn_specs = _column_blocks(wdn, layer, WEIGHT_COLS)
    kern = functools.partial(_layer_sample_kernel, alpha=alpha, layer=layer, n_in=len(in_specs),
                             n_out=len(out_specs), n_up=len(up_specs), n_dn=len(dn_specs),
                             x_batch_major=x_batch_major, out_batch_major=out_batch_major)
    out_shape = (b_dim, t_dim, c_dim) if out_batch_major else (t_dim, b_dim, c_dim)
    whole3 = lambda i: (0, 0, 0)
    inputs = (x, mod, cst, pst, *([w_in] * len(in_specs)), conv_w, sgu_g, sgu_b, wst, bs_full,
              pool_wbd, pool_scale, *([w_out] * len(out_specs)), ln_g, ln_b,
              *([wup] * len(up_specs)), *([wdn] * len(dn_specs)), ln2_g, ln2_b)
    n_inputs = len(inputs)
    return pl.pallas_call(
        kern,
        grid=(1,),
        in_specs=[
            pl.BlockSpec(x.shape, whole3),
            pl.BlockSpec((None, b_dim, mod.shape[2]), lambda i: (layer, 0, 0)),
            _layer_block(cst.shape[1:], layer),
            _layer_block(pst.shape[1:], layer),
            *in_specs,
            _whole(conv_w.shape),
            _whole(sgu_g.shape),
            _whole(sgu_b.shape),
            _layer_block(wst.shape[1:], layer),
            pl.BlockSpec((None, t_dim, w_gmlp), lambda i: (layer, 0, 0)),
            _layer_block(pool_wbd.shape[1:], layer),
            _whole(pool_scale.shape),
            *out_specs,
            _whole(ln_g.shape),
            _whole(ln_b.shape),
            *up_specs, *dn_specs,
            _whole(ln2_g.shape),
            _whole(ln2_b.shape),
            *[pl.BlockSpec(memory_space=pl.ANY) for _ in states],
        ],
        out_specs=[pl.BlockSpec(out_shape, whole3)] + [
            pl.BlockSpec((None,) + s.shape[1:], lambda i: (layer, 0, 0, 0)) for s in states],
        out_shape=[jax.ShapeDtypeStruct(out_shape, F32)] + [
            jax.ShapeDtypeStruct(s.shape, F32) for s in states],
        input_output_aliases={n_inputs + k: 1 + k for k in range(len(states))},
        compiler_params=pltpu.CompilerParams(
            dimension_semantics=("arbitrary",),
            vmem_limit_bytes=VMEM_LIMIT_BYTES),
        name="layer_sample",
    )(*inputs, *states)


def kernel(x_prompt, x_sample, c_prompt, c_sample, state_conv, state_pool, w_ada, b_ada, w_in,
           conv_w, sgu_g, sgu_b, w_s, b_s, pool_w, pool_scale, w_out, ln1_g, ln1_b, w_up,
           w_down, ln2_g, ln2_b):
    depth = w_in.shape[0]
    seq = x_prompt.shape[1]
    dec_batch, dec_seq, _ = x_sample.shape
    n_groups, group_dim, _ = pool_w.shape[1:]
    w_pool = n_groups * group_dim
    alpha = (2.0 * depth) ** 0.25
    assert seq % PROMPT_ROWS == 0 and PROMPT_ROWS % SUB_ROWS == 0 and SUB_ROWS % CHUNK == 0
    assert (dec_seq * dec_batch) % SUB_ROWS == 0 and SUB_ROWS % dec_batch == 0

    mod_p, mod_s = _ada_call(c_prompt, c_sample, w_ada, b_ada)

    w_in_b, w_out_b = w_in.astype(BF16), w_out.astype(BF16)
    wup_b, wdn_b = w_up.astype(BF16), w_down.astype(BF16)
    bs_full = jnp.repeat(jnp.swapaxes(b_s, 1, 2), HEAD_DIM, axis=2)
    ws_small = jnp.repeat(jnp.transpose(w_s[:, :, :dec_seq, :dec_seq], (0, 2, 3, 1)), HEAD_DIM, axis=3)
    eye = jnp.eye(n_groups, dtype=F32)
    pool_wbd = (eye[None, :, None, :, None] * pool_w[:, :, :, None, :]).reshape(
        depth, w_pool, w_pool).astype(BF16)

    xp, xs = x_prompt, x_sample
    conv_p, pool_p = [], []
    w_gmlp = sgu_g.shape[1]
    states_s = [jnp.zeros(state_conv.shape, F32), jnp.zeros(state_pool.shape, F32),
                jnp.zeros((depth, dec_batch, dec_seq, w_gmlp), F32)]
    for l in range(depth):
        xp, ncv, npl = _layer_prompt_call(
            xp, mod_p, w_in_b, conv_w, sgu_g, sgu_b, w_s, bs_full, pool_wbd, pool_scale,
            w_out_b, ln1_g, ln1_b, wup_b, wdn_b, ln2_g, ln2_b, alpha=alpha, layer=l,
            rows=PROMPT_ROWS)
        conv_p.append(ncv)
        pool_p.append(npl)

        xs, *states_s = _layer_sample_call(
            xs, mod_s, state_conv, state_pool, w_in_b, conv_w, sgu_g, sgu_b, ws_small, bs_full,
            pool_wbd, pool_scale, w_out_b, ln1_g, ln1_b, wup_b, wdn_b, ln2_g, ln2_b, states_s,
            alpha=alpha, layer=l, x_batch_major=(l == 0), out_batch_major=(l == depth - 1))

    return (xp, xs, jnp.stack(conv_p), jnp.stack(pool_p), *states_s)
```

```python
import functools

import jax
import jax.numpy as jnp
from jax import lax
from jax.experimental import pallas as pl
from jax.experimental.pallas import tpu as pltpu

F32 = jnp.float32
BF16 = jnp.bfloat16

LN_EPS = 1e-5
HEAD_DIM = 64
CHUNK = 128
POOL_WINDOWS = (2, 4, 8, 16)
CONV_WIDTH = 3
LANES = 128
SUBLANES = 8
VMEM_LIMIT_BYTES = 56 * 1024 * 1024

PROMPT_ROWS = 1024
SUB_ROWS = 256
UP_AHEAD = 2
PROJ_AHEAD = 2
MIX_OUT_AT = 2
HEADS_AT = 5
WEIGHT_COLS = 512
CONV_HIST = SUBLANES
POOL_HIST = 2 * SUBLANES


def _layer_norm(y, g, b):
    mu = jnp.mean(y, axis=-1, keepdims=True)
    d = y - mu
    var = jnp.mean(d * d, axis=-1, keepdims=True)
    return d * lax.rsqrt(var + LN_EPS) * g + b


def _dot(a, b):
    return jnp.dot(a, b, preferred_element_type=F32)


def _layer_block(shape, layer):
    zeros = (0,) * len(shape)
    return pl.BlockSpec((None,) + tuple(shape), lambda *_: (layer,) + zeros,
                        pipeline_mode=pl.Buffered(1))


def _whole(shape):
    zeros = (0,) * len(shape)
    return pl.BlockSpec(tuple(shape), lambda *_: zeros, pipeline_mode=pl.Buffered(1))


def _column_blocks(w, layer, width):
    rows, cols = w.shape[1:]
    return [pl.BlockSpec((None, rows, width), lambda *_, n=n: (layer, 0, n),
                         pipeline_mode=pl.Buffered(1)) for n in range(cols // width)]


def _lane_group(shape, n_groups):
    return lax.broadcasted_iota(jnp.int32, shape, len(shape) - 1) // (shape[-1] // n_groups)


def _select_by_group(values, lane_group):
    out = values[-1]
    for g in range(len(values) - 2, -1, -1):
        out = jnp.where(lane_group == g, values[g], out)
    return out


def _ada_kernel(cp_ref, cs_ref, w_ref, b_ref, op_ref, os_ref):
    l, k = pl.program_id(0), pl.program_id(1)
    bp = cp_ref.shape[0]
    c = jnp.concatenate([cp_ref[...], cs_ref[...]], axis=0)
    part = _dot(jax.nn.silu(c).astype(BF16), w_ref[0].astype(BF16))
    part_p, part_s = part[:bp], part[bp:]

    @pl.when(k == 0)
    def _():
        bias = b_ref[pl.ds(l, 1), :]
        op_ref[0] = part_p + bias
        os_ref[0] = part_s + bias

    @pl.when(k > 0)
    def _():
        op_ref[0] += part_p
        os_ref[0] += part_s


def _ada_call(c_prompt, c_sample, w_ada, b_ada, row_tile=256):
    depth, d_model, n_mod = w_ada.shape
    bp, bs = c_prompt.shape[0], c_sample.shape[0]
    return pl.pallas_call(
        _ada_kernel,
        grid=(depth, d_model // row_tile),
        in_specs=[
            pl.BlockSpec((bp, row_tile), lambda l, k: (0, k)),
            pl.BlockSpec((bs, row_tile), lambda l, k: (0, k)),
            pl.BlockSpec((1, row_tile, n_mod), lambda l, k: (l, k, 0)),
            _whole(b_ada.shape),
        ],
        out_specs=[
            pl.BlockSpec((1, bp, n_mod), lambda l, k: (l, 0, 0)),
            pl.BlockSpec((1, bs, n_mod), lambda l, k: (l, 0, 0)),
        ],
        out_shape=[
            jax.ShapeDtypeStruct((depth, bp, n_mod), F32),
            jax.ShapeDtypeStruct((depth, bs, n_mod), F32),
        ],
        compiler_params=pltpu.CompilerParams(
            dimension_semantics=("arbitrary", "arbitrary"),
            vmem_limit_bytes=VMEM_LIMIT_BYTES),
        name="ada_mod",
    )(c_prompt, c_sample, w_ada, b_ada)


def _layer_pipeline(n_sub, project, heads, mix_out, x_mid, sh2, sc2, gate2, wup_refs, wdn_refs,
                    ln2_g, ln2_b, alpha, store):
    n_up, n_dn = len(wup_refs), len(wdn_refs)
    chunk = wup_refs[0].shape[1]

    def heads_and_project(j):
        heads(j)
        if j + PROJ_AHEAD < n_sub:
            project(j + PROJ_AHEAD)

    for jj in range(min(PROJ_AHEAD, n_sub)):
        project(jj)
    heads_and_project(0)
    mix_out(0)
    if n_sub > 1:
        heads_and_project(1)

    stages = [(j, k) for j in range(n_sub) for k in range(n_up)]
    hs, ups = {}, {}

    def issue_up(s):
        j, k = stages[s]
        if j not in hs:
            xm = x_mid[j]
            hs[j] = (xm * sc2 + sh2).reshape(-1, xm.shape[-1]).astype(BF16)
        ups[s] = _dot(hs[j], wup_refs[k][...])

    for s in range(min(UP_AHEAD, len(stages))):
        issue_up(s)
    acc = [None] * n_dn
    for s, (j, k) in enumerate(stages):
        if k == MIX_OUT_AT and j + 1 < n_sub:
            mix_out(j + 1)
        if k == HEADS_AT and j + 2 < n_sub:
            heads_and_project(j + 2)
        if s + UP_AHEAD < len(stages):
            issue_up(s + UP_AHEAD)
        a = jnp.square(jnp.maximum(ups.pop(s), 0.0)).astype(BF16)
        for n in range(n_dn):
            part = _dot(a, wdn_refs[n][k * chunk:(k + 1) * chunk, :])
            acc[n] = part if acc[n] is None else acc[n] + part
        if k == n_up - 1:
            xm = x_mid.pop(j)
            f = jnp.concatenate(acc, axis=1).reshape(xm.shape)
            store(j, _layer_norm(alpha * xm + gate2 * f, ln2_g, ln2_b))
            acc = [None] * n_dn


def _layer_prompt_kernel(x_ref, mod_ref, *refs, alpha, layer, n_in, n_out, n_up, n_dn):
    win_refs, refs = refs[:n_in], refs[n_in:]
    cw_ref, sg_ref, sb_ref, ws_ref, bs_ref, pw_ref, ps_ref = refs[:7]
    wout_refs, refs = refs[7:7 + n_out], refs[7 + n_out:]
    g_ref, b_ref = refs[:2]
    wup_refs, refs = refs[2:2 + n_up], refs[2 + n_up:]
    wdn_refs, refs = refs[:n_dn], refs[n_dn:]
    g2_ref, b2_ref, o_ref, ncv_ref, npl_ref, zc_ref, pc_ref = refs
    bi, t = pl.program_id(0), pl.program_id(1)
    rows, c_dim = x_ref.shape[1], x_ref.shape[2]
    w_conv, w_gmlp, w_pool = cw_ref.shape[2], sg_ref.shape[1], ps_ref.shape[1]
    n_heads = ws_ref.shape[0]

    @pl.when(t == 0)
    def _():
        zc_ref[...] = jnp.zeros(zc_ref.shape, F32)
        pc_ref[...] = jnp.zeros(pc_ref.shape, F32)

    m = mod_ref[pl.ds(bi, 1), :]
    sh, sc1, gate = m[:, :c_dim], 1.0 + m[:, c_dim:2 * c_dim], m[:, 2 * c_dim:3 * c_dim]
    sh2, sc2, gate2 = (m[:, 3 * c_dim:4 * c_dim], 1.0 + m[:, 4 * c_dim:5 * c_dim],
                       m[:, 5 * c_dim:])
    cw = cw_ref[layer]
    sgu_g, sgu_b = sg_ref[layer:layer + 1, :], sb_ref[layer:layer + 1, :]
    pool_scale = ps_ref[layer:layer + 1, :]
    ln_g, ln_b = g_ref[layer:layer + 1, :], b_ref[layer:layer + 1, :]
    ln2_g, ln2_b = g2_ref[layer:layer + 1, :], b2_ref[layer:layer + 1, :]

    row_i = lax.broadcasted_iota(jnp.int32, (CHUNK, 2 * CHUNK), 0)
    col_j = lax.broadcasted_iota(jnp.int32, (CHUNK, 2 * CHUNK), 1) % CHUNK
    causal = col_j <= row_i
    ws_m = []
    for p in range(n_heads // 2):
        pair = jnp.concatenate([ws_ref[2 * p], ws_ref[2 * p + 1]], axis=1)
        ws_m.append(jnp.where(causal, pair, 0.0).astype(BF16))
    low_half = lax.broadcasted_iota(jnp.int32, (CHUNK, LANES), 1) < HEAD_DIM
    lane_group = _lane_group((SUB_ROWS, w_pool), len(POOL_WINDOWS))
    window = _select_by_group([jnp.full((SUB_ROWS, w_pool), w, jnp.int32) for w in POOL_WINDOWS],
                              lane_group)
    row_id = lax.broadcasted_iota(jnp.int32, (SUB_ROWS, w_pool), 0)
    bias = bs_ref[...]
    n_sub = rows // SUB_ROWS
    n_ch = SUB_ROWS // CHUNK
    hist = {"z": zc_ref[...], "p": pc_ref[...]}
    projs, mids, x_mid = {}, {}, {}

    def project(jj):
        h = (x_ref[0, jj * SUB_ROWS:(jj + 1) * SUB_ROWS, :] * sc1 + sh).astype(BF16)
        projs[jj] = jnp.concatenate([_dot(h, w[...]) for w in win_refs], axis=1)

    def heads(j):
        proj = projs.pop(j)
        o0 = 0
        a_b = proj[:, o0:o0 + w_conv]; o0 += w_conv
        a_c = proj[:, o0:o0 + w_conv]; o0 += w_conv
        a_x = proj[:, o0:o0 + w_conv]; o0 += w_conv
        g_u = proj[:, o0:o0 + w_gmlp]; o0 += w_gmlp
        g_v = proj[:, o0:o0 + w_gmlp]; o0 += w_gmlp
        p_x = proj[:, o0:o0 + w_pool]

        z = a_c * a_x
        z_ext = jnp.concatenate([hist["z"], z], axis=0)
        conv = z_ext * cw[CONV_WIDTH - 1:CONV_WIDTH, :]
        for k in range(1, CONV_WIDTH):
            conv = conv + pltpu.roll(z_ext, k, 0) * cw[CONV_WIDTH - 1 - k:CONV_WIDTH - k, :]
        y_a = a_b * conv[CONV_HIST:, :]
        hist["z"] = z[SUB_ROWS - CONV_HIST:, :]

        v_n = _layer_norm(g_v, sgu_g, sgu_b)
        s_cols = []
        for p in range(n_heads // 2):
            cols = [v_n[c * CHUNK:(c + 1) * CHUNK, p * LANES:(p + 1) * LANES] for c in range(n_ch)]
            rhs = jnp.concatenate(
                [jnp.concatenate([jnp.where(low_half, col, 0.0) for col in cols], axis=1),
                 jnp.concatenate([jnp.where(low_half, 0.0, col) for col in cols], axis=1)],
                axis=0).astype(BF16)
            s_cols.append(_dot(ws_m[p], rhs))

        run = jnp.concatenate([hist["p"], p_x], axis=0)
        sums = []
        for w in POOL_WINDOWS:
            run = run + pltpu.roll(run, w // 2, 0)
            sums.append(run[POOL_HIST:, :])
        count = jnp.minimum(window, t * rows + j * SUB_ROWS + row_id + 1).astype(F32)
        d = (_select_by_group(sums, lane_group) / count - p_x).astype(BF16)
        y_c_raw = _dot(d, pw_ref[...])
        hist["p"] = p_x[SUB_ROWS - POOL_HIST:, :]
        mids[j] = (y_a, g_u, s_cols, y_c_raw)

    def mix_out(j):
        y_a, g_u, s_cols, y_c_raw = mids.pop(j)
        s = jnp.concatenate(
            [jnp.concatenate([sc[:, c * LANES:(c + 1) * LANES] for sc in s_cols], axis=1) + bias
             for c in range(n_ch)], axis=0)
        mix = jnp.concatenate([y_a, g_u * s, y_c_raw * pool_scale], axis=1).astype(BF16)
        out = jnp.concatenate([_dot(mix, w[...]) for w in wout_refs], axis=1)
        x = x_ref[0, j * SUB_ROWS:(j + 1) * SUB_ROWS, :]
        x_mid[j] = _layer_norm(alpha * x + gate * out, ln_g, ln_b)

    def store(j, y):
        o_ref[0, j * SUB_ROWS:(j + 1) * SUB_ROWS, :] = y

    _layer_pipeline(n_sub, project, heads, mix_out, x_mid, sh2, sc2, gate2, wup_refs, wdn_refs,
                    ln2_g, ln2_b, alpha, store)

    zc_ref[...] = hist["z"]
    pc_ref[...] = hist["p"]
    ncv_ref[0] = hist["z"][CONV_HIST - (CONV_WIDTH - 1):, :]
    npl_ref[0] = hist["p"][POOL_HIST - (max(POOL_WINDOWS) - 1):, :]


def _layer_prompt_call(x, mod, w_in, conv_w, sgu_g, sgu_b, w_s, bs_full, pool_wbd, pool_scale,
                       w_out, ln_g, ln_b, wup, wdn, ln2_g, ln2_b, *, alpha, layer, rows):
    batch, seq, c_dim = x.shape
    w_conv, w_pool = conv_w.shape[2], pool_scale.shape[1]
    conv_buf = CONV_WIDTH - 1
    pool_buf = max(POOL_WINDOWS) - 1
    in_specs = _column_blocks(w_in, layer, WEIGHT_COLS)
    out_specs = _column_blocks(w_out, layer, WEIGHT_COLS)
    up_specs = _column_blocks(wup, layer, WEIGHT_COLS)
    dn_specs = _column_blocks(wdn, layer, WEIGHT_COLS)
    kern = functools.partial(_layer_prompt_kernel, alpha=alpha, layer=layer, n_in=len(in_specs),
                             n_out=len(out_specs), n_up=len(up_specs), n_dn=len(dn_specs))
    return pl.pallas_call(
        kern,
        grid=(batch, seq // rows),
        in_specs=[
            pl.BlockSpec((1, rows, c_dim), lambda b, t: (b, t, 0)),
            pl.BlockSpec((None, batch, mod.shape[2]), lambda b, t: (layer, 0, 0)),
            *in_specs,
            _whole(conv_w.shape),
            _whole(sgu_g.shape),
            _whole(sgu_b.shape),
            _layer_block(w_s.shape[1:], layer),
            _layer_block(bs_full.shape[1:], layer),
            _layer_block(pool_wbd.shape[1:], layer),
            _whole(pool_scale.shape),
            *out_specs,
            _whole(ln_g.shape),
            _whole(ln_b.shape),
            *up_specs, *dn_specs,
            _whole(ln2_g.shape),
            _whole(ln2_b.shape),
        ],
        out_specs=[
            pl.BlockSpec((1, rows, c_dim), lambda b, t: (b, t, 0)),
            pl.BlockSpec((1, conv_buf, w_conv), lambda b, t: (b, 0, 0)),
            pl.BlockSpec((1, pool_buf, w_pool), lambda b, t: (b, 0, 0)),
        ],
        out_shape=[
            jax.ShapeDtypeStruct((batch, seq, c_dim), F32),
            jax.ShapeDtypeStruct((batch, conv_buf, w_conv), F32),
            jax.ShapeDtypeStruct((batch, pool_buf, w_pool), F32),
        ],
        scratch_shapes=[
            pltpu.VMEM((CONV_HIST, w_conv), F32),
            pltpu.VMEM((POOL_HIST, w_pool), F32),
        ],
        input_output_aliases={0: 0} if layer > 0 else {},
        compiler_params=pltpu.CompilerParams(
            dimension_semantics=("arbitrary", "arbitrary"),
            vmem_limit_bytes=VMEM_LIMIT_BYTES),
        name="layer_prompt",
    )(x, mod, *([w_in] * len(in_specs)), conv_w, sgu_g, sgu_b, w_s, bs_full, pool_wbd, pool_scale,
      *([w_out] * len(out_specs)), ln_g, ln_b, *([wup] * len(up_specs)),
      *([wdn] * len(dn_specs)), ln2_g, ln2_b)


def _layer_sample_kernel(x_ref, mod_ref, cst_ref, pst_ref, *refs, alpha, layer, n_in, n_out,
                         n_up, n_dn, x_batch_major, out_batch_major):
    win_refs, refs = refs[:n_in], refs[n_in:]
    cw_ref, sg_ref, sb_ref, wst_ref, bst_ref, pw_ref, ps_ref = refs[:7]
    wout_refs, refs = refs[7:7 + n_out], refs[7 + n_out:]
    g_ref, b_ref = refs[:2]
    wup_refs, refs = refs[2:2 + n_up], refs[2 + n_up:]
    wdn_refs, refs = refs[:n_dn], refs[n_dn:]
    g2_ref, b2_ref, o_ref, ncv_ref, npl_ref, v_ref = refs
    xt = jnp.swapaxes(x_ref[...], 0, 1) if x_batch_major else None
    cst = jnp.swapaxes(cst_ref[...], 0, 1)
    pst = jnp.swapaxes(pst_ref[...], 0, 1)
    t_dim, b_dim = v_ref.shape[1], v_ref.shape[0]
    c_dim = x_ref.shape[2]
    w_conv, w_gmlp, w_pool = cw_ref.shape[2], sg_ref.shape[1], ps_ref.shape[1]
    n_valid = pst.shape[0]
    step = SUB_ROWS // b_dim
    n_sub = t_dim // step

    m = mod_ref[...]
    sh, sc1, gate = m[:, :c_dim], 1.0 + m[:, c_dim:2 * c_dim], m[:, 2 * c_dim:3 * c_dim]
    sh2, sc2, gate2 = (m[:, 3 * c_dim:4 * c_dim], 1.0 + m[:, 4 * c_dim:5 * c_dim],
                       m[:, 5 * c_dim:])
    cw = cw_ref[layer]
    sgu_g, sgu_b = sg_ref[layer:layer + 1, :], sb_ref[layer:layer + 1, :]
    pool_scale = ps_ref[layer:layer + 1, :]
    ln_g, ln_b = g_ref[layer:layer + 1, :], b_ref[layer:layer + 1, :]
    ln2_g, ln2_b = g2_ref[layer:layer + 1, :], b2_ref[layer:layer + 1, :]
    lane_group = _lane_group((b_dim, w_pool), len(POOL_WINDOWS))

    z_ext = [cst[i] for i in range(CONV_WIDTH - 1)]
    p_ext = [pst[i] for i in range(n_valid)]
    v_all = []
    projs, mids, x_mid, outs = {}, {}, {}, {}

    def x_rows(j):
        if x_batch_major:
            return xt[j * step:(j + 1) * step]
        return x_ref[j * step:(j + 1) * step]

    def project(j):
        h = (x_rows(j) * sc1 + sh).reshape(step * b_dim, c_dim).astype(BF16)
        projs[j] = jnp.concatenate([_dot(h, w[...]) for w in win_refs], axis=1).reshape(
            step, b_dim, -1)

    def heads(j):
        proj = projs.pop(j)
        o0 = 0
        a_b = proj[:, :, o0:o0 + w_conv]; o0 += w_conv
        a_c = proj[:, :, o0:o0 + w_conv]; o0 += w_conv
        a_x = proj[:, :, o0:o0 + w_conv]; o0 += w_conv
        g_u = proj[:, :, o0:o0 + w_gmlp]; o0 += w_gmlp
        g_v = proj[:, :, o0:o0 + w_gmlp]; o0 += w_gmlp
        p_x = proj[:, :, o0:o0 + w_pool]
        z = a_c * a_x
        v_n = _layer_norm(g_v, sgu_g, sgu_b)
        y_a, y_b, d = [], [], []
        for i in range(step):
            pos = j * step + i
            z_ext.append(z[i])
            conv = z_ext[pos] * cw[0:1, :]
            for k in range(1, CONV_WIDTH):
                conv = conv + z_ext[pos + k] * cw[k:k + 1, :]
            y_a.append(a_b[i] * conv)
            v_all.append(v_n[i])
            s = bst_ref[pos:pos + 1, :] + wst_ref[pos, 0:1, :] * v_all[0]
            for q in range(1, pos + 1):
                s = s + wst_ref[pos, q:q + 1, :] * v_all[q]
            y_b.append(g_u[i] * s)
            p_ext.append(p_x[i])
            last = n_valid + pos
            means, run, k = [], None, 0
            for w in POOL_WINDOWS:
                while k < w:
                    run = p_ext[last - k] if run is None else run + p_ext[last - k]
                    k += 1
                means.append(run / float(min(w, pos + 1 + n_valid)))
            d.append(_select_by_group(means, lane_group) - p_x[i])
        d = jnp.stack(d).reshape(step * b_dim, w_pool).astype(BF16)
        mids[j] = (jnp.stack(y_a), jnp.stack(y_b), _dot(d, pw_ref[...]))

    def mix_out(j):
        y_a, y_b, y_c_raw = mids.pop(j)
        y_c = (y_c_raw * pool_scale).reshape(step, b_dim, w_pool)
        mix = jnp.concatenate([y_a, y_b, y_c], axis=2).reshape(step * b_dim, c_dim).astype(BF16)
        out = jnp.concatenate([_dot(mix, w[...]) for w in wout_refs], axis=1)
        x_mid[j] = _layer_norm(alpha * x_rows(j) + gate * out.reshape(step, b_dim, c_dim),
                               ln_g, ln_b)

    def store(j, y):
        if out_batch_major:
            outs[j] = y
        else:
            o_ref[j * step:(j + 1) * step] = y

    _layer_pipeline(n_sub, project, heads, mix_out, x_mid, sh2, sc2, gate2, wup_refs, wdn_refs,
                    ln2_g, ln2_b, alpha, store)

    if out_batch_major:
        o_ref[...] = jnp.swapaxes(jnp.concatenate([outs[j] for j in range(n_sub)], axis=0), 0, 1)
    ncv_ref[...] = jnp.swapaxes(jnp.stack(z_ext[len(z_ext) - (CONV_WIDTH - 1):]), 0, 1)
    npl_ref[...] = jnp.swapaxes(jnp.stack(p_ext[len(p_ext) - n_valid:]), 0, 1)
    v_ref[...] = jnp.swapaxes(jnp.stack(v_all), 0, 1)


def _layer_sample_call(x, mod, cst, pst, w_in, conv_w, sgu_g, sgu_b, wst, bs_full, pool_wbd,
                       pool_scale, w_out, ln_g, ln_b, wup, wdn, ln2_g, ln2_b, *, alpha, layer,
                       x_batch_major, out_batch_major):
    if x_batch_major:
        b_dim, t_dim, c_dim = x.shape
    else:
        t_dim, b_dim, c_dim = x.shape
    w_gmlp = sgu_g.shape[1]
    in_specs = _column_blocks(w_in, layer, WEIGHT_COLS)
    out_specs = _column_blocks(w_out, layer, WEIGHT_COLS)
    up_specs = _column_blocks(wup, layer, WEIGHT_COLS)
    dn_specs = _column_blocks(wdn, layer, WEIGHT_COLS)
    kern = functools.partial(_layer_sample_kernel, alpha=alpha, layer=layer, n_in=len(in_specs),
                             n_out=len(out_specs), n_up=len(up_specs), n_dn=len(dn_specs),
                             x_batch_major=x_batch_major, out_batch_major=out_batch_major)
    out_shape = (b_dim, t_dim, c_dim) if out_batch_major else (t_dim, b_dim, c_dim)
    whole3 = lambda i: (0, 0, 0)
    return pl.pallas_call(
        kern,
        grid=(1,),
        in_specs=[
            pl.BlockSpec(x.shape, whole3),
            pl.BlockSpec((None, b_dim, mod.shape[2]), lambda i: (layer, 0, 0)),
            _layer_block(cst.shape[1:], layer),
            _layer_block(pst.shape[1:], layer),
            *in_specs,
            _whole(conv_w.shape),
            _whole(sgu_g.shape),
            _whole(sgu_b.shape),
            _layer_block(wst.shape[1:], layer),
            pl.BlockSpec((None, t_dim, w_gmlp), lambda i: (layer, 0, 0)),
            _layer_block(pool_wbd.shape[1:], layer),
            _whole(pool_scale.shape),
            *out_specs,
            _whole(ln_g.shape),
            _whole(ln_b.shape),
            *up_specs, *dn_specs,
            _whole(ln2_g.shape),
            _whole(ln2_b.shape),
        ],
        out_specs=[
            pl.BlockSpec(out_shape, whole3),
            pl.BlockSpec(cst.shape[1:], whole3),
            pl.BlockSpec(pst.shape[1:], whole3),
            pl.BlockSpec((b_dim, t_dim, w_gmlp), whole3),
        ],
        out_shape=[
            jax.ShapeDtypeStruct(out_shape, F32),
            jax.ShapeDtypeStruct(cst.shape[1:], F32),
            jax.ShapeDtypeStruct(pst.shape[1:], F32),
            jax.ShapeDtypeStruct((b_dim, t_dim, w_gmlp), F32),
        ],
        compiler_params=pltpu.CompilerParams(
            dimension_semantics=("arbitrary",),
            vmem_limit_bytes=VMEM_LIMIT_BYTES),
        name="layer_sample",
    )(x, mod, cst, pst, *([w_in] * len(in_specs)), conv_w, sgu_g, sgu_b, wst, bs_full, pool_wbd,
      pool_scale, *([w_out] * len(out_specs)), ln_g, ln_b, *([wup] * len(up_specs)),
      *([wdn] * len(dn_specs)), ln2_g, ln2_b)


def kernel(x_prompt, x_sample, c_prompt, c_sample, state_conv, state_pool, w_ada, b_ada, w_in,
           conv_w, sgu_g, sgu_b, w_s, b_s, pool_w, pool_scale, w_out, ln1_g, ln1_b, w_up,
           w_down, ln2_g, ln2_b):
    depth = w_in.shape[0]
    seq = x_prompt.shape[1]
    dec_batch, dec_seq, _ = x_sample.shape
    n_groups, group_dim, _ = pool_w.shape[1:]
    w_pool = n_groups * group_dim
    alpha = (2.0 * depth) ** 0.25
    assert seq % PROMPT_ROWS == 0 and PROMPT_ROWS % SUB_ROWS == 0 and SUB_ROWS % CHUNK == 0
    assert (dec_seq * dec_batch) % SUB_ROWS == 0 and SUB_ROWS % dec_batch == 0

    mod_p, mod_s = _ada_call(c_prompt, c_sample, w_ada, b_ada)

    w_in_b, w_out_b = w_in.astype(BF16), w_out.astype(BF16)
    wup_b, wdn_b = w_up.astype(BF16), w_down.astype(BF16)
    bs_full = jnp.repeat(jnp.swapaxes(b_s, 1, 2), HEAD_DIM, axis=2)
    ws_small = jnp.repeat(jnp.transpose(w_s[:, :, :dec_seq, :dec_seq], (0, 2, 3, 1)), HEAD_DIM, axis=3)
    eye = jnp.eye(n_groups, dtype=F32)
    pool_wbd = (eye[None, :, None, :, None] * pool_w[:, :, :, None, :]).reshape(
        depth, w_pool, w_pool).astype(BF16)

    xp, xs = x_prompt, x_sample
    conv_p, pool_p, conv_s, pool_s, v_s = [], [], [], [], []
    for l in range(depth):
        xp, ncv, npl = _layer_prompt_call(
            xp, mod_p, w_in_b, conv_w, sgu_g, sgu_b, w_s, bs_full, pool_wbd, pool_scale,
            w_out_b, ln1_g, ln1_b, wup_b, wdn_b, ln2_g, ln2_b, alpha=alpha, layer=l,
            rows=PROMPT_ROWS)
        conv_p.append(ncv)
        pool_p.append(npl)

        xs, ncv, npl, v_rows = _layer_sample_call(
            xs, mod_s, state_conv, state_pool, w_in_b, conv_w, sgu_g, sgu_b, ws_small, bs_full,
            pool_wbd, pool_scale, w_out_b, ln1_g, ln1_b, wup_b, wdn_b, ln2_g, ln2_b, alpha=alpha,
            layer=l, x_batch_major=(l == 0), out_batch_major=(l == depth - 1))
        conv_s.append(ncv)
        pool_s.append(npl)
        v_s.append(v_rows)

    return (xp, xs, jnp.stack(conv_p), jnp.stack(pool_p), jnp.stack(conv_s), jnp.stack(pool_s),
            jnp.stack(v_s))
```

```python
import functools

import jax
import jax.numpy as jnp
from jax import lax
from jax.experimental import pallas as pl
from jax.experimental.pallas import tpu as pltpu

F32 = jnp.float32
BF16 = jnp.bfloat16

LN_EPS = 1e-5
HEAD_DIM = 64
CHUNK = 128
POOL_WINDOWS = (2, 4, 8, 16)
CONV_WIDTH = 3
LANES = 128
SUBLANES = 8
VMEM_LIMIT_BYTES = 56 * 1024 * 1024

PROMPT_ROWS = 1024
SUB_ROWS = 256
UP_AHEAD = 2
PROJ_AHEAD = 2
MIX_OUT_AT = 2
HEADS_AT = 5
WEIGHT_COLS = 512
CONV_HIST = SUBLANES
POOL_HIST = 2 * SUBLANES


def _layer_norm(y, g, b):
    mu = jnp.mean(y, axis=-1, keepdims=True)
    d = y - mu
    var = jnp.mean(d * d, axis=-1, keepdims=True)
    return d * lax.rsqrt(var + LN_EPS) * g + b


def _dot(a, b):
    return jnp.dot(a, b, preferred_element_type=F32)


def _layer_block(shape, layer):
    zeros = (0,) * len(shape)
    return pl.BlockSpec((None,) + tuple(shape), lambda *_: (layer,) + zeros,
                        pipeline_mode=pl.Buffered(1))


def _whole(shape):
    zeros = (0,) * len(shape)
    return pl.BlockSpec(tuple(shape), lambda *_: zeros, pipeline_mode=pl.Buffered(1))


def _column_blocks(w, layer, width):
    rows, cols = w.shape[1:]
    return [pl.BlockSpec((None, rows, width), lambda *_, n=n: (layer, 0, n),
                         pipeline_mode=pl.Buffered(1)) for n in range(cols // width)]


def _lane_group(shape, n_groups):
    return lax.broadcasted_iota(jnp.int32, shape, len(shape) - 1) // (shape[-1] // n_groups)


def _select_by_group(values, lane_group):
    out = values[-1]
    for g in range(len(values) - 2, -1, -1):
        out = jnp.where(lane_group == g, values[g], out)
    return out


def _ada_kernel(cp_ref, cs_ref, w_ref, b_ref, op_ref, os_ref):
    l, k = pl.program_id(0), pl.program_id(1)
    bp = cp_ref.shape[0]
    c = jnp.concatenate([cp_ref[...], cs_ref[...]], axis=0)
    part = _dot(jax.nn.silu(c).astype(BF16), w_ref[0].astype(BF16))
    part_p, part_s = part[:bp], part[bp:]

    @pl.when(k == 0)
    def _():
        bias = b_ref[pl.ds(l, 1), :]
        op_ref[0] = part_p + bias
        os_ref[0] = part_s + bias

    @pl.when(k > 0)
    def _():
        op_ref[0] += part_p
        os_ref[0] += part_s


def _ada_call(c_prompt, c_sample, w_ada, b_ada, row_tile=256):
    depth, d_model, n_mod = w_ada.shape
    bp, bs = c_prompt.shape[0], c_sample.shape[0]
    return pl.pallas_call(
        _ada_kernel,
        grid=(depth, d_model // row_tile),
        in_specs=[
            pl.BlockSpec((bp, row_tile), lambda l, k: (0, k)),
            pl.BlockSpec((bs, row_tile), lambda l, k: (0, k)),
            pl.BlockSpec((1, row_tile, n_mod), lambda l, k: (l, k, 0)),
            _whole(b_ada.shape),
        ],
        out_specs=[
            pl.BlockSpec((1, bp, n_mod), lambda l, k: (l, 0, 0)),
            pl.BlockSpec((1, bs, n_mod), lambda l, k: (l, 0, 0)),
        ],
        out_shape=[
            jax.ShapeDtypeStruct((depth, bp, n_mod), F32),
            jax.ShapeDtypeStruct((depth, bs, n_mod), F32),
        ],
        compiler_params=pltpu.CompilerParams(
            dimension_semantics=("arbitrary", "arbitrary"),
            vmem_limit_bytes=VMEM_LIMIT_BYTES),
        name="ada_mod",
    )(c_prompt, c_sample, w_ada, b_ada)


def _layer_pipeline(n_sub, project, heads, mix_out, x_mid, sh2, sc2, gate2, wup_refs, wdn_refs,
                    ln2_g, ln2_b, alpha, store):
    n_up, n_dn = len(wup_refs), len(wdn_refs)
    chunk = wup_refs[0].shape[1]

    def heads_and_project(j):
        heads(j)
        if j + PROJ_AHEAD < n_sub:
            project(j + PROJ_AHEAD)

    for jj in range(min(PROJ_AHEAD, n_sub)):
        project(jj)
    heads_and_project(0)
    mix_out(0)
    if n_sub > 1:
        heads_and_project(1)

    stages = [(j, k) for j in range(n_sub) for k in range(n_up)]
    hs, ups = {}, {}

    def issue_up(s):
        j, k = stages[s]
        if j not in hs:
            xm = x_mid[j]
            hs[j] = (xm * sc2 + sh2).reshape(-1, xm.shape[-1]).astype(BF16)
        ups[s] = _dot(hs[j], wup_refs[k][...])

    for s in range(min(UP_AHEAD, len(stages))):
        issue_up(s)
    acc = [None] * n_dn
    for s, (j, k) in enumerate(stages):
        if k == MIX_OUT_AT and j + 1 < n_sub:
            mix_out(j + 1)
        if k == HEADS_AT and j + 2 < n_sub:
            heads_and_project(j + 2)
        if s + UP_AHEAD < len(stages):
            issue_up(s + UP_AHEAD)
        a = jnp.square(jnp.maximum(ups.pop(s), 0.0)).astype(BF16)
        for n in range(n_dn):
            part = _dot(a, wdn_refs[n][k * chunk:(k + 1) * chunk, :])
            acc[n] = part if acc[n] is None else acc[n] + part
        if k == n_up - 1:
            xm = x_mid.pop(j)
            f = jnp.concatenate(acc, axis=1).reshape(xm.shape)
            store(j, _layer_norm(alpha * xm + gate2 * f, ln2_g, ln2_b))
            acc = [None] * n_dn


def _layer_prompt_kernel(x_ref, mod_ref, *refs, alpha, layer, n_in, n_out, n_up, n_dn):
    win_refs, refs = refs[:n_in], refs[n_in:]
    cw_ref, sg_ref, sb_ref, ws_ref, bs_ref, pw_ref, ps_ref = refs[:7]
    wout_refs, refs = refs[7:7 + n_out], refs[7 + n_out:]
    g_ref, b_ref = refs[:2]
    wup_refs, refs = refs[2:2 + n_up], refs[2 + n_up:]
    wdn_refs, refs = refs[:n_dn], refs[n_dn:]
    g2_ref, b2_ref, o_ref, ncv_ref, npl_ref, zc_ref, pc_ref = refs
    bi, t = pl.program_id(0), pl.program_id(1)
    rows, c_dim = x_ref.shape[1], x_ref.shape[2]
    w_conv, w_gmlp, w_pool = cw_ref.shape[2], sg_ref.shape[1], ps_ref.shape[1]
    n_heads = ws_ref.shape[0]

    @pl.when(t == 0)
    def _():
        zc_ref[...] = jnp.zeros(zc_ref.shape, F32)
        pc_ref[...] = jnp.zeros(pc_ref.shape, F32)

    m = mod_ref[pl.ds(bi, 1), :]
    sh, sc1, gate = m[:, :c_dim], 1.0 + m[:, c_dim:2 * c_dim], m[:, 2 * c_dim:3 * c_dim]
    sh2, sc2, gate2 = (m[:, 3 * c_dim:4 * c_dim], 1.0 + m[:, 4 * c_dim:5 * c_dim],
                       m[:, 5 * c_dim:])
    cw = cw_ref[layer]
    sgu_g, sgu_b = sg_ref[layer:layer + 1, :], sb_ref[layer:layer + 1, :]
    pool_scale = ps_ref[layer:layer + 1, :]
    ln_g, ln_b = g_ref[layer:layer + 1, :], b_ref[layer:layer + 1, :]
    ln2_g, ln2_b = g2_ref[layer:layer + 1, :], b2_ref[layer:layer + 1, :]

    row_i = lax.broadcasted_iota(jnp.int32, (CHUNK, 2 * CHUNK), 0)
    col_j = lax.broadcasted_iota(jnp.int32, (CHUNK, 2 * CHUNK), 1) % CHUNK
    causal = col_j <= row_i
    ws_m = []
    for p in range(n_heads // 2):
        pair = jnp.concatenate([ws_ref[2 * p], ws_ref[2 * p + 1]], axis=1)
        ws_m.append(jnp.where(causal, pair, 0.0).astype(BF16))
    low_half = lax.broadcasted_iota(jnp.int32, (CHUNK, LANES), 1) < HEAD_DIM
    lane_group = _lane_group((SUB_ROWS, w_pool), len(POOL_WINDOWS))
    window = _select_by_group([jnp.full((SUB_ROWS, w_pool), w, jnp.int32) for w in POOL_WINDOWS],
                              lane_group)
    row_id = lax.broadcasted_iota(jnp.int32, (SUB_ROWS, w_pool), 0)
    bias = bs_ref[...]
    n_sub = rows // SUB_ROWS
    n_ch = SUB_ROWS // CHUNK
    hist = {"z": zc_ref[...], "p": pc_ref[...]}
    projs, mids, x_mid = {}, {}, {}

    def project(jj):
        h = (x_ref[0, jj * SUB_ROWS:(jj + 1) * SUB_ROWS, :] * sc1 + sh).astype(BF16)
        projs[jj] = jnp.concatenate([_dot(h, w[...]) for w in win_refs], axis=1)

    def heads(j):
        proj = projs.pop(j)
        o0 = 0
        a_b = proj[:, o0:o0 + w_conv]; o0 += w_conv
        a_c = proj[:, o0:o0 + w_conv]; o0 += w_conv
        a_x = proj[:, o0:o0 + w_conv]; o0 += w_conv
        g_u = proj[:, o0:o0 + w_gmlp]; o0 += w_gmlp
        g_v = proj[:, o0:o0 + w_gmlp]; o0 += w_gmlp
        p_x = proj[:, o0:o0 + w_pool]

        z = a_c * a_x
        z_ext = jnp.concatenate([hist["z"], z], axis=0)
        conv = z_ext * cw[CONV_WIDTH - 1:CONV_WIDTH, :]
        for k in range(1, CONV_WIDTH):
            conv = conv + pltpu.roll(z_ext, k, 0) * cw[CONV_WIDTH - 1 - k:CONV_WIDTH - k, :]
        y_a = a_b * conv[CONV_HIST:, :]
        hist["z"] = z[SUB_ROWS - CONV_HIST:, :]

        v_n = _layer_norm(g_v, sgu_g, sgu_b)
        s_cols = []
        for p in range(n_heads // 2):
            cols = [v_n[c * CHUNK:(c + 1) * CHUNK, p * LANES:(p + 1) * LANES] for c in range(n_ch)]
            rhs = jnp.concatenate(
                [jnp.concatenate([jnp.where(low_half, col, 0.0) for col in cols], axis=1),
                 jnp.concatenate([jnp.where(low_half, 0.0, col) for col in cols], axis=1)],
                axis=0).astype(BF16)
            s_cols.append(_dot(ws_m[p], rhs))

        run = jnp.concatenate([hist["p"], p_x], axis=0)
        sums = []
        for w in POOL_WINDOWS:
            run = run + pltpu.roll(run, w // 2, 0)
            sums.append(run[POOL_HIST:, :])
        count = jnp.minimum(window, t * rows + j * SUB_ROWS + row_id + 1).astype(F32)
        d = (_select_by_group(sums, lane_group) / count - p_x).astype(BF16)
        y_c_raw = _dot(d, pw_ref[...])
        hist["p"] = p_x[SUB_ROWS - POOL_HIST:, :]
        mids[j] = (y_a, g_u, s_cols, y_c_raw)

    def mix_out(j):
        y_a, g_u, s_cols, y_c_raw = mids.pop(j)
        s = jnp.concatenate(
            [jnp.concatenate([sc[:, c * LANES:(c + 1) * LANES] for sc in s_cols], axis=1) + bias
             for c in range(n_ch)], axis=0)
        mix = jnp.concatenate([y_a, g_u * s, y_c_raw * pool_scale], axis=1).astype(BF16)
        out = jnp.concatenate([_dot(mix, w[...]) for w in wout_refs], axis=1)
        x = x_ref[0, j * SUB_ROWS:(j + 1) * SUB_ROWS, :]
        x_mid[j] = _layer_norm(alpha * x + gate * out, ln_g, ln_b)

    def store(j, y):
        o_ref[0, j * SUB_ROWS:(j + 1) * SUB_ROWS, :] = y

    _layer_pipeline(n_sub, project, heads, mix_out, x_mid, sh2, sc2, gate2, wup_refs, wdn_refs,
                    ln2_g, ln2_b, alpha, store)

    zc_ref[...] = hist["z"]
    pc_ref[...] = hist["p"]
    ncv_ref[0] = hist["z"][CONV_HIST - (CONV_WIDTH - 1):, :]
    npl_ref[0] = hist["p"][POOL_HIST - (max(POOL_WINDOWS) - 1):, :]


def _layer_prompt_call(x, mod, w_in, conv_w, sgu_g, sgu_b, w_s, bs_full, pool_wbd, pool_scale,
                       w_out, ln_g, ln_b, wup, wdn, ln2_g, ln2_b, *, alpha, layer, rows):
    batch, seq, c_dim = x.shape
    w_conv, w_pool = conv_w.shape[2], pool_scale.shape[1]
    conv_buf = CONV_WIDTH - 1
    pool_buf = max(POOL_WINDOWS) - 1
    in_specs = _column_blocks(w_in, layer, WEIGHT_COLS)
    out_specs = _column_blocks(w_out, layer, WEIGHT_COLS)
    up_specs = _column_blocks(wup, layer, WEIGHT_COLS)
    dn_specs = _column_blocks(wdn, layer, WEIGHT_COLS)
    kern = functools.partial(_layer_prompt_kernel, alpha=alpha, layer=layer, n_in=len(in_specs),
                             n_out=len(out_specs), n_up=len(up_specs), n_dn=len(dn_specs))
    return pl.pallas_call(
        kern,
        grid=(batch, seq // rows),
        in_specs=[
            pl.BlockSpec((1, rows, c_dim), lambda b, t: (b, t, 0)),
            pl.BlockSpec((None, batch, mod.shape[2]), lambda b, t: (layer, 0, 0)),
            *in_specs,
            _whole(conv_w.shape),
            _whole(sgu_g.shape),
            _whole(sgu_b.shape),
            _layer_block(w_s.shape[1:], layer),
            _layer_block(bs_full.shape[1:], layer),
            _layer_block(pool_wbd.shape[1:], layer),
            _whole(pool_scale.shape),
            *out_specs,
            _whole(ln_g.shape),
            _whole(ln_b.shape),
            *up_specs, *dn_specs,
            _whole(ln2_g.shape),
            _whole(ln2_b.shape),
        ],
        out_specs=[
            pl.BlockSpec((1, rows, c_dim), lambda b, t: (b, t, 0)),
            pl.BlockSpec((1, conv_buf, w_conv), lambda b, t: (b, 0, 0)),
            pl.BlockSpec((1, pool_buf, w_pool), lambda b, t: (b, 0, 0)),
        ],
        out_shape=[
            jax.ShapeDtypeStruct((batch, seq, c_dim), F32),
            jax.ShapeDtypeStruct((batch, conv_buf, w_conv), F32),
            jax.ShapeDtypeStruct((batch, pool_buf, w_pool), F32),
        ],
        scratch_shapes=[
            pltpu.VMEM((CONV_HIST, w_conv), F32),
            pltpu.VMEM((POOL_HIST, w_pool), F32),
        ],
        input_output_aliases={0: 0} if layer > 0 else {},
        compiler_params=pltpu.CompilerParams(
            dimension_semantics=("arbitrary", "arbitrary"),
            vmem_limit_bytes=VMEM_LIMIT_BYTES),
        name="layer_prompt",
    )(x, mod, *([w_in] * len(in_specs)), conv_w, sgu_g, sgu_b, w_s, bs_full, pool_wbd, pool_scale,
      *([w_out] * len(out_specs)), ln_g, ln_b, *([wup] * len(up_specs)),
      *([wdn] * len(dn_specs)), ln2_g, ln2_b)


def _layer_sample_kernel(x_ref, mod_ref, cst_ref, pst_ref, *refs, alpha, layer, n_in, n_out,
                         n_up, n_dn, x_batch_major, out_batch_major):
    win_refs, refs = refs[:n_in], refs[n_in:]
    cw_ref, sg_ref, sb_ref, wst_ref, bst_ref, pw_ref, ps_ref = refs[:7]
    wout_refs, refs = refs[7:7 + n_out], refs[7 + n_out:]
    g_ref, b_ref = refs[:2]
    wup_refs, refs = refs[2:2 + n_up], refs[2 + n_up:]
    wdn_refs, refs = refs[:n_dn], refs[n_dn:]
    g2_ref, b2_ref, o_ref, ncv_ref, npl_ref, v_ref = refs
    xt = jnp.swapaxes(x_ref[...], 0, 1) if x_batch_major else None
    cst = jnp.swapaxes(cst_ref[...], 0, 1)
    pst = jnp.swapaxes(pst_ref[...], 0, 1)
    t_dim, b_dim = v_ref.shape[1], v_ref.shape[0]
    c_dim = x_ref.shape[2]
    w_conv, w_gmlp, w_pool = cw_ref.shape[2], sg_ref.shape[1], ps_ref.shape[1]
    n_valid = pst.shape[0]
    step = SUB_ROWS // b_dim
    n_sub = t_dim // step

    m = mod_ref[...]
    sh, sc1, gate = m[:, :c_dim], 1.0 + m[:, c_dim:2 * c_dim], m[:, 2 * c_dim:3 * c_dim]
    sh2, sc2, gate2 = (m[:, 3 * c_dim:4 * c_dim], 1.0 + m[:, 4 * c_dim:5 * c_dim],
                       m[:, 5 * c_dim:])
    cw = cw_ref[layer]
    sgu_g, sgu_b = sg_ref[layer:layer + 1, :], sb_ref[layer:layer + 1, :]
    pool_scale = ps_ref[layer:layer + 1, :]
    ln_g, ln_b = g_ref[layer:layer + 1, :], b_ref[layer:layer + 1, :]
    ln2_g, ln2_b = g2_ref[layer:layer + 1, :], b2_ref[layer:layer + 1, :]
    lane_group = _lane_group((b_dim, w_pool), len(POOL_WINDOWS))

    z_ext = [cst[i] for i in range(CONV_WIDTH - 1)]
    p_ext = [pst[i] for i in range(n_valid)]
    v_all = []
    projs, mids, x_mid, outs = {}, {}, {}, {}

    def x_rows(j):
        if x_batch_major:
            return xt[j * step:(j + 1) * step]
        return x_ref[j * step:(j + 1) * step]

    def project(j):
        h = (x_rows(j) * sc1 + sh).reshape(step * b_dim, c_dim).astype(BF16)
        projs[j] = jnp.concatenate([_dot(h, w[...]) for w in win_refs], axis=1).reshape(
            step, b_dim, -1)

    def heads(j):
        proj = projs.pop(j)
        o0 = 0
        a_b = proj[:, :, o0:o0 + w_conv]; o0 += w_conv
        a_c = proj[:, :, o0:o0 + w_conv]; o0 += w_conv
        a_x = proj[:, :, o0:o0 + w_conv]; o0 += w_conv
        g_u = proj[:, :, o0:o0 + w_gmlp]; o0 += w_gmlp
        g_v = proj[:, :, o0:o0 + w_gmlp]; o0 += w_gmlp
        p_x = proj[:, :, o0:o0 + w_pool]
        z = a_c * a_x
        v_n = _layer_norm(g_v, sgu_g, sgu_b)
        y_a, y_b, d = [], [], []
        for i in range(step):
            pos = j * step + i
            z_ext.append(z[i])
            conv = z_ext[pos] * cw[0:1, :]
            for k in range(1, CONV_WIDTH):
                conv = conv + z_ext[pos + k] * cw[k:k + 1, :]
            y_a.append(a_b[i] * conv)
            v_all.append(v_n[i])
            s = bst_ref[pos:pos + 1, :] + wst_ref[pos, 0:1, :] * v_all[0]
            for q in range(1, pos + 1):
                s = s + wst_ref[pos, q:q + 1, :] * v_all[q]
            y_b.append(g_u[i] * s)
            p_ext.append(p_x[i])
            last = n_valid + pos
            means, run, k = [], None, 0
            for w in POOL_WINDOWS:
                while k < w:
                    run = p_ext[last - k] if run is None else run + p_ext[last - k]
                    k += 1
                means.append(run / float(min(w, pos + 1 + n_valid)))
            d.append(_select_by_group(means, lane_group) - p_x[i])
        d = jnp.stack(d).reshape(step * b_dim, w_pool).astype(BF16)
        mids[j] = (jnp.stack(y_a), jnp.stack(y_b), _dot(d, pw_ref[...]))

    def mix_out(j):
        y_a, y_b, y_c_raw = mids.pop(j)
        y_c = (y_c_raw * pool_scale).reshape(step, b_dim, w_pool)
        mix = jnp.concatenate([y_a, y_b, y_c], axis=2).reshape(step * b_dim, c_dim).astype(BF16)
        out = jnp.concatenate([_dot(mix, w[...]) for w in wout_refs], axis=1)
        x_mid[j] = _layer_norm(alpha * x_rows(j) + gate * out.reshape(step, b_dim, c_dim),
                               ln_g, ln_b)

    def store(j, y):
        if out_batch_major:
            outs[j] = y
        else:
            o_ref[j * step:(j + 1) * step] = y

    _layer_pipeline(n_sub, project, heads, mix_out, x_mid, sh2, sc2, gate2, wup_refs, wdn_refs,
                    ln2_g, ln2_b, alpha, store)

    if out_batch_major:
        o_ref[...] = jnp.swapaxes(jnp.concatenate([outs[j] for j in range(n_sub)], axis=0), 0, 1)
    ncv_ref[...] = jnp.swapaxes(jnp.stack(z_ext[len(z_ext) - (CONV_WIDTH - 1):]), 0, 1)
    npl_ref[...] = jnp.swapaxes(jnp.stack(p_ext[len(p_ext) - n_valid:]), 0, 1)
    v_ref[...] = jnp.swapaxes(jnp.stack(v_all), 0, 1)


def _layer_sample_call(x, mod, cst, pst, w_in, conv_w, sgu_g, sgu_b, wst, bs_full, pool_wbd,
                       pool_scale, w_out, ln_g, ln_b, wup, wdn, ln2_g, ln2_b, *, alpha, layer,
                       x_batch_major, out_batch_major):
    if x_batch_major:
        b_dim, t_dim, c_dim = x.shape
    else:
        t_dim, b_dim, c_dim = x.shape
    w_gmlp = sgu_g.shape[1]
    in_specs = _column_blocks(w_in, layer, WEIGHT_COLS)
    out_specs = _column_blocks(w_out, layer, WEIGHT_COLS)
    up_specs = _column_blocks(wup, layer, WEIGHT_COLS)
    dn_specs = _column_blocks(wdn, layer, WEIGHT_COLS)
    kern = functools.partial(_layer_sample_kernel, alpha=alpha, layer=layer, n_in=len(in_specs),
                             n_out=len(out_specs), n_up=len(up_specs), n_dn=len(dn_specs),
                             x_batch_major=x_batch_major, out_batch_major=out_batch_major)
    out_shape = (b_dim, t_dim, c_dim) if out_batch_major else (t_dim, b_dim, c_dim)
    whole3 = lambda i: (0, 0, 0)
    return pl.pallas_call(
        kern,
        grid=(1,),
        in_specs=[
            pl.BlockSpec(x.shape, whole3),
            pl.BlockSpec((None, b_dim, mod.shape[2]), lambda i: (layer, 0, 0)),
            _layer_block(cst.shape[1:], layer),
            _layer_block(pst.shape[1:], layer),
            *in_specs,
            _whole(conv_w.shape),
            _whole(sgu_g.shape),
            _whole(sgu_b.shape),
            _layer_block(wst.shape[1:], layer),
            pl.BlockSpec((None, t_dim, w_gmlp), lambda i: (layer, 0, 0)),
            _layer_block(pool_wbd.shape[1:], layer),
            _whole(pool_scale.shape),
            *out_specs,
            _whole(ln_g.shape),
            _whole(ln_b.shape),
            *up_specs, *dn_specs,
            _whole(ln2_g.shape),
            _whole(ln2_b.shape),
        ],
        out_specs=[
            pl.BlockSpec(out_shape, whole3),
            pl.BlockSpec(cst.shape[1:], whole3),
            pl.BlockSpec(pst.shape[1:], whole3),
            pl.BlockSpec((b_dim, t_dim, w_gmlp), whole3),
        ],
        out_shape=[
            jax.ShapeDtypeStruct(out_shape, F32),
            jax.ShapeDtypeStruct(cst.shape[1:], F32),
            jax.ShapeDtypeStruct(pst.shape[1:], F32),
            jax.ShapeDtypeStruct((b_dim, t_dim, w_gmlp), F32),
        ],
        compiler_params=pltpu.CompilerParams(
            dimension_semantics=("arbitrary",),
            vmem_limit_bytes=VMEM_LIMIT_BYTES),
        name="layer_sample",
    )(x, mod, cst, pst, *([w_in] * len(in_specs)), conv_w, sgu_g, sgu_b, wst, bs_full, pool_wbd,
      pool_scale, *([w_out] * len(out_specs)), ln_g, ln_b, *([wup] * len(up_specs)),
      *([wdn] * len(dn_specs)), ln2_g, ln2_b)


def kernel(x_prompt, x_sample, c_prompt, c_sample, state_conv, state_pool, w_ada, b_ada, w_in,
           conv_w, sgu_g, sgu_b, w_s, b_s, pool_w, pool_scale, w_out, ln1_g, ln1_b, w_up,
           w_down, ln2_g, ln2_b):
    depth = w_in.shape[0]
    seq = x_prompt.shape[1]
    dec_batch, dec_seq, _ = x_sample.shape
    n_groups, group_dim, _ = pool_w.shape[1:]
    w_pool = n_groups * group_dim
    alpha = (2.0 * depth) ** 0.25
    assert seq % PROMPT_ROWS == 0 and PROMPT_ROWS % SUB_ROWS == 0 and SUB_ROWS % CHUNK == 0
    assert (dec_seq * dec_batch) % SUB_ROWS == 0 and SUB_ROWS % dec_batch == 0

    mod_p, mod_s = _ada_call(c_prompt, c_sample, w_ada, b_ada)

    w_in_b, w_out_b = w_in.astype(BF16), w_out.astype(BF16)
    wup_b, wdn_b = w_up.astype(BF16), w_down.astype(BF16)
    bs_full = jnp.repeat(jnp.swapaxes(b_s, 1, 2), HEAD_DIM, axis=2)
    ws_small = jnp.repeat(jnp.transpose(w_s[:, :, :dec_seq, :dec_seq], (0, 2, 3, 1)), HEAD_DIM, axis=3)
    eye = jnp.eye(n_groups, dtype=F32)
    pool_wbd = (eye[None, :, None, :, None] * pool_w[:, :, :, None, :]).reshape(
        depth, w_pool, w_pool).astype(BF16)

    xp, xs = x_prompt, x_sample
    conv_p, pool_p, conv_s, pool_s, v_s = [], [], [], [], []
    for l in range(depth):
        xp, ncv, npl = _layer_prompt_call(
            xp, mod_p, w_in_b, conv_w, sgu_g, sgu_b, w_s, bs_full, pool_wbd, pool_scale,
            w_out_b, ln1_g, ln1_b, wup_b, wdn_b, ln2_g, ln2_b, alpha=alpha, layer=l,
            rows=PROMPT_ROWS)
        conv_p.append(ncv)
        pool_p.append(npl)

        xp, xs = lax.optimization_barrier((xp, xs))
        xs, ncv, npl, v_rows = _layer_sample_call(
            xs, mod_s, state_conv, state_pool, w_in_b, conv_w, sgu_g, sgu_b, ws_small, bs_full,
            pool_wbd, pool_scale, w_out_b, ln1_g, ln1_b, wup_b, wdn_b, ln2_g, ln2_b, alpha=alpha,
            layer=l, x_batch_major=(l == 0), out_batch_major=(l == depth - 1))
        conv_s.append(ncv)
        pool_s.append(npl)
        v_s.append(v_rows)
        xp, xs = lax.optimization_barrier((xp, xs))

    return (xp, xs, jnp.stack(conv_p), jnp.stack(pool_p), jnp.stack(conv_s), jnp.stack(pool_s),
            jnp.stack(v_s))
```

```python
import functools

import jax
import jax.numpy as jnp
from jax import lax
from jax.experimental import pallas as pl
from jax.experimental.pallas import tpu as pltpu

F32 = jnp.float32
BF16 = jnp.bfloat16

LN_EPS = 1e-5
HEAD_DIM = 64
CHUNK = 128
POOL_WINDOWS = (2, 4, 8, 16)
CONV_WIDTH = 3
LANES = 128
SUBLANES = 8
VMEM_LIMIT_BYTES = 60 * 1024 * 1024

PROMPT_ROWS = 1024
SUB_ROWS = 256
UP_AHEAD = 2
PROJ_AHEAD = 2
MIX_OUT_AT = 2
HEADS_AT = 5
WEIGHT_COLS = 512
CONV_HIST = SUBLANES
POOL_HIST = 2 * SUBLANES


def _layer_norm(y, g, b):
    mu = jnp.mean(y, axis=-1, keepdims=True)
    d = y - mu
    var = jnp.mean(d * d, axis=-1, keepdims=True)
    return d * lax.rsqrt(var + LN_EPS) * g + b


def _dot(a, b):
    return jnp.dot(a, b, preferred_element_type=F32)


def _layer_block(shape, layer):
    zeros = (0,) * len(shape)
    return pl.BlockSpec((None,) + tuple(shape), lambda *_: (layer,) + zeros,
                        pipeline_mode=pl.Buffered(1))


def _whole(shape):
    zeros = (0,) * len(shape)
    return pl.BlockSpec(tuple(shape), lambda *_: zeros, pipeline_mode=pl.Buffered(1))


def _column_blocks(w, width):
    rows, cols = w.shape
    return [pl.BlockSpec((rows, width), lambda *_, n=n: (0, n), pipeline_mode=pl.Buffered(1))
            for n in range(cols // width)]


def _lane_group(shape, n_groups):
    return lax.broadcasted_iota(jnp.int32, shape, len(shape) - 1) // (shape[-1] // n_groups)


def _select_by_group(values, lane_group):
    out = values[-1]
    for g in range(len(values) - 2, -1, -1):
        out = jnp.where(lane_group == g, values[g], out)
    return out


def _ada_kernel(cp_ref, cs_ref, w_ref, b_ref, op_ref, os_ref):
    l, k = pl.program_id(0), pl.program_id(1)
    bp = cp_ref.shape[0]
    c = jnp.concatenate([cp_ref[...], cs_ref[...]], axis=0)
    part = _dot(jax.nn.silu(c).astype(BF16), w_ref[0].astype(BF16))
    part_p, part_s = part[:bp], part[bp:]

    @pl.when(k == 0)
    def _():
        bias = b_ref[pl.ds(l, 1), :]
        op_ref[0] = part_p + bias
        os_ref[0] = part_s + bias

    @pl.when(k > 0)
    def _():
        op_ref[0] += part_p
        os_ref[0] += part_s


def _ada_call(c_prompt, c_sample, w_ada, b_ada, row_tile=256):
    depth, d_model, n_mod = w_ada.shape
    bp, bs = c_prompt.shape[0], c_sample.shape[0]
    return pl.pallas_call(
        _ada_kernel,
        grid=(depth, d_model // row_tile),
        in_specs=[
            pl.BlockSpec((bp, row_tile), lambda l, k: (0, k)),
            pl.BlockSpec((bs, row_tile), lambda l, k: (0, k)),
            pl.BlockSpec((1, row_tile, n_mod), lambda l, k: (l, k, 0)),
            _whole(b_ada.shape),
        ],
        out_specs=[
            pl.BlockSpec((1, bp, n_mod), lambda l, k: (l, 0, 0)),
            pl.BlockSpec((1, bs, n_mod), lambda l, k: (l, 0, 0)),
        ],
        out_shape=[
            jax.ShapeDtypeStruct((depth, bp, n_mod), F32),
            jax.ShapeDtypeStruct((depth, bs, n_mod), F32),
        ],
        compiler_params=pltpu.CompilerParams(
            dimension_semantics=("arbitrary", "arbitrary"),
            vmem_limit_bytes=VMEM_LIMIT_BYTES),
        name="ada_mod",
    )(c_prompt, c_sample, w_ada, b_ada)


def _layer_pipeline(n_sub, project, heads, mix_out, x_mid, sh2, sc2, gate2, wup_refs, wdn_refs,
                    ln2_g, ln2_b, alpha, store):
    n_up, n_dn = len(wup_refs), len(wdn_refs)
    chunk = wup_refs[0].shape[1]

    def heads_and_project(j):
        heads(j)
        if j + PROJ_AHEAD < n_sub:
            project(j + PROJ_AHEAD)

    for jj in range(min(PROJ_AHEAD, n_sub)):
        project(jj)
    heads_and_project(0)
    mix_out(0)
    if n_sub > 1:
        heads_and_project(1)

    stages = [(j, k) for j in range(n_sub) for k in range(n_up)]
    hs, ups = {}, {}

    def issue_up(s):
        j, k = stages[s]
        if j not in hs:
            xm = x_mid[j]
            hs[j] = (xm * sc2 + sh2).reshape(-1, xm.shape[-1]).astype(BF16)
        ups[s] = _dot(hs[j], wup_refs[k][...])

    for s in range(min(UP_AHEAD, len(stages))):
        issue_up(s)
    acc = [None] * n_dn
    for s, (j, k) in enumerate(stages):
        if k == MIX_OUT_AT and j + 1 < n_sub:
            mix_out(j + 1)
        if k == HEADS_AT and j + 2 < n_sub:
            heads_and_project(j + 2)
        if s + UP_AHEAD < len(stages):
            issue_up(s + UP_AHEAD)
        a = jnp.square(jnp.maximum(ups.pop(s), 0.0)).astype(BF16)
        for n in range(n_dn):
            part = _dot(a, wdn_refs[n][k * chunk:(k + 1) * chunk, :])
            acc[n] = part if acc[n] is None else acc[n] + part
        if k == n_up - 1:
            xm = x_mid.pop(j)
            f = jnp.concatenate(acc, axis=1).reshape(xm.shape)
            store(j, _layer_norm(alpha * xm + gate2 * f, ln2_g, ln2_b))
            acc = [None] * n_dn


def _layer_prompt_kernel(x_ref, mod_ref, *refs, alpha, layer, n_in, n_out, n_up, n_dn, n_next):
    win_refs, refs = refs[:n_in], refs[n_in:]
    cw_ref, sg_ref, sb_ref, ws_ref, bs_ref, pw_ref, ps_ref = refs[:7]
    wout_refs, refs = refs[7:7 + n_out], refs[7 + n_out:]
    g_ref, b_ref = refs[:2]
    wup_refs, refs = refs[2:2 + n_up], refs[2 + n_up:]
    wdn_refs, refs = refs[:n_dn], refs[n_dn:]
    g2_ref, b2_ref = refs[:2]
    next_f32, refs = refs[2:2 + n_next], refs[2 + n_next:]
    o_ref, ncv_ref, npl_ref = refs[:3]
    next_bf16, (zc_ref, pc_ref) = refs[3:3 + n_next], refs[3 + n_next:]
    for src, dst in zip(next_f32, next_bf16):
        dst[...] = src[...].astype(BF16)
    bi, t = pl.program_id(0), pl.program_id(1)
    rows, c_dim = x_ref.shape[1], x_ref.shape[2]
    w_conv, w_gmlp, w_pool = cw_ref.shape[2], sg_ref.shape[1], ps_ref.shape[1]
    n_heads = ws_ref.shape[0]

    @pl.when(t == 0)
    def _():
        zc_ref[...] = jnp.zeros(zc_ref.shape, F32)
        pc_ref[...] = jnp.zeros(pc_ref.shape, F32)

    m = mod_ref[pl.ds(bi, 1), :]
    sh, sc1, gate = m[:, :c_dim], 1.0 + m[:, c_dim:2 * c_dim], m[:, 2 * c_dim:3 * c_dim]
    sh2, sc2, gate2 = (m[:, 3 * c_dim:4 * c_dim], 1.0 + m[:, 4 * c_dim:5 * c_dim],
                       m[:, 5 * c_dim:])
    cw = cw_ref[layer]
    sgu_g, sgu_b = sg_ref[layer:layer + 1, :], sb_ref[layer:layer + 1, :]
    pool_scale = ps_ref[layer:layer + 1, :]
    ln_g, ln_b = g_ref[layer:layer + 1, :], b_ref[layer:layer + 1, :]
    ln2_g, ln2_b = g2_ref[layer:layer + 1, :], b2_ref[layer:layer + 1, :]

    row_i = lax.broadcasted_iota(jnp.int32, (CHUNK, 2 * CHUNK), 0)
    col_j = lax.broadcasted_iota(jnp.int32, (CHUNK, 2 * CHUNK), 1) % CHUNK
    causal = col_j <= row_i
    ws_m = []
    for p in range(n_heads // 2):
        pair = jnp.concatenate([ws_ref[2 * p], ws_ref[2 * p + 1]], axis=1)
        ws_m.append(jnp.where(causal, pair, 0.0).astype(BF16))
    low_half = lax.broadcasted_iota(jnp.int32, (CHUNK, LANES), 1) < HEAD_DIM
    lane_group = _lane_group((SUB_ROWS, w_pool), len(POOL_WINDOWS))
    window = _select_by_group([jnp.full((SUB_ROWS, w_pool), w, jnp.int32) for w in POOL_WINDOWS],
                              lane_group)
    row_id = lax.broadcasted_iota(jnp.int32, (SUB_ROWS, w_pool), 0)
    bias = bs_ref[...]
    n_sub = rows // SUB_ROWS
    n_ch = SUB_ROWS // CHUNK
    hist = {"z": zc_ref[...], "p": pc_ref[...]}
    projs, mids, x_mid = {}, {}, {}

    def project(jj):
        h = (x_ref[0, jj * SUB_ROWS:(jj + 1) * SUB_ROWS, :] * sc1 + sh).astype(BF16)
        projs[jj] = jnp.concatenate([_dot(h, w[...]) for w in win_refs], axis=1)

    def heads(j):
        proj = projs.pop(j)
        o0 = 0
        a_b = proj[:, o0:o0 + w_conv]; o0 += w_conv
        a_c = proj[:, o0:o0 + w_conv]; o0 += w_conv
        a_x = proj[:, o0:o0 + w_conv]; o0 += w_conv
        g_u = proj[:, o0:o0 + w_gmlp]; o0 += w_gmlp
        g_v = proj[:, o0:o0 + w_gmlp]; o0 += w_gmlp
        p_x = proj[:, o0:o0 + w_pool]

        z = a_c * a_x
        z_ext = jnp.concatenate([hist["z"], z], axis=0)
        conv = z_ext * cw[CONV_WIDTH - 1:CONV_WIDTH, :]
        for k in range(1, CONV_WIDTH):
            conv = conv + pltpu.roll(z_ext, k, 0) * cw[CONV_WIDTH - 1 - k:CONV_WIDTH - k, :]
        y_a = a_b * conv[CONV_HIST:, :]
        hist["z"] = z[SUB_ROWS - CONV_HIST:, :]

        v_n = _layer_norm(g_v, sgu_g, sgu_b)
        s_cols = []
        for p in range(n_heads // 2):
            cols = [v_n[c * CHUNK:(c + 1) * CHUNK, p * LANES:(p + 1) * LANES] for c in range(n_ch)]
            rhs = jnp.concatenate(
                [jnp.concatenate([jnp.where(low_half, col, 0.0) for col in cols], axis=1),
                 jnp.concatenate([jnp.where(low_half, 0.0, col) for col in cols], axis=1)],
                axis=0).astype(BF16)
            s_cols.append(_dot(ws_m[p], rhs))

        run = jnp.concatenate([hist["p"], p_x], axis=0)
        sums = []
        for w in POOL_WINDOWS:
            run = run + pltpu.roll(run, w // 2, 0)
            sums.append(run[POOL_HIST:, :])
        count = jnp.minimum(window, t * rows + j * SUB_ROWS + row_id + 1).astype(F32)
        d = (_select_by_group(sums, lane_group) / count - p_x).astype(BF16)
        y_c_raw = _dot(d, pw_ref[...])
        hist["p"] = p_x[SUB_ROWS - POOL_HIST:, :]
        mids[j] = (y_a, g_u, s_cols, y_c_raw)

    def mix_out(j):
        y_a, g_u, s_cols, y_c_raw = mids.pop(j)
        s = jnp.concatenate(
            [jnp.concatenate([sc[:, c * LANES:(c + 1) * LANES] for sc in s_cols], axis=1) + bias
             for c in range(n_ch)], axis=0)
        mix = jnp.concatenate([y_a, g_u * s, y_c_raw * pool_scale], axis=1).astype(BF16)
        out = jnp.concatenate([_dot(mix, w[...]) for w in wout_refs], axis=1)
        x = x_ref[0, j * SUB_ROWS:(j + 1) * SUB_ROWS, :]
        x_mid[j] = _layer_norm(alpha * x + gate * out, ln_g, ln_b)

    def store(j, y):
        o_ref[0, j * SUB_ROWS:(j + 1) * SUB_ROWS, :] = y

    _layer_pipeline(n_sub, project, heads, mix_out, x_mid, sh2, sc2, gate2, wup_refs, wdn_refs,
                    ln2_g, ln2_b, alpha, store)

    zc_ref[...] = hist["z"]
    pc_ref[...] = hist["p"]
    ncv_ref[0] = hist["z"][CONV_HIST - (CONV_WIDTH - 1):, :]
    npl_ref[0] = hist["p"][POOL_HIST - (max(POOL_WINDOWS) - 1):, :]


def _layer_prompt_call(x, mod, w_in, conv_w, sgu_g, sgu_b, w_s, bs_full, pool_wbd, pool_scale,
                       w_out, ln_g, ln_b, wup, wdn, ln2_g, ln2_b, next_weights, *, alpha, layer,
                       rows):
    batch, seq, c_dim = x.shape
    n_steps = batch * (seq // rows)
    w_conv, w_pool = conv_w.shape[2], pool_scale.shape[1]
    conv_buf = CONV_WIDTH - 1
    pool_buf = max(POOL_WINDOWS) - 1
    in_specs = _column_blocks(w_in, WEIGHT_COLS)
    out_specs = _column_blocks(w_out, WEIGHT_COLS)
    up_specs = _column_blocks(wup, WEIGHT_COLS)
    dn_specs = _column_blocks(wdn, WEIGHT_COLS)
    kern = functools.partial(_layer_prompt_kernel, alpha=alpha, layer=layer, n_in=len(in_specs),
                             n_out=len(out_specs), n_up=len(up_specs), n_dn=len(dn_specs),
                             n_next=len(next_weights))
    n_t = seq // rows
    assert all(w.shape[1] % n_steps == 0 for w in next_weights)
    next_in = [pl.BlockSpec((None, w.shape[1] // n_steps, w.shape[2]),
                            lambda b, t: (layer + 1, b * n_t + t, 0)) for w in next_weights]
    next_out = [pl.BlockSpec((w.shape[1] // n_steps, w.shape[2]), lambda b, t: (b * n_t + t, 0))
                for w in next_weights]
    return pl.pallas_call(
        kern,
        grid=(batch, seq // rows),
        in_specs=[
            pl.BlockSpec((1, rows, c_dim), lambda b, t: (b, t, 0)),
            pl.BlockSpec((None, batch, mod.shape[2]), lambda b, t: (layer, 0, 0)),
            *in_specs,
            _whole(conv_w.shape),
            _whole(sgu_g.shape),
            _whole(sgu_b.shape),
            _layer_block(w_s.shape[1:], layer),
            _layer_block(bs_full.shape[1:], layer),
            _layer_block(pool_wbd.shape[1:], layer),
            _whole(pool_scale.shape),
            *out_specs,
            _whole(ln_g.shape),
            _whole(ln_b.shape),
            *up_specs, *dn_specs,
            _whole(ln2_g.shape),
            _whole(ln2_b.shape),
            *next_in,
        ],
        out_specs=[
            pl.BlockSpec((1, rows, c_dim), lambda b, t: (b, t, 0)),
            pl.BlockSpec((1, conv_buf, w_conv), lambda b, t: (b, 0, 0)),
            pl.BlockSpec((1, pool_buf, w_pool), lambda b, t: (b, 0, 0)),
            *next_out,
        ],
        out_shape=[
            jax.ShapeDtypeStruct((batch, seq, c_dim), F32),
            jax.ShapeDtypeStruct((batch, conv_buf, w_conv), F32),
            jax.ShapeDtypeStruct((batch, pool_buf, w_pool), F32),
            *[jax.ShapeDtypeStruct(w.shape[1:], BF16) for w in next_weights],
        ],
        scratch_shapes=[
            pltpu.VMEM((CONV_HIST, w_conv), F32),
            pltpu.VMEM((POOL_HIST, w_pool), F32),
        ],
        input_output_aliases={0: 0} if layer > 0 else {},
        compiler_params=pltpu.CompilerParams(
            dimension_semantics=("arbitrary", "arbitrary"),
            vmem_limit_bytes=VMEM_LIMIT_BYTES),
        name="layer_prompt",
    )(x, mod, *([w_in] * len(in_specs)), conv_w, sgu_g, sgu_b, w_s, bs_full, pool_wbd, pool_scale,
      *([w_out] * len(out_specs)), ln_g, ln_b, *([wup] * len(up_specs)),
      *([wdn] * len(dn_specs)), ln2_g, ln2_b, *next_weights)


def _layer_sample_kernel(x_ref, mod_ref, cst_ref, pst_ref, *refs, alpha, layer, n_in, n_out,
                         n_up, n_dn, x_batch_major, out_batch_major):
    win_refs, refs = refs[:n_in], refs[n_in:]
    cw_ref, sg_ref, sb_ref, wst_ref, bst_ref, pw_ref, ps_ref = refs[:7]
    wout_refs, refs = refs[7:7 + n_out], refs[7 + n_out:]
    g_ref, b_ref = refs[:2]
    wup_refs, refs = refs[2:2 + n_up], refs[2 + n_up:]
    wdn_refs, refs = refs[:n_dn], refs[n_dn:]
    g2_ref, b2_ref, o_ref, ncv_ref, npl_ref, v_ref = refs
    xt = jnp.swapaxes(x_ref[...], 0, 1) if x_batch_major else None
    cst = jnp.swapaxes(cst_ref[...], 0, 1)
    pst = jnp.swapaxes(pst_ref[...], 0, 1)
    t_dim, b_dim = v_ref.shape[1], v_ref.shape[0]
    c_dim = x_ref.shape[2]
    w_conv, w_gmlp, w_pool = cw_ref.shape[2], sg_ref.shape[1], ps_ref.shape[1]
    n_valid = pst.shape[0]
    step = SUB_ROWS // b_dim
    n_sub = t_dim // step

    m = mod_ref[...]
    sh, sc1, gate = m[:, :c_dim], 1.0 + m[:, c_dim:2 * c_dim], m[:, 2 * c_dim:3 * c_dim]
    sh2, sc2, gate2 = (m[:, 3 * c_dim:4 * c_dim], 1.0 + m[:, 4 * c_dim:5 * c_dim],
                       m[:, 5 * c_dim:])
    cw = cw_ref[layer]
    sgu_g, sgu_b = sg_ref[layer:layer + 1, :], sb_ref[layer:layer + 1, :]
    pool_scale = ps_ref[layer:layer + 1, :]
    ln_g, ln_b = g_ref[layer:layer + 1, :], b_ref[layer:layer + 1, :]
    ln2_g, ln2_b = g2_ref[layer:layer + 1, :], b2_ref[layer:layer + 1, :]
    lane_group = _lane_group((b_dim, w_pool), len(POOL_WINDOWS))

    z_ext = [cst[i] for i in range(CONV_WIDTH - 1)]
    p_ext = [pst[i] for i in range(n_valid)]
    v_all = []
    projs, mids, x_mid, outs = {}, {}, {}, {}

    def x_rows(j):
        if x_batch_major:
            return xt[j * step:(j + 1) * step]
        return x_ref[j * step:(j + 1) * step]

    def project(j):
        h = (x_rows(j) * sc1 + sh).reshape(step * b_dim, c_dim).astype(BF16)
        projs[j] = jnp.concatenate([_dot(h, w[...]) for w in win_refs], axis=1).reshape(
            step, b_dim, -1)

    def heads(j):
        proj = projs.pop(j)
        o0 = 0
        a_b = proj[:, :, o0:o0 + w_conv]; o0 += w_conv
        a_c = proj[:, :, o0:o0 + w_conv]; o0 += w_conv
        a_x = proj[:, :, o0:o0 + w_conv]; o0 += w_conv
        g_u = proj[:, :, o0:o0 + w_gmlp]; o0 += w_gmlp
        g_v = proj[:, :, o0:o0 + w_gmlp]; o0 += w_gmlp
        p_x = proj[:, :, o0:o0 + w_pool]
        z = a_c * a_x
        v_n = _layer_norm(g_v, sgu_g, sgu_b)
        y_a, y_b, d = [], [], []
        for i in range(step):
            pos = j * step + i
            z_ext.append(z[i])
            conv = z_ext[pos] * cw[0:1, :]
            for k in range(1, CONV_WIDTH):
                conv = conv + z_ext[pos + k] * cw[k:k + 1, :]
            y_a.append(a_b[i] * conv)
            v_all.append(v_n[i])
            s = bst_ref[pos:pos + 1, :] + wst_ref[pos, 0:1, :] * v_all[0]
            for q in range(1, pos + 1):
                s = s + wst_ref[pos, q:q + 1, :] * v_all[q]
            y_b.append(g_u[i] * s)
            p_ext.append(p_x[i])
            last = n_valid + pos
            means, run, k = [], None, 0
            for w in POOL_WINDOWS:
                while k < w:
                    run = p_ext[last - k] if run is None else run + p_ext[last - k]
                    k += 1
                means.append(run / float(min(w, pos + 1 + n_valid)))
            d.append(_select_by_group(means, lane_group) - p_x[i])
        d = jnp.stack(d).reshape(step * b_dim, w_pool).astype(BF16)
        mids[j] = (jnp.stack(y_a), jnp.stack(y_b), _dot(d, pw_ref[...]))

    def mix_out(j):
        y_a, y_b, y_c_raw = mids.pop(j)
        y_c = (y_c_raw * pool_scale).reshape(step, b_dim, w_pool)
        mix = jnp.concatenate([y_a, y_b, y_c], axis=2).reshape(step * b_dim, c_dim).astype(BF16)
        out = jnp.concatenate([_dot(mix, w[...]) for w in wout_refs], axis=1)
        x_mid[j] = _layer_norm(alpha * x_rows(j) + gate * out.reshape(step, b_dim, c_dim),
                               ln_g, ln_b)

    def store(j, y):
        if out_batch_major:
            outs[j] = y
        else:
            o_ref[j * step:(j + 1) * step] = y

    _layer_pipeline(n_sub, project, heads, mix_out, x_mid, sh2, sc2, gate2, wup_refs, wdn_refs,
                    ln2_g, ln2_b, alpha, store)

    if out_batch_major:
        o_ref[...] = jnp.swapaxes(jnp.concatenate([outs[j] for j in range(n_sub)], axis=0), 0, 1)
    ncv_ref[...] = jnp.swapaxes(jnp.stack(z_ext[len(z_ext) - (CONV_WIDTH - 1):]), 0, 1)
    npl_ref[...] = jnp.swapaxes(jnp.stack(p_ext[len(p_ext) - n_valid:]), 0, 1)
    v_ref[...] = jnp.swapaxes(jnp.stack(v_all), 0, 1)


def _layer_sample_call(x, mod, cst, pst, w_in, conv_w, sgu_g, sgu_b, wst, bs_full, pool_wbd,
                       pool_scale, w_out, ln_g, ln_b, wup, wdn, ln2_g, ln2_b, *, alpha, layer,
                       x_batch_major, out_batch_major):
    if x_batch_major:
        b_dim, t_dim, c_dim = x.shape
    else:
        t_dim, b_dim, c_dim = x.shape
    w_gmlp = sgu_g.shape[1]
    in_specs = _column_blocks(w_in, WEIGHT_COLS)
    out_specs = _column_blocks(w_out, WEIGHT_COLS)
    up_specs = _column_blocks(wup, WEIGHT_COLS)
    dn_specs = _column_blocks(wdn, WEIGHT_COLS)
    kern = functools.partial(_layer_sample_kernel, alpha=alpha, layer=layer, n_in=len(in_specs),
                             n_out=len(out_specs), n_up=len(up_specs), n_dn=len(dn_specs),
                             x_batch_major=x_batch_major, out_batch_major=out_batch_major)
    out_shape = (b_dim, t_dim, c_dim) if out_batch_major else (t_dim, b_dim, c_dim)
    whole3 = lambda i: (0, 0, 0)
    return pl.pallas_call(
        kern,
        grid=(1,),
        in_specs=[
            pl.BlockSpec(x.shape, whole3),
            pl.BlockSpec((None, b_dim, mod.shape[2]), lambda i: (layer, 0, 0)),
            _layer_block(cst.shape[1:], layer),
            _layer_block(pst.shape[1:], layer),
            *in_specs,
            _whole(conv_w.shape),
            _whole(sgu_g.shape),
            _whole(sgu_b.shape),
            _layer_block(wst.shape[1:], layer),
            pl.BlockSpec((None, t_dim, w_gmlp), lambda i: (layer, 0, 0)),
            _layer_block(pool_wbd.shape[1:], layer),
            _whole(pool_scale.shape),
            *out_specs,
            _whole(ln_g.shape),
            _whole(ln_b.shape),
            *up_specs, *dn_specs,
            _whole(ln2_g.shape),
            _whole(ln2_b.shape),
        ],
        out_specs=[
            pl.BlockSpec(out_shape, whole3),
            pl.BlockSpec(cst.shape[1:], whole3),
            pl.BlockSpec(pst.shape[1:], whole3),
            pl.BlockSpec((b_dim, t_dim, w_gmlp), whole3),
        ],
        out_shape=[
            jax.ShapeDtypeStruct(out_shape, F32),
            jax.ShapeDtypeStruct(cst.shape[1:], F32),
            jax.ShapeDtypeStruct(pst.shape[1:], F32),
            jax.ShapeDtypeStruct((b_dim, t_dim, w_gmlp), F32),
        ],
        compiler_params=pltpu.CompilerParams(
            dimension_semantics=("arbitrary",),
            vmem_limit_bytes=VMEM_LIMIT_BYTES),
        name="layer_sample",
    )(x, mod, cst, pst, *([w_in] * len(in_specs)), conv_w, sgu_g, sgu_b, wst, bs_full, pool_wbd,
      pool_scale, *([w_out] * len(out_specs)), ln_g, ln_b, *([wup] * len(up_specs)),
      *([wdn] * len(dn_specs)), ln2_g, ln2_b)


def kernel(x_prompt, x_sample, c_prompt, c_sample, state_conv, state_pool, w_ada, b_ada, w_in,
           conv_w, sgu_g, sgu_b, w_s, b_s, pool_w, pool_scale, w_out, ln1_g, ln1_b, w_up,
           w_down, ln2_g, ln2_b):
    depth = w_in.shape[0]
    seq = x_prompt.shape[1]
    dec_batch, dec_seq, _ = x_sample.shape
    n_groups, group_dim, _ = pool_w.shape[1:]
    w_pool = n_groups * group_dim
    alpha = (2.0 * depth) ** 0.25
    assert seq % PROMPT_ROWS == 0 and PROMPT_ROWS % SUB_ROWS == 0 and SUB_ROWS % CHUNK == 0
    assert (dec_seq * dec_batch) % SUB_ROWS == 0 and SUB_ROWS % dec_batch == 0

    mod_p, mod_s = _ada_call(c_prompt, c_sample, w_ada, b_ada)

    weights = (w_in, w_out, w_up, w_down)
    weights_b = [w[0].astype(BF16) for w in weights]
    bs_full = jnp.repeat(jnp.swapaxes(b_s, 1, 2), HEAD_DIM, axis=2)
    ws_small = jnp.repeat(jnp.transpose(w_s[:, :, :dec_seq, :dec_seq], (0, 2, 3, 1)), HEAD_DIM, axis=3)
    eye = jnp.eye(n_groups, dtype=F32)
    pool_wbd = (eye[None, :, None, :, None] * pool_w[:, :, :, None, :]).reshape(
        depth, w_pool, w_pool).astype(BF16)

    xp, xs = x_prompt, x_sample
    conv_p, pool_p, conv_s, pool_s, v_s = [], [], [], [], []
    for l in range(depth):
        w_in_b, w_out_b, wup_b, wdn_b = weights_b
        xp, ncv, npl, *weights_b = _layer_prompt_call(
            xp, mod_p, w_in_b, conv_w, sgu_g, sgu_b, w_s, bs_full, pool_wbd, pool_scale,
            w_out_b, ln1_g, ln1_b, wup_b, wdn_b, ln2_g, ln2_b,
            weights if l + 1 < depth else (), alpha=alpha, layer=l, rows=PROMPT_ROWS)
        conv_p.append(ncv)
        pool_p.append(npl)

        xp, xs = lax.optimization_barrier((xp, xs))
        xs, ncv, npl, v_rows = _layer_sample_call(
            xs, mod_s, state_conv, state_pool, w_in_b, conv_w, sgu_g, sgu_b, ws_small, bs_full,
            pool_wbd, pool_scale, w_out_b, ln1_g, ln1_b, wup_b, wdn_b, ln2_g, ln2_b, alpha=alpha,
            layer=l, x_batch_major=(l == 0), out_batch_major=(l == depth - 1))
        conv_s.append(ncv)
        pool_s.append(npl)
        v_s.append(v_rows)
        xp, xs = lax.optimization_barrier((xp, xs))

    return (xp, xs, jnp.stack(conv_p), jnp.stack(pool_p), jnp.stack(conv_s), jnp.stack(pool_s),
            jnp.stack(v_s))
```

```python
import functools

import jax
import jax.numpy as jnp
from jax import lax
from jax.experimental import pallas as pl
from jax.experimental.pallas import tpu as pltpu

F32 = jnp.float32
BF16 = jnp.bfloat16

LN_EPS = 1e-5
HEAD_DIM = 64
CHUNK = 128
POOL_WINDOWS = (2, 4, 8, 16)
CONV_WIDTH = 3
LANES = 128
SUBLANES = 8
VMEM_LIMIT_BYTES = 60 * 1024 * 1024

PROMPT_ROWS = 1024
SUB_ROWS = 256
UP_AHEAD = 2
PROJ_AHEAD = 2
MIX_OUT_AT = 2
HEADS_AT = 5
WEIGHT_COLS = 512
CONV_HIST = SUBLANES
POOL_HIST = 2 * SUBLANES


def _layer_norm(y, g, b):
    mu = jnp.mean(y, axis=-1, keepdims=True)
    d = y - mu
    var = jnp.mean(d * d, axis=-1, keepdims=True)
    return d * lax.rsqrt(var + LN_EPS) * g + b


def _dot(a, b):
    return jnp.dot(a, b, preferred_element_type=F32)


def _layer_block(shape, layer):
    zeros = (0,) * len(shape)
    return pl.BlockSpec((None,) + tuple(shape), lambda *_: (layer,) + zeros,
                        pipeline_mode=pl.Buffered(1))


def _whole(shape):
    zeros = (0,) * len(shape)
    return pl.BlockSpec(tuple(shape), lambda *_: zeros, pipeline_mode=pl.Buffered(1))


def _column_blocks(w, width):
    rows, cols = w.shape
    return [pl.BlockSpec((rows, width), lambda *_, n=n: (0, n), pipeline_mode=pl.Buffered(1))
            for n in range(cols // width)]


def _lane_group(shape, n_groups):
    return lax.broadcasted_iota(jnp.int32, shape, len(shape) - 1) // (shape[-1] // n_groups)


def _select_by_group(values, lane_group):
    out = values[-1]
    for g in range(len(values) - 2, -1, -1):
        out = jnp.where(lane_group == g, values[g], out)
    return out


def _ada_kernel(cp_ref, cs_ref, w_ref, b_ref, *refs):
    n_side = (len(refs) - 2) // 2
    side_f32, (op_ref, os_ref), side_bf16 = refs[:n_side], refs[n_side:n_side + 2], refs[n_side + 2:]
    for src, dst in zip(side_f32, side_bf16):
        dst[...] = src[0].astype(BF16)
    l, k = pl.program_id(0), pl.program_id(1)
    bp = cp_ref.shape[0]
    c = jnp.concatenate([cp_ref[...], cs_ref[...]], axis=0)
    part = _dot(jax.nn.silu(c).astype(BF16), w_ref[0].astype(BF16))
    part_p, part_s = part[:bp], part[bp:]

    @pl.when(k == 0)
    def _():
        bias = b_ref[pl.ds(l, 1), :]
        op_ref[0] = part_p + bias
        os_ref[0] = part_s + bias

    @pl.when(k > 0)
    def _():
        op_ref[0] += part_p
        os_ref[0] += part_s


def _ada_call(c_prompt, c_sample, w_ada, b_ada, weights, row_tile=256):
    depth, d_model, n_mod = w_ada.shape
    bp, bs = c_prompt.shape[0], c_sample.shape[0]
    n_k = d_model // row_tile
    n_steps = depth * n_k
    assert all(w.shape[1] % n_steps == 0 for w in weights)
    step = lambda l, k: l * n_k + k
    return pl.pallas_call(
        _ada_kernel,
        grid=(depth, n_k),
        in_specs=[
            pl.BlockSpec((bp, row_tile), lambda l, k: (0, k)),
            pl.BlockSpec((bs, row_tile), lambda l, k: (0, k)),
            pl.BlockSpec((1, row_tile, n_mod), lambda l, k: (l, k, 0)),
            _whole(b_ada.shape),
            *[pl.BlockSpec((1, w.shape[1] // n_steps, w.shape[2]), lambda l, k: (0, step(l, k), 0))
              for w in weights],
        ],
        out_specs=[
            pl.BlockSpec((1, bp, n_mod), lambda l, k: (l, 0, 0)),
            pl.BlockSpec((1, bs, n_mod), lambda l, k: (l, 0, 0)),
            *[pl.BlockSpec((w.shape[1] // n_steps, w.shape[2]), lambda l, k: (step(l, k), 0))
              for w in weights],
        ],
        out_shape=[
            jax.ShapeDtypeStruct((depth, bp, n_mod), F32),
            jax.ShapeDtypeStruct((depth, bs, n_mod), F32),
            *[jax.ShapeDtypeStruct(w.shape[1:], BF16) for w in weights],
        ],
        compiler_params=pltpu.CompilerParams(
            dimension_semantics=("arbitrary", "arbitrary"),
            vmem_limit_bytes=VMEM_LIMIT_BYTES),
        name="ada_mod",
    )(c_prompt, c_sample, w_ada, b_ada, *weights)


def _layer_pipeline(n_sub, project, heads, mix_out, x_mid, sh2, sc2, gate2, wup_refs, wdn_refs,
                    ln2_g, ln2_b, alpha, store):
    n_up, n_dn = len(wup_refs), len(wdn_refs)
    chunk = wup_refs[0].shape[1]

    def heads_and_project(j):
        heads(j)
        if j + PROJ_AHEAD < n_sub:
            project(j + PROJ_AHEAD)

    for jj in range(min(PROJ_AHEAD, n_sub)):
        project(jj)
    heads_and_project(0)
    mix_out(0)
    if n_sub > 1:
        heads_and_project(1)

    stages = [(j, k) for j in range(n_sub) for k in range(n_up)]
    hs, ups = {}, {}

    def issue_up(s):
        j, k = stages[s]
        if j not in hs:
            xm = x_mid[j]
            hs[j] = (xm * sc2 + sh2).reshape(-1, xm.shape[-1]).astype(BF16)
        ups[s] = _dot(hs[j], wup_refs[k][...])

    for s in range(min(UP_AHEAD, len(stages))):
        issue_up(s)
    acc = [None] * n_dn
    for s, (j, k) in enumerate(stages):
        if k == MIX_OUT_AT and j + 1 < n_sub:
            mix_out(j + 1)
        if k == HEADS_AT and j + 2 < n_sub:
            heads_and_project(j + 2)
        if s + UP_AHEAD < len(stages):
            issue_up(s + UP_AHEAD)
        a = jnp.square(jnp.maximum(ups.pop(s), 0.0)).astype(BF16)
        for n in range(n_dn):
            part = _dot(a, wdn_refs[n][k * chunk:(k + 1) * chunk, :])
            acc[n] = part if acc[n] is None else acc[n] + part
        if k == n_up - 1:
            xm = x_mid.pop(j)
            f = jnp.concatenate(acc, axis=1).reshape(xm.shape)
            store(j, _layer_norm(alpha * xm + gate2 * f, ln2_g, ln2_b))
            acc = [None] * n_dn


def _layer_prompt_kernel(x_ref, mod_ref, *refs, alpha, layer, n_in, n_out, n_up, n_dn, n_next):
    win_refs, refs = refs[:n_in], refs[n_in:]
    cw_ref, sg_ref, sb_ref, ws_ref, bs_ref, pw_ref, ps_ref = refs[:7]
    wout_refs, refs = refs[7:7 + n_out], refs[7 + n_out:]
    g_ref, b_ref = refs[:2]
    wup_refs, refs = refs[2:2 + n_up], refs[2 + n_up:]
    wdn_refs, refs = refs[:n_dn], refs[n_dn:]
    g2_ref, b2_ref = refs[:2]
    next_f32, refs = refs[2:2 + n_next], refs[2 + n_next:]
    o_ref, ncv_ref, npl_ref = refs[:3]
    next_bf16, (zc_ref, pc_ref) = refs[3:3 + n_next], refs[3 + n_next:]
    for src, dst in zip(next_f32, next_bf16):
        dst[...] = src[...].astype(BF16)
    bi, t = pl.program_id(0), pl.program_id(1)
    rows, c_dim = x_ref.shape[1], x_ref.shape[2]
    w_conv, w_gmlp, w_pool = cw_ref.shape[2], sg_ref.shape[1], ps_ref.shape[1]
    n_heads = ws_ref.shape[0]

    @pl.when(t == 0)
    def _():
        zc_ref[...] = jnp.zeros(zc_ref.shape, F32)
        pc_ref[...] = jnp.zeros(pc_ref.shape, F32)

    m = mod_ref[pl.ds(bi, 1), :]
    sh, sc1, gate = m[:, :c_dim], 1.0 + m[:, c_dim:2 * c_dim], m[:, 2 * c_dim:3 * c_dim]
    sh2, sc2, gate2 = (m[:, 3 * c_dim:4 * c_dim], 1.0 + m[:, 4 * c_dim:5 * c_dim],
                       m[:, 5 * c_dim:])
    cw = cw_ref[layer]
    sgu_g, sgu_b = sg_ref[layer:layer + 1, :], sb_ref[layer:layer + 1, :]
    pool_scale = ps_ref[layer:layer + 1, :]
    ln_g, ln_b = g_ref[layer:layer + 1, :], b_ref[layer:layer + 1, :]
    ln2_g, ln2_b = g2_ref[layer:layer + 1, :], b2_ref[layer:layer + 1, :]

    row_i = lax.broadcasted_iota(jnp.int32, (CHUNK, 2 * CHUNK), 0)
    col_j = lax.broadcasted_iota(jnp.int32, (CHUNK, 2 * CHUNK), 1) % CHUNK
    causal = col_j <= row_i
    ws_m = []
    for p in range(n_heads // 2):
        pair = jnp.concatenate([ws_ref[2 * p], ws_ref[2 * p + 1]], axis=1)
        ws_m.append(jnp.where(causal, pair, 0.0).astype(BF16))
    low_half = lax.broadcasted_iota(jnp.int32, (CHUNK, LANES), 1) < HEAD_DIM
    lane_group = _lane_group((SUB_ROWS, w_pool), len(POOL_WINDOWS))
    window = _select_by_group([jnp.full((SUB_ROWS, w_pool), w, jnp.int32) for w in POOL_WINDOWS],
                              lane_group)
    row_id = lax.broadcasted_iota(jnp.int32, (SUB_ROWS, w_pool), 0)
    bias = bs_ref[...]
    n_sub = rows // SUB_ROWS
    n_ch = SUB_ROWS // CHUNK
    hist = {"z": zc_ref[...], "p": pc_ref[...]}
    projs, mids, x_mid = {}, {}, {}

    def project(jj):
        h = (x_ref[0, jj * SUB_ROWS:(jj + 1) * SUB_ROWS, :] * sc1 + sh).astype(BF16)
        projs[jj] = jnp.concatenate([_dot(h, w[...]) for w in win_refs], axis=1)

    def heads(j):
        proj = projs.pop(j)
        o0 = 0
        a_b = proj[:, o0:o0 + w_conv]; o0 += w_conv
        a_c = proj[:, o0:o0 + w_conv]; o0 += w_conv
        a_x = proj[:, o0:o0 + w_conv]; o0 += w_conv
        g_u = proj[:, o0:o0 + w_gmlp]; o0 += w_gmlp
        g_v = proj[:, o0:o0 + w_gmlp]; o0 += w_gmlp
        p_x = proj[:, o0:o0 + w_pool]

        z = a_c * a_x
        z_ext = jnp.concatenate([hist["z"], z], axis=0)
        conv = z_ext * cw[CONV_WIDTH - 1:CONV_WIDTH, :]
        for k in range(1, CONV_WIDTH):
            conv = conv + pltpu.roll(z_ext, k, 0) * cw[CONV_WIDTH - 1 - k:CONV_WIDTH - k, :]
        y_a = a_b * conv[CONV_HIST:, :]
        hist["z"] = z[SUB_ROWS - CONV_HIST:, :]

        v_n = _layer_norm(g_v, sgu_g, sgu_b)
        s_cols = []
        for p in range(n_heads // 2):
            cols = [v_n[c * CHUNK:(c + 1) * CHUNK, p * LANES:(p + 1) * LANES] for c in range(n_ch)]
            rhs = jnp.concatenate(
                [jnp.concatenate([jnp.where(low_half, col, 0.0) for col in cols], axis=1),
                 jnp.concatenate([jnp.where(low_half, 0.0, col) for col in cols], axis=1)],
                axis=0).astype(BF16)
            s_cols.append(_dot(ws_m[p], rhs))

        run = jnp.concatenate([hist["p"], p_x], axis=0)
        sums = []
        for w in POOL_WINDOWS:
            run = run + pltpu.roll(run, w // 2, 0)
            sums.append(run[POOL_HIST:, :])
        count = jnp.minimum(window, t * rows + j * SUB_ROWS + row_id + 1).astype(F32)
        d = (_select_by_group(sums, lane_group) / count - p_x).astype(BF16)
        y_c_raw = _dot(d, pw_ref[...])
        hist["p"] = p_x[SUB_ROWS - POOL_HIST:, :]
        mids[j] = (y_a, g_u, s_cols, y_c_raw)

    def mix_out(j):
        y_a, g_u, s_cols, y_c_raw = mids.pop(j)
        s = jnp.concatenate(
            [jnp.concatenate([sc[:, c * LANES:(c + 1) * LANES] for sc in s_cols], axis=1) + bias
             for c in range(n_ch)], axis=0)
        mix = jnp.concatenate([y_a, g_u * s, y_c_raw * pool_scale], axis=1).astype(BF16)
        out = jnp.concatenate([_dot(mix, w[...]) for w in wout_refs], axis=1)
        x = x_ref[0, j * SUB_ROWS:(j + 1) * SUB_ROWS, :]
        x_mid[j] = _layer_norm(alpha * x + gate * out, ln_g, ln_b)

    def store(j, y):
        o_ref[0, j * SUB_ROWS:(j + 1) * SUB_ROWS, :] = y

    _layer_pipeline(n_sub, project, heads, mix_out, x_mid, sh2, sc2, gate2, wup_refs, wdn_refs,
                    ln2_g, ln2_b, alpha, store)

    zc_ref[...] = hist["z"]
    pc_ref[...] = hist["p"]
    ncv_ref[0] = hist["z"][CONV_HIST - (CONV_WIDTH - 1):, :]
    npl_ref[0] = hist["p"][POOL_HIST - (max(POOL_WINDOWS) - 1):, :]


def _layer_prompt_call(x, mod, w_in, conv_w, sgu_g, sgu_b, w_s, bs_full, pool_wbd, pool_scale,
                       w_out, ln_g, ln_b, wup, wdn, ln2_g, ln2_b, next_weights, *, alpha, layer,
                       rows):
    batch, seq, c_dim = x.shape
    n_steps = batch * (seq // rows)
    w_conv, w_pool = conv_w.shape[2], pool_scale.shape[1]
    conv_buf = CONV_WIDTH - 1
    pool_buf = max(POOL_WINDOWS) - 1
    in_specs = _column_blocks(w_in, WEIGHT_COLS)
    out_specs = _column_blocks(w_out, WEIGHT_COLS)
    up_specs = _column_blocks(wup, WEIGHT_COLS)
    dn_specs = _column_blocks(wdn, WEIGHT_COLS)
    kern = functools.partial(_layer_prompt_kernel, alpha=alpha, layer=layer, n_in=len(in_specs),
                             n_out=len(out_specs), n_up=len(up_specs), n_dn=len(dn_specs),
                             n_next=len(next_weights))
    n_t = seq // rows
    assert all(w.shape[1] % n_steps == 0 for w in next_weights)
    next_in = [pl.BlockSpec((None, w.shape[1] // n_steps, w.shape[2]),
                            lambda b, t: (layer + 1, b * n_t + t, 0)) for w in next_weights]
    next_out = [pl.BlockSpec((w.shape[1] // n_steps, w.shape[2]), lambda b, t: (b * n_t + t, 0))
                for w in next_weights]
    return pl.pallas_call(
        kern,
        grid=(batch, seq // rows),
        in_specs=[
            pl.BlockSpec((1, rows, c_dim), lambda b, t: (b, t, 0)),
            pl.BlockSpec((None, batch, mod.shape[2]), lambda b, t: (layer, 0, 0)),
            *in_specs,
            _whole(conv_w.shape),
            _whole(sgu_g.shape),
            _whole(sgu_b.shape),
            _layer_block(w_s.shape[1:], layer),
            _layer_block(bs_full.shape[1:], layer),
            _layer_block(pool_wbd.shape[1:], layer),
            _whole(pool_scale.shape),
            *out_specs,
            _whole(ln_g.shape),
            _whole(ln_b.shape),
            *up_specs, *dn_specs,
            _whole(ln2_g.shape),
            _whole(ln2_b.shape),
            *next_in,
        ],
        out_specs=[
            pl.BlockSpec((1, rows, c_dim), lambda b, t: (b, t, 0)),
            pl.BlockSpec((1, conv_buf, w_conv), lambda b, t: (b, 0, 0)),
            pl.BlockSpec((1, pool_buf, w_pool), lambda b, t: (b, 0, 0)),
            *next_out,
        ],
        out_shape=[
            jax.ShapeDtypeStruct((batch, seq, c_dim), F32),
            jax.ShapeDtypeStruct((batch, conv_buf, w_conv), F32),
            jax.ShapeDtypeStruct((batch, pool_buf, w_pool), F32),
            *[jax.ShapeDtypeStruct(w.shape[1:], BF16) for w in next_weights],
        ],
        scratch_shapes=[
            pltpu.VMEM((CONV_HIST, w_conv), F32),
            pltpu.VMEM((POOL_HIST, w_pool), F32),
        ],
        input_output_aliases={0: 0} if layer > 0 else {},
        compiler_params=pltpu.CompilerParams(
            dimension_semantics=("arbitrary", "arbitrary"),
            vmem_limit_bytes=VMEM_LIMIT_BYTES),
        name="layer_prompt",
    )(x, mod, *([w_in] * len(in_specs)), conv_w, sgu_g, sgu_b, w_s, bs_full, pool_wbd, pool_scale,
      *([w_out] * len(out_specs)), ln_g, ln_b, *([wup] * len(up_specs)),
      *([wdn] * len(dn_specs)), ln2_g, ln2_b, *next_weights)


def _layer_sample_kernel(x_ref, mod_ref, cst_ref, pst_ref, *refs, alpha, layer, n_in, n_out,
                         n_up, n_dn, x_batch_major, out_batch_major):
    win_refs, refs = refs[:n_in], refs[n_in:]
    cw_ref, sg_ref, sb_ref, wst_ref, bst_ref, pw_ref, ps_ref = refs[:7]
    wout_refs, refs = refs[7:7 + n_out], refs[7 + n_out:]
    g_ref, b_ref = refs[:2]
    wup_refs, refs = refs[2:2 + n_up], refs[2 + n_up:]
    wdn_refs, refs = refs[:n_dn], refs[n_dn:]
    g2_ref, b2_ref, o_ref, ncv_ref, npl_ref, v_ref = refs
    xt = jnp.swapaxes(x_ref[...], 0, 1) if x_batch_major else None
    cst = jnp.swapaxes(cst_ref[...], 0, 1)
    pst = jnp.swapaxes(pst_ref[...], 0, 1)
    t_dim, b_dim = v_ref.shape[1], v_ref.shape[0]
    c_dim = x_ref.shape[2]
    w_conv, w_gmlp, w_pool = cw_ref.shape[2], sg_ref.shape[1], ps_ref.shape[1]
    n_valid = pst.shape[0]
    step = SUB_ROWS // b_dim
    n_sub = t_dim // step

    m = mod_ref[...]
    sh, sc1, gate = m[:, :c_dim], 1.0 + m[:, c_dim:2 * c_dim], m[:, 2 * c_dim:3 * c_dim]
    sh2, sc2, gate2 = (m[:, 3 * c_dim:4 * c_dim], 1.0 + m[:, 4 * c_dim:5 * c_dim],
                       m[:, 5 * c_dim:])
    cw = cw_ref[layer]
    sgu_g, sgu_b = sg_ref[layer:layer + 1, :], sb_ref[layer:layer + 1, :]
    pool_scale = ps_ref[layer:layer + 1, :]
    ln_g, ln_b = g_ref[layer:layer + 1, :], b_ref[layer:layer + 1, :]
    ln2_g, ln2_b = g2_ref[layer:layer + 1, :], b2_ref[layer:layer + 1, :]
    lane_group = _lane_group((b_dim, w_pool), len(POOL_WINDOWS))

    z_ext = [cst[i] for i in range(CONV_WIDTH - 1)]
    p_ext = [pst[i] for i in range(n_valid)]
    v_all = []
    projs, mids, x_mid, outs = {}, {}, {}, {}

    def x_rows(j):
        if x_batch_major:
            return xt[j * step:(j + 1) * step]
        return x_ref[j * step:(j + 1) * step]

    def project(j):
        h = (x_rows(j) * sc1 + sh).reshape(step * b_dim, c_dim).astype(BF16)
        projs[j] = jnp.concatenate([_dot(h, w[...]) for w in win_refs], axis=1).reshape(
            step, b_dim, -1)

    def heads(j):
        proj = projs.pop(j)
        o0 = 0
        a_b = proj[:, :, o0:o0 + w_conv]; o0 += w_conv
        a_c = proj[:, :, o0:o0 + w_conv]; o0 += w_conv
        a_x = proj[:, :, o0:o0 + w_conv]; o0 += w_conv
        g_u = proj[:, :, o0:o0 + w_gmlp]; o0 += w_gmlp
        g_v = proj[:, :, o0:o0 + w_gmlp]; o0 += w_gmlp
        p_x = proj[:, :, o0:o0 + w_pool]
        z = a_c * a_x
        v_n = _layer_norm(g_v, sgu_g, sgu_b)
        y_a, y_b, d = [], [], []
        for i in range(step):
            pos = j * step + i
            z_ext.append(z[i])
            conv = z_ext[pos] * cw[0:1, :]
            for k in range(1, CONV_WIDTH):
                conv = conv + z_ext[pos + k] * cw[k:k + 1, :]
            y_a.append(a_b[i] * conv)
            v_all.append(v_n[i])
            s = bst_ref[pos:pos + 1, :] + wst_ref[pos, 0:1, :] * v_all[0]
            for q in range(1, pos + 1):
                s = s + wst_ref[pos, q:q + 1, :] * v_all[q]
            y_b.append(g_u[i] * s)
            p_ext.append(p_x[i])
            last = n_valid + pos
            means, run, k = [], None, 0
            for w in POOL_WINDOWS:
                while k < w:
                    run = p_ext[last - k] if run is None else run + p_ext[last - k]
                    k += 1
                means.append(run / float(min(w, pos + 1 + n_valid)))
            d.append(_select_by_group(means, lane_group) - p_x[i])
        d = jnp.stack(d).reshape(step * b_dim, w_pool).astype(BF16)
        mids[j] = (jnp.stack(y_a), jnp.stack(y_b), _dot(d, pw_ref[...]))

    def mix_out(j):
        y_a, y_b, y_c_raw = mids.pop(j)
        y_c = (y_c_raw * pool_scale).reshape(step, b_dim, w_pool)
        mix = jnp.concatenate([y_a, y_b, y_c], axis=2).reshape(step * b_dim, c_dim).astype(BF16)
        out = jnp.concatenate([_dot(mix, w[...]) for w in wout_refs], axis=1)
        x_mid[j] = _layer_norm(alpha * x_rows(j) + gate * out.reshape(step, b_dim, c_dim),
                               ln_g, ln_b)

    def store(j, y):
        if out_batch_major:
            outs[j] = y
        else:
            o_ref[j * step:(j + 1) * step] = y

    _layer_pipeline(n_sub, project, heads, mix_out, x_mid, sh2, sc2, gate2, wup_refs, wdn_refs,
                    ln2_g, ln2_b, alpha, store)

    if out_batch_major:
        o_ref[...] = jnp.swapaxes(jnp.concatenate([outs[j] for j in range(n_sub)], axis=0), 0, 1)
    ncv_ref[...] = jnp.swapaxes(jnp.stack(z_ext[len(z_ext) - (CONV_WIDTH - 1):]), 0, 1)
    npl_ref[...] = jnp.swapaxes(jnp.stack(p_ext[len(p_ext) - n_valid:]), 0, 1)
    v_ref[...] = jnp.swapaxes(jnp.stack(v_all), 0, 1)


def _layer_sample_call(x, mod, cst, pst, w_in, conv_w, sgu_g, sgu_b, wst, bs_full, pool_wbd,
                       pool_scale, w_out, ln_g, ln_b, wup, wdn, ln2_g, ln2_b, *, alpha, layer,
                       x_batch_major, out_batch_major):
    if x_batch_major:
        b_dim, t_dim, c_dim = x.shape
    else:
        t_dim, b_dim, c_dim = x.shape
    w_gmlp = sgu_g.shape[1]
    in_specs = _column_blocks(w_in, WEIGHT_COLS)
    out_specs = _column_blocks(w_out, WEIGHT_COLS)
    up_specs = _column_blocks(wup, WEIGHT_COLS)
    dn_specs = _column_blocks(wdn, WEIGHT_COLS)
    kern = functools.partial(_layer_sample_kernel, alpha=alpha, layer=layer, n_in=len(in_specs),
                             n_out=len(out_specs), n_up=len(up_specs), n_dn=len(dn_specs),
                             x_batch_major=x_batch_major, out_batch_major=out_batch_major)
    out_shape = (b_dim, t_dim, c_dim) if out_batch_major else (t_dim, b_dim, c_dim)
    whole3 = lambda i: (0, 0, 0)
    return pl.pallas_call(
        kern,
        grid=(1,),
        in_specs=[
            pl.BlockSpec(x.shape, whole3),
            pl.BlockSpec((None, b_dim, mod.shape[2]), lambda i: (layer, 0, 0)),
            _layer_block(cst.shape[1:], layer),
            _layer_block(pst.shape[1:], layer),
            *in_specs,
            _whole(conv_w.shape),
            _whole(sgu_g.shape),
            _whole(sgu_b.shape),
            _layer_block(wst.shape[1:], layer),
            pl.BlockSpec((None, t_dim, w_gmlp), lambda i: (layer, 0, 0)),
            _layer_block(pool_wbd.shape[1:], layer),
            _whole(pool_scale.shape),
            *out_specs,
            _whole(ln_g.shape),
            _whole(ln_b.shape),
            *up_specs, *dn_specs,
            _whole(ln2_g.shape),
            _whole(ln2_b.shape),
        ],
        out_specs=[
            pl.BlockSpec(out_shape, whole3),
            pl.BlockSpec(cst.shape[1:], whole3),
            pl.BlockSpec(pst.shape[1:], whole3),
            pl.BlockSpec((b_dim, t_dim, w_gmlp), whole3),
        ],
        out_shape=[
            jax.ShapeDtypeStruct(out_shape, F32),
            jax.ShapeDtypeStruct(cst.shape[1:], F32),
            jax.ShapeDtypeStruct(pst.shape[1:], F32),
            jax.ShapeDtypeStruct((b_dim, t_dim, w_gmlp), F32),
        ],
        compiler_params=pltpu.CompilerParams(
            dimension_semantics=("arbitrary",),
            vmem_limit_bytes=VMEM_LIMIT_BYTES),
        name="layer_sample",
    )(x, mod, cst, pst, *([w_in] * len(in_specs)), conv_w, sgu_g, sgu_b, wst, bs_full, pool_wbd,
      pool_scale, *([w_out] * len(out_specs)), ln_g, ln_b, *([wup] * len(up_specs)),
      *([wdn] * len(dn_specs)), ln2_g, ln2_b)


def kernel(x_prompt, x_sample, c_prompt, c_sample, state_conv, state_pool, w_ada, b_ada, w_in,
           conv_w, sgu_g, sgu_b, w_s, b_s, pool_w, pool_scale, w_out, ln1_g, ln1_b, w_up,
           w_down, ln2_g, ln2_b):
    depth = w_in.shape[0]
    seq = x_prompt.shape[1]
    dec_batch, dec_seq, _ = x_sample.shape
    n_groups, group_dim, _ = pool_w.shape[1:]
    w_pool = n_groups * group_dim
    alpha = (2.0 * depth) ** 0.25
    assert seq % PROMPT_ROWS == 0 and PROMPT_ROWS % SUB_ROWS == 0 and SUB_ROWS % CHUNK == 0
    assert (dec_seq * dec_batch) % SUB_ROWS == 0 and SUB_ROWS % dec_batch == 0

    weights = (w_in, w_out, w_up, w_down)
    mod_p, mod_s, *weights_b = _ada_call(c_prompt, c_sample, w_ada, b_ada, weights)

    bs_full = jnp.repeat(jnp.swapaxes(b_s, 1, 2), HEAD_DIM, axis=2)
    ws_small = jnp.repeat(jnp.transpose(w_s[:, :, :dec_seq, :dec_seq], (0, 2, 3, 1)), HEAD_DIM, axis=3)
    eye = jnp.eye(n_groups, dtype=F32)
    pool_wbd = (eye[None, :, None, :, None] * pool_w[:, :, :, None, :]).reshape(
        depth, w_pool, w_pool).astype(BF16)

    xp, xs = x_prompt, x_sample
    conv_p, pool_p, conv_s, pool_s, v_s = [], [], [], [], []
    for l in range(depth):
        w_in_b, w_out_b, wup_b, wdn_b = weights_b
        xp, ncv, npl, *weights_b = _layer_prompt_call(
            xp, mod_p, w_in_b, conv_w, sgu_g, sgu_b, w_s, bs_full, pool_wbd, pool_scale,
            w_out_b, ln1_g, ln1_b, wup_b, wdn_b, ln2_g, ln2_b,
            weights if l + 1 < depth else (), alpha=alpha, layer=l, rows=PROMPT_ROWS)
        conv_p.append(ncv)
        pool_p.append(npl)

        xp, xs = lax.optimization_barrier((xp, xs))
        xs, ncv, npl, v_rows = _layer_sample_call(
            xs, mod_s, state_conv, state_pool, w_in_b, conv_w, sgu_g, sgu_b, ws_small, bs_full,
            pool_wbd, pool_scale, w_out_b, ln1_g, ln1_b, wup_b, wdn_b, ln2_g, ln2_b, alpha=alpha,
            layer=l, x_batch_major=(l == 0), out_batch_major=(l == depth - 1))
        conv_s.append(ncv)
        pool_s.append(npl)
        v_s.append(v_rows)
        xp, xs = lax.optimization_barrier((xp, xs))

    return (xp, xs, jnp.stack(conv_p), jnp.stack(pool_p), jnp.stack(conv_s), jnp.stack(pool_s),
            jnp.stack(v_s))
```

```python
import functools

import jax
import jax.numpy as jnp
from jax import lax
from jax.experimental import pallas as pl
from jax.experimental.pallas import tpu as pltpu

F32 = jnp.float32
BF16 = jnp.bfloat16

LN_EPS = 1e-5
HEAD_DIM = 64
CHUNK = 128
POOL_WINDOWS = (2, 4, 8, 16)
CONV_WIDTH = 3
LANES = 128
SUBLANES = 8
VMEM_LIMIT_BYTES = 60 * 1024 * 1024

PROMPT_ROWS = 1024
SUB_ROWS = 256
UP_AHEAD = 2
PROJ_AHEAD = 2
MIX_OUT_AT = 2
HEADS_AT = 5
WEIGHT_COLS = 512
CONV_HIST = SUBLANES
POOL_HIST = 2 * SUBLANES


def _layer_norm(y, g, b):
    mu = jnp.mean(y, axis=-1, keepdims=True)
    d = y - mu
    var = jnp.mean(d * d, axis=-1, keepdims=True)
    return d * lax.rsqrt(var + LN_EPS) * g + b


def _dot(a, b):
    return jnp.dot(a, b, preferred_element_type=F32)


def _layer_block(shape, layer):
    zeros = (0,) * len(shape)
    return pl.BlockSpec((None,) + tuple(shape), lambda *_: (layer,) + zeros,
                        pipeline_mode=pl.Buffered(1))


def _whole(shape):
    zeros = (0,) * len(shape)
    return pl.BlockSpec(tuple(shape), lambda *_: zeros, pipeline_mode=pl.Buffered(1))


def _column_blocks(w, width):
    rows, cols = w.shape
    return [pl.BlockSpec((rows, width), lambda *_, n=n: (0, n), pipeline_mode=pl.Buffered(1))
            for n in range(cols // width)]


def _lane_group(shape, n_groups):
    return lax.broadcasted_iota(jnp.int32, shape, len(shape) - 1) // (shape[-1] // n_groups)


def _select_by_group(values, lane_group):
    out = values[-1]
    for g in range(len(values) - 2, -1, -1):
        out = jnp.where(lane_group == g, values[g], out)
    return out


def _ada_kernel(cp_ref, cs_ref, w_ref, b_ref, *refs):
    n_side = (len(refs) - 2) // 2
    side_f32, (op_ref, os_ref), side_bf16 = refs[:n_side], refs[n_side:n_side + 2], refs[n_side + 2:]
    for src, dst in zip(side_f32, side_bf16):
        dst[...] = src[0].astype(BF16)
    l, k = pl.program_id(0), pl.program_id(1)
    bp = cp_ref.shape[0]
    c = jnp.concatenate([cp_ref[...], cs_ref[...]], axis=0)
    part = _dot(jax.nn.silu(c).astype(BF16), w_ref[0].astype(BF16))
    part_p, part_s = part[:bp], part[bp:]

    @pl.when(k == 0)
    def _():
        bias = b_ref[pl.ds(l, 1), :]
        op_ref[0] = part_p + bias
        os_ref[0] = part_s + bias

    @pl.when(k > 0)
    def _():
        op_ref[0] += part_p
        os_ref[0] += part_s


def _ada_call(c_prompt, c_sample, w_ada, b_ada, weights, row_tile=256):
    depth, d_model, n_mod = w_ada.shape
    bp, bs = c_prompt.shape[0], c_sample.shape[0]
    n_k = d_model // row_tile
    n_steps = depth * n_k
    assert all(w.shape[1] % n_steps == 0 for w in weights)
    step = lambda l, k: l * n_k + k
    return pl.pallas_call(
        _ada_kernel,
        grid=(depth, n_k),
        in_specs=[
            pl.BlockSpec((bp, row_tile), lambda l, k: (0, k)),
            pl.BlockSpec((bs, row_tile), lambda l, k: (0, k)),
            pl.BlockSpec((1, row_tile, n_mod), lambda l, k: (l, k, 0)),
            _whole(b_ada.shape),
            *[pl.BlockSpec((1, w.shape[1] // n_steps, w.shape[2]), lambda l, k: (0, step(l, k), 0))
              for w in weights],
        ],
        out_specs=[
            pl.BlockSpec((1, bp, n_mod), lambda l, k: (l, 0, 0)),
            pl.BlockSpec((1, bs, n_mod), lambda l, k: (l, 0, 0)),
            *[pl.BlockSpec((w.shape[1] // n_steps, w.shape[2]), lambda l, k: (step(l, k), 0))
              for w in weights],
        ],
        out_shape=[
            jax.ShapeDtypeStruct((depth, bp, n_mod), F32),
            jax.ShapeDtypeStruct((depth, bs, n_mod), F32),
            *[jax.ShapeDtypeStruct(w.shape[1:], BF16) for w in weights],
        ],
        compiler_params=pltpu.CompilerParams(
            dimension_semantics=("arbitrary", "arbitrary"),
            vmem_limit_bytes=VMEM_LIMIT_BYTES),
        name="ada_mod",
    )(c_prompt, c_sample, w_ada, b_ada, *weights)


def _layer_pipeline(n_sub, project, heads, mix_out, x_mid, sh2, sc2, gate2, wup_refs, wdn_refs,
                    ln2_g, ln2_b, alpha, store):
    n_up, n_dn = len(wup_refs), len(wdn_refs)
    chunk = wup_refs[0].shape[1]

    def heads_and_project(j):
        heads(j)
        if j + PROJ_AHEAD < n_sub:
            project(j + PROJ_AHEAD)

    for jj in range(min(PROJ_AHEAD, n_sub)):
        project(jj)
    heads_and_project(0)
    mix_out(0)
    if n_sub > 1:
        heads_and_project(1)

    stages = [(j, k) for j in range(n_sub) for k in range(n_up)]
    hs, ups = {}, {}

    def issue_up(s):
        j, k = stages[s]
        if j not in hs:
            xm = x_mid[j]
            hs[j] = (xm * sc2 + sh2).reshape(-1, xm.shape[-1]).astype(BF16)
        ups[s] = _dot(hs[j], wup_refs[k][...])

    for s in range(min(UP_AHEAD, len(stages))):
        issue_up(s)
    acc = [None] * n_dn
    for s, (j, k) in enumerate(stages):
        if k == MIX_OUT_AT and j + 1 < n_sub:
            mix_out(j + 1)
        if k == HEADS_AT and j + 2 < n_sub:
            heads_and_project(j + 2)
        if s + UP_AHEAD < len(stages):
            issue_up(s + UP_AHEAD)
        a = jnp.square(jnp.maximum(ups.pop(s), 0.0)).astype(BF16)
        for n in range(n_dn):
            part = _dot(a, wdn_refs[n][k * chunk:(k + 1) * chunk, :])
            acc[n] = part if acc[n] is None else acc[n] + part
        if k == n_up - 1:
            xm = x_mid.pop(j)
            f = jnp.concatenate(acc, axis=1).reshape(xm.shape)
            store(j, _layer_norm(alpha * xm + gate2 * f, ln2_g, ln2_b))
            acc = [None] * n_dn


def _layer_prompt_kernel(x_ref, mod_ref, *refs, alpha, layer, n_in, n_out, n_up, n_dn, n_next):
    win_refs, refs = refs[:n_in], refs[n_in:]
    cw_ref, sg_ref, sb_ref, ws_ref, bs_ref, pw_ref, ps_ref = refs[:7]
    wout_refs, refs = refs[7:7 + n_out], refs[7 + n_out:]
    g_ref, b_ref = refs[:2]
    wup_refs, refs = refs[2:2 + n_up], refs[2 + n_up:]
    wdn_refs, refs = refs[:n_dn], refs[n_dn:]
    g2_ref, b2_ref = refs[:2]
    next_f32, refs = refs[2:2 + n_next], refs[2 + n_next:]
    o_ref, ncv_ref, npl_ref = refs[:3]
    next_bf16, (zc_ref, pc_ref) = refs[3:3 + n_next], refs[3 + n_next:]
    for src, dst in zip(next_f32, next_bf16):
        dst[...] = src[...].astype(BF16)
    bi, t = pl.program_id(0), pl.program_id(1)
    rows, c_dim = x_ref.shape[1], x_ref.shape[2]
    w_conv, w_gmlp, w_pool = cw_ref.shape[2], sg_ref.shape[1], ps_ref.shape[1]
    n_heads = ws_ref.shape[0]

    @pl.when(t == 0)
    def _():
        zc_ref[...] = jnp.zeros(zc_ref.shape, F32)
        pc_ref[...] = jnp.zeros(pc_ref.shape, F32)

    m = mod_ref[pl.ds(bi, 1), :]
    sh, sc1, gate = m[:, :c_dim], 1.0 + m[:, c_dim:2 * c_dim], m[:, 2 * c_dim:3 * c_dim]
    sh2, sc2, gate2 = (m[:, 3 * c_dim:4 * c_dim], 1.0 + m[:, 4 * c_dim:5 * c_dim],
                       m[:, 5 * c_dim:])
    cw = cw_ref[layer]
    sgu_g, sgu_b = sg_ref[layer:layer + 1, :], sb_ref[layer:layer + 1, :]
    pool_scale = ps_ref[layer:layer + 1, :]
    ln_g, ln_b = g_ref[layer:layer + 1, :], b_ref[layer:layer + 1, :]
    ln2_g, ln2_b = g2_ref[layer:layer + 1, :], b2_ref[layer:layer + 1, :]

    row_i = lax.broadcasted_iota(jnp.int32, (CHUNK, 2 * CHUNK), 0)
    col_j = lax.broadcasted_iota(jnp.int32, (CHUNK, 2 * CHUNK), 1) % CHUNK
    causal = col_j <= row_i
    ws_m = []
    for p in range(n_heads // 2):
        pair = jnp.concatenate([ws_ref[2 * p], ws_ref[2 * p + 1]], axis=1)
        ws_m.append(jnp.where(causal, pair, 0.0).astype(BF16))
    low_half = lax.broadcasted_iota(jnp.int32, (CHUNK, LANES), 1) < HEAD_DIM
    lane_group = _lane_group((SUB_ROWS, w_pool), len(POOL_WINDOWS))
    window = _select_by_group([jnp.full((SUB_ROWS, w_pool), w, jnp.int32) for w in POOL_WINDOWS],
                              lane_group)
    row_id = lax.broadcasted_iota(jnp.int32, (SUB_ROWS, w_pool), 0)
    bias = bs_ref[...]
    n_sub = rows // SUB_ROWS
    n_ch = SUB_ROWS // CHUNK
    hist = {"z": zc_ref[...], "p": pc_ref[...]}
    projs, mids, x_mid = {}, {}, {}

    def project(jj):
        h = (x_ref[0, jj * SUB_ROWS:(jj + 1) * SUB_ROWS, :] * sc1 + sh).astype(BF16)
        projs[jj] = jnp.concatenate([_dot(h, w[...]) for w in win_refs], axis=1)

    def heads(j):
        proj = projs.pop(j)
        o0 = 0
        a_b = proj[:, o0:o0 + w_conv]; o0 += w_conv
        a_c = proj[:, o0:o0 + w_conv]; o0 += w_conv
        a_x = proj[:, o0:o0 + w_conv]; o0 += w_conv
        g_u = proj[:, o0:o0 + w_gmlp]; o0 += w_gmlp
        g_v = proj[:, o0:o0 + w_gmlp]; o0 += w_gmlp
        p_x = proj[:, o0:o0 + w_pool]

        z = a_c * a_x
        z_ext = jnp.concatenate([hist["z"], z], axis=0)
        conv = z_ext * cw[CONV_WIDTH - 1:CONV_WIDTH, :]
        for k in range(1, CONV_WIDTH):
            conv = conv + pltpu.roll(z_ext, k, 0) * cw[CONV_WIDTH - 1 - k:CONV_WIDTH - k, :]
        y_a = a_b * conv[CONV_HIST:, :]
        hist["z"] = z[SUB_ROWS - CONV_HIST:, :]

        v_n = _layer_norm(g_v, sgu_g, sgu_b)
        s_cols = []
        for p in range(n_heads // 2):
            cols = [v_n[c * CHUNK:(c + 1) * CHUNK, p * LANES:(p + 1) * LANES] for c in range(n_ch)]
            rhs = jnp.concatenate(
                [jnp.concatenate([jnp.where(low_half, col, 0.0) for col in cols], axis=1),
                 jnp.concatenate([jnp.where(low_half, 0.0, col) for col in cols], axis=1)],
                axis=0).astype(BF16)
            s_cols.append(_dot(ws_m[p], rhs))

        run = jnp.concatenate([hist["p"], p_x], axis=0)
        sums = []
        for w in POOL_WINDOWS:
            run = run + pltpu.roll(run, w // 2, 0)
            sums.append(run[POOL_HIST:, :])
        count = jnp.minimum(window, t * rows + j * SUB_ROWS + row_id + 1).astype(F32)
        d = (_select_by_group(sums, lane_group) / count - p_x).astype(BF16)
        y_c_raw = _dot(d, pw_ref[...])
        hist["p"] = p_x[SUB_ROWS - POOL_HIST:, :]
        mids[j] = (y_a, g_u, s_cols, y_c_raw)

    def mix_out(j):
        y_a, g_u, s_cols, y_c_raw = mids.pop(j)
        s = jnp.concatenate(
            [jnp.concatenate([sc[:, c * LANES:(c + 1) * LANES] for sc in s_cols], axis=1) + bias
             for c in range(n_ch)], axis=0)
        mix = jnp.concatenate([y_a, g_u * s, y_c_raw * pool_scale], axis=1).astype(BF16)
        out = jnp.concatenate([_dot(mix, w[...]) for w in wout_refs], axis=1)
        x = x_ref[0, j * SUB_ROWS:(j + 1) * SUB_ROWS, :]
        x_mid[j] = _layer_norm(alpha * x + gate * out, ln_g, ln_b)

    def store(j, y):
        o_ref[0, j * SUB_ROWS:(j + 1) * SUB_ROWS, :] = y

    _layer_pipeline(n_sub, project, heads, mix_out, x_mid, sh2, sc2, gate2, wup_refs, wdn_refs,
                    ln2_g, ln2_b, alpha, store)

    zc_ref[...] = hist["z"]
    pc_ref[...] = hist["p"]
    ncv_ref[0] = hist["z"][CONV_HIST - (CONV_WIDTH - 1):, :]
    npl_ref[0] = hist["p"][POOL_HIST - (max(POOL_WINDOWS) - 1):, :]


def _layer_prompt_call(x, mod, w_in, conv_w, sgu_g, sgu_b, w_s, bs_full, pool_wbd, pool_scale,
                       w_out, ln_g, ln_b, wup, wdn, ln2_g, ln2_b, next_weights, *, alpha, layer,
                       rows):
    batch, seq, c_dim = x.shape
    n_steps = batch * (seq // rows)
    w_conv, w_pool = conv_w.shape[2], pool_scale.shape[1]
    conv_buf = CONV_WIDTH - 1
    pool_buf = max(POOL_WINDOWS) - 1
    in_specs = _column_blocks(w_in, WEIGHT_COLS)
    out_specs = _column_blocks(w_out, WEIGHT_COLS)
    up_specs = _column_blocks(wup, WEIGHT_COLS)
    dn_specs = _column_blocks(wdn, WEIGHT_COLS)
    kern = functools.partial(_layer_prompt_kernel, alpha=alpha, layer=layer, n_in=len(in_specs),
                             n_out=len(out_specs), n_up=len(up_specs), n_dn=len(dn_specs),
                             n_next=len(next_weights))
    n_t = seq // rows
    assert all(w.shape[1] % n_steps == 0 for w in next_weights)
    next_in = [pl.BlockSpec((None, w.shape[1] // n_steps, w.shape[2]),
                            lambda b, t: (layer + 1, b * n_t + t, 0)) for w in next_weights]
    next_out = [pl.BlockSpec((w.shape[1] // n_steps, w.shape[2]), lambda b, t: (b * n_t + t, 0))
                for w in next_weights]
    return pl.pallas_call(
        kern,
        grid=(batch, seq // rows),
        in_specs=[
            pl.BlockSpec((1, rows, c_dim), lambda b, t: (b, t, 0)),
            pl.BlockSpec((None, batch, mod.shape[2]), lambda b, t: (layer, 0, 0)),
            *in_specs,
            _whole(conv_w.shape),
            _whole(sgu_g.shape),
            _whole(sgu_b.shape),
            _layer_block(w_s.shape[1:], layer),
            _layer_block(bs_full.shape[1:], layer),
            _layer_block(pool_wbd.shape[1:], layer),
            _whole(pool_scale.shape),
            *out_specs,
            _whole(ln_g.shape),
            _whole(ln_b.shape),
            *up_specs, *dn_specs,
            _whole(ln2_g.shape),
            _whole(ln2_b.shape),
            *next_in,
        ],
        out_specs=[
            pl.BlockSpec((1, rows, c_dim), lambda b, t: (b, t, 0)),
            pl.BlockSpec((1, conv_buf, w_conv), lambda b, t: (b, 0, 0)),
            pl.BlockSpec((1, pool_buf, w_pool), lambda b, t: (b, 0, 0)),
            *next_out,
        ],
        out_shape=[
            jax.ShapeDtypeStruct((batch, seq, c_dim), F32),
            jax.ShapeDtypeStruct((batch, conv_buf, w_conv), F32),
            jax.ShapeDtypeStruct((batch, pool_buf, w_pool), F32),
            *[jax.ShapeDtypeStruct(w.shape[1:], BF16) for w in next_weights],
        ],
        scratch_shapes=[
            pltpu.VMEM((CONV_HIST, w_conv), F32),
            pltpu.VMEM((POOL_HIST, w_pool), F32),
        ],
        input_output_aliases={0: 0} if layer > 0 else {},
        compiler_params=pltpu.CompilerParams(
            dimension_semantics=("arbitrary", "arbitrary"),
            vmem_limit_bytes=VMEM_LIMIT_BYTES),
        name="layer_prompt",
    )(x, mod, *([w_in] * len(in_specs)), conv_w, sgu_g, sgu_b, w_s, bs_full, pool_wbd, pool_scale,
      *([w_out] * len(out_specs)), ln_g, ln_b, *([wup] * len(up_specs)),
      *([wdn] * len(dn_specs)), ln2_g, ln2_b, *next_weights)


def _layer_sample_kernel(x_ref, mod_ref, cst_ref, pst_ref, *refs, alpha, layer, n_in, n_out,
                         n_up, n_dn, x_batch_major, out_batch_major):
    win_refs, refs = refs[:n_in], refs[n_in:]
    cw_ref, sg_ref, sb_ref, wst_ref, bst_ref, pw_ref, ps_ref = refs[:7]
    wout_refs, refs = refs[7:7 + n_out], refs[7 + n_out:]
    g_ref, b_ref = refs[:2]
    wup_refs, refs = refs[2:2 + n_up], refs[2 + n_up:]
    wdn_refs, refs = refs[:n_dn], refs[n_dn:]
    g2_ref, b2_ref = refs[:2]
    o_ref, ncv_ref, npl_ref, v_ref = refs[len(refs) - 4:]
    xt = jnp.swapaxes(x_ref[...], 0, 1) if x_batch_major else None
    cst = jnp.swapaxes(cst_ref[...], 0, 1)
    pst = jnp.swapaxes(pst_ref[...], 0, 1)
    t_dim, b_dim = v_ref.shape[1], v_ref.shape[0]
    c_dim = x_ref.shape[2]
    w_conv, w_gmlp, w_pool = cw_ref.shape[2], sg_ref.shape[1], ps_ref.shape[1]
    n_valid = pst.shape[0]
    step = SUB_ROWS // b_dim
    n_sub = t_dim // step

    m = mod_ref[...]
    sh, sc1, gate = m[:, :c_dim], 1.0 + m[:, c_dim:2 * c_dim], m[:, 2 * c_dim:3 * c_dim]
    sh2, sc2, gate2 = (m[:, 3 * c_dim:4 * c_dim], 1.0 + m[:, 4 * c_dim:5 * c_dim],
                       m[:, 5 * c_dim:])
    cw = cw_ref[layer]
    sgu_g, sgu_b = sg_ref[layer:layer + 1, :], sb_ref[layer:layer + 1, :]
    pool_scale = ps_ref[layer:layer + 1, :]
    ln_g, ln_b = g_ref[layer:layer + 1, :], b_ref[layer:layer + 1, :]
    ln2_g, ln2_b = g2_ref[layer:layer + 1, :], b2_ref[layer:layer + 1, :]
    lane_group = _lane_group((b_dim, w_pool), len(POOL_WINDOWS))

    z_ext = [cst[i] for i in range(CONV_WIDTH - 1)]
    p_ext = [pst[i] for i in range(n_valid)]
    v_all = []
    projs, mids, x_mid, outs = {}, {}, {}, {}

    def x_rows(j):
        if x_batch_major:
            return xt[j * step:(j + 1) * step]
        return x_ref[j * step:(j + 1) * step]

    def project(j):
        h = (x_rows(j) * sc1 + sh).reshape(step * b_dim, c_dim).astype(BF16)
        projs[j] = jnp.concatenate([_dot(h, w[...]) for w in win_refs], axis=1).reshape(
            step, b_dim, -1)

    def heads(j):
        proj = projs.pop(j)
        o0 = 0
        a_b = proj[:, :, o0:o0 + w_conv]; o0 += w_conv
        a_c = proj[:, :, o0:o0 + w_conv]; o0 += w_conv
        a_x = proj[:, :, o0:o0 + w_conv]; o0 += w_conv
        g_u = proj[:, :, o0:o0 + w_gmlp]; o0 += w_gmlp
        g_v = proj[:, :, o0:o0 + w_gmlp]; o0 += w_gmlp
        p_x = proj[:, :, o0:o0 + w_pool]
        z = a_c * a_x
        v_n = _layer_norm(g_v, sgu_g, sgu_b)
        y_a, y_b, d = [], [], []
        for i in range(step):
            pos = j * step + i
            z_ext.append(z[i])
            conv = z_ext[pos] * cw[0:1, :]
            for k in range(1, CONV_WIDTH):
                conv = conv + z_ext[pos + k] * cw[k:k + 1, :]
            y_a.append(a_b[i] * conv)
            v_all.append(v_n[i])
            s = bst_ref[pos:pos + 1, :] + wst_ref[pos, 0:1, :] * v_all[0]
            for q in range(1, pos + 1):
                s = s + wst_ref[pos, q:q + 1, :] * v_all[q]
            y_b.append(g_u[i] * s)
            p_ext.append(p_x[i])
            last = n_valid + pos
            means, run, k = [], None, 0
            for w in POOL_WINDOWS:
                while k < w:
                    run = p_ext[last - k] if run is None else run + p_ext[last - k]
                    k += 1
                means.append(run / float(min(w, pos + 1 + n_valid)))
            d.append(_select_by_group(means, lane_group) - p_x[i])
        d = jnp.stack(d).reshape(step * b_dim, w_pool).astype(BF16)
        mids[j] = (jnp.stack(y_a), jnp.stack(y_b), _dot(d, pw_ref[...]))

    def mix_out(j):
        y_a, y_b, y_c_raw = mids.pop(j)
        y_c = (y_c_raw * pool_scale).reshape(step, b_dim, w_pool)
        mix = jnp.concatenate([y_a, y_b, y_c], axis=2).reshape(step * b_dim, c_dim).astype(BF16)
        out = jnp.concatenate([_dot(mix, w[...]) for w in wout_refs], axis=1)
        x_mid[j] = _layer_norm(alpha * x_rows(j) + gate * out.reshape(step, b_dim, c_dim),
                               ln_g, ln_b)

    def store(j, y):
        if out_batch_major:
            outs[j] = y
        else:
            o_ref[j * step:(j + 1) * step] = y

    _layer_pipeline(n_sub, project, heads, mix_out, x_mid, sh2, sc2, gate2, wup_refs, wdn_refs,
                    ln2_g, ln2_b, alpha, store)

    if out_batch_major:
        o_ref[...] = jnp.swapaxes(jnp.concatenate([outs[j] for j in range(n_sub)], axis=0), 0, 1)
    ncv_ref[...] = jnp.swapaxes(jnp.stack(z_ext[len(z_ext) - (CONV_WIDTH - 1):]), 0, 1)
    npl_ref[...] = jnp.swapaxes(jnp.stack(p_ext[len(p_ext) - n_valid:]), 0, 1)
    v_ref[...] = jnp.swapaxes(jnp.stack(v_all), 0, 1)


def _layer_sample_call(x, mod, cst, pst, w_in, conv_w, sgu_g, sgu_b, wst, bs_full, pool_wbd,
                       pool_scale, w_out, ln_g, ln_b, wup, wdn, ln2_g, ln2_b, states, *, alpha,
                       layer, x_batch_major, out_batch_major):
    if x_batch_major:
        b_dim, t_dim, c_dim = x.shape
    else:
        t_dim, b_dim, c_dim = x.shape
    w_gmlp = sgu_g.shape[1]
    in_specs = _column_blocks(w_in, WEIGHT_COLS)
    out_specs = _column_blocks(w_out, WEIGHT_COLS)
    up_specs = _column_blocks(wup, WEIGHT_COLS)
    dn_specs = _column_blocks(wdn, WEIGHT_COLS)
    kern = functools.partial(_layer_sample_kernel, alpha=alpha, layer=layer, n_in=len(in_specs),
                             n_out=len(out_specs), n_up=len(up_specs), n_dn=len(dn_specs),
                             x_batch_major=x_batch_major, out_batch_major=out_batch_major)
    out_shape = (b_dim, t_dim, c_dim) if out_batch_major else (t_dim, b_dim, c_dim)
    state_shapes = [cst.shape, pst.shape, (cst.shape[0], b_dim, t_dim, w_gmlp)]
    whole3 = lambda i: (0, 0, 0)
    inputs = (x, mod, cst, pst, *([w_in] * len(in_specs)), conv_w, sgu_g, sgu_b, wst, bs_full,
              pool_wbd, pool_scale, *([w_out] * len(out_specs)), ln_g, ln_b,
              *([wup] * len(up_specs)), *([wdn] * len(dn_specs)), ln2_g, ln2_b)
    return pl.pallas_call(
        kern,
        grid=(1,),
        in_specs=[
            pl.BlockSpec(x.shape, whole3),
            pl.BlockSpec((None, b_dim, mod.shape[2]), lambda i: (layer, 0, 0)),
            _layer_block(cst.shape[1:], layer),
            _layer_block(pst.shape[1:], layer),
            *in_specs,
            _whole(conv_w.shape),
            _whole(sgu_g.shape),
            _whole(sgu_b.shape),
            _layer_block(wst.shape[1:], layer),
            pl.BlockSpec((None, t_dim, w_gmlp), lambda i: (layer, 0, 0)),
            _layer_block(pool_wbd.shape[1:], layer),
            _whole(pool_scale.shape),
            *out_specs,
            _whole(ln_g.shape),
            _whole(ln_b.shape),
            *up_specs, *dn_specs,
            _whole(ln2_g.shape),
            _whole(ln2_b.shape),
            *[pl.BlockSpec(memory_space=pl.ANY) for _ in states],
        ],
        out_specs=[pl.BlockSpec(out_shape, whole3)] + [
            pl.BlockSpec((None,) + shape[1:], lambda i: (layer, 0, 0, 0)) for shape in state_shapes],
        out_shape=[jax.ShapeDtypeStruct(out_shape, F32)] + [
            jax.ShapeDtypeStruct(shape, F32) for shape in state_shapes],
        input_output_aliases={len(inputs) + k: 1 + k for k in range(len(states))},
        compiler_params=pltpu.CompilerParams(
            dimension_semantics=("arbitrary",),
            vmem_limit_bytes=VMEM_LIMIT_BYTES),
        name="layer_sample",
    )(*inputs, *states)


def kernel(x_prompt, x_sample, c_prompt, c_sample, state_conv, state_pool, w_ada, b_ada, w_in,
           conv_w, sgu_g, sgu_b, w_s, b_s, pool_w, pool_scale, w_out, ln1_g, ln1_b, w_up,
           w_down, ln2_g, ln2_b):
    depth = w_in.shape[0]
    seq = x_prompt.shape[1]
    dec_batch, dec_seq, _ = x_sample.shape
    n_groups, group_dim, _ = pool_w.shape[1:]
    w_pool = n_groups * group_dim
    alpha = (2.0 * depth) ** 0.25
    assert seq % PROMPT_ROWS == 0 and PROMPT_ROWS % SUB_ROWS == 0 and SUB_ROWS % CHUNK == 0
    assert (dec_seq * dec_batch) % SUB_ROWS == 0 and SUB_ROWS % dec_batch == 0

    weights = (w_in, w_out, w_up, w_down)
    mod_p, mod_s, *weights_b = _ada_call(c_prompt, c_sample, w_ada, b_ada, weights)

    bs_full = jnp.repeat(jnp.swapaxes(b_s, 1, 2), HEAD_DIM, axis=2)
    ws_small = jnp.repeat(jnp.transpose(w_s[:, :, :dec_seq, :dec_seq], (0, 2, 3, 1)), HEAD_DIM, axis=3)
    eye = jnp.eye(n_groups, dtype=F32)
    pool_wbd = (eye[None, :, None, :, None] * pool_w[:, :, :, None, :]).reshape(
        depth, w_pool, w_pool).astype(BF16)

    xp, xs = x_prompt, x_sample
    conv_p, pool_p, states_s = [], [], ()
    for l in range(depth):
        w_in_b, w_out_b, wup_b, wdn_b = weights_b
        xp, ncv, npl, *weights_b = _layer_prompt_call(
            xp, mod_p, w_in_b, conv_w, sgu_g, sgu_b, w_s, bs_full, pool_wbd, pool_scale,
            w_out_b, ln1_g, ln1_b, wup_b, wdn_b, ln2_g, ln2_b,
            weights if l + 1 < depth else (), alpha=alpha, layer=l, rows=PROMPT_ROWS)
        conv_p.append(ncv)
        pool_p.append(npl)

        xp, xs = lax.optimization_barrier((xp, xs))
        xs, *states_s = _layer_sample_call(
            xs, mod_s, state_conv, state_pool, w_in_b, conv_w, sgu_g, sgu_b, ws_small, bs_full,
            pool_wbd, pool_scale, w_out_b, ln1_g, ln1_b, wup_b, wdn_b, ln2_g, ln2_b, states_s,
            alpha=alpha, layer=l, x_batch_major=(l == 0), out_batch_major=(l == depth - 1))
        xp, xs = lax.optimization_barrier((xp, xs))

    return (xp, xs, jnp.stack(conv_p), jnp.stack(pool_p), *states_s)
```

```python
import functools

import jax
import jax.numpy as jnp
from jax import lax
from jax.experimental import pallas as pl
from jax.experimental.pallas import tpu as pltpu

F32 = jnp.float32
BF16 = jnp.bfloat16

LN_EPS = 1e-5
HEAD_DIM = 64
CHUNK = 128
POOL_WINDOWS = (2, 4, 8, 16)
CONV_WIDTH = 3
LANES = 128
SUBLANES = 8
VMEM_LIMIT_BYTES = 60 * 1024 * 1024

PROMPT_ROWS = 1024
SUB_ROWS = 256
UP_AHEAD = 2
PROJ_AHEAD = 2
MIX_OUT_AT = 2
HEADS_AT = 5
WEIGHT_COLS = 512
CONV_HIST = SUBLANES
POOL_HIST = 2 * SUBLANES


def _layer_norm(y, g, b):
    mu = jnp.mean(y, axis=-1, keepdims=True)
    d = y - mu
    var = jnp.mean(d * d, axis=-1, keepdims=True)
    return d * lax.rsqrt(var + LN_EPS) * g + b


def _dot(a, b):
    return jnp.dot(a, b, preferred_element_type=F32)


def _layer_block(shape, layer):
    zeros = (0,) * len(shape)
    return pl.BlockSpec((None,) + tuple(shape), lambda *_: (layer,) + zeros,
                        pipeline_mode=pl.Buffered(1))


def _whole(shape):
    zeros = (0,) * len(shape)
    return pl.BlockSpec(tuple(shape), lambda *_: zeros, pipeline_mode=pl.Buffered(1))


def _column_blocks(w, width):
    rows, cols = w.shape
    return [pl.BlockSpec((rows, width), lambda *_, n=n: (0, n), pipeline_mode=pl.Buffered(1))
            for n in range(cols // width)]


def _lane_group(shape, n_groups):
    return lax.broadcasted_iota(jnp.int32, shape, len(shape) - 1) // (shape[-1] // n_groups)


def _select_by_group(values, lane_group):
    out = values[-1]
    for g in range(len(values) - 2, -1, -1):
        out = jnp.where(lane_group == g, values[g], out)
    return out


def _ada_kernel(cp_ref, cs_ref, w_ref, b_ref, *refs):
    n_side = (len(refs) - 2) // 2
    side_f32, (op_ref, os_ref), side_bf16 = refs[:n_side], refs[n_side:n_side + 2], refs[n_side + 2:]
    for src, dst in zip(side_f32, side_bf16):
        dst[...] = src[0].astype(BF16)
    l, k = pl.program_id(0), pl.program_id(1)
    bp = cp_ref.shape[0]
    c = jnp.concatenate([cp_ref[...], cs_ref[...]], axis=0)
    part = _dot(jax.nn.silu(c).astype(BF16), w_ref[0].astype(BF16))
    part_p, part_s = part[:bp], part[bp:]

    @pl.when(k == 0)
    def _():
        bias = b_ref[pl.ds(l, 1), :]
        op_ref[0] = part_p + bias
        os_ref[0] = part_s + bias

    @pl.when(k > 0)
    def _():
        op_ref[0] += part_p
        os_ref[0] += part_s


def _ada_call(c_prompt, c_sample, w_ada, b_ada, weights, row_tile=256):
    depth, d_model, n_mod = w_ada.shape
    bp, bs = c_prompt.shape[0], c_sample.shape[0]
    n_k = d_model // row_tile
    n_steps = depth * n_k
    assert all(w.shape[1] % n_steps == 0 for w in weights)
    step = lambda l, k: l * n_k + k
    return pl.pallas_call(
        _ada_kernel,
        grid=(depth, n_k),
        in_specs=[
            pl.BlockSpec((bp, row_tile), lambda l, k: (0, k)),
            pl.BlockSpec((bs, row_tile), lambda l, k: (0, k)),
            pl.BlockSpec((1, row_tile, n_mod), lambda l, k: (l, k, 0)),
            _whole(b_ada.shape),
            *[pl.BlockSpec((1, w.shape[1] // n_steps, w.shape[2]), lambda l, k: (0, step(l, k), 0))
              for w in weights],
        ],
        out_specs=[
            pl.BlockSpec((1, bp, n_mod), lambda l, k: (l, 0, 0)),
            pl.BlockSpec((1, bs, n_mod), lambda l, k: (l, 0, 0)),
            *[pl.BlockSpec((w.shape[1] // n_steps, w.shape[2]), lambda l, k: (step(l, k), 0))
              for w in weights],
        ],
        out_shape=[
            jax.ShapeDtypeStruct((depth, bp, n_mod), F32),
            jax.ShapeDtypeStruct((depth, bs, n_mod), F32),
            *[jax.ShapeDtypeStruct(w.shape[1:], BF16) for w in weights],
        ],
        compiler_params=pltpu.CompilerParams(
            dimension_semantics=("arbitrary", "arbitrary"),
            vmem_limit_bytes=VMEM_LIMIT_BYTES),
        name="ada_mod",
    )(c_prompt, c_sample, w_ada, b_ada, *weights)


def _layer_pipeline(n_sub, project, heads, mix_out, x_mid, sh2, sc2, gate2, wup_refs, wdn_refs,
                    ln2_g, ln2_b, alpha, store):
    n_up, n_dn = len(wup_refs), len(wdn_refs)
    chunk = wup_refs[0].shape[1]

    def heads_and_project(j):
        heads(j)
        if j + PROJ_AHEAD < n_sub:
            project(j + PROJ_AHEAD)

    for jj in range(min(PROJ_AHEAD, n_sub)):
        project(jj)
    heads_and_project(0)
    mix_out(0)
    if n_sub > 1:
        heads_and_project(1)

    stages = [(j, k) for j in range(n_sub) for k in range(n_up)]
    hs, ups = {}, {}

    def issue_up(s):
        j, k = stages[s]
        if j not in hs:
            xm = x_mid[j]
            hs[j] = (xm * sc2 + sh2).reshape(-1, xm.shape[-1]).astype(BF16)
        ups[s] = _dot(hs[j], wup_refs[k][...])

    for s in range(min(UP_AHEAD, len(stages))):
        issue_up(s)
    acc = [None] * n_dn
    for s, (j, k) in enumerate(stages):
        if k == MIX_OUT_AT and j + 1 < n_sub:
            mix_out(j + 1)
        if k == HEADS_AT and j + 2 < n_sub:
            heads_and_project(j + 2)
        if s + UP_AHEAD < len(stages):
            issue_up(s + UP_AHEAD)
        a = jnp.square(jnp.maximum(ups.pop(s), 0.0)).astype(BF16)
        for n in range(n_dn):
            part = _dot(a, wdn_refs[n][k * chunk:(k + 1) * chunk, :])
            acc[n] = part if acc[n] is None else acc[n] + part
        if k == n_up - 1:
            xm = x_mid.pop(j)
            f = jnp.concatenate(acc, axis=1).reshape(xm.shape)
            store(j, _layer_norm(alpha * xm + gate2 * f, ln2_g, ln2_b))
            acc = [None] * n_dn


def _layer_prompt_kernel(x_ref, mod_ref, *refs, alpha, layer, n_in, n_out, n_up, n_dn, n_next):
    win_refs, refs = refs[:n_in], refs[n_in:]
    cw_ref, sg_ref, sb_ref, ws_ref, bs_ref, pw_ref, ps_ref = refs[:7]
    wout_refs, refs = refs[7:7 + n_out], refs[7 + n_out:]
    g_ref, b_ref = refs[:2]
    wup_refs, refs = refs[2:2 + n_up], refs[2 + n_up:]
    wdn_refs, refs = refs[:n_dn], refs[n_dn:]
    g2_ref, b2_ref = refs[:2]
    next_f32, refs = refs[2:2 + n_next], refs[2 + n_next:]
    o_ref, ncv_ref, npl_ref = refs[:3]
    next_bf16, (zc_ref, pc_ref) = refs[3:3 + n_next], refs[3 + n_next:]
    for src, dst in zip(next_f32, next_bf16):
        dst[...] = src[...].astype(BF16)
    bi, t = pl.program_id(0), pl.program_id(1)
    rows, c_dim = x_ref.shape[1], x_ref.shape[2]
    w_conv, w_gmlp, w_pool = cw_ref.shape[2], sg_ref.shape[1], ps_ref.shape[1]
    n_heads = ws_ref.shape[0]

    @pl.when(t == 0)
    def _():
        zc_ref[...] = jnp.zeros(zc_ref.shape, F32)
        pc_ref[...] = jnp.zeros(pc_ref.shape, F32)

    m = mod_ref[pl.ds(bi, 1), :]
    sh, sc1, gate = m[:, :c_dim], 1.0 + m[:, c_dim:2 * c_dim], m[:, 2 * c_dim:3 * c_dim]
    sh2, sc2, gate2 = (m[:, 3 * c_dim:4 * c_dim], 1.0 + m[:, 4 * c_dim:5 * c_dim],
                       m[:, 5 * c_dim:])
    cw = cw_ref[layer]
    sgu_g, sgu_b = sg_ref[layer:layer + 1, :], sb_ref[layer:layer + 1, :]
    pool_scale = ps_ref[layer:layer + 1, :]
    ln_g, ln_b = g_ref[layer:layer + 1, :], b_ref[layer:layer + 1, :]
    ln2_g, ln2_b = g2_ref[layer:layer + 1, :], b2_ref[layer:layer + 1, :]

    row_i = lax.broadcasted_iota(jnp.int32, (CHUNK, 2 * CHUNK), 0)
    col_j = lax.broadcasted_iota(jnp.int32, (CHUNK, 2 * CHUNK), 1) % CHUNK
    causal = col_j <= row_i
    ws_m = []
    for p in range(n_heads // 2):
        pair = jnp.concatenate([ws_ref[2 * p], ws_ref[2 * p + 1]], axis=1)
        ws_m.append(jnp.where(causal, pair, 0.0).astype(BF16))
    low_half = lax.broadcasted_iota(jnp.int32, (CHUNK, LANES), 1) < HEAD_DIM
    lane_group = _lane_group((SUB_ROWS, w_pool), len(POOL_WINDOWS))
    window = _select_by_group([jnp.full((SUB_ROWS, w_pool), w, jnp.int32) for w in POOL_WINDOWS],
                              lane_group)
    row_id = lax.broadcasted_iota(jnp.int32, (SUB_ROWS, w_pool), 0)
    bias = bs_ref[...]
    n_sub = rows // SUB_ROWS
    n_ch = SUB_ROWS // CHUNK
    hist = {"z": zc_ref[...], "p": pc_ref[...]}
    projs, mids, x_mid = {}, {}, {}

    def project(jj):
        h = (x_ref[0, jj * SUB_ROWS:(jj + 1) * SUB_ROWS, :] * sc1 + sh).astype(BF16)
        projs[jj] = jnp.concatenate([_dot(h, w[...]) for w in win_refs], axis=1)

    def heads(j):
        proj = projs.pop(j)
        o0 = 0
        a_b = proj[:, o0:o0 + w_conv]; o0 += w_conv
        a_c = proj[:, o0:o0 + w_conv]; o0 += w_conv
        a_x = proj[:, o0:o0 + w_conv]; o0 += w_conv
        g_u = proj[:, o0:o0 + w_gmlp]; o0 += w_gmlp
        g_v = proj[:, o0:o0 + w_gmlp]; o0 += w_gmlp
        p_x = proj[:, o0:o0 + w_pool]

        z = a_c * a_x
        z_ext = jnp.concatenate([hist["z"], z], axis=0)
        conv = z_ext * cw[CONV_WIDTH - 1:CONV_WIDTH, :]
        for k in range(1, CONV_WIDTH):
            conv = conv + pltpu.roll(z_ext, k, 0) * cw[CONV_WIDTH - 1 - k:CONV_WIDTH - k, :]
        y_a = a_b * conv[CONV_HIST:, :]
        hist["z"] = z[SUB_ROWS - CONV_HIST:, :]

        v_n = _layer_norm(g_v, sgu_g, sgu_b)
        s_cols = []
        for p in range(n_heads // 2):
            cols = [v_n[c * CHUNK:(c + 1) * CHUNK, p * LANES:(p + 1) * LANES] for c in range(n_ch)]
            rhs = jnp.concatenate(
                [jnp.concatenate([jnp.where(low_half, col, 0.0) for col in cols], axis=1),
                 jnp.concatenate([jnp.where(low_half, 0.0, col) for col in cols], axis=1)],
                axis=0).astype(BF16)
            s_cols.append(_dot(ws_m[p], rhs))

        run = jnp.concatenate([hist["p"], p_x], axis=0)
        sums = []
        for w in POOL_WINDOWS:
            run = run + pltpu.roll(run, w // 2, 0)
            sums.append(run[POOL_HIST:, :])
        count = jnp.minimum(window, t * rows + j * SUB_ROWS + row_id + 1).astype(F32)
        d = (_select_by_group(sums, lane_group) / count - p_x).astype(BF16)
        y_c_raw = _dot(d, pw_ref[...])
        hist["p"] = p_x[SUB_ROWS - POOL_HIST:, :]
        mids[j] = (y_a, g_u, s_cols, y_c_raw)

    def mix_out(j):
        y_a, g_u, s_cols, y_c_raw = mids.pop(j)
        s = jnp.concatenate(
            [jnp.concatenate([sc[:, c * LANES:(c + 1) * LANES] for sc in s_cols], axis=1) + bias
             for c in range(n_ch)], axis=0)
        mix = jnp.concatenate([y_a, g_u * s, y_c_raw * pool_scale], axis=1).astype(BF16)
        out = jnp.concatenate([_dot(mix, w[...]) for w in wout_refs], axis=1)
        x = x_ref[0, j * SUB_ROWS:(j + 1) * SUB_ROWS, :]
        x_mid[j] = _layer_norm(alpha * x + gate * out, ln_g, ln_b)

    def store(j, y):
        o_ref[0, j * SUB_ROWS:(j + 1) * SUB_ROWS, :] = y

    _layer_pipeline(n_sub, project, heads, mix_out, x_mid, sh2, sc2, gate2, wup_refs, wdn_refs,
                    ln2_g, ln2_b, alpha, store)

    zc_ref[...] = hist["z"]
    pc_ref[...] = hist["p"]
    ncv_ref[0] = hist["z"][CONV_HIST - (CONV_WIDTH - 1):, :]
    npl_ref[0] = hist["p"][POOL_HIST - (max(POOL_WINDOWS) - 1):, :]


def _layer_prompt_call(x, mod, w_in, conv_w, sgu_g, sgu_b, w_s, bs_full, pool_wbd, pool_scale,
                       w_out, ln_g, ln_b, wup, wdn, ln2_g, ln2_b, next_weights, *, alpha, layer,
                       rows):
    batch, seq, c_dim = x.shape
    n_steps = batch * (seq // rows)
    w_conv, w_pool = conv_w.shape[2], pool_scale.shape[1]
    conv_buf = CONV_WIDTH - 1
    pool_buf = max(POOL_WINDOWS) - 1
    in_specs = _column_blocks(w_in, WEIGHT_COLS)
    out_specs = _column_blocks(w_out, WEIGHT_COLS)
    up_specs = _column_blocks(wup, WEIGHT_COLS)
    dn_specs = _column_blocks(wdn, WEIGHT_COLS)
    kern = functools.partial(_layer_prompt_kernel, alpha=alpha, layer=layer, n_in=len(in_specs),
                             n_out=len(out_specs), n_up=len(up_specs), n_dn=len(dn_specs),
                             n_next=len(next_weights))
    n_t = seq // rows
    assert all(w.shape[1] % n_steps == 0 for w in next_weights)
    next_in = [pl.BlockSpec((None, w.shape[1] // n_steps, w.shape[2]),
                            lambda b, t: (layer + 1, b * n_t + t, 0)) for w in next_weights]
    next_out = [pl.BlockSpec((w.shape[1] // n_steps, w.shape[2]), lambda b, t: (b * n_t + t, 0))
                for w in next_weights]
    return pl.pallas_call(
        kern,
        grid=(batch, seq // rows),
        in_specs=[
            pl.BlockSpec((1, rows, c_dim), lambda b, t: (b, t, 0)),
            pl.BlockSpec((None, batch, mod.shape[2]), lambda b, t: (layer, 0, 0)),
            *in_specs,
            _whole(conv_w.shape),
            _whole(sgu_g.shape),
            _whole(sgu_b.shape),
            _layer_block(w_s.shape[1:], layer),
            _layer_block(bs_full.shape[1:], layer),
            _layer_block(pool_wbd.shape[1:], layer),
            _whole(pool_scale.shape),
            *out_specs,
            _whole(ln_g.shape),
            _whole(ln_b.shape),
            *up_specs, *dn_specs,
            _whole(ln2_g.shape),
            _whole(ln2_b.shape),
            *next_in,
        ],
        out_specs=[
            pl.BlockSpec((1, rows, c_dim), lambda b, t: (b, t, 0)),
            pl.BlockSpec((1, conv_buf, w_conv), lambda b, t: (b, 0, 0)),
            pl.BlockSpec((1, pool_buf, w_pool), lambda b, t: (b, 0, 0)),
            *next_out,
        ],
        out_shape=[
            jax.ShapeDtypeStruct((batch, seq, c_dim), F32),
            jax.ShapeDtypeStruct((batch, conv_buf, w_conv), F32),
            jax.ShapeDtypeStruct((batch, pool_buf, w_pool), F32),
            *[jax.ShapeDtypeStruct(w.shape[1:], BF16) for w in next_weights],
        ],
        scratch_shapes=[
            pltpu.VMEM((CONV_HIST, w_conv), F32),
            pltpu.VMEM((POOL_HIST, w_pool), F32),
        ],
        input_output_aliases={0: 0} if layer > 0 else {},
        compiler_params=pltpu.CompilerParams(
            dimension_semantics=("arbitrary", "arbitrary"),
            vmem_limit_bytes=VMEM_LIMIT_BYTES),
        name="layer_prompt",
    )(x, mod, *([w_in] * len(in_specs)), conv_w, sgu_g, sgu_b, w_s, bs_full, pool_wbd, pool_scale,
      *([w_out] * len(out_specs)), ln_g, ln_b, *([wup] * len(up_specs)),
      *([wdn] * len(dn_specs)), ln2_g, ln2_b, *next_weights)


def _layer_sample_kernel(x_ref, mod_ref, cst_ref, pst_ref, *refs, alpha, layer, n_in, n_out,
                         n_up, n_dn, x_batch_major, out_batch_major):
    win_refs, refs = refs[:n_in], refs[n_in:]
    cw_ref, sg_ref, sb_ref, wst_ref, bst_ref, pw_ref, ps_ref = refs[:7]
    wout_refs, refs = refs[7:7 + n_out], refs[7 + n_out:]
    g_ref, b_ref = refs[:2]
    wup_refs, refs = refs[2:2 + n_up], refs[2 + n_up:]
    wdn_refs, refs = refs[:n_dn], refs[n_dn:]
    g2_ref, b2_ref = refs[:2]
    o_ref, ncv_ref, npl_ref, v_ref = refs[len(refs) - 4:]
    xt = jnp.swapaxes(x_ref[...], 0, 1) if x_batch_major else None
    t_dim, b_dim = v_ref.shape[1], v_ref.shape[0]
    c_dim = x_ref.shape[2]
    w_conv, w_gmlp, w_pool = cw_ref.shape[2], sg_ref.shape[1], ps_ref.shape[1]
    cst = jnp.swapaxes(cst_ref[...].reshape(b_dim, -1, w_conv), 0, 1)
    pst = jnp.swapaxes(pst_ref[...].reshape(b_dim, -1, w_pool), 0, 1)
    n_valid = pst.shape[0]
    step = SUB_ROWS // b_dim
    n_sub = t_dim // step

    m = mod_ref[...]
    sh, sc1, gate = m[:, :c_dim], 1.0 + m[:, c_dim:2 * c_dim], m[:, 2 * c_dim:3 * c_dim]
    sh2, sc2, gate2 = (m[:, 3 * c_dim:4 * c_dim], 1.0 + m[:, 4 * c_dim:5 * c_dim],
                       m[:, 5 * c_dim:])
    cw = cw_ref[layer]
    sgu_g, sgu_b = sg_ref[layer:layer + 1, :], sb_ref[layer:layer + 1, :]
    pool_scale = ps_ref[layer:layer + 1, :]
    ln_g, ln_b = g_ref[layer:layer + 1, :], b_ref[layer:layer + 1, :]
    ln2_g, ln2_b = g2_ref[layer:layer + 1, :], b2_ref[layer:layer + 1, :]
    lane_group = _lane_group((b_dim, w_pool), len(POOL_WINDOWS))

    z_ext = [cst[i] for i in range(CONV_WIDTH - 1)]
    p_ext = [pst[i] for i in range(n_valid)]
    v_all = []
    projs, mids, x_mid, outs = {}, {}, {}, {}

    def x_rows(j):
        if x_batch_major:
            return xt[j * step:(j + 1) * step]
        return x_ref[j * step:(j + 1) * step]

    def project(j):
        h = (x_rows(j) * sc1 + sh).reshape(step * b_dim, c_dim).astype(BF16)
        projs[j] = jnp.concatenate([_dot(h, w[...]) for w in win_refs], axis=1).reshape(
            step, b_dim, -1)

    def heads(j):
        proj = projs.pop(j)
        o0 = 0
        a_b = proj[:, :, o0:o0 + w_conv]; o0 += w_conv
        a_c = proj[:, :, o0:o0 + w_conv]; o0 += w_conv
        a_x = proj[:, :, o0:o0 + w_conv]; o0 += w_conv
        g_u = proj[:, :, o0:o0 + w_gmlp]; o0 += w_gmlp
        g_v = proj[:, :, o0:o0 + w_gmlp]; o0 += w_gmlp
        p_x = proj[:, :, o0:o0 + w_pool]
        z = a_c * a_x
        v_n = _layer_norm(g_v, sgu_g, sgu_b)
        y_a, y_b, d = [], [], []
        for i in range(step):
            pos = j * step + i
            z_ext.append(z[i])
            conv = z_ext[pos] * cw[0:1, :]
            for k in range(1, CONV_WIDTH):
                conv = conv + z_ext[pos + k] * cw[k:k + 1, :]
            y_a.append(a_b[i] * conv)
            v_all.append(v_n[i])
            s = bst_ref[pos:pos + 1, :] + wst_ref[pos, 0:1, :] * v_all[0]
            for q in range(1, pos + 1):
                s = s + wst_ref[pos, q:q + 1, :] * v_all[q]
            y_b.append(g_u[i] * s)
            p_ext.append(p_x[i])
            last = n_valid + pos
            means, run, k = [], None, 0
            for w in POOL_WINDOWS:
                while k < w:
                    run = p_ext[last - k] if run is None else run + p_ext[last - k]
                    k += 1
                means.append(run / float(min(w, pos + 1 + n_valid)))
            d.append(_select_by_group(means, lane_group) - p_x[i])
        d = jnp.stack(d).reshape(step * b_dim, w_pool).astype(BF16)
        mids[j] = (jnp.stack(y_a), jnp.stack(y_b), _dot(d, pw_ref[...]))

    def mix_out(j):
        y_a, y_b, y_c_raw = mids.pop(j)
        y_c = (y_c_raw * pool_scale).reshape(step, b_dim, w_pool)
        mix = jnp.concatenate([y_a, y_b, y_c], axis=2).reshape(step * b_dim, c_dim).astype(BF16)
        out = jnp.concatenate([_dot(mix, w[...]) for w in wout_refs], axis=1)
        x_mid[j] = _layer_norm(alpha * x_rows(j) + gate * out.reshape(step, b_dim, c_dim),
                               ln_g, ln_b)

    def store(j, y):
        if out_batch_major:
            outs[j] = y
        else:
            o_ref[j * step:(j + 1) * step] = y

    _layer_pipeline(n_sub, project, heads, mix_out, x_mid, sh2, sc2, gate2, wup_refs, wdn_refs,
                    ln2_g, ln2_b, alpha, store)

    if out_batch_major:
        o_ref[...] = jnp.swapaxes(jnp.concatenate([outs[j] for j in range(n_sub)], axis=0), 0, 1)
    ncv_ref[...] = jnp.swapaxes(jnp.stack(z_ext[len(z_ext) - (CONV_WIDTH - 1):]), 0, 1).reshape(
        ncv_ref.shape)
    npl_ref[...] = jnp.swapaxes(jnp.stack(p_ext[len(p_ext) - n_valid:]), 0, 1).reshape(
        npl_ref.shape)
    v_ref[...] = jnp.swapaxes(jnp.stack(v_all), 0, 1)


def _layer_sample_call(x, mod, cst, pst, w_in, conv_w, sgu_g, sgu_b, wst, bs_full, pool_wbd,
                       pool_scale, w_out, ln_g, ln_b, wup, wdn, ln2_g, ln2_b, states, *, alpha,
                       layer, x_batch_major, out_batch_major):
    if x_batch_major:
        b_dim, t_dim, c_dim = x.shape
    else:
        t_dim, b_dim, c_dim = x.shape
    w_gmlp = sgu_g.shape[1]
    in_specs = _column_blocks(w_in, WEIGHT_COLS)
    out_specs = _column_blocks(w_out, WEIGHT_COLS)
    up_specs = _column_blocks(wup, WEIGHT_COLS)
    dn_specs = _column_blocks(wdn, WEIGHT_COLS)
    kern = functools.partial(_layer_sample_kernel, alpha=alpha, layer=layer, n_in=len(in_specs),
                             n_out=len(out_specs), n_up=len(up_specs), n_dn=len(dn_specs),
                             x_batch_major=x_batch_major, out_batch_major=out_batch_major)
    out_shape = (b_dim, t_dim, c_dim) if out_batch_major else (t_dim, b_dim, c_dim)
    state_shapes = [cst.shape, pst.shape, (cst.shape[0], b_dim, t_dim, w_gmlp)]
    whole3 = lambda i: (0, 0, 0)
    inputs = (x, mod, cst, pst, *([w_in] * len(in_specs)), conv_w, sgu_g, sgu_b, wst, bs_full,
              pool_wbd, pool_scale, *([w_out] * len(out_specs)), ln_g, ln_b,
              *([wup] * len(up_specs)), *([wdn] * len(dn_specs)), ln2_g, ln2_b)
    return pl.pallas_call(
        kern,
        grid=(1,),
        in_specs=[
            pl.BlockSpec(x.shape, whole3),
            pl.BlockSpec((None, b_dim, mod.shape[2]), lambda i: (layer, 0, 0)),
            _layer_block(cst.shape[1:], layer),
            _layer_block(pst.shape[1:], layer),
            *in_specs,
            _whole(conv_w.shape),
            _whole(sgu_g.shape),
            _whole(sgu_b.shape),
            _layer_block(wst.shape[1:], layer),
            pl.BlockSpec((None, t_dim, w_gmlp), lambda i: (layer, 0, 0)),
            _layer_block(pool_wbd.shape[1:], layer),
            _whole(pool_scale.shape),
            *out_specs,
            _whole(ln_g.shape),
            _whole(ln_b.shape),
            *up_specs, *dn_specs,
            _whole(ln2_g.shape),
            _whole(ln2_b.shape),
            *[pl.BlockSpec(memory_space=pl.ANY) for _ in states],
        ],
        out_specs=[pl.BlockSpec(out_shape, whole3)] + [
            pl.BlockSpec((None,) + shape[1:], lambda i, n=len(shape): (layer,) + (0,) * (n - 1))
            for shape in state_shapes],
        out_shape=[jax.ShapeDtypeStruct(out_shape, F32)] + [
            jax.ShapeDtypeStruct(shape, F32) for shape in state_shapes],
        input_output_aliases={len(inputs) + k: 1 + k for k in range(len(states))},
        compiler_params=pltpu.CompilerParams(
            dimension_semantics=("arbitrary",),
            vmem_limit_bytes=VMEM_LIMIT_BYTES),
        name="layer_sample",
    )(*inputs, *states)


def kernel(x_prompt, x_sample, c_prompt, c_sample, state_conv, state_pool, w_ada, b_ada, w_in,
           conv_w, sgu_g, sgu_b, w_s, b_s, pool_w, pool_scale, w_out, ln1_g, ln1_b, w_up,
           w_down, ln2_g, ln2_b):
    depth = w_in.shape[0]
    seq = x_prompt.shape[1]
    dec_batch, dec_seq, _ = x_sample.shape
    n_groups, group_dim, _ = pool_w.shape[1:]
    w_pool = n_groups * group_dim
    alpha = (2.0 * depth) ** 0.25
    assert seq % PROMPT_ROWS == 0 and PROMPT_ROWS % SUB_ROWS == 0 and SUB_ROWS % CHUNK == 0
    assert (dec_seq * dec_batch) % SUB_ROWS == 0 and SUB_ROWS % dec_batch == 0

    weights = (w_in, w_out, w_up, w_down)
    mod_p, mod_s, *weights_b = _ada_call(c_prompt, c_sample, w_ada, b_ada, weights)

    bs_full = jnp.repeat(jnp.swapaxes(b_s, 1, 2), HEAD_DIM, axis=2)
    ws_small = jnp.repeat(jnp.transpose(w_s[:, :, :dec_seq, :dec_seq], (0, 2, 3, 1)), HEAD_DIM, axis=3)
    eye = jnp.eye(n_groups, dtype=F32)
    pool_wbd = (eye[None, :, None, :, None] * pool_w[:, :, :, None, :]).reshape(
        depth, w_pool, w_pool).astype(BF16)

    xp, xs = x_prompt, x_sample
    cst = state_conv.reshape(depth, -1, state_conv.shape[-1])
    pst = state_pool.reshape(depth, -1, state_pool.shape[-1])
    conv_p, pool_p, states_s = [], [], ()
    for l in range(depth):
        w_in_b, w_out_b, wup_b, wdn_b = weights_b
        xp, ncv, npl, *weights_b = _layer_prompt_call(
            xp, mod_p, w_in_b, conv_w, sgu_g, sgu_b, w_s, bs_full, pool_wbd, pool_scale,
            w_out_b, ln1_g, ln1_b, wup_b, wdn_b, ln2_g, ln2_b,
            weights if l + 1 < depth else (), alpha=alpha, layer=l, rows=PROMPT_ROWS)
        conv_p.append(ncv)
        pool_p.append(npl)

        xp, xs = lax.optimization_barrier((xp, xs))
        xs, *states_s = _layer_sample_call(
            xs, mod_s, cst, pst, w_in_b, conv_w, sgu_g, sgu_b, ws_small, bs_full,
            pool_wbd, pool_scale, w_out_b, ln1_g, ln1_b, wup_b, wdn_b, ln2_g, ln2_b, states_s,
            alpha=alpha, layer=l, x_batch_major=(l == 0), out_batch_major=(l == depth - 1))
        xp, xs = lax.optimization_barrier((xp, xs))

    new_conv_s, new_pool_s, v_s = states_s
    return (xp, xs, jnp.stack(conv_p), jnp.stack(pool_p), new_conv_s.reshape(state_conv.shape),
            new_pool_s.reshape(state_pool.shape), v_s)
```

```python
import functools

import jax
import jax.numpy as jnp
from jax import lax
from jax.experimental import pallas as pl
from jax.experimental.pallas import tpu as pltpu

F32 = jnp.float32
BF16 = jnp.bfloat16

LN_EPS = 1e-5
HEAD_DIM = 64
CHUNK = 128
POOL_WINDOWS = (2, 4, 8, 16)
CONV_WIDTH = 3
LANES = 128
SUBLANES = 8
VMEM_LIMIT_BYTES = 60 * 1024 * 1024

PROMPT_ROWS = 1024
SUB_ROWS = 256
UP_AHEAD = 2
PROJ_AHEAD = 2
MIX_OUT_AT = 2
HEADS_AT = 5
WEIGHT_COLS = 512
CONV_HIST = SUBLANES
POOL_HIST = 2 * SUBLANES


def _layer_norm(y, g, b):
    mu = jnp.mean(y, axis=-1, keepdims=True)
    d = y - mu
    var = jnp.mean(d * d, axis=-1, keepdims=True)
    return d * lax.rsqrt(var + LN_EPS) * g + b


def _dot(a, b):
    return jnp.dot(a, b, preferred_element_type=F32)


def _layer_block(shape, layer):
    zeros = (0,) * len(shape)
    return pl.BlockSpec((None,) + tuple(shape), lambda *_: (layer,) + zeros,
                        pipeline_mode=pl.Buffered(1))


def _whole(shape):
    zeros = (0,) * len(shape)
    return pl.BlockSpec(tuple(shape), lambda *_: zeros, pipeline_mode=pl.Buffered(1))


def _column_blocks(w, width):
    rows, cols = w.shape
    return [pl.BlockSpec((rows, width), lambda *_, n=n: (0, n), pipeline_mode=pl.Buffered(1))
            for n in range(cols // width)]


def _lane_group(shape, n_groups):
    return lax.broadcasted_iota(jnp.int32, shape, len(shape) - 1) // (shape[-1] // n_groups)


def _select_by_group(values, lane_group):
    out = values[-1]
    for g in range(len(values) - 2, -1, -1):
        out = jnp.where(lane_group == g, values[g], out)
    return out


def _ada_kernel(cp_ref, cs_ref, w_ref, b_ref, *refs):
    n_side = (len(refs) - 2) // 2
    side_f32, (op_ref, os_ref), side_bf16 = refs[:n_side], refs[n_side:n_side + 2], refs[n_side + 2:]
    for src, dst in zip(side_f32, side_bf16):
        dst[...] = src[0].astype(BF16)
    l, k = pl.program_id(0), pl.program_id(1)
    bp = cp_ref.shape[0]
    c = jnp.concatenate([cp_ref[...], cs_ref[...]], axis=0)
    part = _dot(jax.nn.silu(c).astype(BF16), w_ref[0].astype(BF16))
    part_p, part_s = part[:bp], part[bp:]

    @pl.when(k == 0)
    def _():
        bias = b_ref[pl.ds(l, 1), :]
        op_ref[0] = part_p + bias
        os_ref[0] = part_s + bias

    @pl.when(k > 0)
    def _():
        op_ref[0] += part_p
        os_ref[0] += part_s


def _ada_call(c_prompt, c_sample, w_ada, b_ada, weights, row_tile=128):
    depth, d_model, n_mod = w_ada.shape
    bp, bs = c_prompt.shape[0], c_sample.shape[0]
    n_k = d_model // row_tile
    n_steps = depth * n_k
    assert all(w.shape[1] % n_steps == 0 for w in weights)
    step = lambda l, k: l * n_k + k
    return pl.pallas_call(
        _ada_kernel,
        grid=(depth, n_k),
        in_specs=[
            pl.BlockSpec((bp, row_tile), lambda l, k: (0, k)),
            pl.BlockSpec((bs, row_tile), lambda l, k: (0, k)),
            pl.BlockSpec((1, row_tile, n_mod), lambda l, k: (l, k, 0)),
            _whole(b_ada.shape),
            *[pl.BlockSpec((1, w.shape[1] // n_steps, w.shape[2]), lambda l, k: (0, step(l, k), 0))
              for w in weights],
        ],
        out_specs=[
            pl.BlockSpec((1, bp, n_mod), lambda l, k: (l, 0, 0)),
            pl.BlockSpec((1, bs, n_mod), lambda l, k: (l, 0, 0)),
            *[pl.BlockSpec((w.shape[1] // n_steps, w.shape[2]), lambda l, k: (step(l, k), 0))
              for w in weights],
        ],
        out_shape=[
            jax.ShapeDtypeStruct((depth, bp, n_mod), F32),
            jax.ShapeDtypeStruct((depth, bs, n_mod), F32),
            *[jax.ShapeDtypeStruct(w.shape[1:], BF16) for w in weights],
        ],
        compiler_params=pltpu.CompilerParams(
            dimension_semantics=("arbitrary", "arbitrary"),
            vmem_limit_bytes=VMEM_LIMIT_BYTES),
        name="ada_mod",
    )(c_prompt, c_sample, w_ada, b_ada, *weights)


def _layer_pipeline(n_sub, project, heads, mix_out, x_mid, sh2, sc2, gate2, wup_refs, wdn_refs,
                    ln2_g, ln2_b, alpha, store):
    n_up, n_dn = len(wup_refs), len(wdn_refs)
    chunk = wup_refs[0].shape[1]

    def heads_and_project(j):
        heads(j)
        if j + PROJ_AHEAD < n_sub:
            project(j + PROJ_AHEAD)

    for jj in range(min(PROJ_AHEAD, n_sub)):
        project(jj)
    heads_and_project(0)
    mix_out(0)
    if n_sub > 1:
        heads_and_project(1)

    stages = [(j, k) for j in range(n_sub) for k in range(n_up)]
    hs, ups = {}, {}

    def issue_up(s):
        j, k = stages[s]
        if j not in hs:
            xm = x_mid[j]
            hs[j] = (xm * sc2 + sh2).reshape(-1, xm.shape[-1]).astype(BF16)
        ups[s] = _dot(hs[j], wup_refs[k][...])

    for s in range(min(UP_AHEAD, len(stages))):
        issue_up(s)
    acc = [None] * n_dn
    for s, (j, k) in enumerate(stages):
        if k == MIX_OUT_AT and j + 1 < n_sub:
            mix_out(j + 1)
        if k == HEADS_AT and j + 2 < n_sub:
            heads_and_project(j + 2)
        if s + UP_AHEAD < len(stages):
            issue_up(s + UP_AHEAD)
        a = jnp.square(jnp.maximum(ups.pop(s), 0.0)).astype(BF16)
        for n in range(n_dn):
            part = _dot(a, wdn_refs[n][k * chunk:(k + 1) * chunk, :])
            acc[n] = part if acc[n] is None else acc[n] + part
        if k == n_up - 1:
            xm = x_mid.pop(j)
            f = jnp.concatenate(acc, axis=1).reshape(xm.shape)
            store(j, _layer_norm(alpha * xm + gate2 * f, ln2_g, ln2_b))
            acc = [None] * n_dn


def _layer_prompt_kernel(x_ref, mod_ref, *refs, alpha, layer, n_in, n_out, n_up, n_dn, n_next):
    win_refs, refs = refs[:n_in], refs[n_in:]
    cw_ref, sg_ref, sb_ref, ws_ref, bs_ref, pw_ref, ps_ref = refs[:7]
    wout_refs, refs = refs[7:7 + n_out], refs[7 + n_out:]
    g_ref, b_ref = refs[:2]
    wup_refs, refs = refs[2:2 + n_up], refs[2 + n_up:]
    wdn_refs, refs = refs[:n_dn], refs[n_dn:]
    g2_ref, b2_ref = refs[:2]
    next_f32, refs = refs[2:2 + n_next], refs[2 + n_next:]
    o_ref, ncv_ref, npl_ref = refs[:3]
    next_bf16, (zc_ref, pc_ref) = refs[3:3 + n_next], refs[3 + n_next:]
    for src, dst in zip(next_f32, next_bf16):
        dst[...] = src[...].astype(BF16)
    bi, t = pl.program_id(0), pl.program_id(1)
    rows, c_dim = x_ref.shape[1], x_ref.shape[2]
    w_conv, w_gmlp, w_pool = cw_ref.shape[2], sg_ref.shape[1], ps_ref.shape[1]
    n_heads = ws_ref.shape[0]

    @pl.when(t == 0)
    def _():
        zc_ref[...] = jnp.zeros(zc_ref.shape, F32)
        pc_ref[...] = jnp.zeros(pc_ref.shape, F32)

    m = mod_ref[pl.ds(bi, 1), :]
    sh, sc1, gate = m[:, :c_dim], 1.0 + m[:, c_dim:2 * c_dim], m[:, 2 * c_dim:3 * c_dim]
    sh2, sc2, gate2 = (m[:, 3 * c_dim:4 * c_dim], 1.0 + m[:, 4 * c_dim:5 * c_dim],
                       m[:, 5 * c_dim:])
    cw = cw_ref[layer]
    sgu_g, sgu_b = sg_ref[layer:layer + 1, :], sb_ref[layer:layer + 1, :]
    pool_scale = ps_ref[layer:layer + 1, :]
    ln_g, ln_b = g_ref[layer:layer + 1, :], b_ref[layer:layer + 1, :]
    ln2_g, ln2_b = g2_ref[layer:layer + 1, :], b2_ref[layer:layer + 1, :]

    row_i = lax.broadcasted_iota(jnp.int32, (CHUNK, 2 * CHUNK), 0)
    col_j = lax.broadcasted_iota(jnp.int32, (CHUNK, 2 * CHUNK), 1) % CHUNK
    causal = col_j <= row_i
    ws_m = []
    for p in range(n_heads // 2):
        pair = jnp.concatenate([ws_ref[2 * p], ws_ref[2 * p + 1]], axis=1)
        ws_m.append(jnp.where(causal, pair, 0.0).astype(BF16))
    low_half = lax.broadcasted_iota(jnp.int32, (CHUNK, LANES), 1) < HEAD_DIM
    lane_group = _lane_group((SUB_ROWS, w_pool), len(POOL_WINDOWS))
    window = _select_by_group([jnp.full((SUB_ROWS, w_pool), w, jnp.int32) for w in POOL_WINDOWS],
                              lane_group)
    row_id = lax.broadcasted_iota(jnp.int32, (SUB_ROWS, w_pool), 0)
    bias = bs_ref[...]
    n_sub = rows // SUB_ROWS
    n_ch = SUB_ROWS // CHUNK
    hist = {"z": zc_ref[...], "p": pc_ref[...]}
    projs, mids, x_mid = {}, {}, {}

    def project(jj):
        h = (x_ref[0, jj * SUB_ROWS:(jj + 1) * SUB_ROWS, :] * sc1 + sh).astype(BF16)
        projs[jj] = jnp.concatenate([_dot(h, w[...]) for w in win_refs], axis=1)

    def heads(j):
        proj = projs.pop(j)
        o0 = 0
        a_b = proj[:, o0:o0 + w_conv]; o0 += w_conv
        a_c = proj[:, o0:o0 + w_conv]; o0 += w_conv
        a_x = proj[:, o0:o0 + w_conv]; o0 += w_conv
        g_u = proj[:, o0:o0 + w_gmlp]; o0 += w_gmlp
        g_v = proj[:, o0:o0 + w_gmlp]; o0 += w_gmlp
        p_x = proj[:, o0:o0 + w_pool]

        z = a_c * a_x
        z_ext = jnp.concatenate([hist["z"], z], axis=0)
        conv = z_ext * cw[CONV_WIDTH - 1:CONV_WIDTH, :]
        for k in range(1, CONV_WIDTH):
            conv = conv + pltpu.roll(z_ext, k, 0) * cw[CONV_WIDTH - 1 - k:CONV_WIDTH - k, :]
        y_a = a_b * conv[CONV_HIST:, :]
        hist["z"] = z[SUB_ROWS - CONV_HIST:, :]

        v_n = _layer_norm(g_v, sgu_g, sgu_b)
        s_cols = []
        for p in range(n_heads // 2):
            cols = [v_n[c * CHUNK:(c + 1) * CHUNK, p * LANES:(p + 1) * LANES] for c in range(n_ch)]
            rhs = jnp.concatenate(
                [jnp.concatenate([jnp.where(low_half, col, 0.0) for col in cols], axis=1),
                 jnp.concatenate([jnp.where(low_half, 0.0, col) for col in cols], axis=1)],
                axis=0).astype(BF16)
            s_cols.append(_dot(ws_m[p], rhs))

        run = jnp.concatenate([hist["p"], p_x], axis=0)
        sums = []
        for w in POOL_WINDOWS:
            run = run + pltpu.roll(run, w // 2, 0)
            sums.append(run[POOL_HIST:, :])
        count = jnp.minimum(window, t * rows + j * SUB_ROWS + row_id + 1).astype(F32)
        d = (_select_by_group(sums, lane_group) / count - p_x).astype(BF16)
        y_c_raw = _dot(d, pw_ref[...])
        hist["p"] = p_x[SUB_ROWS - POOL_HIST:, :]
        mids[j] = (y_a, g_u, s_cols, y_c_raw)

    def mix_out(j):
        y_a, g_u, s_cols, y_c_raw = mids.pop(j)
        s = jnp.concatenate(
            [jnp.concatenate([sc[:, c * LANES:(c + 1) * LANES] for sc in s_cols], axis=1) + bias
             for c in range(n_ch)], axis=0)
        mix = jnp.concatenate([y_a, g_u * s, y_c_raw * pool_scale], axis=1).astype(BF16)
        out = jnp.concatenate([_dot(mix, w[...]) for w in wout_refs], axis=1)
        x = x_ref[0, j * SUB_ROWS:(j + 1) * SUB_ROWS, :]
        x_mid[j] = _layer_norm(alpha * x + gate * out, ln_g, ln_b)

    def store(j, y):
        o_ref[0, j * SUB_ROWS:(j + 1) * SUB_ROWS, :] = y

    _layer_pipeline(n_sub, project, heads, mix_out, x_mid, sh2, sc2, gate2, wup_refs, wdn_refs,
                    ln2_g, ln2_b, alpha, store)

    zc_ref[...] = hist["z"]
    pc_ref[...] = hist["p"]
    ncv_ref[0] = hist["z"][CONV_HIST - (CONV_WIDTH - 1):, :]
    npl_ref[0] = hist["p"][POOL_HIST - (max(POOL_WINDOWS) - 1):, :]


def _layer_prompt_call(x, mod, w_in, conv_w, sgu_g, sgu_b, w_s, bs_full, pool_wbd, pool_scale,
                       w_out, ln_g, ln_b, wup, wdn, ln2_g, ln2_b, next_weights, *, alpha, layer,
                       rows):
    batch, seq, c_dim = x.shape
    n_steps = batch * (seq // rows)
    w_conv, w_pool = conv_w.shape[2], pool_scale.shape[1]
    conv_buf = CONV_WIDTH - 1
    pool_buf = max(POOL_WINDOWS) - 1
    in_specs = _column_blocks(w_in, WEIGHT_COLS)
    out_specs = _column_blocks(w_out, WEIGHT_COLS)
    up_specs = _column_blocks(wup, WEIGHT_COLS)
    dn_specs = _column_blocks(wdn, WEIGHT_COLS)
    kern = functools.partial(_layer_prompt_kernel, alpha=alpha, layer=layer, n_in=len(in_specs),
                             n_out=len(out_specs), n_up=len(up_specs), n_dn=len(dn_specs),
                             n_next=len(next_weights))
    n_t = seq // rows
    assert all(w.shape[1] % n_steps == 0 for w in next_weights)
    next_in = [pl.BlockSpec((None, w.shape[1] // n_steps, w.shape[2]),
                            lambda b, t: (layer + 1, b * n_t + t, 0)) for w in next_weights]
    next_out = [pl.BlockSpec((w.shape[1] // n_steps, w.shape[2]), lambda b, t: (b * n_t + t, 0))
                for w in next_weights]
    return pl.pallas_call(
        kern,
        grid=(batch, seq // rows),
        in_specs=[
            pl.BlockSpec((1, rows, c_dim), lambda b, t: (b, t, 0)),
            pl.BlockSpec((None, batch, mod.shape[2]), lambda b, t: (layer, 0, 0)),
            *in_specs,
            _whole(conv_w.shape),
            _whole(sgu_g.shape),
            _whole(sgu_b.shape),
            _layer_block(w_s.shape[1:], layer),
            _layer_block(bs_full.shape[1:], layer),
            _layer_block(pool_wbd.shape[1:], layer),
            _whole(pool_scale.shape),
            *out_specs,
            _whole(ln_g.shape),
            _whole(ln_b.shape),
            *up_specs, *dn_specs,
            _whole(ln2_g.shape),
            _whole(ln2_b.shape),
            *next_in,
        ],
        out_specs=[
            pl.BlockSpec((1, rows, c_dim), lambda b, t: (b, t, 0)),
            pl.BlockSpec((1, conv_buf, w_conv), lambda b, t: (b, 0, 0)),
            pl.BlockSpec((1, pool_buf, w_pool), lambda b, t: (b, 0, 0)),
            *next_out,
        ],
        out_shape=[
            jax.ShapeDtypeStruct((batch, seq, c_dim), F32),
            jax.ShapeDtypeStruct((batch, conv_buf, w_conv), F32),
            jax.ShapeDtypeStruct((batch, pool_buf, w_pool), F32),
            *[jax.ShapeDtypeStruct(w.shape[1:], BF16) for w in next_weights],
        ],
        scratch_shapes=[
            pltpu.VMEM((CONV_HIST, w_conv), F32),
            pltpu.VMEM((POOL_HIST, w_pool), F32),
        ],
        input_output_aliases={0: 0} if layer > 0 else {},
        compiler_params=pltpu.CompilerParams(
            dimension_semantics=("arbitrary", "arbitrary"),
            vmem_limit_bytes=VMEM_LIMIT_BYTES),
        name="layer_prompt",
    )(x, mod, *([w_in] * len(in_specs)), conv_w, sgu_g, sgu_b, w_s, bs_full, pool_wbd, pool_scale,
      *([w_out] * len(out_specs)), ln_g, ln_b, *([wup] * len(up_specs)),
      *([wdn] * len(dn_specs)), ln2_g, ln2_b, *next_weights)


def _layer_sample_kernel(x_ref, mod_ref, cst_ref, pst_ref, *refs, alpha, layer, n_in, n_out,
                         n_up, n_dn, x_batch_major, out_batch_major):
    win_refs, refs = refs[:n_in], refs[n_in:]
    cw_ref, sg_ref, sb_ref, wst_ref, bst_ref, pw_ref, ps_ref = refs[:7]
    wout_refs, refs = refs[7:7 + n_out], refs[7 + n_out:]
    g_ref, b_ref = refs[:2]
    wup_refs, refs = refs[2:2 + n_up], refs[2 + n_up:]
    wdn_refs, refs = refs[:n_dn], refs[n_dn:]
    g2_ref, b2_ref = refs[:2]
    o_ref, ncv_ref, npl_ref, v_ref = refs[len(refs) - 4:]
    xt = jnp.swapaxes(x_ref[...], 0, 1) if x_batch_major else None
    cst = jnp.swapaxes(cst_ref[...], 0, 1)
    pst = jnp.swapaxes(pst_ref[...], 0, 1)
    t_dim, b_dim = v_ref.shape[1], v_ref.shape[0]
    c_dim = x_ref.shape[2]
    w_conv, w_gmlp, w_pool = cw_ref.shape[2], sg_ref.shape[1], ps_ref.shape[1]
    n_valid = pst.shape[0]
    step = SUB_ROWS // b_dim
    n_sub = t_dim // step

    m = mod_ref[...]
    sh, sc1, gate = m[:, :c_dim], 1.0 + m[:, c_dim:2 * c_dim], m[:, 2 * c_dim:3 * c_dim]
    sh2, sc2, gate2 = (m[:, 3 * c_dim:4 * c_dim], 1.0 + m[:, 4 * c_dim:5 * c_dim],
                       m[:, 5 * c_dim:])
    cw = cw_ref[layer]
    sgu_g, sgu_b = sg_ref[layer:layer + 1, :], sb_ref[layer:layer + 1, :]
    pool_scale = ps_ref[layer:layer + 1, :]
    ln_g, ln_b = g_ref[layer:layer + 1, :], b_ref[layer:layer + 1, :]
    ln2_g, ln2_b = g2_ref[layer:layer + 1, :], b2_ref[layer:layer + 1, :]
    lane_group = _lane_group((b_dim, w_pool), len(POOL_WINDOWS))

    z_ext = [cst[i] for i in range(CONV_WIDTH - 1)]
    p_ext = [pst[i] for i in range(n_valid)]
    v_all = []
    projs, mids, x_mid, outs = {}, {}, {}, {}

    def x_rows(j):
        if x_batch_major:
            return xt[j * step:(j + 1) * step]
        return x_ref[j * step:(j + 1) * step]

    def project(j):
        h = (x_rows(j) * sc1 + sh).reshape(step * b_dim, c_dim).astype(BF16)
        projs[j] = jnp.concatenate([_dot(h, w[...]) for w in win_refs], axis=1).reshape(
            step, b_dim, -1)

    def heads(j):
        proj = projs.pop(j)
        o0 = 0
        a_b = proj[:, :, o0:o0 + w_conv]; o0 += w_conv
        a_c = proj[:, :, o0:o0 + w_conv]; o0 += w_conv
        a_x = proj[:, :, o0:o0 + w_conv]; o0 += w_conv
        g_u = proj[:, :, o0:o0 + w_gmlp]; o0 += w_gmlp
        g_v = proj[:, :, o0:o0 + w_gmlp]; o0 += w_gmlp
        p_x = proj[:, :, o0:o0 + w_pool]
        z = a_c * a_x
        v_n = _layer_norm(g_v, sgu_g, sgu_b)
        y_a, y_b, d = [], [], []
        for i in range(step):
            pos = j * step + i
            z_ext.append(z[i])
            conv = z_ext[pos] * cw[0:1, :]
            for k in range(1, CONV_WIDTH):
                conv = conv + z_ext[pos + k] * cw[k:k + 1, :]
            y_a.append(a_b[i] * conv)
            v_all.append(v_n[i])
            s = bst_ref[pos:pos + 1, :] + wst_ref[pos, 0:1, :] * v_all[0]
            for q in range(1, pos + 1):
                s = s + wst_ref[pos, q:q + 1, :] * v_all[q]
            y_b.append(g_u[i] * s)
            p_ext.append(p_x[i])
            last = n_valid + pos
            means, run, k = [], None, 0
            for w in POOL_WINDOWS:
                while k < w:
                    run = p_ext[last - k] if run is None else run + p_ext[last - k]
                    k += 1
                means.append(run / float(min(w, pos + 1 + n_valid)))
            d.append(_select_by_group(means, lane_group) - p_x[i])
        d = jnp.stack(d).reshape(step * b_dim, w_pool).astype(BF16)
        mids[j] = (jnp.stack(y_a), jnp.stack(y_b), _dot(d, pw_ref[...]))

    def mix_out(j):
        y_a, y_b, y_c_raw = mids.pop(j)
        y_c = (y_c_raw * pool_scale).reshape(step, b_dim, w_pool)
        mix = jnp.concatenate([y_a, y_b, y_c], axis=2).reshape(step * b_dim, c_dim).astype(BF16)
        out = jnp.concatenate([_dot(mix, w[...]) for w in wout_refs], axis=1)
        x_mid[j] = _layer_norm(alpha * x_rows(j) + gate * out.reshape(step, b_dim, c_dim),
                               ln_g, ln_b)

    def store(j, y):
        if out_batch_major:
            outs[j] = y
        else:
            o_ref[j * step:(j + 1) * step] = y

    _layer_pipeline(n_sub, project, heads, mix_out, x_mid, sh2, sc2, gate2, wup_refs, wdn_refs,
                    ln2_g, ln2_b, alpha, store)

    if out_batch_major:
        o_ref[...] = jnp.swapaxes(jnp.concatenate([outs[j] for j in range(n_sub)], axis=0), 0, 1)
    ncv_ref[...] = jnp.swapaxes(jnp.stack(z_ext[len(z_ext) - (CONV_WIDTH - 1):]), 0, 1)
    npl_ref[...] = jnp.swapaxes(jnp.stack(p_ext[len(p_ext) - n_valid:]), 0, 1)
    v_ref[...] = jnp.swapaxes(jnp.stack(v_all), 0, 1)


def _layer_sample_call(x, mod, cst, pst, w_in, conv_w, sgu_g, sgu_b, wst, bs_full, pool_wbd,
                       pool_scale, w_out, ln_g, ln_b, wup, wdn, ln2_g, ln2_b, states, *, alpha,
                       layer, x_batch_major, out_batch_major):
    if x_batch_major:
        b_dim, t_dim, c_dim = x.shape
    else:
        t_dim, b_dim, c_dim = x.shape
    w_gmlp = sgu_g.shape[1]
    in_specs = _column_blocks(w_in, WEIGHT_COLS)
    out_specs = _column_blocks(w_out, WEIGHT_COLS)
    up_specs = _column_blocks(wup, WEIGHT_COLS)
    dn_specs = _column_blocks(wdn, WEIGHT_COLS)
    kern = functools.partial(_layer_sample_kernel, alpha=alpha, layer=layer, n_in=len(in_specs),
                             n_out=len(out_specs), n_up=len(up_specs), n_dn=len(dn_specs),
                             x_batch_major=x_batch_major, out_batch_major=out_batch_major)
    out_shape = (b_dim, t_dim, c_dim) if out_batch_major else (t_dim, b_dim, c_dim)
    state_shapes = [cst.shape, pst.shape, (cst.shape[0], b_dim, t_dim, w_gmlp)]
    whole3 = lambda i: (0, 0, 0)
    inputs = (x, mod, cst, pst, *([w_in] * len(in_specs)), conv_w, sgu_g, sgu_b, wst, bs_full,
              pool_wbd, pool_scale, *([w_out] * len(out_specs)), ln_g, ln_b,
              *([wup] * len(up_specs)), *([wdn] * len(dn_specs)), ln2_g, ln2_b)
    return pl.pallas_call(
        kern,
        grid=(1,),
        in_specs=[
            pl.BlockSpec(x.shape, whole3),
            pl.BlockSpec((None, b_dim, mod.shape[2]), lambda i: (layer, 0, 0)),
            _layer_block(cst.shape[1:], layer),
            _layer_block(pst.shape[1:], layer),
            *in_specs,
            _whole(conv_w.shape),
            _whole(sgu_g.shape),
            _whole(sgu_b.shape),
            _layer_block(wst.shape[1:], layer),
            pl.BlockSpec((None, t_dim, w_gmlp), lambda i: (layer, 0, 0)),
            _layer_block(pool_wbd.shape[1:], layer),
            _whole(pool_scale.shape),
            *out_specs,
            _whole(ln_g.shape),
            _whole(ln_b.shape),
            *up_specs, *dn_specs,
            _whole(ln2_g.shape),
            _whole(ln2_b.shape),
            *[pl.BlockSpec(memory_space=pl.ANY) for _ in states],
        ],
        out_specs=[pl.BlockSpec(out_shape, whole3)] + [
            pl.BlockSpec((None,) + shape[1:], lambda i: (layer, 0, 0, 0)) for shape in state_shapes],
        out_shape=[jax.ShapeDtypeStruct(out_shape, F32)] + [
            jax.ShapeDtypeStruct(shape, F32) for shape in state_shapes],
        input_output_aliases={len(inputs) + k: 1 + k for k in range(len(states))},
        compiler_params=pltpu.CompilerParams(
            dimension_semantics=("arbitrary",),
            vmem_limit_bytes=VMEM_LIMIT_BYTES),
        name="layer_sample",
    )(*inputs, *states)


def kernel(x_prompt, x_sample, c_prompt, c_sample, state_conv, state_pool, w_ada, b_ada, w_in,
           conv_w, sgu_g, sgu_b, w_s, b_s, pool_w, pool_scale, w_out, ln1_g, ln1_b, w_up,
           w_down, ln2_g, ln2_b):
    depth = w_in.shape[0]
    seq = x_prompt.shape[1]
    dec_batch, dec_seq, _ = x_sample.shape
    n_groups, group_dim, _ = pool_w.shape[1:]
    w_pool = n_groups * group_dim
    alpha = (2.0 * depth) ** 0.25
    assert seq % PROMPT_ROWS == 0 and PROMPT_ROWS % SUB_ROWS == 0 and SUB_ROWS % CHUNK == 0
    assert (dec_seq * dec_batch) % SUB_ROWS == 0 and SUB_ROWS % dec_batch == 0

    weights = (w_in, w_out, w_up, w_down)
    mod_p, mod_s, *weights_b = _ada_call(c_prompt, c_sample, w_ada, b_ada, weights)

    bs_full = jnp.repeat(jnp.swapaxes(b_s, 1, 2), HEAD_DIM, axis=2)
    ws_small = jnp.repeat(jnp.transpose(w_s[:, :, :dec_seq, :dec_seq], (0, 2, 3, 1)), HEAD_DIM, axis=3)
    eye = jnp.eye(n_groups, dtype=F32)
    pool_wbd = (eye[None, :, None, :, None] * pool_w[:, :, :, None, :]).reshape(
        depth, w_pool, w_pool).astype(BF16)

    xp, xs = x_prompt, x_sample
    conv_p, pool_p, states_s = [], [], ()
    for l in range(depth):
        w_in_b, w_out_b, wup_b, wdn_b = weights_b
        xp, ncv, npl, *weights_b = _layer_prompt_call(
            xp, mod_p, w_in_b, conv_w, sgu_g, sgu_b, w_s, bs_full, pool_wbd, pool_scale,
            w_out_b, ln1_g, ln1_b, wup_b, wdn_b, ln2_g, ln2_b,
            weights if l + 1 < depth else (), alpha=alpha, layer=l, rows=PROMPT_ROWS)
        conv_p.append(ncv)
        pool_p.append(npl)

        xp, xs = lax.optimization_barrier((xp, xs))
        xs, *states_s = _layer_sample_call(
            xs, mod_s, state_conv, state_pool, w_in_b, conv_w, sgu_g, sgu_b, ws_small, bs_full,
            pool_wbd, pool_scale, w_out_b, ln1_g, ln1_b, wup_b, wdn_b, ln2_g, ln2_b, states_s,
            alpha=alpha, layer=l, x_batch_major=(l == 0), out_batch_major=(l == depth - 1))
        xp, xs = lax.optimization_barrier((xp, xs))

    return (xp, xs, jnp.stack(conv_p), jnp.stack(pool_p), *states_s)
```

```python
import functools

import jax
import jax.numpy as jnp
from jax import lax
from jax.experimental import pallas as pl
from jax.experimental.pallas import tpu as pltpu

F32 = jnp.float32
BF16 = jnp.bfloat16

LN_EPS = 1e-5
HEAD_DIM = 64
CHUNK = 128
POOL_WINDOWS = (2, 4, 8, 16)
CONV_WIDTH = 3
LANES = 128
SUBLANES = 8
VMEM_LIMIT_BYTES = 60 * 1024 * 1024

PROMPT_ROWS = 1024
SUB_ROWS = 256
UP_AHEAD = 3
PROJ_AHEAD = 2
MIX_OUT_AT = 2
HEADS_AT = 5
WEIGHT_COLS = 512
CONV_HIST = SUBLANES
POOL_HIST = 2 * SUBLANES


def _layer_norm(y, g, b):
    mu = jnp.mean(y, axis=-1, keepdims=True)
    d = y - mu
    var = jnp.mean(d * d, axis=-1, keepdims=True)
    return d * lax.rsqrt(var + LN_EPS) * g + b


def _dot(a, b):
    return jnp.dot(a, b, preferred_element_type=F32)


def _layer_block(shape, layer):
    zeros = (0,) * len(shape)
    return pl.BlockSpec((None,) + tuple(shape), lambda *_: (layer,) + zeros,
                        pipeline_mode=pl.Buffered(1))


def _whole(shape):
    zeros = (0,) * len(shape)
    return pl.BlockSpec(tuple(shape), lambda *_: zeros, pipeline_mode=pl.Buffered(1))


def _column_blocks(w, width):
    rows, cols = w.shape
    return [pl.BlockSpec((rows, width), lambda *_, n=n: (0, n), pipeline_mode=pl.Buffered(1))
            for n in range(cols // width)]


def _lane_group(shape, n_groups):
    return lax.broadcasted_iota(jnp.int32, shape, len(shape) - 1) // (shape[-1] // n_groups)


def _select_by_group(values, lane_group):
    out = values[-1]
    for g in range(len(values) - 2, -1, -1):
        out = jnp.where(lane_group == g, values[g], out)
    return out


def _ada_kernel(cp_ref, cs_ref, w_ref, b_ref, *refs):
    n_side = (len(refs) - 2) // 2
    side_f32, (op_ref, os_ref), side_bf16 = refs[:n_side], refs[n_side:n_side + 2], refs[n_side + 2:]
    for src, dst in zip(side_f32, side_bf16):
        dst[...] = src[0].astype(BF16)
    l, k = pl.program_id(0), pl.program_id(1)
    bp = cp_ref.shape[0]
    c = jnp.concatenate([cp_ref[...], cs_ref[...]], axis=0)
    part = _dot(jax.nn.silu(c).astype(BF16), w_ref[0].astype(BF16))
    part_p, part_s = part[:bp], part[bp:]

    @pl.when(k == 0)
    def _():
        bias = b_ref[pl.ds(l, 1), :]
        op_ref[0] = part_p + bias
        os_ref[0] = part_s + bias

    @pl.when(k > 0)
    def _():
        op_ref[0] += part_p
        os_ref[0] += part_s


def _ada_call(c_prompt, c_sample, w_ada, b_ada, weights, row_tile=256):
    depth, d_model, n_mod = w_ada.shape
    bp, bs = c_prompt.shape[0], c_sample.shape[0]
    n_k = d_model // row_tile
    n_steps = depth * n_k
    assert all(w.shape[1] % n_steps == 0 for w in weights)
    step = lambda l, k: l * n_k + k
    return pl.pallas_call(
        _ada_kernel,
        grid=(depth, n_k),
        in_specs=[
            pl.BlockSpec((bp, row_tile), lambda l, k: (0, k)),
            pl.BlockSpec((bs, row_tile), lambda l, k: (0, k)),
            pl.BlockSpec((1, row_tile, n_mod), lambda l, k: (l, k, 0)),
            _whole(b_ada.shape),
            *[pl.BlockSpec((1, w.shape[1] // n_steps, w.shape[2]), lambda l, k: (0, step(l, k), 0))
              for w in weights],
        ],
        out_specs=[
            pl.BlockSpec((1, bp, n_mod), lambda l, k: (l, 0, 0)),
            pl.BlockSpec((1, bs, n_mod), lambda l, k: (l, 0, 0)),
            *[pl.BlockSpec((w.shape[1] // n_steps, w.shape[2]), lambda l, k: (step(l, k), 0))
              for w in weights],
        ],
        out_shape=[
            jax.ShapeDtypeStruct((depth, bp, n_mod), F32),
            jax.ShapeDtypeStruct((depth, bs, n_mod), F32),
            *[jax.ShapeDtypeStruct(w.shape[1:], BF16) for w in weights],
        ],
        compiler_params=pltpu.CompilerParams(
            dimension_semantics=("arbitrary", "arbitrary"),
            vmem_limit_bytes=VMEM_LIMIT_BYTES),
        name="ada_mod",
    )(c_prompt, c_sample, w_ada, b_ada, *weights)


def _layer_pipeline(n_sub, project, heads, mix_out, x_mid, sh2, sc2, gate2, wup_refs, wdn_refs,
                    ln2_g, ln2_b, alpha, store):
    n_up, n_dn = len(wup_refs), len(wdn_refs)
    chunk = wup_refs[0].shape[1]

    def heads_and_project(j):
        heads(j)
        if j + PROJ_AHEAD < n_sub:
            project(j + PROJ_AHEAD)

    for jj in range(min(PROJ_AHEAD, n_sub)):
        project(jj)
    heads_and_project(0)
    mix_out(0)
    if n_sub > 1:
        heads_and_project(1)

    stages = [(j, k) for j in range(n_sub) for k in range(n_up)]
    hs, ups = {}, {}

    def issue_up(s):
        j, k = stages[s]
        if j not in hs:
            xm = x_mid[j]
            hs[j] = (xm * sc2 + sh2).reshape(-1, xm.shape[-1]).astype(BF16)
        ups[s] = _dot(hs[j], wup_refs[k][...])

    for s in range(min(UP_AHEAD, len(stages))):
        issue_up(s)
    acc = [None] * n_dn
    for s, (j, k) in enumerate(stages):
        if k == MIX_OUT_AT and j + 1 < n_sub:
            mix_out(j + 1)
        if k == HEADS_AT and j + 2 < n_sub:
            heads_and_project(j + 2)
        if s + UP_AHEAD < len(stages):
            issue_up(s + UP_AHEAD)
        a = jnp.square(jnp.maximum(ups.pop(s), 0.0)).astype(BF16)
        for n in range(n_dn):
            part = _dot(a, wdn_refs[n][k * chunk:(k + 1) * chunk, :])
            acc[n] = part if acc[n] is None else acc[n] + part
        if k == n_up - 1:
            xm = x_mid.pop(j)
            f = jnp.concatenate(acc, axis=1).reshape(xm.shape)
            store(j, _layer_norm(alpha * xm + gate2 * f, ln2_g, ln2_b))
            acc = [None] * n_dn


def _layer_prompt_kernel(x_ref, mod_ref, *refs, alpha, layer, n_in, n_out, n_up, n_dn, n_next):
    win_refs, refs = refs[:n_in], refs[n_in:]
    cw_ref, sg_ref, sb_ref, ws_ref, bs_ref, pw_ref, ps_ref = refs[:7]
    wout_refs, refs = refs[7:7 + n_out], refs[7 + n_out:]
    g_ref, b_ref = refs[:2]
    wup_refs, refs = refs[2:2 + n_up], refs[2 + n_up:]
    wdn_refs, refs = refs[:n_dn], refs[n_dn:]
    g2_ref, b2_ref = refs[:2]
    next_f32, refs = refs[2:2 + n_next], refs[2 + n_next:]
    o_ref, ncv_ref, npl_ref = refs[:3]
    next_bf16, (zc_ref, pc_ref) = refs[3:3 + n_next], refs[3 + n_next:]
    for src, dst in zip(next_f32, next_bf16):
        dst[...] = src[...].astype(BF16)
    bi, t = pl.program_id(0), pl.program_id(1)
    rows, c_dim = x_ref.shape[1], x_ref.shape[2]
    w_conv, w_gmlp, w_pool = cw_ref.shape[2], sg_ref.shape[1], ps_ref.shape[1]
    n_heads = ws_ref.shape[0]

    @pl.when(t == 0)
    def _():
        zc_ref[...] = jnp.zeros(zc_ref.shape, F32)
        pc_ref[...] = jnp.zeros(pc_ref.shape, F32)

    m = mod_ref[pl.ds(bi, 1), :]
    sh, sc1, gate = m[:, :c_dim], 1.0 + m[:, c_dim:2 * c_dim], m[:, 2 * c_dim:3 * c_dim]
    sh2, sc2, gate2 = (m[:, 3 * c_dim:4 * c_dim], 1.0 + m[:, 4 * c_dim:5 * c_dim],
                       m[:, 5 * c_dim:])
    cw = cw_ref[layer]
    sgu_g, sgu_b = sg_ref[layer:layer + 1, :], sb_ref[layer:layer + 1, :]
    pool_scale = ps_ref[layer:layer + 1, :]
    ln_g, ln_b = g_ref[layer:layer + 1, :], b_ref[layer:layer + 1, :]
    ln2_g, ln2_b = g2_ref[layer:layer + 1, :], b2_ref[layer:layer + 1, :]

    row_i = lax.broadcasted_iota(jnp.int32, (CHUNK, 2 * CHUNK), 0)
    col_j = lax.broadcasted_iota(jnp.int32, (CHUNK, 2 * CHUNK), 1) % CHUNK
    causal = col_j <= row_i
    ws_m = []
    for p in range(n_heads // 2):
        pair = jnp.concatenate([ws_ref[2 * p], ws_ref[2 * p + 1]], axis=1)
        ws_m.append(jnp.where(causal, pair, 0.0).astype(BF16))
    low_half = lax.broadcasted_iota(jnp.int32, (CHUNK, LANES), 1) < HEAD_DIM
    lane_group = _lane_group((SUB_ROWS, w_pool), len(POOL_WINDOWS))
    window = _select_by_group([jnp.full((SUB_ROWS, w_pool), w, jnp.int32) for w in POOL_WINDOWS],
                              lane_group)
    row_id = lax.broadcasted_iota(jnp.int32, (SUB_ROWS, w_pool), 0)
    bias = bs_ref[...]
    n_sub = rows // SUB_ROWS
    n_ch = SUB_ROWS // CHUNK
    hist = {"z": zc_ref[...], "p": pc_ref[...]}
    projs, mids, x_mid = {}, {}, {}

    def project(jj):
        h = (x_ref[0, jj * SUB_ROWS:(jj + 1) * SUB_ROWS, :] * sc1 + sh).astype(BF16)
        projs[jj] = jnp.concatenate([_dot(h, w[...]) for w in win_refs], axis=1)

    def heads(j):
        proj = projs.pop(j)
        o0 = 0
        a_b = proj[:, o0:o0 + w_conv]; o0 += w_conv
        a_c = proj[:, o0:o0 + w_conv]; o0 += w_conv
        a_x = proj[:, o0:o0 + w_conv]; o0 += w_conv
        g_u = proj[:, o0:o0 + w_gmlp]; o0 += w_gmlp
        g_v = proj[:, o0:o0 + w_gmlp]; o0 += w_gmlp
        p_x = proj[:, o0:o0 + w_pool]

        z = a_c * a_x
        z_ext = jnp.concatenate([hist["z"], z], axis=0)
        conv = z_ext * cw[CONV_WIDTH - 1:CONV_WIDTH, :]
        for k in range(1, CONV_WIDTH):
            conv = conv + pltpu.roll(z_ext, k, 0) * cw[CONV_WIDTH - 1 - k:CONV_WIDTH - k, :]
        y_a = a_b * conv[CONV_HIST:, :]
        hist["z"] = z[SUB_ROWS - CONV_HIST:, :]

        v_n = _layer_norm(g_v, sgu_g, sgu_b)
        s_cols = []
        for p in range(n_heads // 2):
            cols = [v_n[c * CHUNK:(c + 1) * CHUNK, p * LANES:(p + 1) * LANES] for c in range(n_ch)]
            rhs = jnp.concatenate(
                [jnp.concatenate([jnp.where(low_half, col, 0.0) for col in cols], axis=1),
                 jnp.concatenate([jnp.where(low_half, 0.0, col) for col in cols], axis=1)],
                axis=0).astype(BF16)
            s_cols.append(_dot(ws_m[p], rhs))

        run = jnp.concatenate([hist["p"], p_x], axis=0)
        sums = []
        for w in POOL_WINDOWS:
            run = run + pltpu.roll(run, w // 2, 0)
            sums.append(run[POOL_HIST:, :])
        count = jnp.minimum(window, t * rows + j * SUB_ROWS + row_id + 1).astype(F32)
        d = (_select_by_group(sums, lane_group) / count - p_x).astype(BF16)
        y_c_raw = _dot(d, pw_ref[...])
        hist["p"] = p_x[SUB_ROWS - POOL_HIST:, :]
        mids[j] = (y_a, g_u, s_cols, y_c_raw)

    def mix_out(j):
        y_a, g_u, s_cols, y_c_raw = mids.pop(j)
        s = jnp.concatenate(
            [jnp.concatenate([sc[:, c * LANES:(c + 1) * LANES] for sc in s_cols], axis=1) + bias
             for c in range(n_ch)], axis=0)
        mix = jnp.concatenate([y_a, g_u * s, y_c_raw * pool_scale], axis=1).astype(BF16)
        out = jnp.concatenate([_dot(mix, w[...]) for w in wout_refs], axis=1)
        x = x_ref[0, j * SUB_ROWS:(j + 1) * SUB_ROWS, :]
        x_mid[j] = _layer_norm(alpha * x + gate * out, ln_g, ln_b)

    def store(j, y):
        o_ref[0, j * SUB_ROWS:(j + 1) * SUB_ROWS, :] = y

    _layer_pipeline(n_sub, project, heads, mix_out, x_mid, sh2, sc2, gate2, wup_refs, wdn_refs,
                    ln2_g, ln2_b, alpha, store)

    zc_ref[...] = hist["z"]
    pc_ref[...] = hist["p"]
    ncv_ref[0] = hist["z"][CONV_HIST - (CONV_WIDTH - 1):, :]
    npl_ref[0] = hist["p"][POOL_HIST - (max(POOL_WINDOWS) - 1):, :]


def _layer_prompt_call(x, mod, w_in, conv_w, sgu_g, sgu_b, w_s, bs_full, pool_wbd, pool_scale,
                       w_out, ln_g, ln_b, wup, wdn, ln2_g, ln2_b, next_weights, *, alpha, layer,
                       rows):
    batch, seq, c_dim = x.shape
    n_steps = batch * (seq // rows)
    w_conv, w_pool = conv_w.shape[2], pool_scale.shape[1]
    conv_buf = CONV_WIDTH - 1
    pool_buf = max(POOL_WINDOWS) - 1
    in_specs = _column_blocks(w_in, WEIGHT_COLS)
    out_specs = _column_blocks(w_out, WEIGHT_COLS)
    up_specs = _column_blocks(wup, WEIGHT_COLS)
    dn_specs = _column_blocks(wdn, WEIGHT_COLS)
    kern = functools.partial(_layer_prompt_kernel, alpha=alpha, layer=layer, n_in=len(in_specs),
                             n_out=len(out_specs), n_up=len(up_specs), n_dn=len(dn_specs),
                             n_next=len(next_weights))
    n_t = seq // rows
    assert all(w.shape[1] % n_steps == 0 for w in next_weights)
    next_in = [pl.BlockSpec((None, w.shape[1] // n_steps, w.shape[2]),
                            lambda b, t: (layer + 1, b * n_t + t, 0)) for w in next_weights]
    next_out = [pl.BlockSpec((w.shape[1] // n_steps, w.shape[2]), lambda b, t: (b * n_t + t, 0))
                for w in next_weights]
    return pl.pallas_call(
        kern,
        grid=(batch, seq // rows),
        in_specs=[
            pl.BlockSpec((1, rows, c_dim), lambda b, t: (b, t, 0)),
            pl.BlockSpec((None, batch, mod.shape[2]), lambda b, t: (layer, 0, 0)),
            *in_specs,
            _whole(conv_w.shape),
            _whole(sgu_g.shape),
            _whole(sgu_b.shape),
            _layer_block(w_s.shape[1:], layer),
            _layer_block(bs_full.shape[1:], layer),
            _layer_block(pool_wbd.shape[1:], layer),
            _whole(pool_scale.shape),
            *out_specs,
            _whole(ln_g.shape),
            _whole(ln_b.shape),
            *up_specs, *dn_specs,
            _whole(ln2_g.shape),
            _whole(ln2_b.shape),
            *next_in,
        ],
        out_specs=[
            pl.BlockSpec((1, rows, c_dim), lambda b, t: (b, t, 0)),
            pl.BlockSpec((1, conv_buf, w_conv), lambda b, t: (b, 0, 0)),
            pl.BlockSpec((1, pool_buf, w_pool), lambda b, t: (b, 0, 0)),
            *next_out,
        ],
        out_shape=[
            jax.ShapeDtypeStruct((batch, seq, c_dim), F32),
            jax.ShapeDtypeStruct((batch, conv_buf, w_conv), F32),
            jax.ShapeDtypeStruct((batch, pool_buf, w_pool), F32),
            *[jax.ShapeDtypeStruct(w.shape[1:], BF16) for w in next_weights],
        ],
        scratch_shapes=[
            pltpu.VMEM((CONV_HIST, w_conv), F32),
            pltpu.VMEM((POOL_HIST, w_pool), F32),
        ],
        input_output_aliases={0: 0} if layer > 0 else {},
        compiler_params=pltpu.CompilerParams(
            dimension_semantics=("arbitrary", "arbitrary"),
            vmem_limit_bytes=VMEM_LIMIT_BYTES),
        name="layer_prompt",
    )(x, mod, *([w_in] * len(in_specs)), conv_w, sgu_g, sgu_b, w_s, bs_full, pool_wbd, pool_scale,
      *([w_out] * len(out_specs)), ln_g, ln_b, *([wup] * len(up_specs)),
      *([wdn] * len(dn_specs)), ln2_g, ln2_b, *next_weights)


def _layer_sample_kernel(x_ref, mod_ref, cst_ref, pst_ref, *refs, alpha, layer, n_in, n_out,
                         n_up, n_dn, x_batch_major, out_batch_major):
    win_refs, refs = refs[:n_in], refs[n_in:]
    cw_ref, sg_ref, sb_ref, wst_ref, bst_ref, pw_ref, ps_ref = refs[:7]
    wout_refs, refs = refs[7:7 + n_out], refs[7 + n_out:]
    g_ref, b_ref = refs[:2]
    wup_refs, refs = refs[2:2 + n_up], refs[2 + n_up:]
    wdn_refs, refs = refs[:n_dn], refs[n_dn:]
    g2_ref, b2_ref = refs[:2]
    o_ref, ncv_ref, npl_ref, v_ref = refs[len(refs) - 4:]
    xt = jnp.swapaxes(x_ref[...], 0, 1) if x_batch_major else None
    cst = jnp.swapaxes(cst_ref[...], 0, 1)
    pst = jnp.swapaxes(pst_ref[...], 0, 1)
    t_dim, b_dim = v_ref.shape[1], v_ref.shape[0]
    c_dim = x_ref.shape[2]
    w_conv, w_gmlp, w_pool = cw_ref.shape[2], sg_ref.shape[1], ps_ref.shape[1]
    n_valid = pst.shape[0]
    step = SUB_ROWS // b_dim
    n_sub = t_dim // step

    m = mod_ref[...]
    sh, sc1, gate = m[:, :c_dim], 1.0 + m[:, c_dim:2 * c_dim], m[:, 2 * c_dim:3 * c_dim]
    sh2, sc2, gate2 = (m[:, 3 * c_dim:4 * c_dim], 1.0 + m[:, 4 * c_dim:5 * c_dim],
                       m[:, 5 * c_dim:])
    cw = cw_ref[layer]
    sgu_g, sgu_b = sg_ref[layer:layer + 1, :], sb_ref[layer:layer + 1, :]
    pool_scale = ps_ref[layer:layer + 1, :]
    ln_g, ln_b = g_ref[layer:layer + 1, :], b_ref[layer:layer + 1, :]
    ln2_g, ln2_b = g2_ref[layer:layer + 1, :], b2_ref[layer:layer + 1, :]
    lane_group = _lane_group((b_dim, w_pool), len(POOL_WINDOWS))

    z_ext = [cst[i] for i in range(CONV_WIDTH - 1)]
    p_ext = [pst[i] for i in range(n_valid)]
    v_all = []
    projs, mids, x_mid, outs = {}, {}, {}, {}

    def x_rows(j):
        if x_batch_major:
            return xt[j * step:(j + 1) * step]
        return x_ref[j * step:(j + 1) * step]

    def project(j):
        h = (x_rows(j) * sc1 + sh).reshape(step * b_dim, c_dim).astype(BF16)
        projs[j] = jnp.concatenate([_dot(h, w[...]) for w in win_refs], axis=1).reshape(
            step, b_dim, -1)

    def heads(j):
        proj = projs.pop(j)
        o0 = 0
        a_b = proj[:, :, o0:o0 + w_conv]; o0 += w_conv
        a_c = proj[:, :, o0:o0 + w_conv]; o0 += w_conv
        a_x = proj[:, :, o0:o0 + w_conv]; o0 += w_conv
        g_u = proj[:, :, o0:o0 + w_gmlp]; o0 += w_gmlp
        g_v = proj[:, :, o0:o0 + w_gmlp]; o0 += w_gmlp
        p_x = proj[:, :, o0:o0 + w_pool]
        z = a_c * a_x
        v_n = _layer_norm(g_v, sgu_g, sgu_b)
        y_a, y_b, d = [], [], []
        for i in range(step):
            pos = j * step + i
            z_ext.append(z[i])
            conv = z_ext[pos] * cw[0:1, :]
            for k in range(1, CONV_WIDTH):
                conv = conv + z_ext[pos + k] * cw[k:k + 1, :]
            y_a.append(a_b[i] * conv)
            v_all.append(v_n[i])
            s = bst_ref[pos:pos + 1, :] + wst_ref[pos, 0:1, :] * v_all[0]
            for q in range(1, pos + 1):
                s = s + wst_ref[pos, q:q + 1, :] * v_all[q]
            y_b.append(g_u[i] * s)
            p_ext.append(p_x[i])
            last = n_valid + pos
            means, run, k = [], None, 0
            for w in POOL_WINDOWS:
                while k < w:
                    run = p_ext[last - k] if run is None else run + p_ext[last - k]
                    k += 1
                means.append(run / float(min(w, pos + 1 + n_valid)))
            d.append(_select_by_group(means, lane_group) - p_x[i])
        d = jnp.stack(d).reshape(step * b_dim, w_pool).astype(BF16)
        mids[j] = (jnp.stack(y_a), jnp.stack(y_b), _dot(d, pw_ref[...]))

    def mix_out(j):
        y_a, y_b, y_c_raw = mids.pop(j)
        y_c = (y_c_raw * pool_scale).reshape(step, b_dim, w_pool)
        mix = jnp.concatenate([y_a, y_b, y_c], axis=2).reshape(step * b_dim, c_dim).astype(BF16)
        out = jnp.concatenate([_dot(mix, w[...]) for w in wout_refs], axis=1)
        x_mid[j] = _layer_norm(alpha * x_rows(j) + gate * out.reshape(step, b_dim, c_dim),
                               ln_g, ln_b)

    def store(j, y):
        if out_batch_major:
            outs[j] = y
        else:
            o_ref[j * step:(j + 1) * step] = y

    _layer_pipeline(n_sub, project, heads, mix_out, x_mid, sh2, sc2, gate2, wup_refs, wdn_refs,
                    ln2_g, ln2_b, alpha, store)

    if out_batch_major:
        o_ref[...] = jnp.swapaxes(jnp.concatenate([outs[j] for j in range(n_sub)], axis=0), 0, 1)
    ncv_ref[...] = jnp.swapaxes(jnp.stack(z_ext[len(z_ext) - (CONV_WIDTH - 1):]), 0, 1)
    npl_ref[...] = jnp.swapaxes(jnp.stack(p_ext[len(p_ext) - n_valid:]), 0, 1)
    v_ref[...] = jnp.swapaxes(jnp.stack(v_all), 0, 1)


def _layer_sample_call(x, mod, cst, pst, w_in, conv_w, sgu_g, sgu_b, wst, bs_full, pool_wbd,
                       pool_scale, w_out, ln_g, ln_b, wup, wdn, ln2_g, ln2_b, states, *, alpha,
                       layer, x_batch_major, out_batch_major):
    if x_batch_major:
        b_dim, t_dim, c_dim = x.shape
    else:
        t_dim, b_dim, c_dim = x.shape
    w_gmlp = sgu_g.shape[1]
    in_specs = _column_blocks(w_in, WEIGHT_COLS)
    out_specs = _column_blocks(w_out, WEIGHT_COLS)
    up_specs = _column_blocks(wup, WEIGHT_COLS)
    dn_specs = _column_blocks(wdn, WEIGHT_COLS)
    kern = functools.partial(_layer_sample_kernel, alpha=alpha, layer=layer, n_in=len(in_specs),
                             n_out=len(out_specs), n_up=len(up_specs), n_dn=len(dn_specs),
                             x_batch_major=x_batch_major, out_batch_major=out_batch_major)
    out_shape = (b_dim, t_dim, c_dim) if out_batch_major else (t_dim, b_dim, c_dim)
    state_shapes = [cst.shape, pst.shape, (cst.shape[0], b_dim, t_dim, w_gmlp)]
    whole3 = lambda i: (0, 0, 0)
    inputs = (x, mod, cst, pst, *([w_in] * len(in_specs)), conv_w, sgu_g, sgu_b, wst, bs_full,
              pool_wbd, pool_scale, *([w_out] * len(out_specs)), ln_g, ln_b,
              *([wup] * len(up_specs)), *([wdn] * len(dn_specs)), ln2_g, ln2_b)
    return pl.pallas_call(
        kern,
        grid=(1,),
        in_specs=[
            pl.BlockSpec(x.shape, whole3),
            pl.BlockSpec((None, b_dim, mod.shape[2]), lambda i: (layer, 0, 0)),
            _layer_block(cst.shape[1:], layer),
            _layer_block(pst.shape[1:], layer),
            *in_specs,
            _whole(conv_w.shape),
            _whole(sgu_g.shape),
            _whole(sgu_b.shape),
            _layer_block(wst.shape[1:], layer),
            pl.BlockSpec((None, t_dim, w_gmlp), lambda i: (layer, 0, 0)),
            _layer_block(pool_wbd.shape[1:], layer),
            _whole(pool_scale.shape),
            *out_specs,
            _whole(ln_g.shape),
            _whole(ln_b.shape),
            *up_specs, *dn_specs,
            _whole(ln2_g.shape),
            _whole(ln2_b.shape),
            *[pl.BlockSpec(memory_space=pl.ANY) for _ in states],
        ],
        out_specs=[pl.BlockSpec(out_shape, whole3)] + [
            pl.BlockSpec((None,) + shape[1:], lambda i: (layer, 0, 0, 0)) for shape in state_shapes],
        out_shape=[jax.ShapeDtypeStruct(out_shape, F32)] + [
            jax.ShapeDtypeStruct(shape, F32) for shape in state_shapes],
        input_output_aliases={len(inputs) + k: 1 + k for k in range(len(states))},
        compiler_params=pltpu.CompilerParams(
            dimension_semantics=("arbitrary",),
            vmem_limit_bytes=VMEM_LIMIT_BYTES),
        name="layer_sample",
    )(*inputs, *states)


def kernel(x_prompt, x_sample, c_prompt, c_sample, state_conv, state_pool, w_ada, b_ada, w_in,
           conv_w, sgu_g, sgu_b, w_s, b_s, pool_w, pool_scale, w_out, ln1_g, ln1_b, w_up,
           w_down, ln2_g, ln2_b):
    depth = w_in.shape[0]
    seq = x_prompt.shape[1]
    dec_batch, dec_seq, _ = x_sample.shape
    n_groups, group_dim, _ = pool_w.shape[1:]
    w_pool = n_groups * group_dim
    alpha = (2.0 * depth) ** 0.25
    assert seq % PROMPT_ROWS == 0 and PROMPT_ROWS % SUB_ROWS == 0 and SUB_ROWS % CHUNK == 0
    assert (dec_seq * dec_batch) % SUB_ROWS == 0 and SUB_ROWS % dec_batch == 0

    weights = (w_in, w_out, w_up, w_down)
    mod_p, mod_s, *weights_b = _ada_call(c_prompt, c_sample, w_ada, b_ada, weights)

    bs_full = jnp.repeat(jnp.swapaxes(b_s, 1, 2), HEAD_DIM, axis=2)
    ws_small = jnp.repeat(jnp.transpose(w_s[:, :, :dec_seq, :dec_seq], (0, 2, 3, 1)), HEAD_DIM, axis=3)
    eye = jnp.eye(n_groups, dtype=F32)
    pool_wbd = (eye[None, :, None, :, None] * pool_w[:, :, :, None, :]).reshape(
        depth, w_pool, w_pool).astype(BF16)

    xp, xs = x_prompt, x_sample
    conv_p, pool_p, states_s = [], [], ()
    for l in range(depth):
        w_in_b, w_out_b, wup_b, wdn_b = weights_b
        xp, ncv, npl, *weights_b = _layer_prompt_call(
            xp, mod_p, w_in_b, conv_w, sgu_g, sgu_b, w_s, bs_full, pool_wbd, pool_scale,
            w_out_b, ln1_g, ln1_b, wup_b, wdn_b, ln2_g, ln2_b,
            weights if l + 1 < depth else (), alpha=alpha, layer=l, rows=PROMPT_ROWS)
        conv_p.append(ncv)
        pool_p.append(npl)

        xp, xs = lax.optimization_barrier((xp, xs))
        xs, *states_s = _layer_sample_call(
            xs, mod_s, state_conv, state_pool, w_in_b, conv_w, sgu_g, sgu_b, ws_small, bs_full,
            pool_wbd, pool_scale, w_out_b, ln1_g, ln1_b, wup_b, wdn_b, ln2_g, ln2_b, states_s,
            alpha=alpha, layer=l, x_batch_major=(l == 0), out_batch_major=(l == depth - 1))
        xp, xs = lax.optimization_barrier((xp, xs))

    return (xp, xs, jnp.stack(conv_p), jnp.stack(pool_p), *states_s)
```

```python
import functools

import jax
import jax.numpy as jnp
from jax import lax
from jax.experimental import pallas as pl
from jax.experimental.pallas import tpu as pltpu

F32 = jnp.float32
BF16 = jnp.bfloat16

LN_EPS = 1e-5
HEAD_DIM = 64
CHUNK = 128
POOL_WINDOWS = (2, 4, 8, 16)
CONV_WIDTH = 3
LANES = 128
SUBLANES = 8
VMEM_LIMIT_BYTES = 60 * 1024 * 1024

PROMPT_ROWS = 1024
SUB_ROWS = 256
UP_AHEAD = 3
PROJ_AHEAD = 2
MIX_OUT_AT = 2
HEADS_AT = 4
WEIGHT_COLS = 512
CONV_HIST = SUBLANES
POOL_HIST = 2 * SUBLANES


def _layer_norm(y, g, b):
    mu = jnp.mean(y, axis=-1, keepdims=True)
    d = y - mu
    var = jnp.mean(d * d, axis=-1, keepdims=True)
    return d * lax.rsqrt(var + LN_EPS) * g + b


def _dot(a, b):
    return jnp.dot(a, b, preferred_element_type=F32)


def _layer_block(shape, layer):
    zeros = (0,) * len(shape)
    return pl.BlockSpec((None,) + tuple(shape), lambda *_: (layer,) + zeros,
                        pipeline_mode=pl.Buffered(1))


def _whole(shape):
    zeros = (0,) * len(shape)
    return pl.BlockSpec(tuple(shape), lambda *_: zeros, pipeline_mode=pl.Buffered(1))


def _column_blocks(w, width):
    rows, cols = w.shape
    return [pl.BlockSpec((rows, width), lambda *_, n=n: (0, n), pipeline_mode=pl.Buffered(1))
            for n in range(cols // width)]


def _lane_group(shape, n_groups):
    return lax.broadcasted_iota(jnp.int32, shape, len(shape) - 1) // (shape[-1] // n_groups)


def _select_by_group(values, lane_group):
    out = values[-1]
    for g in range(len(values) - 2, -1, -1):
        out = jnp.where(lane_group == g, values[g], out)
    return out


def _ada_kernel(cp_ref, cs_ref, w_ref, b_ref, *refs):
    n_side = (len(refs) - 2) // 2
    side_f32, (op_ref, os_ref), side_bf16 = refs[:n_side], refs[n_side:n_side + 2], refs[n_side + 2:]
    for src, dst in zip(side_f32, side_bf16):
        dst[...] = src[0].astype(BF16)
    l, k = pl.program_id(0), pl.program_id(1)
    bp = cp_ref.shape[0]
    c = jnp.concatenate([cp_ref[...], cs_ref[...]], axis=0)
    part = _dot(jax.nn.silu(c).astype(BF16), w_ref[0].astype(BF16))
    part_p, part_s = part[:bp], part[bp:]

    @pl.when(k == 0)
    def _():
        bias = b_ref[pl.ds(l, 1), :]
        op_ref[0] = part_p + bias
        os_ref[0] = part_s + bias

    @pl.when(k > 0)
    def _():
        op_ref[0] += part_p
        os_ref[0] += part_s


def _ada_call(c_prompt, c_sample, w_ada, b_ada, weights, row_tile=256):
    depth, d_model, n_mod = w_ada.shape
    bp, bs = c_prompt.shape[0], c_sample.shape[0]
    n_k = d_model // row_tile
    n_steps = depth * n_k
    assert all(w.shape[1] % n_steps == 0 for w in weights)
    step = lambda l, k: l * n_k + k
    return pl.pallas_call(
        _ada_kernel,
        grid=(depth, n_k),
        in_specs=[
            pl.BlockSpec((bp, row_tile), lambda l, k: (0, k)),
            pl.BlockSpec((bs, row_tile), lambda l, k: (0, k)),
            pl.BlockSpec((1, row_tile, n_mod), lambda l, k: (l, k, 0)),
            _whole(b_ada.shape),
            *[pl.BlockSpec((1, w.shape[1] // n_steps, w.shape[2]), lambda l, k: (0, step(l, k), 0))
              for w in weights],
        ],
        out_specs=[
            pl.BlockSpec((1, bp, n_mod), lambda l, k: (l, 0, 0)),
            pl.BlockSpec((1, bs, n_mod), lambda l, k: (l, 0, 0)),
            *[pl.BlockSpec((w.shape[1] // n_steps, w.shape[2]), lambda l, k: (step(l, k), 0))
              for w in weights],
        ],
        out_shape=[
            jax.ShapeDtypeStruct((depth, bp, n_mod), F32),
            jax.ShapeDtypeStruct((depth, bs, n_mod), F32),
            *[jax.ShapeDtypeStruct(w.shape[1:], BF16) for w in weights],
        ],
        compiler_params=pltpu.CompilerParams(
            dimension_semantics=("arbitrary", "arbitrary"),
            vmem_limit_bytes=VMEM_LIMIT_BYTES),
        name="ada_mod",
    )(c_prompt, c_sample, w_ada, b_ada, *weights)


def _layer_pipeline(n_sub, project, heads, mix_out, x_mid, sh2, sc2, gate2, wup_refs, wdn_refs,
                    ln2_g, ln2_b, alpha, store):
    n_up, n_dn = len(wup_refs), len(wdn_refs)
    chunk = wup_refs[0].shape[1]

    def heads_and_project(j):
        heads(j)
        if j + PROJ_AHEAD < n_sub:
            project(j + PROJ_AHEAD)

    for jj in range(min(PROJ_AHEAD, n_sub)):
        project(jj)
    heads_and_project(0)
    mix_out(0)
    if n_sub > 1:
        heads_and_project(1)

    stages = [(j, k) for j in range(n_sub) for k in range(n_up)]
    hs, ups = {}, {}

    def issue_up(s):
        j, k = stages[s]
        if j not in hs:
            xm = x_mid[j]
            hs[j] = (xm * sc2 + sh2).reshape(-1, xm.shape[-1]).astype(BF16)
        ups[s] = _dot(hs[j], wup_refs[k][...])

    for s in range(min(UP_AHEAD, len(stages))):
        issue_up(s)
    acc = [None] * n_dn
    for s, (j, k) in enumerate(stages):
        if k == MIX_OUT_AT and j + 1 < n_sub:
            mix_out(j + 1)
        if k == HEADS_AT and j + 2 < n_sub:
            heads_and_project(j + 2)
        if s + UP_AHEAD < len(stages):
            issue_up(s + UP_AHEAD)
        a = jnp.square(jnp.maximum(ups.pop(s), 0.0)).astype(BF16)
        for n in range(n_dn):
            part = _dot(a, wdn_refs[n][k * chunk:(k + 1) * chunk, :])
            acc[n] = part if acc[n] is None else acc[n] + part
        if k == n_up - 1:
            xm = x_mid.pop(j)
            f = jnp.concatenate(acc, axis=1).reshape(xm.shape)
            store(j, _layer_norm(alpha * xm + gate2 * f, ln2_g, ln2_b))
            acc = [None] * n_dn


def _layer_prompt_kernel(x_ref, mod_ref, *refs, alpha, layer, n_in, n_out, n_up, n_dn, n_next):
    win_refs, refs = refs[:n_in], refs[n_in:]
    cw_ref, sg_ref, sb_ref, ws_ref, bs_ref, pw_ref, ps_ref = refs[:7]
    wout_refs, refs = refs[7:7 + n_out], refs[7 + n_out:]
    g_ref, b_ref = refs[:2]
    wup_refs, refs = refs[2:2 + n_up], refs[2 + n_up:]
    wdn_refs, refs = refs[:n_dn], refs[n_dn:]
    g2_ref, b2_ref = refs[:2]
    next_f32, refs = refs[2:2 + n_next], refs[2 + n_next:]
    o_ref, ncv_ref, npl_ref = refs[:3]
    next_bf16, (zc_ref, pc_ref) = refs[3:3 + n_next], refs[3 + n_next:]
    for src, dst in zip(next_f32, next_bf16):
        dst[...] = src[...].astype(BF16)
    bi, t = pl.program_id(0), pl.program_id(1)
    rows, c_dim = x_ref.shape[1], x_ref.shape[2]
    w_conv, w_gmlp, w_pool = cw_ref.shape[2], sg_ref.shape[1], ps_ref.shape[1]
    n_heads = ws_ref.shape[0]

    @pl.when(t == 0)
    def _():
        zc_ref[...] = jnp.zeros(zc_ref.shape, F32)
        pc_ref[...] = jnp.zeros(pc_ref.shape, F32)

    m = mod_ref[pl.ds(bi, 1), :]
    sh, sc1, gate = m[:, :c_dim], 1.0 + m[:, c_dim:2 * c_dim], m[:, 2 * c_dim:3 * c_dim]
    sh2, sc2, gate2 = (m[:, 3 * c_dim:4 * c_dim], 1.0 + m[:, 4 * c_dim:5 * c_dim],
                       m[:, 5 * c_dim:])
    cw = cw_ref[layer]
    sgu_g, sgu_b = sg_ref[layer:layer + 1, :], sb_ref[layer:layer + 1, :]
    pool_scale = ps_ref[layer:layer + 1, :]
    ln_g, ln_b = g_ref[layer:layer + 1, :], b_ref[layer:layer + 1, :]
    ln2_g, ln2_b = g2_ref[layer:layer + 1, :], b2_ref[layer:layer + 1, :]

    row_i = lax.broadcasted_iota(jnp.int32, (CHUNK, 2 * CHUNK), 0)
    col_j = lax.broadcasted_iota(jnp.int32, (CHUNK, 2 * CHUNK), 1) % CHUNK
    causal = col_j <= row_i
    ws_m = []
    for p in range(n_heads // 2):
        pair = jnp.concatenate([ws_ref[2 * p], ws_ref[2 * p + 1]], axis=1)
        ws_m.append(jnp.where(causal, pair, 0.0).astype(BF16))
    low_half = lax.broadcasted_iota(jnp.int32, (CHUNK, LANES), 1) < HEAD_DIM
    lane_group = _lane_group((SUB_ROWS, w_pool), len(POOL_WINDOWS))
    window = _select_by_group([jnp.full((SUB_ROWS, w_pool), w, jnp.int32) for w in POOL_WINDOWS],
                              lane_group)
    row_id = lax.broadcasted_iota(jnp.int32, (SUB_ROWS, w_pool), 0)
    bias = bs_ref[...]
    n_sub = rows // SUB_ROWS
    n_ch = SUB_ROWS // CHUNK
    hist = {"z": zc_ref[...], "p": pc_ref[...]}
    projs, mids, x_mid = {}, {}, {}

    def project(jj):
        h = (x_ref[0, jj * SUB_ROWS:(jj + 1) * SUB_ROWS, :] * sc1 + sh).astype(BF16)
        projs[jj] = jnp.concatenate([_dot(h, w[...]) for w in win_refs], axis=1)

    def heads(j):
        proj = projs.pop(j)
        o0 = 0
        a_b = proj[:, o0:o0 + w_conv]; o0 += w_conv
        a_c = proj[:, o0:o0 + w_conv]; o0 += w_conv
        a_x = proj[:, o0:o0 + w_conv]; o0 += w_conv
        g_u = proj[:, o0:o0 + w_gmlp]; o0 += w_gmlp
        g_v = proj[:, o0:o0 + w_gmlp]; o0 += w_gmlp
        p_x = proj[:, o0:o0 + w_pool]

        z = a_c * a_x
        z_ext = jnp.concatenate([hist["z"], z], axis=0)
        conv = z_ext * cw[CONV_WIDTH - 1:CONV_WIDTH, :]
        for k in range(1, CONV_WIDTH):
            conv = conv + pltpu.roll(z_ext, k, 0) * cw[CONV_WIDTH - 1 - k:CONV_WIDTH - k, :]
        y_a = a_b * conv[CONV_HIST:, :]
        hist["z"] = z[SUB_ROWS - CONV_HIST:, :]

        v_n = _layer_norm(g_v, sgu_g, sgu_b)
        s_cols = []
        for p in range(n_heads // 2):
            cols = [v_n[c * CHUNK:(c + 1) * CHUNK, p * LANES:(p + 1) * LANES] for c in range(n_ch)]
            rhs = jnp.concatenate(
                [jnp.concatenate([jnp.where(low_half, col, 0.0) for col in cols], axis=1),
                 jnp.concatenate([jnp.where(low_half, 0.0, col) for col in cols], axis=1)],
                axis=0).astype(BF16)
            s_cols.append(_dot(ws_m[p], rhs))

        run = jnp.concatenate([hist["p"], p_x], axis=0)
        sums = []
        for w in POOL_WINDOWS:
            run = run + pltpu.roll(run, w // 2, 0)
            sums.append(run[POOL_HIST:, :])
        count = jnp.minimum(window, t * rows + j * SUB_ROWS + row_id + 1).astype(F32)
        d = (_select_by_group(sums, lane_group) / count - p_x).astype(BF16)
        y_c_raw = _dot(d, pw_ref[...])
        hist["p"] = p_x[SUB_ROWS - POOL_HIST:, :]
        mids[j] = (y_a, g_u, s_cols, y_c_raw)

    def mix_out(j):
        y_a, g_u, s_cols, y_c_raw = mids.pop(j)
        s = jnp.concatenate(
            [jnp.concatenate([sc[:, c * LANES:(c + 1) * LANES] for sc in s_cols], axis=1) + bias
             for c in range(n_ch)], axis=0)
        mix = jnp.concatenate([y_a, g_u * s, y_c_raw * pool_scale], axis=1).astype(BF16)
        out = jnp.concatenate([_dot(mix, w[...]) for w in wout_refs], axis=1)
        x = x_ref[0, j * SUB_ROWS:(j + 1) * SUB_ROWS, :]
        x_mid[j] = _layer_norm(alpha * x + gate * out, ln_g, ln_b)

    def store(j, y):
        o_ref[0, j * SUB_ROWS:(j + 1) * SUB_ROWS, :] = y

    _layer_pipeline(n_sub, project, heads, mix_out, x_mid, sh2, sc2, gate2, wup_refs, wdn_refs,
                    ln2_g, ln2_b, alpha, store)

    zc_ref[...] = hist["z"]
    pc_ref[...] = hist["p"]
    ncv_ref[0] = hist["z"][CONV_HIST - (CONV_WIDTH - 1):, :]
    npl_ref[0] = hist["p"][POOL_HIST - (max(POOL_WINDOWS) - 1):, :]


def _layer_prompt_call(x, mod, w_in, conv_w, sgu_g, sgu_b, w_s, bs_full, pool_wbd, pool_scale,
                       w_out, ln_g, ln_b, wup, wdn, ln2_g, ln2_b, next_weights, *, alpha, layer,
                       rows):
    batch, seq, c_dim = x.shape
    n_steps = batch * (seq // rows)
    w_conv, w_pool = conv_w.shape[2], pool_scale.shape[1]
    conv_buf = CONV_WIDTH - 1
    pool_buf = max(POOL_WINDOWS) - 1
    in_specs = _column_blocks(w_in, WEIGHT_COLS)
    out_specs = _column_blocks(w_out, WEIGHT_COLS)
    up_specs = _column_blocks(wup, WEIGHT_COLS)
    dn_specs = _column_blocks(wdn, WEIGHT_COLS)
    kern = functools.partial(_layer_prompt_kernel, alpha=alpha, layer=layer, n_in=len(in_specs),
                             n_out=len(out_specs), n_up=len(up_specs), n_dn=len(dn_specs),
                             n_next=len(next_weights))
    n_t = seq // rows
    assert all(w.shape[1] % n_steps == 0 for w in next_weights)
    next_in = [pl.BlockSpec((None, w.shape[1] // n_steps, w.shape[2]),
                            lambda b, t: (layer + 1, b * n_t + t, 0)) for w in next_weights]
    next_out = [pl.BlockSpec((w.shape[1] // n_steps, w.shape[2]), lambda b, t: (b * n_t + t, 0))
                for w in next_weights]
    return pl.pallas_call(
        kern,
        grid=(batch, seq // rows),
        in_specs=[
            pl.BlockSpec((1, rows, c_dim), lambda b, t: (b, t, 0)),
            pl.BlockSpec((None, batch, mod.shape[2]), lambda b, t: (layer, 0, 0)),
            *in_specs,
            _whole(conv_w.shape),
            _whole(sgu_g.shape),
            _whole(sgu_b.shape),
            _layer_block(w_s.shape[1:], layer),
            _layer_block(bs_full.shape[1:], layer),
            _layer_block(pool_wbd.shape[1:], layer),
            _whole(pool_scale.shape),
            *out_specs,
            _whole(ln_g.shape),
            _whole(ln_b.shape),
            *up_specs, *dn_specs,
            _whole(ln2_g.shape),
            _whole(ln2_b.shape),
            *next_in,
        ],
        out_specs=[
            pl.BlockSpec((1, rows, c_dim), lambda b, t: (b, t, 0)),
            pl.BlockSpec((1, conv_buf, w_conv), lambda b, t: (b, 0, 0)),
            pl.BlockSpec((1, pool_buf, w_pool), lambda b, t: (b, 0, 0)),
            *next_out,
        ],
        out_shape=[
            jax.ShapeDtypeStruct((batch, seq, c_dim), F32),
            jax.ShapeDtypeStruct((batch, conv_buf, w_conv), F32),
            jax.ShapeDtypeStruct((batch, pool_buf, w_pool), F32),
            *[jax.ShapeDtypeStruct(w.shape[1:], BF16) for w in next_weights],
        ],
        scratch_shapes=[
            pltpu.VMEM((CONV_HIST, w_conv), F32),
            pltpu.VMEM((POOL_HIST, w_pool), F32),
        ],
        input_output_aliases={0: 0} if layer > 0 else {},
        compiler_params=pltpu.CompilerParams(
            dimension_semantics=("arbitrary", "arbitrary"),
            vmem_limit_bytes=VMEM_LIMIT_BYTES),
        name="layer_prompt",
    )(x, mod, *([w_in] * len(in_specs)), conv_w, sgu_g, sgu_b, w_s, bs_full, pool_wbd, pool_scale,
      *([w_out] * len(out_specs)), ln_g, ln_b, *([wup] * len(up_specs)),
      *([wdn] * len(dn_specs)), ln2_g, ln2_b, *next_weights)


def _layer_sample_kernel(x_ref, mod_ref, cst_ref, pst_ref, *refs, alpha, layer, n_in, n_out,
                         n_up, n_dn, x_batch_major, out_batch_major):
    win_refs, refs = refs[:n_in], refs[n_in:]
    cw_ref, sg_ref, sb_ref, wst_ref, bst_ref, pw_ref, ps_ref = refs[:7]
    wout_refs, refs = refs[7:7 + n_out], refs[7 + n_out:]
    g_ref, b_ref = refs[:2]
    wup_refs, refs = refs[2:2 + n_up], refs[2 + n_up:]
    wdn_refs, refs = refs[:n_dn], refs[n_dn:]
    g2_ref, b2_ref = refs[:2]
    o_ref, ncv_ref, npl_ref, v_ref = refs[len(refs) - 4:]
    xt = jnp.swapaxes(x_ref[...], 0, 1) if x_batch_major else None
    cst = jnp.swapaxes(cst_ref[...], 0, 1)
    pst = jnp.swapaxes(pst_ref[...], 0, 1)
    t_dim, b_dim = v_ref.shape[1], v_ref.shape[0]
    c_dim = x_ref.shape[2]
    w_conv, w_gmlp, w_pool = cw_ref.shape[2], sg_ref.shape[1], ps_ref.shape[1]
    n_valid = pst.shape[0]
    step = SUB_ROWS // b_dim
    n_sub = t_dim // step

    m = mod_ref[...]
    sh, sc1, gate = m[:, :c_dim], 1.0 + m[:, c_dim:2 * c_dim], m[:, 2 * c_dim:3 * c_dim]
    sh2, sc2, gate2 = (m[:, 3 * c_dim:4 * c_dim], 1.0 + m[:, 4 * c_dim:5 * c_dim],
                       m[:, 5 * c_dim:])
    cw = cw_ref[layer]
    sgu_g, sgu_b = sg_ref[layer:layer + 1, :], sb_ref[layer:layer + 1, :]
    pool_scale = ps_ref[layer:layer + 1, :]
    ln_g, ln_b = g_ref[layer:layer + 1, :], b_ref[layer:layer + 1, :]
    ln2_g, ln2_b = g2_ref[layer:layer + 1, :], b2_ref[layer:layer + 1, :]
    lane_group = _lane_group((b_dim, w_pool), len(POOL_WINDOWS))

    z_ext = [cst[i] for i in range(CONV_WIDTH - 1)]
    p_ext = [pst[i] for i in range(n_valid)]
    v_all = []
    projs, mids, x_mid, outs = {}, {}, {}, {}

    def x_rows(j):
        if x_batch_major:
            return xt[j * step:(j + 1) * step]
        return x_ref[j * step:(j + 1) * step]

    def project(j):
        h = (x_rows(j) * sc1 + sh).reshape(step * b_dim, c_dim).astype(BF16)
        projs[j] = jnp.concatenate([_dot(h, w[...]) for w in win_refs], axis=1).reshape(
            step, b_dim, -1)

    def heads(j):
        proj = projs.pop(j)
        o0 = 0
        a_b = proj[:, :, o0:o0 + w_conv]; o0 += w_conv
        a_c = proj[:, :, o0:o0 + w_conv]; o0 += w_conv
        a_x = proj[:, :, o0:o0 + w_conv]; o0 += w_conv
        g_u = proj[:, :, o0:o0 + w_gmlp]; o0 += w_gmlp
        g_v = proj[:, :, o0:o0 + w_gmlp]; o0 += w_gmlp
        p_x = proj[:, :, o0:o0 + w_pool]
        z = a_c * a_x
        v_n = _layer_norm(g_v, sgu_g, sgu_b)
        y_a, y_b, d = [], [], []
        for i in range(step):
            pos = j * step + i
            z_ext.append(z[i])
            conv = z_ext[pos] * cw[0:1, :]
            for k in range(1, CONV_WIDTH):
                conv = conv + z_ext[pos + k] * cw[k:k + 1, :]
            y_a.append(a_b[i] * conv)
            v_all.append(v_n[i])
            s = bst_ref[pos:pos + 1, :] + wst_ref[pos, 0:1, :] * v_all[0]
            for q in range(1, pos + 1):
                s = s + wst_ref[pos, q:q + 1, :] * v_all[q]
            y_b.append(g_u[i] * s)
            p_ext.append(p_x[i])
            last = n_valid + pos
            means, run, k = [], None, 0
            for w in POOL_WINDOWS:
                while k < w:
                    run = p_ext[last - k] if run is None else run + p_ext[last - k]
                    k += 1
                means.append(run / float(min(w, pos + 1 + n_valid)))
            d.append(_select_by_group(means, lane_group) - p_x[i])
        d = jnp.stack(d).reshape(step * b_dim, w_pool).astype(BF16)
        mids[j] = (jnp.stack(y_a), jnp.stack(y_b), _dot(d, pw_ref[...]))

    def mix_out(j):
        y_a, y_b, y_c_raw = mids.pop(j)
        y_c = (y_c_raw * pool_scale).reshape(step, b_dim, w_pool)
        mix = jnp.concatenate([y_a, y_b, y_c], axis=2).reshape(step * b_dim, c_dim).astype(BF16)
        out = jnp.concatenate([_dot(mix, w[...]) for w in wout_refs], axis=1)
        x_mid[j] = _layer_norm(alpha * x_rows(j) + gate * out.reshape(step, b_dim, c_dim),
                               ln_g, ln_b)

    def store(j, y):
        if out_batch_major:
            outs[j] = y
        else:
            o_ref[j * step:(j + 1) * step] = y

    _layer_pipeline(n_sub, project, heads, mix_out, x_mid, sh2, sc2, gate2, wup_refs, wdn_refs,
                    ln2_g, ln2_b, alpha, store)

    if out_batch_major:
        o_ref[...] = jnp.swapaxes(jnp.concatenate([outs[j] for j in range(n_sub)], axis=0), 0, 1)
    ncv_ref[...] = jnp.swapaxes(jnp.stack(z_ext[len(z_ext) - (CONV_WIDTH - 1):]), 0, 1)
    npl_ref[...] = jnp.swapaxes(jnp.stack(p_ext[len(p_ext) - n_valid:]), 0, 1)
    v_ref[...] = jnp.swapaxes(jnp.stack(v_all), 0, 1)


def _layer_sample_call(x, mod, cst, pst, w_in, conv_w, sgu_g, sgu_b, wst, bs_full, pool_wbd,
                       pool_scale, w_out, ln_g, ln_b, wup, wdn, ln2_g, ln2_b, states, *, alpha,
                       layer, x_batch_major, out_batch_major):
    if x_batch_major:
        b_dim, t_dim, c_dim = x.shape
    else:
        t_dim, b_dim, c_dim = x.shape
    w_gmlp = sgu_g.shape[1]
    in_specs = _column_blocks(w_in, WEIGHT_COLS)
    out_specs = _column_blocks(w_out, WEIGHT_COLS)
    up_specs = _column_blocks(wup, WEIGHT_COLS)
    dn_specs = _column_blocks(wdn, WEIGHT_COLS)
    kern = functools.partial(_layer_sample_kernel, alpha=alpha, layer=layer, n_in=len(in_specs),
                             n_out=len(out_specs), n_up=len(up_specs), n_dn=len(dn_specs),
                             x_batch_major=x_batch_major, out_batch_major=out_batch_major)
    out_shape = (b_dim, t_dim, c_dim) if out_batch_major else (t_dim, b_dim, c_dim)
    state_shapes = [cst.shape, pst.shape, (cst.shape[0], b_dim, t_dim, w_gmlp)]
    whole3 = lambda i: (0, 0, 0)
    inputs = (x, mod, cst, pst, *([w_in] * len(in_specs)), conv_w, sgu_g, sgu_b, wst, bs_full,
              pool_wbd, pool_scale, *([w_out] * len(out_specs)), ln_g, ln_b,
              *([wup] * len(up_specs)), *([wdn] * len(dn_specs)), ln2_g, ln2_b)
    return pl.pallas_call(
        kern,
        grid=(1,),
        in_specs=[
            pl.BlockSpec(x.shape, whole3),
            pl.BlockSpec((None, b_dim, mod.shape[2]), lambda i: (layer, 0, 0)),
            _layer_block(cst.shape[1:], layer),
            _layer_block(pst.shape[1:], layer),
            *in_specs,
            _whole(conv_w.shape),
            _whole(sgu_g.shape),
            _whole(sgu_b.shape),
            _layer_block(wst.shape[1:], layer),
            pl.BlockSpec((None, t_dim, w_gmlp), lambda i: (layer, 0, 0)),
            _layer_block(pool_wbd.shape[1:], layer),
            _whole(pool_scale.shape),
            *out_specs,
            _whole(ln_g.shape),
            _whole(ln_b.shape),
            *up_specs, *dn_specs,
            _whole(ln2_g.shape),
            _whole(ln2_b.shape),
            *[pl.BlockSpec(memory_space=pl.ANY) for _ in states],
        ],
        out_specs=[pl.BlockSpec(out_shape, whole3)] + [
            pl.BlockSpec((None,) + shape[1:], lambda i: (layer, 0, 0, 0)) for shape in state_shapes],
        out_shape=[jax.ShapeDtypeStruct(out_shape, F32)] + [
            jax.ShapeDtypeStruct(shape, F32) for shape in state_shapes],
        input_output_aliases={len(inputs) + k: 1 + k for k in range(len(states))},
        compiler_params=pltpu.CompilerParams(
            dimension_semantics=("arbitrary",),
            vmem_limit_bytes=VMEM_LIMIT_BYTES),
        name="layer_sample",
    )(*inputs, *states)


def kernel(x_prompt, x_sample, c_prompt, c_sample, state_conv, state_pool, w_ada, b_ada, w_in,
           conv_w, sgu_g, sgu_b, w_s, b_s, pool_w, pool_scale, w_out, ln1_g, ln1_b, w_up,
           w_down, ln2_g, ln2_b):
    depth = w_in.shape[0]
    seq = x_prompt.shape[1]
    dec_batch, dec_seq, _ = x_sample.shape
    n_groups, group_dim, _ = pool_w.shape[1:]
    w_pool = n_groups * group_dim
    alpha = (2.0 * depth) ** 0.25
    assert seq % PROMPT_ROWS == 0 and PROMPT_ROWS % SUB_ROWS == 0 and SUB_ROWS % CHUNK == 0
    assert (dec_seq * dec_batch) % SUB_ROWS == 0 and SUB_ROWS % dec_batch == 0

    weights = (w_in, w_out, w_up, w_down)
    mod_p, mod_s, *weights_b = _ada_call(c_prompt, c_sample, w_ada, b_ada, weights)

    bs_full = jnp.repeat(jnp.swapaxes(b_s, 1, 2), HEAD_DIM, axis=2)
    ws_small = jnp.repeat(jnp.transpose(w_s[:, :, :dec_seq, :dec_seq], (0, 2, 3, 1)), HEAD_DIM, axis=3)
    eye = jnp.eye(n_groups, dtype=F32)
    pool_wbd = (eye[None, :, None, :, None] * pool_w[:, :, :, None, :]).reshape(
        depth, w_pool, w_pool).astype(BF16)

    xp, xs = x_prompt, x_sample
    conv_p, pool_p, states_s = [], [], ()
    for l in range(depth):
        w_in_b, w_out_b, wup_b, wdn_b = weights_b
        xp, ncv, npl, *weights_b = _layer_prompt_call(
            xp, mod_p, w_in_b, conv_w, sgu_g, sgu_b, w_s, bs_full, pool_wbd, pool_scale,
            w_out_b, ln1_g, ln1_b, wup_b, wdn_b, ln2_g, ln2_b,
            weights if l + 1 < depth else (), alpha=alpha, layer=l, rows=PROMPT_ROWS)
        conv_p.append(ncv)
        pool_p.append(npl)

        xp, xs = lax.optimization_barrier((xp, xs))
        xs, *states_s = _layer_sample_call(
            xs, mod_s, state_conv, state_pool, w_in_b, conv_w, sgu_g, sgu_b, ws_small, bs_full,
            pool_wbd, pool_scale, w_out_b, ln1_g, ln1_b, wup_b, wdn_b, ln2_g, ln2_b, states_s,
            alpha=alpha, layer=l, x_batch_major=(l == 0), out_batch_major=(l == depth - 1))
        xp, xs = lax.optimization_barrier((xp, xs))

    return (xp, xs, jnp.stack(conv_p), jnp.stack(pool_p), *states_s)
```

```python
import functools

import jax
import jax.numpy as jnp
from jax import lax
from jax.experimental import pallas as pl
from jax.experimental.pallas import tpu as pltpu

F32 = jnp.float32
BF16 = jnp.bfloat16

LN_EPS = 1e-5
HEAD_DIM = 64
CHUNK = 128
POOL_WINDOWS = (2, 4, 8, 16)
CONV_WIDTH = 3
LANES = 128
SUBLANES = 8
VMEM_LIMIT_BYTES = 60 * 1024 * 1024

PROMPT_ROWS = 1024
SUB_ROWS = 256
UP_AHEAD = 3
PROJ_AHEAD = 2
MIX_OUT_AT = 1
HEADS_AT = 5
WEIGHT_COLS = 512
CONV_HIST = SUBLANES
POOL_HIST = 2 * SUBLANES


def _layer_norm(y, g, b):
    mu = jnp.mean(y, axis=-1, keepdims=True)
    d = y - mu
    var = jnp.mean(d * d, axis=-1, keepdims=True)
    return d * lax.rsqrt(var + LN_EPS) * g + b


def _dot(a, b):
    return jnp.dot(a, b, preferred_element_type=F32)


def _layer_block(shape, layer):
    zeros = (0,) * len(shape)
    return pl.BlockSpec((None,) + tuple(shape), lambda *_: (layer,) + zeros,
                        pipeline_mode=pl.Buffered(1))


def _whole(shape):
    zeros = (0,) * len(shape)
    return pl.BlockSpec(tuple(shape), lambda *_: zeros, pipeline_mode=pl.Buffered(1))


def _column_blocks(w, width):
    rows, cols = w.shape
    return [pl.BlockSpec((rows, width), lambda *_, n=n: (0, n), pipeline_mode=pl.Buffered(1))
            for n in range(cols // width)]


def _lane_group(shape, n_groups):
    return lax.broadcasted_iota(jnp.int32, shape, len(shape) - 1) // (shape[-1] // n_groups)


def _select_by_group(values, lane_group):
    out = values[-1]
    for g in range(len(values) - 2, -1, -1):
        out = jnp.where(lane_group == g, values[g], out)
    return out


def _ada_kernel(cp_ref, cs_ref, w_ref, b_ref, *refs):
    n_side = (len(refs) - 2) // 2
    side_f32, (op_ref, os_ref), side_bf16 = refs[:n_side], refs[n_side:n_side + 2], refs[n_side + 2:]
    for src, dst in zip(side_f32, side_bf16):
        dst[...] = src[0].astype(BF16)
    l, k = pl.program_id(0), pl.program_id(1)
    bp = cp_ref.shape[0]
    c = jnp.concatenate([cp_ref[...], cs_ref[...]], axis=0)
    part = _dot(jax.nn.silu(c).astype(BF16), w_ref[0].astype(BF16))
    part_p, part_s = part[:bp], part[bp:]

    @pl.when(k == 0)
    def _():
        bias = b_ref[pl.ds(l, 1), :]
        op_ref[0] = part_p + bias
        os_ref[0] = part_s + bias

    @pl.when(k > 0)
    def _():
        op_ref[0] += part_p
        os_ref[0] += part_s


def _ada_call(c_prompt, c_sample, w_ada, b_ada, weights, row_tile=256):
    depth, d_model, n_mod = w_ada.shape
    bp, bs = c_prompt.shape[0], c_sample.shape[0]
    n_k = d_model // row_tile
    n_steps = depth * n_k
    assert all(w.shape[1] % n_steps == 0 for w in weights)
    step = lambda l, k: l * n_k + k
    return pl.pallas_call(
        _ada_kernel,
        grid=(depth, n_k),
        in_specs=[
            pl.BlockSpec((bp, row_tile), lambda l, k: (0, k)),
            pl.BlockSpec((bs, row_tile), lambda l, k: (0, k)),
            pl.BlockSpec((1, row_tile, n_mod), lambda l, k: (l, k, 0)),
            _whole(b_ada.shape),
            *[pl.BlockSpec((1, w.shape[1] // n_steps, w.shape[2]), lambda l, k: (0, step(l, k), 0))
              for w in weights],
        ],
        out_specs=[
            pl.BlockSpec((1, bp, n_mod), lambda l, k: (l, 0, 0)),
            pl.BlockSpec((1, bs, n_mod), lambda l, k: (l, 0, 0)),
            *[pl.BlockSpec((w.shape[1] // n_steps, w.shape[2]), lambda l, k: (step(l, k), 0))
              for w in weights],
        ],
        out_shape=[
            jax.ShapeDtypeStruct((depth, bp, n_mod), F32),
            jax.ShapeDtypeStruct((depth, bs, n_mod), F32),
            *[jax.ShapeDtypeStruct(w.shape[1:], BF16) for w in weights],
        ],
        compiler_params=pltpu.CompilerParams(
            dimension_semantics=("arbitrary", "arbitrary"),
            vmem_limit_bytes=VMEM_LIMIT_BYTES),
        name="ada_mod",
    )(c_prompt, c_sample, w_ada, b_ada, *weights)


def _layer_pipeline(n_sub, project, heads, mix_out, x_mid, sh2, sc2, gate2, wup_refs, wdn_refs,
                    ln2_g, ln2_b, alpha, store):
    n_up, n_dn = len(wup_refs), len(wdn_refs)
    chunk = wup_refs[0].shape[1]

    def heads_and_project(j):
        heads(j)
        if j + PROJ_AHEAD < n_sub:
            project(j + PROJ_AHEAD)

    for jj in range(min(PROJ_AHEAD, n_sub)):
        project(jj)
    heads_and_project(0)
    mix_out(0)
    if n_sub > 1:
        heads_and_project(1)

    stages = [(j, k) for j in range(n_sub) for k in range(n_up)]
    hs, ups = {}, {}

    def issue_up(s):
        j, k = stages[s]
        if j not in hs:
            xm = x_mid[j]
            hs[j] = (xm * sc2 + sh2).reshape(-1, xm.shape[-1]).astype(BF16)
        ups[s] = _dot(hs[j], wup_refs[k][...])

    for s in range(min(UP_AHEAD, len(stages))):
        issue_up(s)
    acc = [None] * n_dn
    for s, (j, k) in enumerate(stages):
        if k == MIX_OUT_AT and j + 1 < n_sub:
            mix_out(j + 1)
        if k == HEADS_AT and j + 2 < n_sub:
            heads_and_project(j + 2)
        if s + UP_AHEAD < len(stages):
            issue_up(s + UP_AHEAD)
        a = jnp.square(jnp.maximum(ups.pop(s), 0.0)).astype(BF16)
        for n in range(n_dn):
            part = _dot(a, wdn_refs[n][k * chunk:(k + 1) * chunk, :])
            acc[n] = part if acc[n] is None else acc[n] + part
        if k == n_up - 1:
            xm = x_mid.pop(j)
            f = jnp.concatenate(acc, axis=1).reshape(xm.shape)
            store(j, _layer_norm(alpha * xm + gate2 * f, ln2_g, ln2_b))
            acc = [None] * n_dn


def _layer_prompt_kernel(x_ref, mod_ref, *refs, alpha, layer, n_in, n_out, n_up, n_dn, n_next):
    win_refs, refs = refs[:n_in], refs[n_in:]
    cw_ref, sg_ref, sb_ref, ws_ref, bs_ref, pw_ref, ps_ref = refs[:7]
    wout_refs, refs = refs[7:7 + n_out], refs[7 + n_out:]
    g_ref, b_ref = refs[:2]
    wup_refs, refs = refs[2:2 + n_up], refs[2 + n_up:]
    wdn_refs, refs = refs[:n_dn], refs[n_dn:]
    g2_ref, b2_ref = refs[:2]
    next_f32, refs = refs[2:2 + n_next], refs[2 + n_next:]
    o_ref, ncv_ref, npl_ref = refs[:3]
    next_bf16, (zc_ref, pc_ref) = refs[3:3 + n_next], refs[3 + n_next:]
    for src, dst in zip(next_f32, next_bf16):
        dst[...] = src[...].astype(BF16)
    bi, t = pl.program_id(0), pl.program_id(1)
    rows, c_dim = x_ref.shape[1], x_ref.shape[2]
    w_conv, w_gmlp, w_pool = cw_ref.shape[2], sg_ref.shape[1], ps_ref.shape[1]
    n_heads = ws_ref.shape[0]

    @pl.when(t == 0)
    def _():
        zc_ref[...] = jnp.zeros(zc_ref.shape, F32)
        pc_ref[...] = jnp.zeros(pc_ref.shape, F32)

    m = mod_ref[pl.ds(bi, 1), :]
    sh, sc1, gate = m[:, :c_dim], 1.0 + m[:, c_dim:2 * c_dim], m[:, 2 * c_dim:3 * c_dim]
    sh2, sc2, gate2 = (m[:, 3 * c_dim:4 * c_dim], 1.0 + m[:, 4 * c_dim:5 * c_dim],
                       m[:, 5 * c_dim:])
    cw = cw_ref[layer]
    sgu_g, sgu_b = sg_ref[layer:layer + 1, :], sb_ref[layer:layer + 1, :]
    pool_scale = ps_ref[layer:layer + 1, :]
    ln_g, ln_b = g_ref[layer:layer + 1, :], b_ref[layer:layer + 1, :]
    ln2_g, ln2_b = g2_ref[layer:layer + 1, :], b2_ref[layer:layer + 1, :]

    row_i = lax.broadcasted_iota(jnp.int32, (CHUNK, 2 * CHUNK), 0)
    col_j = lax.broadcasted_iota(jnp.int32, (CHUNK, 2 * CHUNK), 1) % CHUNK
    causal = col_j <= row_i
    ws_m = []
    for p in range(n_heads // 2):
        pair = jnp.concatenate([ws_ref[2 * p], ws_ref[2 * p + 1]], axis=1)
        ws_m.append(jnp.where(causal, pair, 0.0).astype(BF16))
    low_half = lax.broadcasted_iota(jnp.int32, (CHUNK, LANES), 1) < HEAD_DIM
    lane_group = _lane_group((SUB_ROWS, w_pool), len(POOL_WINDOWS))
    window = _select_by_group([jnp.full((SUB_ROWS, w_pool), w, jnp.int32) for w in POOL_WINDOWS],
                              lane_group)
    row_id = lax.broadcasted_iota(jnp.int32, (SUB_ROWS, w_pool), 0)
    bias = bs_ref[...]
    n_sub = rows // SUB_ROWS
    n_ch = SUB_ROWS // CHUNK
    hist = {"z": zc_ref[...], "p": pc_ref[...]}
    projs, mids, x_mid = {}, {}, {}

    def project(jj):
        h = (x_ref[0, jj * SUB_ROWS:(jj + 1) * SUB_ROWS, :] * sc1 + sh).astype(BF16)
        projs[jj] = jnp.concatenate([_dot(h, w[...]) for w in win_refs], axis=1)

    def heads(j):
        proj = projs.pop(j)
        o0 = 0
        a_b = proj[:, o0:o0 + w_conv]; o0 += w_conv
        a_c = proj[:, o0:o0 + w_conv]; o0 += w_conv
        a_x = proj[:, o0:o0 + w_conv]; o0 += w_conv
        g_u = proj[:, o0:o0 + w_gmlp]; o0 += w_gmlp
        g_v = proj[:, o0:o0 + w_gmlp]; o0 += w_gmlp
        p_x = proj[:, o0:o0 + w_pool]

        z = a_c * a_x
        z_ext = jnp.concatenate([hist["z"], z], axis=0)
        conv = z_ext * cw[CONV_WIDTH - 1:CONV_WIDTH, :]
        for k in range(1, CONV_WIDTH):
            conv = conv + pltpu.roll(z_ext, k, 0) * cw[CONV_WIDTH - 1 - k:CONV_WIDTH - k, :]
        y_a = a_b * conv[CONV_HIST:, :]
        hist["z"] = z[SUB_ROWS - CONV_HIST:, :]

        v_n = _layer_norm(g_v, sgu_g, sgu_b)
        s_cols = []
        for p in range(n_heads // 2):
            cols = [v_n[c * CHUNK:(c + 1) * CHUNK, p * LANES:(p + 1) * LANES] for c in range(n_ch)]
            rhs = jnp.concatenate(
                [jnp.concatenate([jnp.where(low_half, col, 0.0) for col in cols], axis=1),
                 jnp.concatenate([jnp.where(low_half, 0.0, col) for col in cols], axis=1)],
                axis=0).astype(BF16)
            s_cols.append(_dot(ws_m[p], rhs))

        run = jnp.concatenate([hist["p"], p_x], axis=0)
        sums = []
        for w in POOL_WINDOWS:
            run = run + pltpu.roll(run, w // 2, 0)
            sums.append(run[POOL_HIST:, :])
        count = jnp.minimum(window, t * rows + j * SUB_ROWS + row_id + 1).astype(F32)
        d = (_select_by_group(sums, lane_group) / count - p_x).astype(BF16)
        y_c_raw = _dot(d, pw_ref[...])
        hist["p"] = p_x[SUB_ROWS - POOL_HIST:, :]
        mids[j] = (y_a, g_u, s_cols, y_c_raw)

    def mix_out(j):
        y_a, g_u, s_cols, y_c_raw = mids.pop(j)
        s = jnp.concatenate(
            [jnp.concatenate([sc[:, c * LANES:(c + 1) * LANES] for sc in s_cols], axis=1) + bias
             for c in range(n_ch)], axis=0)
        mix = jnp.concatenate([y_a, g_u * s, y_c_raw * pool_scale], axis=1).astype(BF16)
        out = jnp.concatenate([_dot(mix, w[...]) for w in wout_refs], axis=1)
        x = x_ref[0, j * SUB_ROWS:(j + 1) * SUB_ROWS, :]
        x_mid[j] = _layer_norm(alpha * x + gate * out, ln_g, ln_b)

    def store(j, y):
        o_ref[0, j * SUB_ROWS:(j + 1) * SUB_ROWS, :] = y

    _layer_pipeline(n_sub, project, heads, mix_out, x_mid, sh2, sc2, gate2, wup_refs, wdn_refs,
                    ln2_g, ln2_b, alpha, store)

    zc_ref[...] = hist["z"]
    pc_ref[...] = hist["p"]
    ncv_ref[0] = hist["z"][CONV_HIST - (CONV_WIDTH - 1):, :]
    npl_ref[0] = hist["p"][POOL_HIST - (max(POOL_WINDOWS) - 1):, :]


def _layer_prompt_call(x, mod, w_in, conv_w, sgu_g, sgu_b, w_s, bs_full, pool_wbd, pool_scale,
                       w_out, ln_g, ln_b, wup, wdn, ln2_g, ln2_b, next_weights, *, alpha, layer,
                       rows):
    batch, seq, c_dim = x.shape
    n_steps = batch * (seq // rows)
    w_conv, w_pool = conv_w.shape[2], pool_scale.shape[1]
    conv_buf = CONV_WIDTH - 1
    pool_buf = max(POOL_WINDOWS) - 1
    in_specs = _column_blocks(w_in, WEIGHT_COLS)
    out_specs = _column_blocks(w_out, WEIGHT_COLS)
    up_specs = _column_blocks(wup, WEIGHT_COLS)
    dn_specs = _column_blocks(wdn, WEIGHT_COLS)
    kern = functools.partial(_layer_prompt_kernel, alpha=alpha, layer=layer, n_in=len(in_specs),
                             n_out=len(out_specs), n_up=len(up_specs), n_dn=len(dn_specs),
                             n_next=len(next_weights))
    n_t = seq // rows
    assert all(w.shape[1] % n_steps == 0 for w in next_weights)
    next_in = [pl.BlockSpec((None, w.shape[1] // n_steps, w.shape[2]),
                            lambda b, t: (layer + 1, b * n_t + t, 0)) for w in next_weights]
    next_out = [pl.BlockSpec((w.shape[1] // n_steps, w.shape[2]), lambda b, t: (b * n_t + t, 0))
                for w in next_weights]
    return pl.pallas_call(
        kern,
        grid=(batch, seq // rows),
        in_specs=[
            pl.BlockSpec((1, rows, c_dim), lambda b, t: (b, t, 0)),
            pl.BlockSpec((None, batch, mod.shape[2]), lambda b, t: (layer, 0, 0)),
            *in_specs,
            _whole(conv_w.shape),
            _whole(sgu_g.shape),
            _whole(sgu_b.shape),
            _layer_block(w_s.shape[1:], layer),
            _layer_block(bs_full.shape[1:], layer),
            _layer_block(pool_wbd.shape[1:], layer),
            _whole(pool_scale.shape),
            *out_specs,
            _whole(ln_g.shape),
            _whole(ln_b.shape),
            *up_specs, *dn_specs,
            _whole(ln2_g.shape),
            _whole(ln2_b.shape),
            *next_in,
        ],
        out_specs=[
            pl.BlockSpec((1, rows, c_dim), lambda b, t: (b, t, 0)),
            pl.BlockSpec((1, conv_buf, w_conv), lambda b, t: (b, 0, 0)),
            pl.BlockSpec((1, pool_buf, w_pool), lambda b, t: (b, 0, 0)),
            *next_out,
        ],
        out_shape=[
            jax.ShapeDtypeStruct((batch, seq, c_dim), F32),
            jax.ShapeDtypeStruct((batch, conv_buf, w_conv), F32),
            jax.ShapeDtypeStruct((batch, pool_buf, w_pool), F32),
            *[jax.ShapeDtypeStruct(w.shape[1:], BF16) for w in next_weights],
        ],
        scratch_shapes=[
            pltpu.VMEM((CONV_HIST, w_conv), F32),
            pltpu.VMEM((POOL_HIST, w_pool), F32),
        ],
        input_output_aliases={0: 0} if layer > 0 else {},
        compiler_params=pltpu.CompilerParams(
            dimension_semantics=("arbitrary", "arbitrary"),
            vmem_limit_bytes=VMEM_LIMIT_BYTES),
        name="layer_prompt",
    )(x, mod, *([w_in] * len(in_specs)), conv_w, sgu_g, sgu_b, w_s, bs_full, pool_wbd, pool_scale,
      *([w_out] * len(out_specs)), ln_g, ln_b, *([wup] * len(up_specs)),
      *([wdn] * len(dn_specs)), ln2_g, ln2_b, *next_weights)


def _layer_sample_kernel(x_ref, mod_ref, cst_ref, pst_ref, *refs, alpha, layer, n_in, n_out,
                         n_up, n_dn, x_batch_major, out_batch_major):
    win_refs, refs = refs[:n_in], refs[n_in:]
    cw_ref, sg_ref, sb_ref, wst_ref, bst_ref, pw_ref, ps_ref = refs[:7]
    wout_refs, refs = refs[7:7 + n_out], refs[7 + n_out:]
    g_ref, b_ref = refs[:2]
    wup_refs, refs = refs[2:2 + n_up], refs[2 + n_up:]
    wdn_refs, refs = refs[:n_dn], refs[n_dn:]
    g2_ref, b2_ref = refs[:2]
    o_ref, ncv_ref, npl_ref, v_ref = refs[len(refs) - 4:]
    xt = jnp.swapaxes(x_ref[...], 0, 1) if x_batch_major else None
    cst = jnp.swapaxes(cst_ref[...], 0, 1)
    pst = jnp.swapaxes(pst_ref[...], 0, 1)
    t_dim, b_dim = v_ref.shape[1], v_ref.shape[0]
    c_dim = x_ref.shape[2]
    w_conv, w_gmlp, w_pool = cw_ref.shape[2], sg_ref.shape[1], ps_ref.shape[1]
    n_valid = pst.shape[0]
    step = SUB_ROWS // b_dim
    n_sub = t_dim // step

    m = mod_ref[...]
    sh, sc1, gate = m[:, :c_dim], 1.0 + m[:, c_dim:2 * c_dim], m[:, 2 * c_dim:3 * c_dim]
    sh2, sc2, gate2 = (m[:, 3 * c_dim:4 * c_dim], 1.0 + m[:, 4 * c_dim:5 * c_dim],
                       m[:, 5 * c_dim:])
    cw = cw_ref[layer]
    sgu_g, sgu_b = sg_ref[layer:layer + 1, :], sb_ref[layer:layer + 1, :]
    pool_scale = ps_ref[layer:layer + 1, :]
    ln_g, ln_b = g_ref[layer:layer + 1, :], b_ref[layer:layer + 1, :]
    ln2_g, ln2_b = g2_ref[layer:layer + 1, :], b2_ref[layer:layer + 1, :]
    lane_group = _lane_group((b_dim, w_pool), len(POOL_WINDOWS))

    z_ext = [cst[i] for i in range(CONV_WIDTH - 1)]
    p_ext = [pst[i] for i in range(n_valid)]
    v_all = []
    projs, mids, x_mid, outs = {}, {}, {}, {}

    def x_rows(j):
        if x_batch_major:
            return xt[j * step:(j + 1) * step]
        return x_ref[j * step:(j + 1) * step]

    def project(j):
        h = (x_rows(j) * sc1 + sh).reshape(step * b_dim, c_dim).astype(BF16)
        projs[j] = jnp.concatenate([_dot(h, w[...]) for w in win_refs], axis=1).reshape(
            step, b_dim, -1)

    def heads(j):
        proj = projs.pop(j)
        o0 = 0
        a_b = proj[:, :, o0:o0 + w_conv]; o0 += w_conv
        a_c = proj[:, :, o0:o0 + w_conv]; o0 += w_conv
        a_x = proj[:, :, o0:o0 + w_conv]; o0 += w_conv
        g_u = proj[:, :, o0:o0 + w_gmlp]; o0 += w_gmlp
        g_v = proj[:, :, o0:o0 + w_gmlp]; o0 += w_gmlp
        p_x = proj[:, :, o0:o0 + w_pool]
        z = a_c * a_x
        v_n = _layer_norm(g_v, sgu_g, sgu_b)
        y_a, y_b, d = [], [], []
        for i in range(step):
            pos = j * step + i
            z_ext.append(z[i])
            conv = z_ext[pos] * cw[0:1, :]
            for k in range(1, CONV_WIDTH):
                conv = conv + z_ext[pos + k] * cw[k:k + 1, :]
            y_a.append(a_b[i] * conv)
            v_all.append(v_n[i])
            s = bst_ref[pos:pos + 1, :] + wst_ref[pos, 0:1, :] * v_all[0]
            for q in range(1, pos + 1):
                s = s + wst_ref[pos, q:q + 1, :] * v_all[q]
            y_b.append(g_u[i] * s)
            p_ext.append(p_x[i])
            last = n_valid + pos
            means, run, k = [], None, 0
            for w in POOL_WINDOWS:
                while k < w:
                    run = p_ext[last - k] if run is None else run + p_ext[last - k]
                    k += 1
                means.append(run / float(min(w, pos + 1 + n_valid)))
            d.append(_select_by_group(means, lane_group) - p_x[i])
        d = jnp.stack(d).reshape(step * b_dim, w_pool).astype(BF16)
        mids[j] = (jnp.stack(y_a), jnp.stack(y_b), _dot(d, pw_ref[...]))

    def mix_out(j):
        y_a, y_b, y_c_raw = mids.pop(j)
        y_c = (y_c_raw * pool_scale).reshape(step, b_dim, w_pool)
        mix = jnp.concatenate([y_a, y_b, y_c], axis=2).reshape(step * b_dim, c_dim).astype(BF16)
        out = jnp.concatenate([_dot(mix, w[...]) for w in wout_refs], axis=1)
        x_mid[j] = _layer_norm(alpha * x_rows(j) + gate * out.reshape(step, b_dim, c_dim),
                               ln_g, ln_b)

    def store(j, y):
        if out_batch_major:
            outs[j] = y
        else:
            o_ref[j * step:(j + 1) * step] = y

    _layer_pipeline(n_sub, project, heads, mix_out, x_mid, sh2, sc2, gate2, wup_refs, wdn_refs,
                    ln2_g, ln2_b, alpha, store)

    if out_batch_major:
        o_ref[...] = jnp.swapaxes(jnp.concatenate([outs[j] for j in range(n_sub)], axis=0), 0, 1)
    ncv_ref[...] = jnp.swapaxes(jnp.stack(z_ext[len(z_ext) - (CONV_WIDTH - 1):]), 0, 1)
    npl_ref[...] = jnp.swapaxes(jnp.stack(p_ext[len(p_ext) - n_valid:]), 0, 1)
    v_ref[...] = jnp.swapaxes(jnp.stack(v_all), 0, 1)


def _layer_sample_call(x, mod, cst, pst, w_in, conv_w, sgu_g, sgu_b, wst, bs_full, pool_wbd,
                       pool_scale, w_out, ln_g, ln_b, wup, wdn, ln2_g, ln2_b, states, *, alpha,
                       layer, x_batch_major, out_batch_major):
    if x_batch_major:
        b_dim, t_dim, c_dim = x.shape
    else:
        t_dim, b_dim, c_dim = x.shape
    w_gmlp = sgu_g.shape[1]
    in_specs = _column_blocks(w_in, WEIGHT_COLS)
    out_specs = _column_blocks(w_out, WEIGHT_COLS)
    up_specs = _column_blocks(wup, WEIGHT_COLS)
    dn_specs = _column_blocks(wdn, WEIGHT_COLS)
    kern = functools.partial(_layer_sample_kernel, alpha=alpha, layer=layer, n_in=len(in_specs),
                             n_out=len(out_specs), n_up=len(up_specs), n_dn=len(dn_specs),
                             x_batch_major=x_batch_major, out_batch_major=out_batch_major)
    out_shape = (b_dim, t_dim, c_dim) if out_batch_major else (t_dim, b_dim, c_dim)
    state_shapes = [cst.shape, pst.shape, (cst.shape[0], b_dim, t_dim, w_gmlp)]
    whole3 = lambda i: (0, 0, 0)
    inputs = (x, mod, cst, pst, *([w_in] * len(in_specs)), conv_w, sgu_g, sgu_b, wst, bs_full,
              pool_wbd, pool_scale, *([w_out] * len(out_specs)), ln_g, ln_b,
              *([wup] * len(up_specs)), *([wdn] * len(dn_specs)), ln2_g, ln2_b)
    return pl.pallas_call(
        kern,
        grid=(1,),
        in_specs=[
            pl.BlockSpec(x.shape, whole3),
            pl.BlockSpec((None, b_dim, mod.shape[2]), lambda i: (layer, 0, 0)),
            _layer_block(cst.shape[1:], layer),
            _layer_block(pst.shape[1:], layer),
            *in_specs,
            _whole(conv_w.shape),
            _whole(sgu_g.shape),
            _whole(sgu_b.shape),
            _layer_block(wst.shape[1:], layer),
            pl.BlockSpec((None, t_dim, w_gmlp), lambda i: (layer, 0, 0)),
            _layer_block(pool_wbd.shape[1:], layer),
            _whole(pool_scale.shape),
            *out_specs,
            _whole(ln_g.shape),
            _whole(ln_b.shape),
            *up_specs, *dn_specs,
            _whole(ln2_g.shape),
            _whole(ln2_b.shape),
            *[pl.BlockSpec(memory_space=pl.ANY) for _ in states],
        ],
        out_specs=[pl.BlockSpec(out_shape, whole3)] + [
            pl.BlockSpec((None,) + shape[1:], lambda i: (layer, 0, 0, 0)) for shape in state_shapes],
        out_shape=[jax.ShapeDtypeStruct(out_shape, F32)] + [
            jax.ShapeDtypeStruct(shape, F32) for shape in state_shapes],
        input_output_aliases={len(inputs) + k: 1 + k for k in range(len(states))},
        compiler_params=pltpu.CompilerParams(
            dimension_semantics=("arbitrary",),
            vmem_limit_bytes=VMEM_LIMIT_BYTES),
        name="layer_sample",
    )(*inputs, *states)


def kernel(x_prompt, x_sample, c_prompt, c_sample, state_conv, state_pool, w_ada, b_ada, w_in,
           conv_w, sgu_g, sgu_b, w_s, b_s, pool_w, pool_scale, w_out, ln1_g, ln1_b, w_up,
           w_down, ln2_g, ln2_b):
    depth = w_in.shape[0]
    seq = x_prompt.shape[1]
    dec_batch, dec_seq, _ = x_sample.shape
    n_groups, group_dim, _ = pool_w.shape[1:]
    w_pool = n_groups * group_dim
    alpha = (2.0 * depth) ** 0.25
    assert seq % PROMPT_ROWS == 0 and PROMPT_ROWS % SUB_ROWS == 0 and SUB_ROWS % CHUNK == 0
    assert (dec_seq * dec_batch) % SUB_ROWS == 0 and SUB_ROWS % dec_batch == 0

    weights = (w_in, w_out, w_up, w_down)
    mod_p, mod_s, *weights_b = _ada_call(c_prompt, c_sample, w_ada, b_ada, weights)

    bs_full = jnp.repeat(jnp.swapaxes(b_s, 1, 2), HEAD_DIM, axis=2)
    ws_small = jnp.repeat(jnp.transpose(w_s[:, :, :dec_seq, :dec_seq], (0, 2, 3, 1)), HEAD_DIM, axis=3)
    eye = jnp.eye(n_groups, dtype=F32)
    pool_wbd = (eye[None, :, None, :, None] * pool_w[:, :, :, None, :]).reshape(
        depth, w_pool, w_pool).astype(BF16)

    xp, xs = x_prompt, x_sample
    conv_p, pool_p, states_s = [], [], ()
    for l in range(depth):
        w_in_b, w_out_b, wup_b, wdn_b = weights_b
        xp, ncv, npl, *weights_b = _layer_prompt_call(
            xp, mod_p, w_in_b, conv_w, sgu_g, sgu_b, w_s, bs_full, pool_wbd, pool_scale,
            w_out_b, ln1_g, ln1_b, wup_b, wdn_b, ln2_g, ln2_b,
            weights if l + 1 < depth else (), alpha=alpha, layer=l, rows=PROMPT_ROWS)
        conv_p.append(ncv)
        pool_p.append(npl)

        xp, xs = lax.optimization_barrier((xp, xs))
        xs, *states_s = _layer_sample_call(
            xs, mod_s, state_conv, state_pool, w_in_b, conv_w, sgu_g, sgu_b, ws_small, bs_full,
            pool_wbd, pool_scale, w_out_b, ln1_g, ln1_b, wup_b, wdn_b, ln2_g, ln2_b, states_s,
            alpha=alpha, layer=l, x_batch_major=(l == 0), out_batch_major=(l == depth - 1))
        xp, xs = lax.optimization_barrier((xp, xs))

    return (xp, xs, jnp.stack(conv_p), jnp.stack(pool_p), *states_s)
```

```python
import functools

import jax
import jax.numpy as jnp
from jax import lax
from jax.experimental import pallas as pl
from jax.experimental.pallas import tpu as pltpu

F32 = jnp.float32
BF16 = jnp.bfloat16

LN_EPS = 1e-5
HEAD_DIM = 64
CHUNK = 128
POOL_WINDOWS = (2, 4, 8, 16)
CONV_WIDTH = 3
LANES = 128
SUBLANES = 8
VMEM_LIMIT_BYTES = 60 * 1024 * 1024

PROMPT_ROWS = 1024
SUB_ROWS = 256
UP_AHEAD = 3
PROJ_AHEAD = 2
MIX_OUT_AT = 2
HEADS_AT = 5
WEIGHT_COLS = 512
CONV_HIST = SUBLANES
POOL_HIST = 2 * SUBLANES


def _layer_norm(y, g, b):
    mu = jnp.mean(y, axis=-1, keepdims=True)
    d = y - mu
    var = jnp.mean(d * d, axis=-1, keepdims=True)
    return d * lax.rsqrt(var + LN_EPS) * g + b


def _dot(a, b):
    return jnp.dot(a, b, preferred_element_type=F32)


def _layer_block(shape, layer):
    zeros = (0,) * len(shape)
    return pl.BlockSpec((None,) + tuple(shape), lambda *_: (layer,) + zeros,
                        pipeline_mode=pl.Buffered(1))


def _whole(shape):
    zeros = (0,) * len(shape)
    return pl.BlockSpec(tuple(shape), lambda *_: zeros, pipeline_mode=pl.Buffered(1))


def _column_blocks(w, width):
    rows, cols = w.shape
    return [pl.BlockSpec((rows, width), lambda *_, n=n: (0, n), pipeline_mode=pl.Buffered(1))
            for n in range(cols // width)]


class _Streamed:
    def __init__(self, buf, copies, rows_per_part):
        self.buf, self.copies, self.rows = buf, list(copies), rows_per_part
        self.shape = buf.shape

    def __getitem__(self, idx):
        part = 0 if idx is Ellipsis else idx[0].start // self.rows
        copy, self.copies[part] = self.copies[part], None
        if copy is not None:
            copy.wait()
        return self.buf[idx]


def _lane_group(shape, n_groups):
    return lax.broadcasted_iota(jnp.int32, shape, len(shape) - 1) // (shape[-1] // n_groups)


def _select_by_group(values, lane_group):
    out = values[-1]
    for g in range(len(values) - 2, -1, -1):
        out = jnp.where(lane_group == g, values[g], out)
    return out


def _ada_kernel(cp_ref, cs_ref, w_ref, b_ref, *refs):
    n_side = (len(refs) - 2) // 2
    side_f32, (op_ref, os_ref), side_bf16 = refs[:n_side], refs[n_side:n_side + 2], refs[n_side + 2:]
    for src, dst in zip(side_f32, side_bf16):
        dst[...] = src[0].astype(BF16)
    l, k = pl.program_id(0), pl.program_id(1)
    bp = cp_ref.shape[0]
    c = jnp.concatenate([cp_ref[...], cs_ref[...]], axis=0)
    part = _dot(jax.nn.silu(c).astype(BF16), w_ref[0].astype(BF16))
    part_p, part_s = part[:bp], part[bp:]

    @pl.when(k == 0)
    def _():
        bias = b_ref[pl.ds(l, 1), :]
        op_ref[0] = part_p + bias
        os_ref[0] = part_s + bias

    @pl.when(k > 0)
    def _():
        op_ref[0] += part_p
        os_ref[0] += part_s


def _ada_call(c_prompt, c_sample, w_ada, b_ada, weights, row_tile=256):
    depth, d_model, n_mod = w_ada.shape
    bp, bs = c_prompt.shape[0], c_sample.shape[0]
    n_k = d_model // row_tile
    n_steps = depth * n_k
    assert all(w.shape[1] % n_steps == 0 for w in weights)
    step = lambda l, k: l * n_k + k
    return pl.pallas_call(
        _ada_kernel,
        grid=(depth, n_k),
        in_specs=[
            pl.BlockSpec((bp, row_tile), lambda l, k: (0, k)),
            pl.BlockSpec((bs, row_tile), lambda l, k: (0, k)),
            pl.BlockSpec((1, row_tile, n_mod), lambda l, k: (l, k, 0)),
            _whole(b_ada.shape),
            *[pl.BlockSpec((1, w.shape[1] // n_steps, w.shape[2]), lambda l, k: (0, step(l, k), 0))
              for w in weights],
        ],
        out_specs=[
            pl.BlockSpec((1, bp, n_mod), lambda l, k: (l, 0, 0)),
            pl.BlockSpec((1, bs, n_mod), lambda l, k: (l, 0, 0)),
            *[pl.BlockSpec((w.shape[1] // n_steps, w.shape[2]), lambda l, k: (step(l, k), 0))
              for w in weights],
        ],
        out_shape=[
            jax.ShapeDtypeStruct((depth, bp, n_mod), F32),
            jax.ShapeDtypeStruct((depth, bs, n_mod), F32),
            *[jax.ShapeDtypeStruct(w.shape[1:], BF16) for w in weights],
        ],
        compiler_params=pltpu.CompilerParams(
            dimension_semantics=("arbitrary", "arbitrary"),
            vmem_limit_bytes=VMEM_LIMIT_BYTES),
        name="ada_mod",
    )(c_prompt, c_sample, w_ada, b_ada, *weights)


def _layer_pipeline(n_sub, project, heads, mix_out, x_mid, sh2, sc2, gate2, wup_refs, wdn_refs,
                    ln2_g, ln2_b, alpha, store):
    n_up, n_dn = len(wup_refs), len(wdn_refs)
    chunk = WEIGHT_COLS

    def heads_and_project(j):
        heads(j)
        if j + PROJ_AHEAD < n_sub:
            project(j + PROJ_AHEAD)

    for jj in range(min(PROJ_AHEAD, n_sub)):
        project(jj)
    heads_and_project(0)
    mix_out(0)
    if n_sub > 1:
        heads_and_project(1)

    stages = [(j, k) for j in range(n_sub) for k in range(n_up)]
    hs, ups = {}, {}

    def issue_up(s):
        j, k = stages[s]
        if j not in hs:
            xm = x_mid[j]
            hs[j] = (xm * sc2 + sh2).reshape(-1, xm.shape[-1]).astype(BF16)
        ups[s] = _dot(hs[j], wup_refs[k][...])

    for s in range(min(UP_AHEAD, len(stages))):
        issue_up(s)
    acc = [None] * n_dn
    for s, (j, k) in enumerate(stages):
        if k == MIX_OUT_AT and j + 1 < n_sub:
            mix_out(j + 1)
        if k == HEADS_AT and j + 2 < n_sub:
            heads_and_project(j + 2)
        if s + UP_AHEAD < len(stages):
            issue_up(s + UP_AHEAD)
        a = jnp.square(jnp.maximum(ups.pop(s), 0.0)).astype(BF16)
        for n in range(n_dn):
            part = _dot(a, wdn_refs[n][k * chunk:(k + 1) * chunk, :])
            acc[n] = part if acc[n] is None else acc[n] + part
        if k == n_up - 1:
            xm = x_mid.pop(j)
            f = jnp.concatenate(acc, axis=1).reshape(xm.shape)
            store(j, _layer_norm(alpha * xm + gate2 * f, ln2_g, ln2_b))
            acc = [None] * n_dn


def _layer_prompt_kernel(x_ref, mod_ref, *refs, alpha, layer, n_in, n_out, n_up, n_dn, n_next):
    win_refs, refs = refs[:n_in], refs[n_in:]
    cw_ref, sg_ref, sb_ref, ws_ref, bs_ref, pw_ref, ps_ref = refs[:7]
    wout_refs, refs = refs[7:7 + n_out], refs[7 + n_out:]
    g_ref, b_ref = refs[:2]
    wup_refs, refs = refs[2:2 + n_up], refs[2 + n_up:]
    wdn_refs, refs = refs[:n_dn], refs[n_dn:]
    g2_ref, b2_ref = refs[:2]
    next_f32, refs = refs[2:2 + n_next], refs[2 + n_next:]
    o_ref, ncv_ref, npl_ref = refs[:3]
    next_bf16, (zc_ref, pc_ref) = refs[3:3 + n_next], refs[3 + n_next:]
    for src, dst in zip(next_f32, next_bf16):
        dst[...] = src[...].astype(BF16)
    bi, t = pl.program_id(0), pl.program_id(1)
    rows, c_dim = x_ref.shape[1], x_ref.shape[2]
    w_conv, w_gmlp, w_pool = cw_ref.shape[2], sg_ref.shape[1], ps_ref.shape[1]
    n_heads = ws_ref.shape[0]

    @pl.when(t == 0)
    def _():
        zc_ref[...] = jnp.zeros(zc_ref.shape, F32)
        pc_ref[...] = jnp.zeros(pc_ref.shape, F32)

    m = mod_ref[pl.ds(bi, 1), :]
    sh, sc1, gate = m[:, :c_dim], 1.0 + m[:, c_dim:2 * c_dim], m[:, 2 * c_dim:3 * c_dim]
    sh2, sc2, gate2 = (m[:, 3 * c_dim:4 * c_dim], 1.0 + m[:, 4 * c_dim:5 * c_dim],
                       m[:, 5 * c_dim:])
    cw = cw_ref[layer]
    sgu_g, sgu_b = sg_ref[layer:layer + 1, :], sb_ref[layer:layer + 1, :]
    pool_scale = ps_ref[layer:layer + 1, :]
    ln_g, ln_b = g_ref[layer:layer + 1, :], b_ref[layer:layer + 1, :]
    ln2_g, ln2_b = g2_ref[layer:layer + 1, :], b2_ref[layer:layer + 1, :]

    row_i = lax.broadcasted_iota(jnp.int32, (CHUNK, 2 * CHUNK), 0)
    col_j = lax.broadcasted_iota(jnp.int32, (CHUNK, 2 * CHUNK), 1) % CHUNK
    causal = col_j <= row_i
    ws_m = []
    for p in range(n_heads // 2):
        pair = jnp.concatenate([ws_ref[2 * p], ws_ref[2 * p + 1]], axis=1)
        ws_m.append(jnp.where(causal, pair, 0.0).astype(BF16))
    low_half = lax.broadcasted_iota(jnp.int32, (CHUNK, LANES), 1) < HEAD_DIM
    lane_group = _lane_group((SUB_ROWS, w_pool), len(POOL_WINDOWS))
    window = _select_by_group([jnp.full((SUB_ROWS, w_pool), w, jnp.int32) for w in POOL_WINDOWS],
                              lane_group)
    row_id = lax.broadcasted_iota(jnp.int32, (SUB_ROWS, w_pool), 0)
    bias = bs_ref[...]
    n_sub = rows // SUB_ROWS
    n_ch = SUB_ROWS // CHUNK
    hist = {"z": zc_ref[...], "p": pc_ref[...]}
    projs, mids, x_mid = {}, {}, {}

    def project(jj):
        h = (x_ref[0, jj * SUB_ROWS:(jj + 1) * SUB_ROWS, :] * sc1 + sh).astype(BF16)
        projs[jj] = jnp.concatenate([_dot(h, w[...]) for w in win_refs], axis=1)

    def heads(j):
        proj = projs.pop(j)
        o0 = 0
        a_b = proj[:, o0:o0 + w_conv]; o0 += w_conv
        a_c = proj[:, o0:o0 + w_conv]; o0 += w_conv
        a_x = proj[:, o0:o0 + w_conv]; o0 += w_conv
        g_u = proj[:, o0:o0 + w_gmlp]; o0 += w_gmlp
        g_v = proj[:, o0:o0 + w_gmlp]; o0 += w_gmlp
        p_x = proj[:, o0:o0 + w_pool]

        z = a_c * a_x
        z_ext = jnp.concatenate([hist["z"], z], axis=0)
        conv = z_ext * cw[CONV_WIDTH - 1:CONV_WIDTH, :]
        for k in range(1, CONV_WIDTH):
            conv = conv + pltpu.roll(z_ext, k, 0) * cw[CONV_WIDTH - 1 - k:CONV_WIDTH - k, :]
        y_a = a_b * conv[CONV_HIST:, :]
        hist["z"] = z[SUB_ROWS - CONV_HIST:, :]

        v_n = _layer_norm(g_v, sgu_g, sgu_b)
        s_cols = []
        for p in range(n_heads // 2):
            cols = [v_n[c * CHUNK:(c + 1) * CHUNK, p * LANES:(p + 1) * LANES] for c in range(n_ch)]
            rhs = jnp.concatenate(
                [jnp.concatenate([jnp.where(low_half, col, 0.0) for col in cols], axis=1),
                 jnp.concatenate([jnp.where(low_half, 0.0, col) for col in cols], axis=1)],
                axis=0).astype(BF16)
            s_cols.append(_dot(ws_m[p], rhs))

        run = jnp.concatenate([hist["p"], p_x], axis=0)
        sums = []
        for w in POOL_WINDOWS:
            run = run + pltpu.roll(run, w // 2, 0)
            sums.append(run[POOL_HIST:, :])
        count = jnp.minimum(window, t * rows + j * SUB_ROWS + row_id + 1).astype(F32)
        d = (_select_by_group(sums, lane_group) / count - p_x).astype(BF16)
        y_c_raw = _dot(d, pw_ref[...])
        hist["p"] = p_x[SUB_ROWS - POOL_HIST:, :]
        mids[j] = (y_a, g_u, s_cols, y_c_raw)

    def mix_out(j):
        y_a, g_u, s_cols, y_c_raw = mids.pop(j)
        s = jnp.concatenate(
            [jnp.concatenate([sc[:, c * LANES:(c + 1) * LANES] for sc in s_cols], axis=1) + bias
             for c in range(n_ch)], axis=0)
        mix = jnp.concatenate([y_a, g_u * s, y_c_raw * pool_scale], axis=1).astype(BF16)
        out = jnp.concatenate([_dot(mix, w[...]) for w in wout_refs], axis=1)
        x = x_ref[0, j * SUB_ROWS:(j + 1) * SUB_ROWS, :]
        x_mid[j] = _layer_norm(alpha * x + gate * out, ln_g, ln_b)

    def store(j, y):
        o_ref[0, j * SUB_ROWS:(j + 1) * SUB_ROWS, :] = y

    _layer_pipeline(n_sub, project, heads, mix_out, x_mid, sh2, sc2, gate2, wup_refs, wdn_refs,
                    ln2_g, ln2_b, alpha, store)

    zc_ref[...] = hist["z"]
    pc_ref[...] = hist["p"]
    ncv_ref[0] = hist["z"][CONV_HIST - (CONV_WIDTH - 1):, :]
    npl_ref[0] = hist["p"][POOL_HIST - (max(POOL_WINDOWS) - 1):, :]


def _layer_prompt_call(x, mod, w_in, conv_w, sgu_g, sgu_b, w_s, bs_full, pool_wbd, pool_scale,
                       w_out, ln_g, ln_b, wup, wdn, ln2_g, ln2_b, next_weights, *, alpha, layer,
                       rows):
    batch, seq, c_dim = x.shape
    n_steps = batch * (seq // rows)
    w_conv, w_pool = conv_w.shape[2], pool_scale.shape[1]
    conv_buf = CONV_WIDTH - 1
    pool_buf = max(POOL_WINDOWS) - 1
    in_specs = _column_blocks(w_in, WEIGHT_COLS)
    out_specs = _column_blocks(w_out, WEIGHT_COLS)
    up_specs = _column_blocks(wup, WEIGHT_COLS)
    dn_specs = _column_blocks(wdn, WEIGHT_COLS)
    kern = functools.partial(_layer_prompt_kernel, alpha=alpha, layer=layer, n_in=len(in_specs),
                             n_out=len(out_specs), n_up=len(up_specs), n_dn=len(dn_specs),
                             n_next=len(next_weights))
    n_t = seq // rows
    assert all(w.shape[1] % n_steps == 0 for w in next_weights)
    next_in = [pl.BlockSpec((None, w.shape[1] // n_steps, w.shape[2]),
                            lambda b, t: (layer + 1, b * n_t + t, 0)) for w in next_weights]
    next_out = [pl.BlockSpec((w.shape[1] // n_steps, w.shape[2]), lambda b, t: (b * n_t + t, 0))
                for w in next_weights]
    return pl.pallas_call(
        kern,
        grid=(batch, seq // rows),
        in_specs=[
            pl.BlockSpec((1, rows, c_dim), lambda b, t: (b, t, 0)),
            pl.BlockSpec((None, batch, mod.shape[2]), lambda b, t: (layer, 0, 0)),
            *in_specs,
            _whole(conv_w.shape),
            _whole(sgu_g.shape),
            _whole(sgu_b.shape),
            _layer_block(w_s.shape[1:], layer),
            _layer_block(bs_full.shape[1:], layer),
            _layer_block(pool_wbd.shape[1:], layer),
            _whole(pool_scale.shape),
            *out_specs,
            _whole(ln_g.shape),
            _whole(ln_b.shape),
            *up_specs, *dn_specs,
            _whole(ln2_g.shape),
            _whole(ln2_b.shape),
            *next_in,
        ],
        out_specs=[
            pl.BlockSpec((1, rows, c_dim), lambda b, t: (b, t, 0)),
            pl.BlockSpec((1, conv_buf, w_conv), lambda b, t: (b, 0, 0)),
            pl.BlockSpec((1, pool_buf, w_pool), lambda b, t: (b, 0, 0)),
            *next_out,
        ],
        out_shape=[
            jax.ShapeDtypeStruct((batch, seq, c_dim), F32),
            jax.ShapeDtypeStruct((batch, conv_buf, w_conv), F32),
            jax.ShapeDtypeStruct((batch, pool_buf, w_pool), F32),
            *[jax.ShapeDtypeStruct(w.shape[1:], BF16) for w in next_weights],
        ],
        scratch_shapes=[
            pltpu.VMEM((CONV_HIST, w_conv), F32),
            pltpu.VMEM((POOL_HIST, w_pool), F32),
        ],
        input_output_aliases={0: 0} if layer > 0 else {},
        compiler_params=pltpu.CompilerParams(
            dimension_semantics=("arbitrary", "arbitrary"),
            vmem_limit_bytes=VMEM_LIMIT_BYTES),
        name="layer_prompt",
    )(x, mod, *([w_in] * len(in_specs)), conv_w, sgu_g, sgu_b, w_s, bs_full, pool_wbd, pool_scale,
      *([w_out] * len(out_specs)), ln_g, ln_b, *([wup] * len(up_specs)),
      *([wdn] * len(dn_specs)), ln2_g, ln2_b, *next_weights)


def _layer_sample_kernel(x_ref, mod_ref, cst_ref, pst_ref, *refs, alpha, layer, n_in, n_out,
                         n_up, n_dn, x_batch_major, out_batch_major):
    win_refs, refs = refs[:n_in], refs[n_in:]
    cw_ref, sg_ref, sb_ref, wst_ref, bst_ref, pw_ref, ps_ref = refs[:7]
    wout_refs, refs = refs[7:7 + n_out], refs[7 + n_out:]
    g_ref, b_ref = refs[:2]
    wup_hbm, wdn_hbm = refs[2:4]
    refs = refs[4:]
    g2_ref, b2_ref = refs[:2]
    o_ref, ncv_ref, npl_ref, v_ref = refs[len(refs) - 7:len(refs) - 3]
    upbuf, dnbuf, sem = refs[len(refs) - 3:]
    wup_refs, dn_copies = [], [[] for _ in range(n_dn)]
    for k in range(n_up):
        cols = pl.ds(k * WEIGHT_COLS, WEIGHT_COLS)
        copy = pltpu.make_async_copy(wup_hbm.at[:, cols], upbuf.at[k], sem.at[k * (1 + n_dn)])
        copy.start()
        wup_refs.append(_Streamed(upbuf.at[k], [copy], upbuf.shape[1]))
        for n in range(n_dn):
            copy = pltpu.make_async_copy(
                wdn_hbm.at[cols, pl.ds(n * WEIGHT_COLS, WEIGHT_COLS)], dnbuf.at[n, cols],
                sem.at[k * (1 + n_dn) + 1 + n])
            copy.start()
            dn_copies[n].append(copy)
    wdn_refs = [_Streamed(dnbuf.at[n], dn_copies[n], WEIGHT_COLS) for n in range(n_dn)]
    xt = jnp.swapaxes(x_ref[...], 0, 1) if x_batch_major else None
    cst = jnp.swapaxes(cst_ref[...], 0, 1)
    pst = jnp.swapaxes(pst_ref[...], 0, 1)
    t_dim, b_dim = v_ref.shape[1], v_ref.shape[0]
    c_dim = x_ref.shape[2]
    w_conv, w_gmlp, w_pool = cw_ref.shape[2], sg_ref.shape[1], ps_ref.shape[1]
    n_valid = pst.shape[0]
    step = SUB_ROWS // b_dim
    n_sub = t_dim // step

    m = mod_ref[...]
    sh, sc1, gate = m[:, :c_dim], 1.0 + m[:, c_dim:2 * c_dim], m[:, 2 * c_dim:3 * c_dim]
    sh2, sc2, gate2 = (m[:, 3 * c_dim:4 * c_dim], 1.0 + m[:, 4 * c_dim:5 * c_dim],
                       m[:, 5 * c_dim:])
    cw = cw_ref[layer]
    sgu_g, sgu_b = sg_ref[layer:layer + 1, :], sb_ref[layer:layer + 1, :]
    pool_scale = ps_ref[layer:layer + 1, :]
    ln_g, ln_b = g_ref[layer:layer + 1, :], b_ref[layer:layer + 1, :]
    ln2_g, ln2_b = g2_ref[layer:layer + 1, :], b2_ref[layer:layer + 1, :]
    lane_group = _lane_group((b_dim, w_pool), len(POOL_WINDOWS))

    z_ext = [cst[i] for i in range(CONV_WIDTH - 1)]
    p_ext = [pst[i] for i in range(n_valid)]
    v_all = []
    projs, mids, x_mid, outs = {}, {}, {}, {}

    def x_rows(j):
        if x_batch_major:
            return xt[j * step:(j + 1) * step]
        return x_ref[j * step:(j + 1) * step]

    def project(j):
        h = (x_rows(j) * sc1 + sh).reshape(step * b_dim, c_dim).astype(BF16)
        projs[j] = jnp.concatenate([_dot(h, w[...]) for w in win_refs], axis=1).reshape(
            step, b_dim, -1)

    def heads(j):
        proj = projs.pop(j)
        o0 = 0
        a_b = proj[:, :, o0:o0 + w_conv]; o0 += w_conv
        a_c = proj[:, :, o0:o0 + w_conv]; o0 += w_conv
        a_x = proj[:, :, o0:o0 + w_conv]; o0 += w_conv
        g_u = proj[:, :, o0:o0 + w_gmlp]; o0 += w_gmlp
        g_v = proj[:, :, o0:o0 + w_gmlp]; o0 += w_gmlp
        p_x = proj[:, :, o0:o0 + w_pool]
        z = a_c * a_x
        v_n = _layer_norm(g_v, sgu_g, sgu_b)
        y_a, y_b, d = [], [], []
        for i in range(step):
            pos = j * step + i
            z_ext.append(z[i])
            conv = z_ext[pos] * cw[0:1, :]
            for k in range(1, CONV_WIDTH):
                conv = conv + z_ext[pos + k] * cw[k:k + 1, :]
            y_a.append(a_b[i] * conv)
            v_all.append(v_n[i])
            s = bst_ref[pos:pos + 1, :] + wst_ref[pos, 0:1, :] * v_all[0]
            for q in range(1, pos + 1):
                s = s + wst_ref[pos, q:q + 1, :] * v_all[q]
            y_b.append(g_u[i] * s)
            p_ext.append(p_x[i])
            last = n_valid + pos
            means, run, k = [], None, 0
            for w in POOL_WINDOWS:
                while k < w:
                    run = p_ext[last - k] if run is None else run + p_ext[last - k]
                    k += 1
                means.append(run / float(min(w, pos + 1 + n_valid)))
            d.append(_select_by_group(means, lane_group) - p_x[i])
        d = jnp.stack(d).reshape(step * b_dim, w_pool).astype(BF16)
        mids[j] = (jnp.stack(y_a), jnp.stack(y_b), _dot(d, pw_ref[...]))

    def mix_out(j):
        y_a, y_b, y_c_raw = mids.pop(j)
        y_c = (y_c_raw * pool_scale).reshape(step, b_dim, w_pool)
        mix = jnp.concatenate([y_a, y_b, y_c], axis=2).reshape(step * b_dim, c_dim).astype(BF16)
        out = jnp.concatenate([_dot(mix, w[...]) for w in wout_refs], axis=1)
        x_mid[j] = _layer_norm(alpha * x_rows(j) + gate * out.reshape(step, b_dim, c_dim),
                               ln_g, ln_b)

    def store(j, y):
        if out_batch_major:
            outs[j] = y
        else:
            o_ref[j * step:(j + 1) * step] = y

    _layer_pipeline(n_sub, project, heads, mix_out, x_mid, sh2, sc2, gate2, wup_refs, wdn_refs,
                    ln2_g, ln2_b, alpha, store)

    if out_batch_major:
        o_ref[...] = jnp.swapaxes(jnp.concatenate([outs[j] for j in range(n_sub)], axis=0), 0, 1)
    ncv_ref[...] = jnp.swapaxes(jnp.stack(z_ext[len(z_ext) - (CONV_WIDTH - 1):]), 0, 1)
    npl_ref[...] = jnp.swapaxes(jnp.stack(p_ext[len(p_ext) - n_valid:]), 0, 1)
    v_ref[...] = jnp.swapaxes(jnp.stack(v_all), 0, 1)


def _layer_sample_call(x, mod, cst, pst, w_in, conv_w, sgu_g, sgu_b, wst, bs_full, pool_wbd,
                       pool_scale, w_out, ln_g, ln_b, wup, wdn, ln2_g, ln2_b, states, *, alpha,
                       layer, x_batch_major, out_batch_major):
    if x_batch_major:
        b_dim, t_dim, c_dim = x.shape
    else:
        t_dim, b_dim, c_dim = x.shape
    w_gmlp = sgu_g.shape[1]
    in_specs = _column_blocks(w_in, WEIGHT_COLS)
    out_specs = _column_blocks(w_out, WEIGHT_COLS)
    n_up, n_dn = wup.shape[1] // WEIGHT_COLS, wdn.shape[1] // WEIGHT_COLS
    kern = functools.partial(_layer_sample_kernel, alpha=alpha, layer=layer, n_in=len(in_specs),
                             n_out=len(out_specs), n_up=n_up, n_dn=n_dn,
                             x_batch_major=x_batch_major, out_batch_major=out_batch_major)
    out_shape = (b_dim, t_dim, c_dim) if out_batch_major else (t_dim, b_dim, c_dim)
    state_shapes = [cst.shape, pst.shape, (cst.shape[0], b_dim, t_dim, w_gmlp)]
    whole3 = lambda i: (0, 0, 0)
    inputs = (x, mod, cst, pst, *([w_in] * len(in_specs)), conv_w, sgu_g, sgu_b, wst, bs_full,
              pool_wbd, pool_scale, *([w_out] * len(out_specs)), ln_g, ln_b,
              wup, wdn, ln2_g, ln2_b)
    return pl.pallas_call(
        kern,
        grid=(1,),
        in_specs=[
            pl.BlockSpec(x.shape, whole3),
            pl.BlockSpec((None, b_dim, mod.shape[2]), lambda i: (layer, 0, 0)),
            _layer_block(cst.shape[1:], layer),
            _layer_block(pst.shape[1:], layer),
            *in_specs,
            _whole(conv_w.shape),
            _whole(sgu_g.shape),
            _whole(sgu_b.shape),
            _layer_block(wst.shape[1:], layer),
            pl.BlockSpec((None, t_dim, w_gmlp), lambda i: (layer, 0, 0)),
            _layer_block(pool_wbd.shape[1:], layer),
            _whole(pool_scale.shape),
            *out_specs,
            _whole(ln_g.shape),
            _whole(ln_b.shape),
            pl.BlockSpec(memory_space=pl.ANY),
            pl.BlockSpec(memory_space=pl.ANY),
            _whole(ln2_g.shape),
            _whole(ln2_b.shape),
            *[pl.BlockSpec(memory_space=pl.ANY) for _ in states],
        ],
        scratch_shapes=[
            pltpu.VMEM((n_up, wup.shape[0], WEIGHT_COLS), BF16),
            pltpu.VMEM((n_dn, wdn.shape[0], WEIGHT_COLS), BF16),
            pltpu.SemaphoreType.DMA((n_up * (1 + n_dn),)),
        ],
        out_specs=[pl.BlockSpec(out_shape, whole3)] + [
            pl.BlockSpec((None,) + shape[1:], lambda i: (layer, 0, 0, 0)) for shape in state_shapes],
        out_shape=[jax.ShapeDtypeStruct(out_shape, F32)] + [
            jax.ShapeDtypeStruct(shape, F32) for shape in state_shapes],
        input_output_aliases={len(inputs) + k: 1 + k for k in range(len(states))},
        compiler_params=pltpu.CompilerParams(
            dimension_semantics=("arbitrary",),
            vmem_limit_bytes=VMEM_LIMIT_BYTES),
        name="layer_sample",
    )(*inputs, *states)


def kernel(x_prompt, x_sample, c_prompt, c_sample, state_conv, state_pool, w_ada, b_ada, w_in,
           conv_w, sgu_g, sgu_b, w_s, b_s, pool_w, pool_scale, w_out, ln1_g, ln1_b, w_up,
           w_down, ln2_g, ln2_b):
    depth = w_in.shape[0]
    seq = x_prompt.shape[1]
    dec_batch, dec_seq, _ = x_sample.shape
    n_groups, group_dim, _ = pool_w.shape[1:]
    w_pool = n_groups * group_dim
    alpha = (2.0 * depth) ** 0.25
    assert seq % PROMPT_ROWS == 0 and PROMPT_ROWS % SUB_ROWS == 0 and SUB_ROWS % CHUNK == 0
    assert (dec_seq * dec_batch) % SUB_ROWS == 0 and SUB_ROWS % dec_batch == 0

    weights = (w_in, w_out, w_up, w_down)
    mod_p, mod_s, *weights_b = _ada_call(c_prompt, c_sample, w_ada, b_ada, weights)

    bs_full = jnp.repeat(jnp.swapaxes(b_s, 1, 2), HEAD_DIM, axis=2)
    ws_small = jnp.repeat(jnp.transpose(w_s[:, :, :dec_seq, :dec_seq], (0, 2, 3, 1)), HEAD_DIM, axis=3)
    eye = jnp.eye(n_groups, dtype=F32)
    pool_wbd = (eye[None, :, None, :, None] * pool_w[:, :, :, None, :]).reshape(
        depth, w_pool, w_pool).astype(BF16)

    xp, xs = x_prompt, x_sample
    conv_p, pool_p, states_s = [], [], ()
    for l in range(depth):
        w_in_b, w_out_b, wup_b, wdn_b = weights_b
        xp, ncv, npl, *weights_b = _layer_prompt_call(
            xp, mod_p, w_in_b, conv_w, sgu_g, sgu_b, w_s, bs_full, pool_wbd, pool_scale,
            w_out_b, ln1_g, ln1_b, wup_b, wdn_b, ln2_g, ln2_b,
            weights if l + 1 < depth else (), alpha=alpha, layer=l, rows=PROMPT_ROWS)
        conv_p.append(ncv)
        pool_p.append(npl)

        xp, xs = lax.optimization_barrier((xp, xs))
        xs, *states_s = _layer_sample_call(
            xs, mod_s, state_conv, state_pool, w_in_b, conv_w, sgu_g, sgu_b, ws_small, bs_full,
            pool_wbd, pool_scale, w_out_b, ln1_g, ln1_b, wup_b, wdn_b, ln2_g, ln2_b, states_s,
            alpha=alpha, layer=l, x_batch_major=(l == 0), out_batch_major=(l == depth - 1))
        xp, xs = lax.optimization_barrier((xp, xs))

    return (xp, xs, jnp.stack(conv_p), jnp.stack(pool_p), *states_s)
```
